```python
import jax, jax.numpy as jnp
from jax import lax
import numpy as np

D_MODEL = 1024
BATCH = 8
SEQ = 4096
DEPTH = 2

N_A_LAYERS = DEPTH // 2
N_B_LAYERS = DEPTH - N_A_LAYERS

HEAD_DIM = 64
EPS = 1e-6
ROPE_THETA = 10000.0

MLSTM_HEADS = 4
MLSTM_DV = (3 * D_MODEL // 4) // MLSTM_HEADS
MLSTM_DQK = MLSTM_DV // 2
MLSTM_QK_W = MLSTM_HEADS * MLSTM_DQK
MLSTM_V_W = MLSTM_HEADS * MLSTM_DV
MLSTM_CHUNK = 64
GATE_SOFTCAP = 15.0
M_INIT = -1e30

SWA_Q_HEADS = (3 * D_MODEL // 4) // HEAD_DIM
SWA_KV_HEADS = 4
SWA_Q_W = SWA_Q_HEADS * HEAD_DIM
SWA_KV_W = SWA_KV_HEADS * HEAD_DIM
SWA_WINDOW = 128

MEM_TOKENS = 256
MEM_HEADS = 4
MEM_HEAD_DIM = (D_MODEL // 4) // MEM_HEADS
MEM_W = MEM_HEADS * MEM_HEAD_DIM

A_IN_W = 2 * MLSTM_QK_W + 2 * MLSTM_V_W + 2 * MLSTM_HEADS + MEM_W
B_IN_W = SWA_Q_W + MEM_W
MIX_W = MLSTM_V_W + MEM_W

D_FF = ((8 * D_MODEL // 3 + 255) // 256) * 256

kernel_name = "yoco_mlstm_swa_sink_hybrid"


def rmsnorm(x, g):
    xf = x.astype(jnp.float32)
    y = xf * lax.rsqrt(jnp.mean(xf * xf, axis=-1, keepdims=True) + EPS)
    return (y * g.astype(jnp.float32)).astype(x.dtype)


def rope_tables(positions):
    inv = 1.0 / (ROPE_THETA ** (jnp.arange(0, HEAD_DIM, 2, dtype=jnp.float32) / HEAD_DIM))
    ang = positions.astype(jnp.float32)[..., None] * inv
    return jnp.cos(ang), jnp.sin(ang)


def apply_rope(t, cos, sin):
    t1, t2 = jnp.split(t.astype(jnp.float32), 2, axis=-1)
    c = cos[:, :, None, :]
    s = sin[:, :, None, :]
    return jnp.concatenate([t1 * c - t2 * s, t2 * c + t1 * s], axis=-1).astype(t.dtype)


def mlstm_chunkwise(q, k, v, i_pre, f_pre):
    f32 = jnp.float32
    B_, S_, H, dqk = q.shape
    L = MLSTM_CHUNK
    NC = S_ // L

    def chunks(t):
        return t.astype(f32).reshape(B_, NC, L, H, -1).transpose(0, 3, 1, 2, 4)

    qc = chunks(q)
    kc = chunks(k) * (dqk ** -0.5)
    vc = chunks(v)
    ig = i_pre.astype(f32).reshape(B_, NC, L, H).transpose(0, 3, 1, 2)
    logf = jax.nn.log_sigmoid(f_pre.astype(f32)).reshape(B_, NC, L, H).transpose(0, 3, 1, 2)
    b = jnp.cumsum(logf, axis=-1)
    g = b[..., -1]
    a = g[..., None] - b + ig

    def step(carry, inp):
        C, n, m = carry
        g_c, a_c, k_c, v_c = inp
        m_new = jnp.maximum(g_c + m, jnp.max(a_c, axis=-1))
        decay = jnp.exp(g_c + m - m_new)
        w = jnp.exp(a_c - m_new[..., None])
        C_new = decay[..., None, None] * C + jnp.einsum('bhl,bhlv,bhlk->bhvk', w, v_c, k_c)
        n_new = decay[..., None] * n + jnp.einsum('bhl,bhlk->bhk', w, k_c)
        return (C_new, n_new, m_new), (C, n, m)

    dv = v.shape[-1]
    init = (jnp.zeros((B_, H, dv, dqk), f32), jnp.zeros((B_, H, dqk), f32), jnp.full((B_, H), M_INIT, f32))
    xs = (g.transpose(2, 0, 1), a.transpose(2, 0, 1, 3), kc.transpose(2, 0, 1, 3, 4), vc.transpose(2, 0, 1, 3, 4))
    _, (C_st, n_st, m_st) = lax.scan(step, init, xs)
    C_st = C_st.transpose(1, 2, 0, 3, 4)
    n_st = n_st.transpose(1, 2, 0, 3)
    m_st = m_st.transpose(1, 2, 0)

    causal = jnp.tril(jnp.ones((L, L), dtype=bool))
    dmat = jnp.where(causal, b[..., :, None] - b[..., None, :] + ig[..., None, :], -jnp.inf)
    inter_log = b + m_st[..., None]
    m_row = jnp.maximum(inter_log, jnp.max(dmat, axis=-1))
    scores = jnp.einsum('bhcjd,bhcsd->bhcjs', qc, kc) * jnp.exp(dmat - m_row[..., None])
    inter_w = jnp.exp(inter_log - m_row)
    num = (jnp.einsum('bhcjs,bhcsv->bhcjv', scores, vc)
           + inter_w[..., None] * jnp.einsum('bhcjk,bhcvk->bhcjv', qc, C_st))
    den = jnp.sum(scores, axis=-1) + inter_w * jnp.einsum('bhcjk,bhck->bhcj', qc, n_st)
    h = num / jnp.maximum(jnp.abs(den), jnp.exp(-m_row))[..., None]
    return h.transpose(0, 2, 3, 1, 4).reshape(B_, S_, H, dv)


def mlstm_mixer(h, w_in, b_gates, g_out):
    B_, S_, _ = h.shape
    p = h @ w_in
    q, k, v, o, gates, mq = jnp.split(
        p, [MLSTM_QK_W, 2 * MLSTM_QK_W, 2 * MLSTM_QK_W + MLSTM_V_W, 2 * MLSTM_QK_W + 2 * MLSTM_V_W,
            2 * MLSTM_QK_W + 2 * MLSTM_V_W + 2 * MLSTM_HEADS], axis=-1)
    gates = gates.astype(jnp.float32) + b_gates.astype(jnp.float32)
    gates = GATE_SOFTCAP * jnp.tanh(gates / GATE_SOFTCAP)
    i_pre, f_pre = jnp.split(gates, 2, axis=-1)
    hm = mlstm_chunkwise(q.reshape(B_, S_, MLSTM_HEADS, MLSTM_DQK), k.reshape(B_, S_, MLSTM_HEADS, MLSTM_DQK),
                         v.reshape(B_, S_, MLSTM_HEADS, MLSTM_DV), i_pre, f_pre)
    hm = hm * lax.rsqrt(jnp.mean(hm * hm, axis=-1, keepdims=True) + EPS)
    hm = hm * g_out.astype(jnp.float32).reshape(MLSTM_HEADS, MLSTM_DV)
    hm = jax.nn.sigmoid(o.astype(jnp.float32)) * hm.reshape(B_, S_, MLSTM_V_W)
    return hm.astype(h.dtype), mq.reshape(B_, S_, MEM_HEADS, MEM_HEAD_DIM)


def swa_sink_attention(q, k, v, sinks):
    B_, S_, Hq, dh = q.shape
    Hkv = k.shape[2]
    G = Hq // Hkv
    W = SWA_WINDOW
    NB = S_ // W
    qb = q.reshape(B_, NB, W, Hkv, G, dh)

    def band(t):
        tb = jnp.pad(t, ((0, 0), (W, 0), (0, 0), (0, 0))).reshape(B_, NB + 1, W, Hkv, dh)
        return jnp.concatenate([tb[:, :-1], tb[:, 1:]], axis=2)

    kband = band(k)
    vband = band(v)
    s = jnp.einsum('bnqhgd,bnkhd->bnhgqk', qb, kband).astype(jnp.float32) * (dh ** -0.5)
    qpos = jnp.arange(W)[:, None] + W
    kpos = jnp.arange(2 * W)[None, :]
    diff = qpos - kpos
    in_band = (diff >= 0) & (diff < W)
    blk = jnp.arange(NB)[:, None, None]
    valid = in_band[None] & (blk * W + kpos[None] - W >= 0)
    s = jnp.where(valid[None, :, None, None, :, :], s, -jnp.inf)
    sink = sinks.astype(jnp.float32).reshape(Hkv, G)[None, None, :, :, None, None]
    m = jnp.maximum(jnp.max(s, axis=-1, keepdims=True), sink)
    p = jnp.exp(s - m)
    p = p / (jnp.sum(p, axis=-1, keepdims=True) + jnp.exp(sink - m))
    o = jnp.einsum('bnhgqk,bnkhd->bnqhgd', p.astype(v.dtype), vband)
    return o.reshape(B_, S_, Hq * dh)


def swa_mixer(h, w_in, sinks, k_sh, v_sh, cos, sin):
    B_, S_, _ = h.shape
    p = h @ w_in
    q, mq = jnp.split(p, [SWA_Q_W], axis=-1)
    q = apply_rope(q.reshape(B_, S_, SWA_Q_HEADS, HEAD_DIM), cos, sin)
    o = swa_sink_attention(q, k_sh, v_sh, sinks)
    return o, mq.reshape(B_, S_, MEM_HEADS, MEM_HEAD_DIM)


def memory_attention(mq, mk, mv):
    B_, S_, Hm, dh = mq.shape
    s = jnp.einsum('bqhd,bmhd->bhqm', mq, mk).astype(jnp.float32) * (dh ** -0.5)
    p = jax.nn.softmax(s, axis=-1)
    o = jnp.einsum('bhqm,bmhd->bqhd', p.astype(mv.dtype), mv)
    return o.reshape(B_, S_, Hm * dh)


def shared_kv(x, g_kv, w_kv, cos, sin):
    B_, S_, _ = x.shape
    kv = rmsnorm(x, g_kv) @ w_kv
    k, v = jnp.split(kv, 2, axis=-1)
    k = apply_rope(k.reshape(B_, S_, SWA_KV_HEADS, HEAD_DIM), cos, sin)
    return k, v.reshape(B_, S_, SWA_KV_HEADS, HEAD_DIM)


def swiglu(h, w_in, w_out):
    gate, up = jnp.split(h @ w_in, 2, axis=-1)
    return (jax.nn.silu(gate) * up) @ w_out


def setup_inputs(seed: int = 0) -> dict:
    key = jax.random.key(seed)
    ks = jax.random.split(key, 24)
    f32 = jnp.float32

    def w(k, shape, fan_in):
        return jax.random.normal(k, shape, f32) * (fan_in ** -0.5)

    def gain(k, shape):
        return 1.0 + 0.05 * jax.random.normal(k, shape, f32)

    x = jax.random.normal(ks[0], (BATCH, SEQ, D_MODEL), f32)
    mem = jax.random.normal(ks[1], (BATCH, MEM_TOKENS, D_MODEL), f32)
    offset = jax.random.randint(ks[2], (BATCH,), 0, 1024, dtype=jnp.int32)
    positions = (offset[:, None] + jnp.arange(SEQ, dtype=jnp.int32)[None, :]).astype(jnp.int32)

    i_bias = 0.1 * jax.random.normal(ks[3], (N_A_LAYERS, MLSTM_HEADS), f32)
    f_bias = 3.0 + 3.0 * jax.random.uniform(ks[4], (N_A_LAYERS, MLSTM_HEADS), f32)
    return {
        "x": x,
        "mem": mem,
        "positions": positions,
        "g_mix_pre": gain(ks[5], (DEPTH, D_MODEL)),
        "g_mix_post": gain(ks[6], (DEPTH, D_MODEL)),
        "g_ffn_pre": gain(ks[7], (DEPTH, D_MODEL)),
        "g_ffn_post": gain(ks[8], (DEPTH, D_MODEL)),
        "g_mem": gain(ks[9], (DEPTH, D_MODEL)),
        "w_mem_kv": w(ks[10], (DEPTH, D_MODEL, 2 * MEM_W), D_MODEL),
        "w_out": w(ks[11], (DEPTH, MIX_W, D_MODEL), MIX_W),
        "w_ffn_in": w(ks[12], (DEPTH, D_MODEL, 2 * D_FF), D_MODEL),
        "w_ffn_out": w(ks[13], (DEPTH, D_FF, D_MODEL), D_FF),
        "w_in_a": w(ks[14], (N_A_LAYERS, D_MODEL, A_IN_W), D_MODEL),
        "b_gates_a": jnp.concatenate([i_bias, f_bias], axis=-1),
        "g_mlstm_out": gain(ks[15], (N_A_LAYERS, MLSTM_V_W)),
        "g_kv": gain(ks[16], (D_MODEL,)),
        "w_kv": w(ks[17], (D_MODEL, 2 * SWA_KV_W), D_MODEL),
        "w_in_b": w(ks[18], (N_B_LAYERS, D_MODEL, B_IN_W), D_MODEL),
        "sinks_b": 0.5 * jax.random.normal(ks[19], (N_B_LAYERS, SWA_Q_HEADS), f32),
    }


def reference(x, mem, positions, g_mix_pre, g_mix_post, g_ffn_pre, g_ffn_post, g_mem, w_mem_kv, w_out,
              w_ffn_in, w_ffn_out, w_in_a, b_gates_a, g_mlstm_out, g_kv, w_kv, w_in_b, sinks_b):
    B_ = x.shape[0]
    cos, sin = rope_tables(positions)
    k_sh = None
    v_sh = None
    for l in range(DEPTH):
        h = rmsnorm(x, g_mix_pre[l])
        mk, mv = jnp.split(rmsnorm(mem, g_mem[l]) @ w_mem_kv[l], 2, axis=-1)
        mk = mk.reshape(B_, MEM_TOKENS, MEM_HEADS, MEM_HEAD_DIM)
        mv = mv.reshape(B_, MEM_TOKENS, MEM_HEADS, MEM_HEAD_DIM)
        if l < N_A_LAYERS:
            main, mq = mlstm_mixer(h, w_in_a[l], b_gates_a[l], g_mlstm_out[l])
        else:
            j = l - N_A_LAYERS
            main, mq = swa_mixer(h, w_in_b[j], sinks_b[j], k_sh, v_sh, cos, sin)
        mo = memory_attention(mq, mk, mv)
        mix = jnp.concatenate([main, mo], axis=-1) @ w_out[l]
        x = x + rmsnorm(mix, g_mix_post[l])
        x = x + rmsnorm(swiglu(rmsnorm(x, g_ffn_pre[l]), w_ffn_in[l], w_ffn_out[l]), g_ffn_post[l])
        if l == N_A_LAYERS - 1:
            k_sh, v_sh = shared_kv(x, g_kv, w_kv, cos, sin)
    return x
```

```python
import functools

import numpy as np
import jax
import jax.numpy as jnp
from jax import lax
from jax.experimental import pallas as pl
from jax.experimental.pallas import tpu as pltpu

F32 = jnp.float32
BF16 = jnp.bfloat16

D_MODEL = 1024
DEPTH = 2
HEAD_DIM = 64
EPS = 1e-6
ROPE_THETA = 10000.0

MLSTM_HEADS = 4
MLSTM_DV = 192
MLSTM_DQK = 96
MLSTM_QK_W = MLSTM_HEADS * MLSTM_DQK
MLSTM_V_W = MLSTM_HEADS * MLSTM_DV
GATE_SOFTCAP = 15.0
M_INIT = -1e30

SWA_Q_HEADS = 12
SWA_KV_HEADS = 4
SWA_GROUPS = SWA_Q_HEADS // SWA_KV_HEADS
SWA_Q_W = SWA_Q_HEADS * HEAD_DIM
SWA_KV_W = SWA_KV_HEADS * HEAD_DIM
SWA_WINDOW = 128

MEM_TOKENS = 256
MEM_HEADS = 4
MEM_HEAD_DIM = 64
MEM_W = MEM_HEADS * MEM_HEAD_DIM

D_FF = 2816

LANES = 128
MXU_TILE = 256

DQK_PAD = LANES
QK_PAD_W = MLSTM_HEADS * DQK_PAD
V_WIN = MXU_TILE
MLSTM_CHUNK = 256
FF_CHUNK = MXU_TILE
N_FF_CHUNKS = D_FF // FF_CHUNK
ROW_TILE = 512
VMEM_LIMIT = 56 * 1024 * 1024

_V_WIN_START = (0, 128, 384, 512)
_V_WIN_OFF = (0, 64, 0, 64)
_V_ONES_COL = (192, 0, 192, 0)


def _rms(x, g):
    return x * lax.rsqrt(jnp.mean(x * x, axis=-1, keepdims=True) + EPS) * g


def _dot(a, b):
    return jnp.dot(a, b, preferred_element_type=F32)


def _dot_nt(a, b):
    return lax.dot_general(a, b, (((1,), (1,)), ((), ())), preferred_element_type=F32)


def _dot_tn(a, b):
    return lax.dot_general(a, b, (((0,), (0,)), ((), ())), preferred_element_type=F32)


def _sigmoid(x):
    return 1.0 / (1.0 + jnp.exp(-x))


def _resident(shape):
    nd = len(shape)
    return pl.BlockSpec(shape, lambda *_: (0,) * nd, pipeline_mode=pl.Buffered(1))


def _mem_kv_kernel(mem_ref, g_ref, w_ref, mk_ref, mv_ref):
    hn = _rms(mem_ref[0], g_ref[0]).astype(BF16)
    kv = _dot(hn, w_ref[0])
    mk = kv[:, :MEM_W] * (MEM_HEAD_DIM ** -0.5)
    mv = kv[:, MEM_W:]
    lane_head = lax.broadcasted_iota(jnp.int32, (MEM_TOKENS, MEM_W), 1) >> 6
    for h in range(MEM_HEADS):
        sel = lane_head == h
        mk_ref[0, 0, h] = jnp.where(sel, mk, 0.0).astype(BF16)
        mv_ref[0, 0, h] = jnp.where(sel, mv, 0.0).astype(BF16)


def _mem_kv(mem, g_mem, w_mem_kv_bf16):
    B = mem.shape[0]
    out_sds = jax.ShapeDtypeStruct((DEPTH, B, MEM_HEADS, MEM_TOKENS, MEM_W), BF16)
    out_spec = pl.BlockSpec((1, 1, MEM_HEADS, MEM_TOKENS, MEM_W), lambda l, b: (l, b, 0, 0, 0))
    return pl.pallas_call(
        _mem_kv_kernel,
        grid=(DEPTH, B),
        in_specs=[
            pl.BlockSpec((1, MEM_TOKENS, D_MODEL), lambda l, b: (b, 0, 0)),
            pl.BlockSpec((1, 1, D_MODEL), lambda l, b: (l, 0, 0)),
            pl.BlockSpec((1, D_MODEL, 2 * MEM_W), lambda l, b: (l, 0, 0)),
        ],
        out_specs=[out_spec, out_spec],
        out_shape=[out_sds, out_sds],
        compiler_params=pltpu.CompilerParams(dimension_semantics=("arbitrary", "arbitrary")),
        name="mem_kv",
    )(mem, g_mem.reshape(DEPTH, 1, D_MODEL), w_mem_kv_bf16)


def _mem_attention(mq, mk_ref, mv_ref):
    out = None
    for h in range(MEM_HEADS):
        s = _dot_nt(mq, mk_ref[0, 0, h])
        e = jnp.exp(s - jnp.max(s, axis=-1, keepdims=True))
        p = e * (1.0 / jnp.sum(e, axis=-1, keepdims=True))
        o = _dot(p.astype(BF16), mv_ref[0, 0, h])
        out = o if out is None else out + o
    return out


def _proj_a_kernel(x_ref, g_ref, wq_ref, wk_ref, wv_ref, wo_ref, wmq_ref, wg_ref, bg_ref, mk_ref, mv_ref,
                   q_ref, k_ref, v_ref, o_ref, gt_ref, mo_ref):
    hn = _rms(x_ref[...], g_ref[...]).astype(BF16)
    q_ref[...] = _dot(hn, wq_ref[...]).astype(BF16)
    k_ref[...] = (_dot(hn, wk_ref[...]) * (MLSTM_DQK ** -0.5)).astype(BF16)
    v_ref[...] = _dot(hn, wv_ref[...]).astype(BF16)
    o_ref[...] = _dot(hn, wo_ref[...]).astype(BF16)
    gates = _dot(hn, wg_ref[...]) + bg_ref[...]
    gates = GATE_SOFTCAP * jnp.tanh(gates * (1.0 / GATE_SOFTCAP))
    log_sig = jnp.minimum(gates, 0.0) - jnp.log1p(jnp.exp(-jnp.abs(gates)))
    lane = lax.broadcasted_iota(jnp.int32, gates.shape, 1)
    gt_ref[...] = jnp.where(lane < MLSTM_HEADS, gates, log_sig)
    mq = _dot(hn, wmq_ref[...]).astype(BF16)
    mo_ref[...] = _mem_attention(mq, mk_ref, mv_ref).astype(BF16)


def _proj_a(x2d, g_pre, wq, wk, wv, wo, wmq, wg, bg, mkm, mvm, rows_per_batch):
    N = x2d.shape[0]
    tm = ROW_TILE
    steps_per_batch = rows_per_batch // tm
    row = lambda w: pl.BlockSpec((tm, w), lambda i: (i, 0))
    mem_spec = pl.BlockSpec((1, 1, MEM_HEADS, MEM_TOKENS, MEM_W),
                            lambda i: (0, i // steps_per_batch, 0, 0, 0))
    return pl.pallas_call(
        _proj_a_kernel,
        grid=(N // tm,),
        in_specs=[row(D_MODEL), _resident((1, D_MODEL)),
                  _resident(wq.shape), _resident(wk.shape), _resident(wv.shape), _resident(wo.shape),
                  _resident(wmq.shape), _resident(wg.shape), _resident((1, LANES)),
                  mem_spec, mem_spec],
        out_specs=[row(QK_PAD_W), row(QK_PAD_W), row(MLSTM_V_W), row(MLSTM_V_W), row(LANES), row(MEM_W)],
        out_shape=[jax.ShapeDtypeStruct((N, QK_PAD_W), BF16), jax.ShapeDtypeStruct((N, QK_PAD_W), BF16),
                   jax.ShapeDtypeStruct((N, MLSTM_V_W), BF16), jax.ShapeDtypeStruct((N, MLSTM_V_W), BF16),
                   jax.ShapeDtypeStruct((N, LANES), F32), jax.ShapeDtypeStruct((N, MEM_W), BF16)],
        compiler_params=pltpu.CompilerParams(dimension_semantics=("arbitrary",),
                                             vmem_limit_bytes=VMEM_LIMIT),
        name="proj_a",
    )(x2d, g_pre, wq, wk, wv, wo, wmq, wg, bg, mkm, mvm)


def _mlstm_kernel(q_ref, k_ref, v_ref, o_ref, gt_ref, gout_ref, hm_ref, c_scr, m_scr):
    L = q_ref.shape[0]

    @pl.when(pl.program_id(1) == 0)
    def _():
        c_scr[...] = jnp.zeros(c_scr.shape, F32)
        m_scr[...] = jnp.full(m_scr.shape, M_INIT, F32)

    gates = gt_ref[...]
    row = lax.broadcasted_iota(jnp.int32, (L, L), 0)
    col = lax.broadcasted_iota(jnp.int32, (L, L), 1)
    causal = col <= row
    csum = jnp.dot(causal.astype(F32), gates, precision=lax.Precision.HIGHEST,
                   preferred_element_type=F32)
    lane = lax.broadcasted_iota(jnp.int32, (L, LANES), 1)
    y = jnp.where(lane < MLSTM_HEADS, gates, csum)
    yt = y.T
    lane_w = lax.broadcasted_iota(jnp.int32, (L, V_WIN), 1)

    res = []
    for h in range(MLSTM_HEADS):
        ws, off, ones_col = _V_WIN_START[h], _V_WIN_OFF[h], _V_ONES_COL[h]
        ig_col = y[:, h:h + 1]
        b_col = y[:, MLSTM_HEADS + h:MLSTM_HEADS + h + 1]
        ig_row = yt[h:h + 1, :]
        b_row = yt[MLSTM_HEADS + h:MLSTM_HEADS + h + 1, :]
        m_st = m_scr[h][0:1, 0:1]

        dmat = jnp.where(causal, b_col - b_row + ig_row, -jnp.inf)
        inter_log = b_col + m_st
        m_row = jnp.maximum(inter_log, jnp.max(dmat, axis=-1, keepdims=True))
        decay_mat = jnp.exp(dmat - m_row)
        qh = q_ref[:, h * DQK_PAD:(h + 1) * DQK_PAD]
        kh = k_ref[:, h * DQK_PAD:(h + 1) * DQK_PAD]
        scores = (_dot_nt(qh, kh) * decay_mat).astype(BF16)
        v_aug = jnp.where(lane_w == ones_col, jnp.ones((), BF16), v_ref[:, ws:ws + V_WIN])
        c_h = c_scr[h]
        inter_w = jnp.exp(inter_log - m_row)
        num = _dot(scores, v_aug) + inter_w * _dot(qh, c_h.astype(BF16))
        den = num[:, ones_col:ones_col + 1]
        hv = num * (1.0 / jnp.maximum(jnp.abs(den), jnp.exp(-m_row)))
        in_head = (lane_w >= off) & (lane_w < off + MLSTM_DV)
        ms = jnp.sum(jnp.where(in_head, hv * hv, 0.0), axis=-1, keepdims=True) * (1.0 / MLSTM_DV)
        og = _sigmoid(o_ref[:, ws:ws + V_WIN].astype(F32))
        res.append(hv * lax.rsqrt(ms + EPS) * gout_ref[:, ws:ws + V_WIN] * og)

        g_tot = b_col[L - 1:L, :]
        a = g_tot - b_col + ig_col
        m_new = jnp.maximum(g_tot + m_st, jnp.max(a, axis=0, keepdims=True))
        w = jnp.exp(a - m_new)
        kw = (kh.astype(F32) * w).astype(BF16)
        c_scr[h] = jnp.exp(g_tot + m_st - m_new) * c_h + _dot_tn(kw, v_aug)
        m_scr[h] = jnp.broadcast_to(m_new, m_scr.shape[1:])

    low = lax.broadcasted_iota(jnp.int32, (L, LANES), 1) < (MLSTM_DV - LANES)
    r0, r1, r2, r3 = res
    hm_ref[:, 0:128] = r0[:, :LANES].astype(BF16)
    hm_ref[:, 128:256] = jnp.where(low, r0[:, LANES:], r1[:, :LANES]).astype(BF16)
    hm_ref[:, 256:384] = r1[:, LANES:].astype(BF16)
    hm_ref[:, 384:512] = r2[:, :LANES].astype(BF16)
    hm_ref[:, 512:640] = jnp.where(low, r2[:, LANES:], r3[:, :LANES]).astype(BF16)
    hm_ref[:, 640:768] = r3[:, LANES:].astype(BF16)


def _mlstm(q, k, v, o, gt, g_out, B, S):
    L = MLSTM_CHUNK
    nc = S // L
    row = lambda w: pl.BlockSpec((L, w), lambda b, c: (b * nc + c, 0))
    return pl.pallas_call(
        _mlstm_kernel,
        grid=(B, nc),
        in_specs=[row(QK_PAD_W), row(QK_PAD_W), row(MLSTM_V_W), row(MLSTM_V_W), row(LANES),
                  pl.BlockSpec((1, MLSTM_V_W), lambda b, c: (0, 0))],
        out_specs=row(MLSTM_V_W),
        out_shape=jax.ShapeDtypeStruct((B * S, MLSTM_V_W), BF16),
        scratch_shapes=[pltpu.VMEM((MLSTM_HEADS, DQK_PAD, V_WIN), F32),
                        pltpu.VMEM((MLSTM_HEADS, 8, LANES), F32)],
        compiler_params=pltpu.CompilerParams(dimension_semantics=("arbitrary", "arbitrary"),
                                             vmem_limit_bytes=VMEM_LIMIT),
        name="mlstm",
    )(q, k, v, o, gt, g_out)


def _mix_ffn_core(x_ref, main_ref, mo_ref, wom_ref, woe_ref, gpost_ref, gfpre_ref, gfpost_ref,
                  wg_ref, wu_ref, wd_ref, acc_scr):
    mix = _dot(main_ref[...], wom_ref[...]) + _dot(mo_ref[...], woe_ref[...])
    x1 = x_ref[...] + _rms(mix, gpost_ref[...])
    hf = _rms(x1, gfpre_ref[...]).astype(BF16)
    acc_scr[...] = jnp.zeros(acc_scr.shape, F32)

    def body(c, carry):
        g = _dot(hf, wg_ref[c])
        u = _dot(hf, wu_ref[c])
        act = (g * _sigmoid(g) * u).astype(BF16)
        acc_scr[...] += _dot(act, wd_ref[c])
        return carry

    lax.fori_loop(0, N_FF_CHUNKS, body, 0)
    return x1 + _rms(acc_scr[...], gfpost_ref[...])


def _mix_ffn_last_kernel(x_ref, main_ref, mo_ref, wom_ref, woe_ref, gpost_ref, gfpre_ref, gfpost_ref,
                         wg_ref, wu_ref, wd_ref, xo_ref, acc_scr):
    xo_ref[...] = _mix_ffn_core(x_ref, main_ref, mo_ref, wom_ref, woe_ref, gpost_ref, gfpre_ref,
                                gfpost_ref, wg_ref, wu_ref, wd_ref, acc_scr)


def _rope(t1, t2, cos, sin):
    return t1 * cos - t2 * sin, t2 * cos + t1 * sin


def _mix_ffn_first_kernel(x_ref, main_ref, mo_ref, wom_ref, woe_ref, gpost_ref, gfpre_ref, gfpost_ref,
                          wg_ref, wu_ref, wd_ref,
                          pos_ref, inv_ref, gkv_ref, wkv_ref, gpre1_ref, wb_ref, mk_ref, mv_ref,
                          xo_ref, ks_ref, vs_ref, q1_ref, mo1_ref, acc_scr):
    x2 = _mix_ffn_core(x_ref, main_ref, mo_ref, wom_ref, woe_ref, gpost_ref, gfpre_ref, gfpost_ref,
                       wg_ref, wu_ref, wd_ref, acc_scr)
    xo_ref[...] = x2
    xn = x2 * lax.rsqrt(jnp.mean(x2 * x2, axis=-1, keepdims=True) + EPS)
    ang = pos_ref[...].astype(F32) * inv_ref[...]
    cos = jnp.cos(ang)
    sin = jnp.sin(ang)

    kv = _dot((xn * gkv_ref[...]).astype(BF16), wkv_ref[...])
    k1, k2 = _rope(kv[:, 0:LANES], kv[:, LANES:2 * LANES], cos, sin)
    scale = HEAD_DIM ** -0.5
    ks_ref[:, 0:LANES] = (k1 * scale).astype(BF16)
    ks_ref[:, LANES:2 * LANES] = (k2 * scale).astype(BF16)
    vs_ref[...] = kv[:, SWA_KV_W:].astype(BF16)

    pb = _dot((xn * gpre1_ref[...]).astype(BF16), wb_ref[...])
    for gi in range(SWA_GROUPS):
        base = gi * 2 * LANES
        t1, t2 = _rope(pb[:, base:base + LANES], pb[:, base + LANES:base + 2 * LANES], cos, sin)
        q1_ref[:, base:base + LANES] = t1.astype(BF16)
        q1_ref[:, base + LANES:base + 2 * LANES] = t2.astype(BF16)
    mq = pb[:, SWA_Q_W:].astype(BF16)
    mo1_ref[...] = _mem_attention(mq, mk_ref, mv_ref).astype(BF16)


def _mix_ffn(x2d, main, mo, wom, woe, g_post, g_fpre, g_fpost, wg, wu, wd, first=None, rows_per_batch=None):
    N = x2d.shape[0]
    tm = ROW_TILE
    row = lambda w: pl.BlockSpec((tm, w), lambda i: (i, 0))
    in_specs = [row(D_MODEL), row(main.shape[1]), row(MEM_W),
                _resident(wom.shape), _resident(woe.shape),
                _resident((1, D_MODEL)), _resident((1, D_MODEL)), _resident((1, D_MODEL)),
                _resident(wg.shape), _resident(wu.shape), _resident(wd.shape)]
    args = [x2d, main, mo, wom, woe, g_post, g_fpre, g_fpost, wg, wu, wd]
    out_specs = [row(D_MODEL)]
    out_shape = [jax.ShapeDtypeStruct((N, D_MODEL), F32)]
    if first is None:
        body = _mix_ffn_last_kernel
        name = "mix_ffn_last"
    else:
        body = _mix_ffn_first_kernel
        name = "mix_ffn_first"
        pos, inv, g_kv, wkv, g_pre1, wb, mkm, mvm = first
        steps_per_batch = rows_per_batch // tm
        mem_spec = pl.BlockSpec((1, 1, MEM_HEADS, MEM_TOKENS, MEM_W),
                                lambda i: (1, i // steps_per_batch, 0, 0, 0))
        in_specs += [row(1), _resident((1, LANES)), _resident((1, D_MODEL)), _resident(wkv.shape),
                     _resident((1, D_MODEL)), _resident(wb.shape), mem_spec, mem_spec]
        args += [pos, inv, g_kv, wkv, g_pre1, wb, mkm, mvm]
        out_specs += [row(SWA_KV_W), row(SWA_KV_W), row(SWA_Q_W), row(MEM_W)]
        out_shape += [jax.ShapeDtypeStruct((N, SWA_KV_W), BF16), jax.ShapeDtypeStruct((N, SWA_KV_W), BF16),
                      jax.ShapeDtypeStruct((N, SWA_Q_W), BF16), jax.ShapeDtypeStruct((N, MEM_W), BF16)]
    return pl.pallas_call(
        body,
        grid=(N // tm,),
        in_specs=in_specs,
        out_specs=out_specs,
        out_shape=out_shape,
        scratch_shapes=[pltpu.VMEM((tm, D_MODEL), F32)],
        compiler_params=pltpu.CompilerParams(dimension_semantics=("arbitrary",),
                                             vmem_limit_bytes=VMEM_LIMIT),
        name=name,
    )(*args)


def _swa_kernel(sink_ref, q_ref, kp_ref, kc_ref, vp_ref, vc_ref, o_ref):
    W = SWA_WINDOW
    G = SWA_GROUPS
    n = pl.program_id(1)
    q = jnp.concatenate([q_ref[:, gi * 2 * LANES:(gi + 1) * 2 * LANES] for gi in range(G)], axis=0)
    k = jnp.concatenate([kp_ref[...], kc_ref[...]], axis=0)
    v = jnp.concatenate([vp_ref[...], vc_ref[...]], axis=0)
    rows = lax.broadcasted_iota(jnp.int32, (G * W, 2 * W), 0)
    kpos = lax.broadcasted_iota(jnp.int32, (G * W, 2 * W), 1)
    qpos = (rows & (W - 1)) + W
    diff = qpos - kpos
    valid = (diff >= 0) & (diff < W) & ((kpos >= W) | (n > 0))
    grp = lax.broadcasted_iota(jnp.int32, (G * W, 1), 0) >> 7
    k_lane = lax.broadcasted_iota(jnp.int32, (2 * W, SWA_KV_W), 1)
    out = None
    for j in range(SWA_KV_HEADS):
        km = jnp.where(((k_lane & (LANES - 1)) >> 5) == j, k, jnp.zeros((), BF16))
        vm = jnp.where((k_lane >> 6) == j, v, jnp.zeros((), BF16))
        sink = jnp.zeros((G * W, 1), F32)
        for gi in range(G):
            sink = jnp.where(grp == gi, sink_ref[j * G + gi], sink)
        s = jnp.where(valid, _dot_nt(q, km), -jnp.inf)
        m = jnp.maximum(jnp.max(s, axis=-1, keepdims=True), sink)
        e = jnp.exp(s - m)
        p = e * (1.0 / (jnp.sum(e, axis=-1, keepdims=True) + jnp.exp(sink - m)))
        o = _dot(p.astype(BF16), vm)
        out = o if out is None else out + o
    for gi in range(G):
        o_ref[:, gi * 2 * LANES:(gi + 1) * 2 * LANES] = out[gi * W:(gi + 1) * W].astype(BF16)


def _swa(sinks, q1, ks, vs, B, S):
    W = SWA_WINDOW
    nb = S // W
    cur = lambda w: pl.BlockSpec((W, w), lambda b, n: (b * nb + n, 0))
    prev = lambda w: pl.BlockSpec((W, w), lambda b, n: (b * nb + jnp.maximum(n - 1, 0), 0))
    return pl.pallas_call(
        _swa_kernel,
        grid=(B, nb),
        in_specs=[pl.BlockSpec(memory_space=pltpu.SMEM),
                  cur(SWA_Q_W), prev(SWA_KV_W), cur(SWA_KV_W), prev(SWA_KV_W), cur(SWA_KV_W)],
        out_specs=cur(SWA_Q_W),
        out_shape=jax.ShapeDtypeStruct((B * S, SWA_Q_W), BF16),
        compiler_params=pltpu.CompilerParams(dimension_semantics=("arbitrary", "arbitrary"),
                                             vmem_limit_bytes=VMEM_LIMIT),
        name="swa",
    )(sinks, q1, ks, ks, vs, vs)


def _pad_heads(w, heads, width, padded):
    w = w.reshape(w.shape[0], heads, width)
    return jnp.pad(w, ((0, 0), (0, 0), (0, padded - width))).reshape(w.shape[0], heads * padded)


def _swa_q_perm():
    idx = np.empty((SWA_GROUPS, 2, SWA_KV_HEADS, HEAD_DIM // 2), np.int32)
    for gi in range(SWA_GROUPS):
        for half in range(2):
            for j in range(SWA_KV_HEADS):
                idx[gi, half, j] = (j * SWA_GROUPS + gi) * HEAD_DIM + half * (HEAD_DIM // 2) + np.arange(HEAD_DIM // 2)
    return idx.reshape(-1)


def _swa_k_perm():
    idx = np.empty((2, SWA_KV_HEADS, HEAD_DIM // 2), np.int32)
    for half in range(2):
        for j in range(SWA_KV_HEADS):
            idx[half, j] = j * HEAD_DIM + half * (HEAD_DIM // 2) + np.arange(HEAD_DIM // 2)
    return idx.reshape(-1)


def _swa_out_perm():
    idx = np.empty((SWA_GROUPS, SWA_KV_HEADS, HEAD_DIM), np.int32)
    for gi in range(SWA_GROUPS):
        for j in range(SWA_KV_HEADS):
            idx[gi, j] = (j * SWA_GROUPS + gi) * HEAD_DIM + np.arange(HEAD_DIM)
    return idx.reshape(-1)


def _ffn_weights(w_in, w_out):
    wg = w_in[:, :D_FF].reshape(D_MODEL, N_FF_CHUNKS, FF_CHUNK).transpose(1, 0, 2).astype(BF16)
    wu = w_in[:, D_FF:].reshape(D_MODEL, N_FF_CHUNKS, FF_CHUNK).transpose(1, 0, 2).astype(BF16)
    wd = w_out.reshape(N_FF_CHUNKS, FF_CHUNK, D_MODEL).astype(BF16)
    return wg, wu, wd


def kernel(x, mem, positions, g_mix_pre, g_mix_post, g_ffn_pre, g_ffn_post, g_mem, w_mem_kv, w_out,
           w_ffn_in, w_ffn_out, w_in_a, b_gates_a, g_mlstm_out, g_kv, w_kv, w_in_b, sinks_b):
    B, S, _ = x.shape
    N = B * S
    assert S % ROW_TILE == 0 and S % MLSTM_CHUNK == 0 and S % SWA_WINDOW == 0
    x2d = x.reshape(N, D_MODEL)
    vec = lambda g: g.reshape(1, -1).astype(F32)

    mkm, mvm = _mem_kv(mem, g_mem, w_mem_kv.astype(BF16))

    wa = w_in_a[0]
    c0 = 0
    wq = _pad_heads(wa[:, c0:c0 + MLSTM_QK_W], MLSTM_HEADS, MLSTM_DQK, DQK_PAD).astype(BF16)
    c0 += MLSTM_QK_W
    wk = _pad_heads(wa[:, c0:c0 + MLSTM_QK_W], MLSTM_HEADS, MLSTM_DQK, DQK_PAD).astype(BF16)
    c0 += MLSTM_QK_W
    wv = wa[:, c0:c0 + MLSTM_V_W].astype(BF16)
    c0 += MLSTM_V_W
    wo = wa[:, c0:c0 + MLSTM_V_W].astype(BF16)
    c0 += MLSTM_V_W
    wgt = jnp.pad(wa[:, c0:c0 + 2 * MLSTM_HEADS], ((0, 0), (0, LANES - 2 * MLSTM_HEADS))).astype(BF16)
    c0 += 2 * MLSTM_HEADS
    wmq = wa[:, c0:c0 + MEM_W].astype(BF16)
    bg = jnp.pad(b_gates_a[0].astype(F32), (0, LANES - 2 * MLSTM_HEADS)).reshape(1, LANES)

    q, k, v, o, gt, mo0 = _proj_a(x2d, vec(g_mix_pre[0]), wq, wk, wv, wo, wmq, wgt, bg, mkm, mvm, S)
    hm = _mlstm(q, k, v, o, gt, vec(g_mlstm_out[0]), B, S)

    wg0, wu0, wd0 = _ffn_weights(w_ffn_in[0], w_ffn_out[0])
    wo0 = w_out[0].astype(BF16)
    inv = 1.0 / (ROPE_THETA ** (jnp.arange(0, HEAD_DIM, 2, dtype=F32) / HEAD_DIM))
    inv = jnp.tile(inv, SWA_KV_HEADS).reshape(1, LANES)
    wkv = jnp.concatenate([w_kv[:, :SWA_KV_W][:, _swa_k_perm()], w_kv[:, SWA_KV_W:]], axis=1).astype(BF16)
    wb = jnp.concatenate([w_in_b[0][:, :SWA_Q_W][:, _swa_q_perm()], w_in_b[0][:, SWA_Q_W:]], axis=1).astype(BF16)
    first = (positions.reshape(N, 1), inv, vec(g_kv), wkv, vec(g_mix_pre[1]), wb, mkm, mvm)
    x1, ks, vs, q1, mo1 = _mix_ffn(x2d, hm, mo0, wo0[:MLSTM_V_W], wo0[MLSTM_V_W:],
                                   vec(g_mix_post[0]), vec(g_ffn_pre[0]), vec(g_ffn_post[0]),
                                   wg0, wu0, wd0, first=first, rows_per_batch=S)

    attn = _swa(sinks_b[0].astype(F32), q1, ks, vs, B, S)
    wg1, wu1, wd1 = _ffn_weights(w_ffn_in[1], w_ffn_out[1])
    wo1 = w_out[1]
    (xo,) = _mix_ffn(x1, attn, mo1, wo1[:SWA_Q_W][_swa_out_perm()].astype(BF16), wo1[SWA_Q_W:].astype(BF16),
                     vec(g_mix_post[1]), vec(g_ffn_pre[1]), vec(g_ffn_post[1]), wg1, wu1, wd1)
    return xo.reshape(B, S, D_MODEL)
```

```python
import functools

import numpy as np
import jax
import jax.numpy as jnp
from jax import lax
from jax.experimental import pallas as pl
from jax.experimental.pallas import tpu as pltpu

F32 = jnp.float32
BF16 = jnp.bfloat16

D_MODEL = 1024
DEPTH = 2
HEAD_DIM = 64
EPS = 1e-6
ROPE_THETA = 10000.0

MLSTM_HEADS = 4
MLSTM_DV = 192
MLSTM_DQK = 96
MLSTM_QK_W = MLSTM_HEADS * MLSTM_DQK
MLSTM_V_W = MLSTM_HEADS * MLSTM_DV
GATE_SOFTCAP = 15.0
M_INIT = -1e30

SWA_Q_HEADS = 12
SWA_KV_HEADS = 4
SWA_GROUPS = SWA_Q_HEADS // SWA_KV_HEADS
SWA_Q_W = SWA_Q_HEADS * HEAD_DIM
SWA_KV_W = SWA_KV_HEADS * HEAD_DIM
SWA_WINDOW = 128

MEM_TOKENS = 256
MEM_HEADS = 4
MEM_HEAD_DIM = 64
MEM_W = MEM_HEADS * MEM_HEAD_DIM

D_FF = 2816

LANES = 128
MXU_TILE = 256

DQK_PAD = LANES
QK_PAD_W = MLSTM_HEADS * DQK_PAD
V_WIN = MXU_TILE
MLSTM_CHUNK = 256
FF_CHUNK = MXU_TILE
N_FF_CHUNKS = D_FF // FF_CHUNK
ROW_TILE = 512
VMEM_LIMIT = 56 * 1024 * 1024

_V_WIN_START = (0, 128, 384, 512)
_V_WIN_OFF = (0, 64, 0, 64)
_V_ONES_COL = (192, 0, 192, 0)


def _rms(x, g):
    return x * lax.rsqrt(jnp.mean(x * x, axis=-1, keepdims=True) + EPS) * g


def _dot(a, b):
    return jnp.dot(a, b, preferred_element_type=F32)


def _dot_nt(a, b):
    return lax.dot_general(a, b, (((1,), (1,)), ((), ())), preferred_element_type=F32)


def _dot_tn(a, b):
    return lax.dot_general(a, b, (((0,), (0,)), ((), ())), preferred_element_type=F32)


def _sigmoid(x):
    return 1.0 / (1.0 + jnp.exp(-x))


def _resident(shape):
    nd = len(shape)
    return pl.BlockSpec(shape, lambda *_: (0,) * nd, pipeline_mode=pl.Buffered(1))


def _mem_kv_kernel(mem_ref, g_ref, w_ref, mk_ref, mv_ref):
    hn = _rms(mem_ref[0], g_ref[0]).astype(BF16)
    kv = _dot(hn, w_ref[0])
    mk = kv[:, :MEM_W] * (MEM_HEAD_DIM ** -0.5)
    mv = kv[:, MEM_W:]
    lane_head = lax.broadcasted_iota(jnp.int32, (MEM_TOKENS, MEM_W), 1) >> 6
    for h in range(MEM_HEADS):
        sel = lane_head == h
        mk_ref[0, 0, h] = jnp.where(sel, mk, 0.0).astype(BF16)
        mv_ref[0, 0, h] = jnp.where(sel, mv, 0.0).astype(BF16)


def _mem_kv(mem, g_mem, w_mem_kv_bf16):
    B = mem.shape[0]
    out_sds = jax.ShapeDtypeStruct((DEPTH, B, MEM_HEADS, MEM_TOKENS, MEM_W), BF16)
    out_spec = pl.BlockSpec((1, 1, MEM_HEADS, MEM_TOKENS, MEM_W), lambda l, b: (l, b, 0, 0, 0))
    return pl.pallas_call(
        _mem_kv_kernel,
        grid=(DEPTH, B),
        in_specs=[
            pl.BlockSpec((1, MEM_TOKENS, D_MODEL), lambda l, b: (b, 0, 0)),
            pl.BlockSpec((1, 1, D_MODEL), lambda l, b: (l, 0, 0)),
            pl.BlockSpec((1, D_MODEL, 2 * MEM_W), lambda l, b: (l, 0, 0)),
        ],
        out_specs=[out_spec, out_spec],
        out_shape=[out_sds, out_sds],
        compiler_params=pltpu.CompilerParams(dimension_semantics=("arbitrary", "arbitrary")),
        name="mem_kv",
    )(mem, g_mem.reshape(DEPTH, 1, D_MODEL), w_mem_kv_bf16)


def _mem_attention(mq, mk_ref, mv_ref):
    out = None
    for h in range(MEM_HEADS):
        s = _dot_nt(mq, mk_ref[0, 0, h])
        e = jnp.exp(s - jnp.max(s, axis=-1, keepdims=True))
        p = e * (1.0 / jnp.sum(e, axis=-1, keepdims=True))
        o = _dot(p.astype(BF16), mv_ref[0, 0, h])
        out = o if out is None else out + o
    return out


def _proj_a_kernel(x_ref, g_ref, wq_ref, wk_ref, wv_ref, wo_ref, wmq_ref, wg_ref, bg_ref, mk_ref, mv_ref,
                   q_ref, k_ref, v_ref, o_ref, gt_ref, mo_ref):
    hn = _rms(x_ref[...], g_ref[...]).astype(BF16)
    q_ref[...] = _dot(hn, wq_ref[...]).astype(BF16)
    k_ref[...] = (_dot(hn, wk_ref[...]) * (MLSTM_DQK ** -0.5)).astype(BF16)
    v_ref[...] = _dot(hn, wv_ref[...]).astype(BF16)
    o_ref[...] = _dot(hn, wo_ref[...]).astype(BF16)
    gates = _dot(hn, wg_ref[...]) + bg_ref[...]
    gates = GATE_SOFTCAP * jnp.tanh(gates * (1.0 / GATE_SOFTCAP))
    log_sig = jnp.minimum(gates, 0.0) - jnp.log1p(jnp.exp(-jnp.abs(gates)))
    lane = lax.broadcasted_iota(jnp.int32, gates.shape, 1)
    gt_ref[...] = jnp.where(lane < MLSTM_HEADS, gates, log_sig)
    mq = _dot(hn, wmq_ref[...]).astype(BF16)
    mo_ref[...] = _mem_attention(mq, mk_ref, mv_ref).astype(BF16)


def _proj_a(x2d, g_pre, wq, wk, wv, wo, wmq, wg, bg, mkm, mvm, rows_per_batch):
    N = x2d.shape[0]
    tm = ROW_TILE
    steps_per_batch = rows_per_batch // tm
    row = lambda w: pl.BlockSpec((tm, w), lambda i: (i, 0))
    mem_spec = pl.BlockSpec((1, 1, MEM_HEADS, MEM_TOKENS, MEM_W),
                            lambda i: (0, i // steps_per_batch, 0, 0, 0))
    return pl.pallas_call(
        _proj_a_kernel,
        grid=(N // tm,),
        in_specs=[row(D_MODEL), _resident((1, D_MODEL)),
                  _resident(wq.shape), _resident(wk.shape), _resident(wv.shape), _resident(wo.shape),
                  _resident(wmq.shape), _resident(wg.shape), _resident((1, LANES)),
                  mem_spec, mem_spec],
        out_specs=[row(QK_PAD_W), row(QK_PAD_W), row(MLSTM_V_W), row(MLSTM_V_W), row(LANES), row(MEM_W)],
        out_shape=[jax.ShapeDtypeStruct((N, QK_PAD_W), BF16), jax.ShapeDtypeStruct((N, QK_PAD_W), BF16),
                   jax.ShapeDtypeStruct((N, MLSTM_V_W), BF16), jax.ShapeDtypeStruct((N, MLSTM_V_W), BF16),
                   jax.ShapeDtypeStruct((N, LANES), F32), jax.ShapeDtypeStruct((N, MEM_W), BF16)],
        compiler_params=pltpu.CompilerParams(dimension_semantics=("arbitrary",),
                                             vmem_limit_bytes=VMEM_LIMIT),
        name="proj_a",
    )(x2d, g_pre, wq, wk, wv, wo, wmq, wg, bg, mkm, mvm)


def _mlstm_kernel(q_ref, k_ref, v_ref, o_ref, gt_ref, gout_ref, hm_ref, c_scr, m_scr):
    L = q_ref.shape[0]

    @pl.when(pl.program_id(1) == 0)
    def _():
        c_scr[...] = jnp.zeros(c_scr.shape, F32)
        m_scr[...] = jnp.full(m_scr.shape, M_INIT, F32)

    gates = gt_ref[...]
    row = lax.broadcasted_iota(jnp.int32, (L, L), 0)
    col = lax.broadcasted_iota(jnp.int32, (L, L), 1)
    causal = col <= row
    csum = jnp.dot(causal.astype(F32), gates, precision=lax.Precision.HIGHEST,
                   preferred_element_type=F32)
    lane = lax.broadcasted_iota(jnp.int32, (L, LANES), 1)
    y = jnp.where(lane < MLSTM_HEADS, gates, csum)
    yt = y.T
    lane_w = lax.broadcasted_iota(jnp.int32, (L, V_WIN), 1)

    res = []
    for h in range(MLSTM_HEADS):
        ws, off, ones_col = _V_WIN_START[h], _V_WIN_OFF[h], _V_ONES_COL[h]
        ig_col = y[:, h:h + 1]
        b_col = y[:, MLSTM_HEADS + h:MLSTM_HEADS + h + 1]
        ig_row = yt[h:h + 1, :]
        b_row = yt[MLSTM_HEADS + h:MLSTM_HEADS + h + 1, :]
        m_st = m_scr[h][0:1, 0:1]

        dmat = jnp.where(causal, b_col - b_row + ig_row, -jnp.inf)
        inter_log = b_col + m_st
        m_row = jnp.maximum(inter_log, jnp.max(dmat, axis=-1, keepdims=True))
        decay_mat = jnp.exp(dmat - m_row)
        qh = q_ref[:, h * DQK_PAD:(h + 1) * DQK_PAD]
        kh = k_ref[:, h * DQK_PAD:(h + 1) * DQK_PAD]
        scores = (_dot_nt(qh, kh) * decay_mat).astype(BF16)
        v_aug = jnp.where(lane_w == ones_col, jnp.ones((), BF16), v_ref[:, ws:ws + V_WIN])
        c_h = c_scr[h]
        inter_w = jnp.exp(inter_log - m_row)
        num = _dot(scores, v_aug) + inter_w * _dot(qh, c_h.astype(BF16))
        den = num[:, ones_col:ones_col + 1]
        hv = num * (1.0 / jnp.maximum(jnp.abs(den), jnp.exp(-m_row)))
        in_head = (lane_w >= off) & (lane_w < off + MLSTM_DV)
        ms = jnp.sum(jnp.where(in_head, hv * hv, 0.0), axis=-1, keepdims=True) * (1.0 / MLSTM_DV)
        og = _sigmoid(o_ref[:, ws:ws + V_WIN].astype(F32))
        res.append(hv * lax.rsqrt(ms + EPS) * gout_ref[:, ws:ws + V_WIN] * og)

        g_tot = b_col[L - 1:L, :]
        a = g_tot - b_col + ig_col
        m_new = jnp.maximum(g_tot + m_st, jnp.max(a, axis=0, keepdims=True))
        w = jnp.exp(a - m_new)
        kw = (kh.astype(F32) * w).astype(BF16)
        c_scr[h] = jnp.exp(g_tot + m_st - m_new) * c_h + _dot_tn(kw, v_aug)
        m_scr[h] = jnp.broadcast_to(m_new, m_scr.shape[1:])

    low = lax.broadcasted_iota(jnp.int32, (L, LANES), 1) < (MLSTM_DV - LANES)
    r0, r1, r2, r3 = res
    hm_ref[:, 0:128] = r0[:, :LANES].astype(BF16)
    hm_ref[:, 128:256] = jnp.where(low, r0[:, LANES:], r1[:, :LANES]).astype(BF16)
    hm_ref[:, 256:384] = r1[:, LANES:].astype(BF16)
    hm_ref[:, 384:512] = r2[:, :LANES].astype(BF16)
    hm_ref[:, 512:640] = jnp.where(low, r2[:, LANES:], r3[:, :LANES]).astype(BF16)
    hm_ref[:, 640:768] = r3[:, LANES:].astype(BF16)


def _mlstm(q, k, v, o, gt, g_out, B, S):
    L = MLSTM_CHUNK
    nc = S // L
    row = lambda w: pl.BlockSpec((L, w), lambda b, c: (b * nc + c, 0))
    return pl.pallas_call(
        _mlstm_kernel,
        grid=(B, nc),
        in_specs=[row(QK_PAD_W), row(QK_PAD_W), row(MLSTM_V_W), row(MLSTM_V_W), row(LANES),
                  pl.BlockSpec((1, MLSTM_V_W), lambda b, c: (0, 0))],
        out_specs=row(MLSTM_V_W),
        out_shape=jax.ShapeDtypeStruct((B * S, MLSTM_V_W), BF16),
        scratch_shapes=[pltpu.VMEM((MLSTM_HEADS, DQK_PAD, V_WIN), F32),
                        pltpu.VMEM((MLSTM_HEADS, 8, LANES), F32)],
        compiler_params=pltpu.CompilerParams(dimension_semantics=("arbitrary", "arbitrary"),
                                             vmem_limit_bytes=VMEM_LIMIT),
        name="mlstm",
    )(q, k, v, o, gt, g_out)


def _mix_ffn_core(x_ref, main_ref, mo_ref, wom_ref, woe_ref, gpost_ref, gfpre_ref, gfpost_ref,
                  wi_ref, wd_ref):
    mix = _dot(main_ref[...], wom_ref[...]) + _dot(mo_ref[...], woe_ref[...])
    x1 = x_ref[...] + _rms(mix, gpost_ref[...])
    hf = _rms(x1, gfpre_ref[...]).astype(BF16)
    acc = None
    for c in range(N_FF_CHUNKS):
        lo = c * FF_CHUNK
        g = _dot(hf, wi_ref[:, lo:lo + FF_CHUNK])
        u = _dot(hf, wi_ref[:, D_FF + lo:D_FF + lo + FF_CHUNK])
        act = (g * _sigmoid(g) * u).astype(BF16)
        d = _dot(act, wd_ref[lo:lo + FF_CHUNK, :])
        acc = d if acc is None else acc + d
    return x1 + _rms(acc, gfpost_ref[...])


def _mix_ffn_last_kernel(x_ref, main_ref, mo_ref, wom_ref, woe_ref, gpost_ref, gfpre_ref, gfpost_ref,
                         wi_ref, wd_ref, xo_ref):
    xo_ref[...] = _mix_ffn_core(x_ref, main_ref, mo_ref, wom_ref, woe_ref, gpost_ref, gfpre_ref,
                                gfpost_ref, wi_ref, wd_ref)


def _rope(t1, t2, cos, sin):
    return t1 * cos - t2 * sin, t2 * cos + t1 * sin


def _mix_ffn_first_kernel(x_ref, main_ref, mo_ref, wom_ref, woe_ref, gpost_ref, gfpre_ref, gfpost_ref,
                          wi_ref, wd_ref,
                          pos_ref, inv_ref, gkv_ref, wkv_ref, gpre1_ref, wb_ref, mk_ref, mv_ref,
                          xo_ref, ks_ref, vs_ref, q1_ref, mo1_ref):
    x2 = _mix_ffn_core(x_ref, main_ref, mo_ref, wom_ref, woe_ref, gpost_ref, gfpre_ref, gfpost_ref,
                       wi_ref, wd_ref)
    xo_ref[...] = x2
    xn = x2 * lax.rsqrt(jnp.mean(x2 * x2, axis=-1, keepdims=True) + EPS)
    ang = pos_ref[...].astype(F32) * inv_ref[...]
    cos = jnp.cos(ang)
    sin = jnp.sin(ang)

    kv = _dot((xn * gkv_ref[...]).astype(BF16), wkv_ref[...])
    k1, k2 = _rope(kv[:, 0:LANES], kv[:, LANES:2 * LANES], cos, sin)
    scale = HEAD_DIM ** -0.5
    ks_ref[:, 0:LANES] = (k1 * scale).astype(BF16)
    ks_ref[:, LANES:2 * LANES] = (k2 * scale).astype(BF16)
    vs_ref[...] = kv[:, SWA_KV_W:].astype(BF16)

    pb = _dot((xn * gpre1_ref[...]).astype(BF16), wb_ref[...])
    for gi in range(SWA_GROUPS):
        base = gi * 2 * LANES
        t1, t2 = _rope(pb[:, base:base + LANES], pb[:, base + LANES:base + 2 * LANES], cos, sin)
        q1_ref[:, base:base + LANES] = t1.astype(BF16)
        q1_ref[:, base + LANES:base + 2 * LANES] = t2.astype(BF16)
    mq = pb[:, SWA_Q_W:].astype(BF16)
    mo1_ref[...] = _mem_attention(mq, mk_ref, mv_ref).astype(BF16)


def _mix_ffn(layer, x2d, main, mo, wom, woe, g_post, g_fpre, g_fpost, wi, wd, first=None, rows_per_batch=None):
    N = x2d.shape[0]
    tm = ROW_TILE
    row = lambda w: pl.BlockSpec((tm, w), lambda i: (i, 0))
    slab = lambda w: pl.BlockSpec((None,) + w.shape[1:], lambda i: (layer, 0, 0), pipeline_mode=pl.Buffered(1))
    in_specs = [row(D_MODEL), row(main.shape[1]), row(MEM_W),
                _resident(wom.shape), _resident(woe.shape),
                _resident((1, D_MODEL)), _resident((1, D_MODEL)), _resident((1, D_MODEL)),
                slab(wi), slab(wd)]
    args = [x2d, main, mo, wom, woe, g_post, g_fpre, g_fpost, wi, wd]
    out_specs = [row(D_MODEL)]
    out_shape = [jax.ShapeDtypeStruct((N, D_MODEL), F32)]
    if first is None:
        body = _mix_ffn_last_kernel
        name = "mix_ffn_last"
    else:
        body = _mix_ffn_first_kernel
        name = "mix_ffn_first"
        pos, inv, g_kv, wkv, g_pre1, wb, mkm, mvm = first
        steps_per_batch = rows_per_batch // tm
        mem_spec = pl.BlockSpec((1, 1, MEM_HEADS, MEM_TOKENS, MEM_W),
                                lambda i: (1, i // steps_per_batch, 0, 0, 0))
        in_specs += [row(1), _resident((1, LANES)), _resident((1, D_MODEL)), _resident(wkv.shape),
                     _resident((1, D_MODEL)), _resident(wb.shape), mem_spec, mem_spec]
        args += [pos, inv, g_kv, wkv, g_pre1, wb, mkm, mvm]
        out_specs += [row(SWA_KV_W), row(SWA_KV_W), row(SWA_Q_W), row(MEM_W)]
        out_shape += [jax.ShapeDtypeStruct((N, SWA_KV_W), BF16), jax.ShapeDtypeStruct((N, SWA_KV_W), BF16),
                      jax.ShapeDtypeStruct((N, SWA_Q_W), BF16), jax.ShapeDtypeStruct((N, MEM_W), BF16)]
    return pl.pallas_call(
        body,
        grid=(N // tm,),
        in_specs=in_specs,
        out_specs=out_specs,
        out_shape=out_shape,
        compiler_params=pltpu.CompilerParams(dimension_semantics=("arbitrary",),
                                             vmem_limit_bytes=VMEM_LIMIT),
        name=name,
    )(*args)


def _swa_kernel(sink_ref, q_ref, kp_ref, kc_ref, vp_ref, vc_ref, o_ref):
    W = SWA_WINDOW
    G = SWA_GROUPS
    n = pl.program_id(1)
    q = jnp.concatenate([q_ref[:, gi * 2 * LANES:(gi + 1) * 2 * LANES] for gi in range(G)], axis=0)
    k = jnp.concatenate([kp_ref[...], kc_ref[...]], axis=0)
    v = jnp.concatenate([vp_ref[...], vc_ref[...]], axis=0)
    rows = lax.broadcasted_iota(jnp.int32, (G * W, 2 * W), 0)
    kpos = lax.broadcasted_iota(jnp.int32, (G * W, 2 * W), 1)
    qpos = (rows & (W - 1)) + W
    diff = qpos - kpos
    valid = (diff >= 0) & (diff < W) & ((kpos >= W) | (n > 0))
    grp = lax.broadcasted_iota(jnp.int32, (G * W, 1), 0) >> 7
    k_lane = lax.broadcasted_iota(jnp.int32, (2 * W, SWA_KV_W), 1)
    out = None
    for j in range(SWA_KV_HEADS):
        km = jnp.where(((k_lane & (LANES - 1)) >> 5) == j, k, jnp.zeros((), BF16))
        vm = jnp.where((k_lane >> 6) == j, v, jnp.zeros((), BF16))
        sink = jnp.zeros((G * W, 1), F32)
        for gi in range(G):
            sink = jnp.where(grp == gi, sink_ref[j * G + gi], sink)
        s = jnp.where(valid, _dot_nt(q, km), -jnp.inf)
        m = jnp.maximum(jnp.max(s, axis=-1, keepdims=True), sink)
        e = jnp.exp(s - m)
        p = e * (1.0 / (jnp.sum(e, axis=-1, keepdims=True) + jnp.exp(sink - m)))
        o = _dot(p.astype(BF16), vm)
        out = o if out is None else out + o
    for gi in range(G):
        o_ref[:, gi * 2 * LANES:(gi + 1) * 2 * LANES] = out[gi * W:(gi + 1) * W].astype(BF16)


def _swa(sinks, q1, ks, vs, B, S):
    W = SWA_WINDOW
    nb = S // W
    cur = lambda w: pl.BlockSpec((W, w), lambda b, n: (b * nb + n, 0))
    prev = lambda w: pl.BlockSpec((W, w), lambda b, n: (b * nb + jnp.maximum(n - 1, 0), 0))
    return pl.pallas_call(
        _swa_kernel,
        grid=(B, nb),
        in_specs=[pl.BlockSpec(memory_space=pltpu.SMEM),
                  cur(SWA_Q_W), prev(SWA_KV_W), cur(SWA_KV_W), prev(SWA_KV_W), cur(SWA_KV_W)],
        out_specs=cur(SWA_Q_W),
        out_shape=jax.ShapeDtypeStruct((B * S, SWA_Q_W), BF16),
        compiler_params=pltpu.CompilerParams(dimension_semantics=("arbitrary", "arbitrary"),
                                             vmem_limit_bytes=VMEM_LIMIT),
        name="swa",
    )(sinks, q1, ks, ks, vs, vs)


def _pad_heads(w, heads, width, padded):
    w = w.reshape(w.shape[0], heads, width)
    return jnp.pad(w, ((0, 0), (0, 0), (0, padded - width))).reshape(w.shape[0], heads * padded)


def _swa_q_perm():
    idx = np.empty((SWA_GROUPS, 2, SWA_KV_HEADS, HEAD_DIM // 2), np.int32)
    for gi in range(SWA_GROUPS):
        for half in range(2):
            for j in range(SWA_KV_HEADS):
                idx[gi, half, j] = (j * SWA_GROUPS + gi) * HEAD_DIM + half * (HEAD_DIM // 2) + np.arange(HEAD_DIM // 2)
    return idx.reshape(-1)


def _swa_k_perm():
    idx = np.empty((2, SWA_KV_HEADS, HEAD_DIM // 2), np.int32)
    for half in range(2):
        for j in range(SWA_KV_HEADS):
            idx[half, j] = j * HEAD_DIM + half * (HEAD_DIM // 2) + np.arange(HEAD_DIM // 2)
    return idx.reshape(-1)


def _swa_out_perm():
    idx = np.empty((SWA_GROUPS, SWA_KV_HEADS, HEAD_DIM), np.int32)
    for gi in range(SWA_GROUPS):
        for j in range(SWA_KV_HEADS):
            idx[gi, j] = (j * SWA_GROUPS + gi) * HEAD_DIM + np.arange(HEAD_DIM)
    return idx.reshape(-1)


def kernel(x, mem, positions, g_mix_pre, g_mix_post, g_ffn_pre, g_ffn_post, g_mem, w_mem_kv, w_out,
           w_ffn_in, w_ffn_out, w_in_a, b_gates_a, g_mlstm_out, g_kv, w_kv, w_in_b, sinks_b):
    B, S, _ = x.shape
    N = B * S
    assert S % ROW_TILE == 0 and S % MLSTM_CHUNK == 0 and S % SWA_WINDOW == 0
    x2d = x.reshape(N, D_MODEL)
    vec = lambda g: g.reshape(1, -1).astype(F32)

    mkm, mvm = _mem_kv(mem, g_mem, w_mem_kv.astype(BF16))

    wa = w_in_a[0]
    c0 = 0
    wq = _pad_heads(wa[:, c0:c0 + MLSTM_QK_W], MLSTM_HEADS, MLSTM_DQK, DQK_PAD).astype(BF16)
    c0 += MLSTM_QK_W
    wk = _pad_heads(wa[:, c0:c0 + MLSTM_QK_W], MLSTM_HEADS, MLSTM_DQK, DQK_PAD).astype(BF16)
    c0 += MLSTM_QK_W
    wv = wa[:, c0:c0 + MLSTM_V_W].astype(BF16)
    c0 += MLSTM_V_W
    wo = wa[:, c0:c0 + MLSTM_V_W].astype(BF16)
    c0 += MLSTM_V_W
    wgt = jnp.pad(wa[:, c0:c0 + 2 * MLSTM_HEADS], ((0, 0), (0, LANES - 2 * MLSTM_HEADS))).astype(BF16)
    c0 += 2 * MLSTM_HEADS
    wmq = wa[:, c0:c0 + MEM_W].astype(BF16)
    bg = jnp.pad(b_gates_a[0].astype(F32), (0, LANES - 2 * MLSTM_HEADS)).reshape(1, LANES)

    q, k, v, o, gt, mo0 = _proj_a(x2d, vec(g_mix_pre[0]), wq, wk, wv, wo, wmq, wgt, bg, mkm, mvm, S)
    hm = _mlstm(q, k, v, o, gt, vec(g_mlstm_out[0]), B, S)

    w_ffn_in_bf = w_ffn_in.astype(BF16)
    w_ffn_out_bf = w_ffn_out.astype(BF16)
    wo0 = w_out[0].astype(BF16)
    inv = 1.0 / (ROPE_THETA ** (jnp.arange(0, HEAD_DIM, 2, dtype=F32) / HEAD_DIM))
    inv = jnp.tile(inv, SWA_KV_HEADS).reshape(1, LANES)
    wkv = jnp.concatenate([w_kv[:, :SWA_KV_W][:, _swa_k_perm()], w_kv[:, SWA_KV_W:]], axis=1).astype(BF16)
    wb = jnp.concatenate([w_in_b[0][:, :SWA_Q_W][:, _swa_q_perm()], w_in_b[0][:, SWA_Q_W:]], axis=1).astype(BF16)
    first = (positions.reshape(N, 1), inv, vec(g_kv), wkv, vec(g_mix_pre[1]), wb, mkm, mvm)
    x1, ks, vs, q1, mo1 = _mix_ffn(0, x2d, hm, mo0, wo0[:MLSTM_V_W], wo0[MLSTM_V_W:],
                                   vec(g_mix_post[0]), vec(g_ffn_pre[0]), vec(g_ffn_post[0]),
                                   w_ffn_in_bf, w_ffn_out_bf, first=first, rows_per_batch=S)

    attn = _swa(sinks_b[0].astype(F32), q1, ks, vs, B, S)
    wo1 = w_out[1]
    (xo,) = _mix_ffn(1, x1, attn, mo1, wo1[:SWA_Q_W][_swa_out_perm()].astype(BF16), wo1[SWA_Q_W:].astype(BF16),
                     vec(g_mix_post[1]), vec(g_ffn_pre[1]), vec(g_ffn_post[1]),
                     w_ffn_in_bf, w_ffn_out_bf)
    return xo.reshape(B, S, D_MODEL)
```

```python
import numpy as np
import jax
import jax.numpy as jnp
from jax import lax
from jax.experimental import pallas as pl
from jax.experimental.pallas import tpu as pltpu

F32 = jnp.float32
BF16 = jnp.bfloat16

D_MODEL = 1024
DEPTH = 2
HEAD_DIM = 64
EPS = 1e-6
ROPE_THETA = 10000.0
LOG2E = 1.4426950408889634

MLSTM_HEADS = 4
MLSTM_DV = 192
MLSTM_DQK = 96
MLSTM_QK_W = MLSTM_HEADS * MLSTM_DQK
MLSTM_V_W = MLSTM_HEADS * MLSTM_DV
GATE_SOFTCAP = 15.0
M_INIT = -1e30

SWA_Q_HEADS = 12
SWA_KV_HEADS = 4
SWA_GROUPS = SWA_Q_HEADS // SWA_KV_HEADS
SWA_Q_W = SWA_Q_HEADS * HEAD_DIM
SWA_KV_W = SWA_KV_HEADS * HEAD_DIM
SWA_WINDOW = 128

MEM_TOKENS = 256
MEM_HEADS = 4
MEM_HEAD_DIM = 64
MEM_W = MEM_HEADS * MEM_HEAD_DIM

D_FF = 2816

LANES = 128
MXU_TILE = 256

DQK_PAD = LANES
QK_PAD_W = MLSTM_HEADS * DQK_PAD
V_WIN = MXU_TILE
MLSTM_CHUNK = 256
FF_CHUNK = MXU_TILE
N_FF_CHUNKS = D_FF // FF_CHUNK
ROW_TILE = 512
MLSTM_BATCH_PER_STEP = 1
SWA_BLOCKS_PER_STEP = 2
VMEM_LIMIT = 56 * 1024 * 1024

_V_WIN_START = (0, 128, 384, 512)
_V_WIN_OFF = (0, 64, 0, 64)
_V_ONES_COL = (192, 0, 192, 0)


def _rms(x, g):
    return x * lax.rsqrt(jnp.mean(x * x, axis=-1, keepdims=True) + EPS) * g


def _dot(a, b):
    return jnp.dot(a, b, preferred_element_type=F32)


def _dot_nt(a, b):
    return lax.dot_general(a, b, (((1,), (1,)), ((), ())), preferred_element_type=F32)


def _dot_tn(a, b):
    return lax.dot_general(a, b, (((0,), (0,)), ((), ())), preferred_element_type=F32)


def _sigmoid(x):
    return 1.0 / (1.0 + jnp.exp(-x))


def _resident(shape):
    nd = len(shape)
    return pl.BlockSpec(shape, lambda *_: (0,) * nd, pipeline_mode=pl.Buffered(1))


def _mem_kv_kernel(mem_ref, g_ref, w_ref, mk_ref, mv_ref):
    hn = _rms(mem_ref[0], g_ref[0]).astype(BF16)
    kv = _dot(hn, w_ref[0])
    mk = kv[:, :MEM_W] * (MEM_HEAD_DIM ** -0.5 * LOG2E)
    mv = kv[:, MEM_W:]
    lane_head = lax.broadcasted_iota(jnp.int32, (MEM_TOKENS, MEM_W), 1) >> 6
    for h in range(MEM_HEADS):
        sel = lane_head == h
        mk_ref[0, 0, h] = jnp.where(sel, mk, 0.0).astype(BF16)
        mv_ref[0, 0, h] = jnp.where(sel, mv, 0.0).astype(BF16)


def _mem_kv(mem, g_mem, w_mem_kv_bf16):
    B = mem.shape[0]
    out_sds = jax.ShapeDtypeStruct((DEPTH, B, MEM_HEADS, MEM_TOKENS, MEM_W), BF16)
    out_spec = pl.BlockSpec((1, 1, MEM_HEADS, MEM_TOKENS, MEM_W), lambda l, b: (l, b, 0, 0, 0))
    return pl.pallas_call(
        _mem_kv_kernel,
        grid=(DEPTH, B),
        in_specs=[
            pl.BlockSpec((1, MEM_TOKENS, D_MODEL), lambda l, b: (b, 0, 0)),
            pl.BlockSpec((1, 1, D_MODEL), lambda l, b: (l, 0, 0)),
            pl.BlockSpec((1, D_MODEL, 2 * MEM_W), lambda l, b: (l, 0, 0)),
        ],
        out_specs=[out_spec, out_spec],
        out_shape=[out_sds, out_sds],
        compiler_params=pltpu.CompilerParams(dimension_semantics=("arbitrary", "arbitrary")),
        name="mem_kv",
    )(mem, g_mem.reshape(DEPTH, 1, D_MODEL), w_mem_kv_bf16)


def _mem_attention(mq, mk_ref, mv_ref):
    out = None
    for h in range(MEM_HEADS):
        s = _dot_nt(mq, mk_ref[0, 0, h])
        e = jnp.exp2(s - jnp.max(s, axis=-1, keepdims=True))
        p = e * (1.0 / jnp.sum(e, axis=-1, keepdims=True))
        o = _dot(p.astype(BF16), mv_ref[0, 0, h])
        out = o if out is None else out + o
    return out


def _proj_a_kernel(x_ref, g_ref, wq_ref, wk_ref, wv_ref, wo_ref, wmq_ref, wg_ref, bg_ref, mk_ref, mv_ref,
                   q_ref, k_ref, v_ref, o_ref, gt_ref, mo_ref):
    hn = _rms(x_ref[...], g_ref[...]).astype(BF16)
    q_ref[...] = _dot(hn, wq_ref[...]).astype(BF16)
    k_ref[...] = (_dot(hn, wk_ref[...]) * (MLSTM_DQK ** -0.5)).astype(BF16)
    v_ref[...] = _dot(hn, wv_ref[...]).astype(BF16)
    o_ref[...] = _dot(hn, wo_ref[...]).astype(BF16)
    gates = _dot(hn, wg_ref[...]) + bg_ref[...]
    gates = GATE_SOFTCAP * jnp.tanh(gates * (1.0 / GATE_SOFTCAP))
    log_sig = jnp.minimum(gates, 0.0) - jnp.log1p(jnp.exp(-jnp.abs(gates)))
    lane = lax.broadcasted_iota(jnp.int32, gates.shape, 1)
    gt_ref[...] = jnp.where(lane < MLSTM_HEADS, gates, log_sig)
    mq = _dot(hn, wmq_ref[...]).astype(BF16)
    mo_ref[...] = _mem_attention(mq, mk_ref, mv_ref).astype(BF16)


def _proj_a(x2d, g_pre, wq, wk, wv, wo, wmq, wg, bg, mkm, mvm, rows_per_batch):
    N = x2d.shape[0]
    tm = ROW_TILE
    steps_per_batch = rows_per_batch // tm
    row = lambda w: pl.BlockSpec((tm, w), lambda i: (i, 0))
    mem_spec = pl.BlockSpec((1, 1, MEM_HEADS, MEM_TOKENS, MEM_W),
                            lambda i: (0, i // steps_per_batch, 0, 0, 0))
    return pl.pallas_call(
        _proj_a_kernel,
        grid=(N // tm,),
        in_specs=[row(D_MODEL), _resident((1, D_MODEL)),
                  _resident(wq.shape), _resident(wk.shape), _resident(wv.shape), _resident(wo.shape),
                  _resident(wmq.shape), _resident(wg.shape), _resident((1, LANES)),
                  mem_spec, mem_spec],
        out_specs=[row(QK_PAD_W), row(QK_PAD_W), row(MLSTM_V_W), row(MLSTM_V_W), row(LANES), row(MEM_W)],
        out_shape=[jax.ShapeDtypeStruct((N, QK_PAD_W), BF16), jax.ShapeDtypeStruct((N, QK_PAD_W), BF16),
                   jax.ShapeDtypeStruct((N, MLSTM_V_W), BF16), jax.ShapeDtypeStruct((N, MLSTM_V_W), BF16),
                   jax.ShapeDtypeStruct((N, LANES), F32), jax.ShapeDtypeStruct((N, MEM_W), BF16)],
        compiler_params=pltpu.CompilerParams(dimension_semantics=("arbitrary",),
                                             vmem_limit_bytes=VMEM_LIMIT),
        name="proj_a",
    )(x2d, g_pre, wq, wk, wv, wo, wmq, wg, bg, mkm, mvm)


def _mlstm_kernel(q_ref, k_ref, v_ref, o_ref, gt_ref, gout_ref, hm_ref, c_scr, m_scr):
    nb, L = q_ref.shape[0], q_ref.shape[1]

    @pl.when(pl.program_id(1) == 0)
    def _():
        c_scr[...] = jnp.zeros(c_scr.shape, F32)
        m_scr[...] = jnp.full(m_scr.shape, M_INIT, F32)

    row = lax.broadcasted_iota(jnp.int32, (L, L), 0)
    col = lax.broadcasted_iota(jnp.int32, (L, L), 1)
    causal = col <= row
    tri = causal.astype(F32)
    lane = lax.broadcasted_iota(jnp.int32, (L, LANES), 1)
    lane_w = lax.broadcasted_iota(jnp.int32, (L, V_WIN), 1)
    low = lane < (MLSTM_DV - LANES)

    m_old = [m_scr[st][0:1, 0:1] for st in range(nb * MLSTM_HEADS)]
    c_old = [c_scr[st] for st in range(nb * MLSTM_HEADS)]
    m_next = [None] * (nb * MLSTM_HEADS)
    c_next = [None] * (nb * MLSTM_HEADS)

    for bi in range(nb):
        gates = gt_ref[bi]
        csum = jnp.dot(tri, gates, precision=lax.Precision.HIGHEST,
                       preferred_element_type=F32)
        y = jnp.where(lane < MLSTM_HEADS, gates, csum)
        yt = y.T
        res = []
        for h in range(MLSTM_HEADS):
            st = bi * MLSTM_HEADS + h
            ws, off, ones_col = _V_WIN_START[h], _V_WIN_OFF[h], _V_ONES_COL[h]
            ig_col = y[:, h:h + 1]
            b_col = y[:, MLSTM_HEADS + h:MLSTM_HEADS + h + 1]
            ig_row = yt[h:h + 1, :]
            b_row = yt[MLSTM_HEADS + h:MLSTM_HEADS + h + 1, :]
            m_st = m_old[st]

            dmat = jnp.where(causal, b_col - b_row + ig_row, -jnp.inf)
            inter_log = b_col + m_st
            m_row = jnp.maximum(inter_log, jnp.max(dmat, axis=-1, keepdims=True))
            decay_mat = jnp.exp(dmat - m_row)
            qh = q_ref[bi, :, h * DQK_PAD:(h + 1) * DQK_PAD]
            kh = k_ref[bi, :, h * DQK_PAD:(h + 1) * DQK_PAD]
            scores = (_dot_nt(qh, kh) * decay_mat).astype(BF16)
            v_aug = jnp.where(lane_w == ones_col, jnp.ones((), BF16), v_ref[bi, :, ws:ws + V_WIN])
            c_h = c_old[st]
            inter_w = jnp.exp(inter_log - m_row)
            num = _dot(scores, v_aug) + inter_w * _dot(qh, c_h.astype(BF16))
            den = num[:, ones_col:ones_col + 1]
            hv = num * (1.0 / jnp.maximum(jnp.abs(den), jnp.exp(-m_row)))
            in_head = (lane_w >= off) & (lane_w < off + MLSTM_DV)
            ms = jnp.sum(jnp.where(in_head, hv * hv, 0.0), axis=-1, keepdims=True) * (1.0 / MLSTM_DV)
            og = _sigmoid(o_ref[bi, :, ws:ws + V_WIN].astype(F32))
            res.append(hv * lax.rsqrt(ms + EPS) * gout_ref[:, ws:ws + V_WIN] * og)

            g_tot = b_col[L - 1:L, :]
            a = g_tot - b_col + ig_col
            m_new = jnp.maximum(g_tot + m_st, jnp.max(a, axis=0, keepdims=True))
            w = jnp.exp(a - m_new)
            kw = (kh.astype(F32) * w).astype(BF16)
            c_next[st] = jnp.exp(g_tot + m_st - m_new) * c_h + _dot_tn(kw, v_aug)
            m_next[st] = m_new

        r0, r1, r2, r3 = res
        hm_ref[bi, :, 0:128] = r0[:, :LANES].astype(BF16)
        hm_ref[bi, :, 128:256] = jnp.where(low, r0[:, LANES:], r1[:, :LANES]).astype(BF16)
        hm_ref[bi, :, 256:384] = r1[:, LANES:].astype(BF16)
        hm_ref[bi, :, 384:512] = r2[:, :LANES].astype(BF16)
        hm_ref[bi, :, 512:640] = jnp.where(low, r2[:, LANES:], r3[:, :LANES]).astype(BF16)
        hm_ref[bi, :, 640:768] = r3[:, LANES:].astype(BF16)

    for st in range(nb * MLSTM_HEADS):
        c_scr[st] = c_next[st]
        m_scr[st] = jnp.broadcast_to(m_next[st], m_scr.shape[1:])


def _mlstm(q, k, v, o, gt, g_out, B, S):
    L = MLSTM_CHUNK
    nb = MLSTM_BATCH_PER_STEP
    r3 = lambda t: t.reshape(B, S, t.shape[-1])
    blk = lambda w: pl.BlockSpec((nb, L, w), lambda b, c: (b, c, 0))
    out = pl.pallas_call(
        _mlstm_kernel,
        grid=(B // nb, S // L),
        in_specs=[blk(QK_PAD_W), blk(QK_PAD_W), blk(MLSTM_V_W), blk(MLSTM_V_W), blk(LANES),
                  pl.BlockSpec((1, MLSTM_V_W), lambda b, c: (0, 0))],
        out_specs=blk(MLSTM_V_W),
        out_shape=jax.ShapeDtypeStruct((B, S, MLSTM_V_W), BF16),
        scratch_shapes=[pltpu.VMEM((nb * MLSTM_HEADS, DQK_PAD, V_WIN), F32),
                        pltpu.VMEM((nb * MLSTM_HEADS, 8, LANES), F32)],
        compiler_params=pltpu.CompilerParams(dimension_semantics=("arbitrary", "arbitrary"),
                                             vmem_limit_bytes=VMEM_LIMIT),
        name="mlstm",
    )(r3(q), r3(k), r3(v), r3(o), r3(gt), g_out)
    return out.reshape(B * S, MLSTM_V_W)


def _mix_ffn_core(x_ref, main_ref, mo_ref, wom_ref, woe_ref, gpost_ref, gfpre_ref, gfpost_ref,
                  wi_ref, wd_ref):
    mix = _dot(main_ref[...], wom_ref[...]) + _dot(mo_ref[...], woe_ref[...])
    x1 = x_ref[...] + _rms(mix, gpost_ref[...])
    hf = _rms(x1, gfpre_ref[...]).astype(BF16)
    acc = None
    for c in range(N_FF_CHUNKS):
        lo = c * FF_CHUNK
        g = _dot(hf, wi_ref[:, lo:lo + FF_CHUNK])
        u = _dot(hf, wi_ref[:, D_FF + lo:D_FF + lo + FF_CHUNK])
        act = (g * _sigmoid(g) * u).astype(BF16)
        d = _dot(act, wd_ref[lo:lo + FF_CHUNK, :])
        acc = d if acc is None else acc + d
    return x1 + _rms(acc, gfpost_ref[...])


def _mix_ffn_last_kernel(x_ref, main_ref, mo_ref, wom_ref, woe_ref, gpost_ref, gfpre_ref, gfpost_ref,
                         wi_ref, wd_ref, xo_ref):
    xo_ref[...] = _mix_ffn_core(x_ref, main_ref, mo_ref, wom_ref, woe_ref, gpost_ref, gfpre_ref,
                                gfpost_ref, wi_ref, wd_ref)


def _rope(t1, t2, cos, sin):
    return t1 * cos - t2 * sin, t2 * cos + t1 * sin


def _rope_tables(pos_ref, inv_ref):
    ang = pos_ref[...].astype(F32) * inv_ref[...]
    cos_d = jnp.cos(ang)
    sin_d = jnp.sin(ang)
    grp = lax.broadcasted_iota(jnp.int32, ang.shape, 1) >> 5

    def spread(t, g):
        y = jnp.where(grp == g, t, 0.0)
        y = y + pltpu.roll(y, HEAD_DIM // 2, 1)
        return y + pltpu.roll(y, HEAD_DIM, 1)

    n_grp = LANES // (HEAD_DIM // 2)
    cos = jnp.concatenate([spread(cos_d, g) for g in range(n_grp)], axis=0)
    sin = jnp.concatenate([spread(sin_d, g) for g in range(n_grp)], axis=0)
    return cos, sin


def _mix_ffn_first_kernel(x_ref, main_ref, mo_ref, wom_ref, woe_ref, gpost_ref, gfpre_ref, gfpost_ref,
                          wi_ref, wd_ref,
                          pos_ref, inv_ref, gkv_ref, wkv_ref, gpre1_ref, wb_ref, mk_ref, mv_ref,
                          xo_ref, ks_ref, vs_ref, q1_ref, mo1_ref):
    cos, sin = _rope_tables(pos_ref, inv_ref)
    x2 = _mix_ffn_core(x_ref, main_ref, mo_ref, wom_ref, woe_ref, gpost_ref, gfpre_ref, gfpost_ref,
                       wi_ref, wd_ref)
    xo_ref[...] = x2
    xn = x2 * lax.rsqrt(jnp.mean(x2 * x2, axis=-1, keepdims=True) + EPS)

    kv = _dot((xn * gkv_ref[...]).astype(BF16), wkv_ref[...])
    k1, k2 = _rope(kv[:, 0:LANES], kv[:, LANES:2 * LANES], cos, sin)
    scale = HEAD_DIM ** -0.5 * LOG2E
    ks_ref[:, 0:LANES] = (k1 * scale).astype(BF16)
    ks_ref[:, LANES:2 * LANES] = (k2 * scale).astype(BF16)
    vs_ref[...] = kv[:, SWA_KV_W:].astype(BF16)

    pb = _dot((xn * gpre1_ref[...]).astype(BF16), wb_ref[...])
    for gi in range(SWA_GROUPS):
        base = gi * 2 * LANES
        t1, t2 = _rope(pb[:, base:base + LANES], pb[:, base + LANES:base + 2 * LANES], cos, sin)
        q1_ref[:, base:base + LANES] = t1.astype(BF16)
        q1_ref[:, base + LANES:base + 2 * LANES] = t2.astype(BF16)
    mq = pb[:, SWA_Q_W:].astype(BF16)
    mo1_ref[...] = _mem_attention(mq, mk_ref, mv_ref).astype(BF16)


def _mix_ffn(layer, x2d, main, mo, wom, woe, g_post, g_fpre, g_fpost, wi, wd, first=None, rows_per_batch=None):
    N = x2d.shape[0]
    tm = ROW_TILE
    row = lambda w: pl.BlockSpec((tm, w), lambda i: (i, 0))
    slab = lambda w: pl.BlockSpec((None,) + w.shape[1:], lambda i: (layer, 0, 0), pipeline_mode=pl.Buffered(1))
    in_specs = [row(D_MODEL), row(main.shape[1]), row(MEM_W),
                _resident(wom.shape), _resident(woe.shape),
                _resident((1, D_MODEL)), _resident((1, D_MODEL)), _resident((1, D_MODEL)),
                slab(wi), slab(wd)]
    args = [x2d, main, mo, wom, woe, g_post, g_fpre, g_fpost, wi, wd]
    out_specs = [row(D_MODEL)]
    out_shape = [jax.ShapeDtypeStruct((N, D_MODEL), F32)]
    if first is None:
        body = _mix_ffn_last_kernel
        name = "mix_ffn_last"
    else:
        body = _mix_ffn_first_kernel
        name = "mix_ffn_first"
        pos, inv, g_kv, wkv, g_pre1, wb, mkm, mvm = first
        steps_per_batch = rows_per_batch // tm
        mem_spec = pl.BlockSpec((1, 1, MEM_HEADS, MEM_TOKENS, MEM_W),
                                lambda i: (1, i // steps_per_batch, 0, 0, 0))
        pos_spec = pl.BlockSpec((tm // (LANES // (HEAD_DIM // 2)), LANES), lambda i: (i, 0))
        in_specs += [pos_spec, _resident((1, LANES)), _resident((1, D_MODEL)), _resident(wkv.shape),
                     _resident((1, D_MODEL)), _resident(wb.shape), mem_spec, mem_spec]
        args += [pos, inv, g_kv, wkv, g_pre1, wb, mkm, mvm]
        out_specs += [row(SWA_KV_W), row(SWA_KV_W), row(SWA_Q_W), row(MEM_W)]
        out_shape += [jax.ShapeDtypeStruct((N, SWA_KV_W), BF16), jax.ShapeDtypeStruct((N, SWA_KV_W), BF16),
                      jax.ShapeDtypeStruct((N, SWA_Q_W), BF16), jax.ShapeDtypeStruct((N, MEM_W), BF16)]
    return pl.pallas_call(
        body,
        grid=(N // tm,),
        in_specs=in_specs,
        out_specs=out_specs,
        out_shape=out_shape,
        compiler_params=pltpu.CompilerParams(dimension_semantics=("arbitrary",),
                                             vmem_limit_bytes=VMEM_LIMIT),
        name=name,
    )(*args)


def _swa_kernel(sink_ref, q_ref, kp_ref, kc_ref, vp_ref, vc_ref, o_ref):
    W = SWA_WINDOW
    G = SWA_GROUPS
    n = pl.program_id(1)
    n_sub = q_ref.shape[0] // W
    rows = lax.broadcasted_iota(jnp.int32, (G * W, 2 * W), 0)
    kpos = lax.broadcasted_iota(jnp.int32, (G * W, 2 * W), 1)
    diff = (rows & (W - 1)) + W - kpos
    in_band = (diff >= 0) & (diff < W)
    k_lane = lax.broadcasted_iota(jnp.int32, (2 * W, SWA_KV_W), 1)
    grp = lax.broadcasted_iota(jnp.int32, (G * W, 1), 0) >> 7
    for sub in range(n_sub):
        q = jnp.concatenate([q_ref[sub * W:(sub + 1) * W, gi * 2 * LANES:(gi + 1) * 2 * LANES]
                             for gi in range(G)], axis=0)
        if sub == 0:
            k = jnp.concatenate([kp_ref[...], kc_ref[0:W, :]], axis=0)
            v = jnp.concatenate([vp_ref[...], vc_ref[0:W, :]], axis=0)
            valid = in_band & ((kpos >= W) | (n > 0))
        else:
            k = kc_ref[(sub - 1) * W:(sub + 1) * W, :]
            v = vc_ref[(sub - 1) * W:(sub + 1) * W, :]
            valid = in_band
        out = None
        for j in range(SWA_KV_HEADS):
            km = jnp.where(((k_lane & (LANES - 1)) >> 5) == j, k, jnp.zeros((), BF16))
            vm = jnp.where((k_lane >> 6) == j, v, jnp.zeros((), BF16))
            sink = jnp.zeros((G * W, 1), F32)
            for gi in range(G):
                sink = jnp.where(grp == gi, sink_ref[j * G + gi], sink)
            s = jnp.where(valid, _dot_nt(q, km), -jnp.inf)
            m = jnp.maximum(jnp.max(s, axis=-1, keepdims=True), sink)
            e = jnp.exp2(s - m)
            p = e * (1.0 / (jnp.sum(e, axis=-1, keepdims=True) + jnp.exp2(sink - m)))
            o = _dot(p.astype(BF16), vm)
            out = o if out is None else out + o
        for gi in range(G):
            o_ref[sub * W:(sub + 1) * W, gi * 2 * LANES:(gi + 1) * 2 * LANES] = (
                out[gi * W:(gi + 1) * W].astype(BF16))


def _swa(sink_cols, q1, ks, vs, B, S):
    W = SWA_WINDOW
    n_sub = SWA_BLOCKS_PER_STEP
    tq = n_sub * W
    steps = S // tq
    cur = lambda w: pl.BlockSpec((tq, w), lambda b, n: (b * steps + n, 0))
    prev = lambda w: pl.BlockSpec((W, w), lambda b, n: (b * (S // W) + jnp.maximum(n_sub * n - 1, 0), 0))
    return pl.pallas_call(
        _swa_kernel,
        grid=(B, steps),
        in_specs=[pl.BlockSpec(memory_space=pltpu.SMEM),
                  cur(SWA_Q_W), prev(SWA_KV_W), cur(SWA_KV_W), prev(SWA_KV_W), cur(SWA_KV_W)],
        out_specs=cur(SWA_Q_W),
        out_shape=jax.ShapeDtypeStruct((B * S, SWA_Q_W), BF16),
        compiler_params=pltpu.CompilerParams(dimension_semantics=("arbitrary", "arbitrary"),
                                             vmem_limit_bytes=VMEM_LIMIT),
        name="swa",
    )(sink_cols, q1, ks, ks, vs, vs)


def _pad_heads(w, heads, width, padded):
    w = w.reshape(w.shape[0], heads, width)
    return jnp.pad(w, ((0, 0), (0, 0), (0, padded - width))).reshape(w.shape[0], heads * padded)


def _swa_q_perm():
    idx = np.empty((SWA_GROUPS, 2, SWA_KV_HEADS, HEAD_DIM // 2), np.int32)
    for gi in range(SWA_GROUPS):
        for half in range(2):
            for j in range(SWA_KV_HEADS):
                idx[gi, half, j] = (j * SWA_GROUPS + gi) * HEAD_DIM + half * (HEAD_DIM // 2) + np.arange(HEAD_DIM // 2)
    return idx.reshape(-1)


def _swa_k_perm():
    idx = np.empty((2, SWA_KV_HEADS, HEAD_DIM // 2), np.int32)
    for half in range(2):
        for j in range(SWA_KV_HEADS):
            idx[half, j] = j * HEAD_DIM + half * (HEAD_DIM // 2) + np.arange(HEAD_DIM // 2)
    return idx.reshape(-1)


def _swa_out_perm():
    idx = np.empty((SWA_GROUPS, SWA_KV_HEADS, HEAD_DIM), np.int32)
    for gi in range(SWA_GROUPS):
        for j in range(SWA_KV_HEADS):
            idx[gi, j] = (j * SWA_GROUPS + gi) * HEAD_DIM + np.arange(HEAD_DIM)
    return idx.reshape(-1)


def kernel(x, mem, positions, g_mix_pre, g_mix_post, g_ffn_pre, g_ffn_post, g_mem, w_mem_kv, w_out,
           w_ffn_in, w_ffn_out, w_in_a, b_gates_a, g_mlstm_out, g_kv, w_kv, w_in_b, sinks_b):
    B, S, _ = x.shape
    N = B * S
    assert S % ROW_TILE == 0 and S % MLSTM_CHUNK == 0 and S % (SWA_WINDOW * SWA_BLOCKS_PER_STEP) == 0
    assert B % MLSTM_BATCH_PER_STEP == 0
    x2d = x.reshape(N, D_MODEL)
    vec = lambda g: g.reshape(1, -1).astype(F32)

    mkm, mvm = _mem_kv(mem, g_mem, w_mem_kv.astype(BF16))

    wa = w_in_a[0]
    c0 = 0
    wq = _pad_heads(wa[:, c0:c0 + MLSTM_QK_W], MLSTM_HEADS, MLSTM_DQK, DQK_PAD).astype(BF16)
    c0 += MLSTM_QK_W
    wk = _pad_heads(wa[:, c0:c0 + MLSTM_QK_W], MLSTM_HEADS, MLSTM_DQK, DQK_PAD).astype(BF16)
    c0 += MLSTM_QK_W
    wv = wa[:, c0:c0 + MLSTM_V_W].astype(BF16)
    c0 += MLSTM_V_W
    wo = wa[:, c0:c0 + MLSTM_V_W].astype(BF16)
    c0 += MLSTM_V_W
    wgt = jnp.pad(wa[:, c0:c0 + 2 * MLSTM_HEADS], ((0, 0), (0, LANES - 2 * MLSTM_HEADS))).astype(BF16)
    c0 += 2 * MLSTM_HEADS
    wmq = wa[:, c0:c0 + MEM_W].astype(BF16)
    bg = jnp.pad(b_gates_a[0].astype(F32), (0, LANES - 2 * MLSTM_HEADS)).reshape(1, LANES)

    q, k, v, o, gt, mo0 = _proj_a(x2d, vec(g_mix_pre[0]), wq, wk, wv, wo, wmq, wgt, bg, mkm, mvm, S)
    hm = _mlstm(q, k, v, o, gt, vec(g_mlstm_out[0]), B, S)

    w_ffn_in_bf = w_ffn_in.astype(BF16)
    w_ffn_out_bf = w_ffn_out.astype(BF16)
    wo0 = w_out[0].astype(BF16)
    inv = 1.0 / (ROPE_THETA ** (jnp.arange(0, HEAD_DIM, 2, dtype=F32) / HEAD_DIM))
    inv = jnp.tile(inv, SWA_KV_HEADS).reshape(1, LANES)
    wkv = jnp.concatenate([w_kv[:, :SWA_KV_W][:, _swa_k_perm()], w_kv[:, SWA_KV_W:]], axis=1).astype(BF16)
    wb = jnp.concatenate([w_in_b[0][:, :SWA_Q_W][:, _swa_q_perm()], w_in_b[0][:, SWA_Q_W:]], axis=1).astype(BF16)
    n_grp = LANES // (HEAD_DIM // 2)
    pos_dense = positions.reshape(N // ROW_TILE, n_grp, ROW_TILE // n_grp).transpose(0, 2, 1)
    pos_dense = jnp.repeat(pos_dense, HEAD_DIM // 2, axis=2).reshape(N // n_grp, LANES)
    first = (pos_dense, inv, vec(g_kv), wkv, vec(g_mix_pre[1]), wb, mkm, mvm)
    x1, ks, vs, q1, mo1 = _mix_ffn(0, x2d, hm, mo0, wo0[:MLSTM_V_W], wo0[MLSTM_V_W:],
                                   vec(g_mix_post[0]), vec(g_ffn_pre[0]), vec(g_ffn_post[0]),
                                   w_ffn_in_bf, w_ffn_out_bf, first=first, rows_per_batch=S)

    attn = _swa(sinks_b[0].astype(F32) * LOG2E, q1, ks, vs, B, S)
    wo1 = w_out[1]
    (xo,) = _mix_ffn(1, x1, attn, mo1, wo1[:SWA_Q_W][_swa_out_perm()].astype(BF16), wo1[SWA_Q_W:].astype(BF16),
                     vec(g_mix_post[1]), vec(g_ffn_pre[1]), vec(g_ffn_post[1]),
                     w_ffn_in_bf, w_ffn_out_bf)
    return xo.reshape(B, S, D_MODEL)
```

```python
import numpy as np
import jax
import jax.numpy as jnp
from jax import lax
from jax.experimental import pallas as pl
from jax.experimental.pallas import tpu as pltpu

F32 = jnp.float32
BF16 = jnp.bfloat16

D_MODEL = 1024
DEPTH = 2
HEAD_DIM = 64
EPS = 1e-6
ROPE_THETA = 10000.0
LOG2E = 1.4426950408889634

MLSTM_HEADS = 4
MLSTM_DV = 192
MLSTM_DQK = 96
MLSTM_QK_W = MLSTM_HEADS * MLSTM_DQK
MLSTM_V_W = MLSTM_HEADS * MLSTM_DV
GATE_SOFTCAP = 15.0
M_INIT = -1e30

SWA_Q_HEADS = 12
SWA_KV_HEADS = 4
SWA_GROUPS = SWA_Q_HEADS // SWA_KV_HEADS
SWA_Q_W = SWA_Q_HEADS * HEAD_DIM
SWA_KV_W = SWA_KV_HEADS * HEAD_DIM
SWA_WINDOW = 128

MEM_TOKENS = 256
MEM_HEADS = 4
MEM_HEAD_DIM = 64
MEM_W = MEM_HEADS * MEM_HEAD_DIM

D_FF = 2816

LANES = 128
MXU_TILE = 256

DQK_PAD = LANES
QK_PAD_W = MLSTM_HEADS * DQK_PAD
V_WIN = MXU_TILE
MLSTM_CHUNK = 256
FF_CHUNK = MXU_TILE
N_FF_CHUNKS = D_FF // FF_CHUNK
ROW_TILE = 512
SWA_BLOCKS_PER_STEP = 2
VMEM_LIMIT = 56 * 1024 * 1024

_V_WIN_START = (0, 128, 384, 512)
_V_WIN_OFF = (0, 64, 0, 64)
_V_ONES_COL = (192, 0, 192, 0)


def _rms(x, g):
    return x * lax.rsqrt(jnp.mean(x * x, axis=-1, keepdims=True) + EPS) * g


def _dot(a, b):
    return jnp.dot(a, b, preferred_element_type=F32)


def _dot_nt(a, b):
    return lax.dot_general(a, b, (((1,), (1,)), ((), ())), preferred_element_type=F32)


def _dot_tn(a, b):
    return lax.dot_general(a, b, (((0,), (0,)), ((), ())), preferred_element_type=F32)


def _sigmoid(x):
    return 1.0 / (1.0 + jnp.exp(-x))


def _resident(shape):
    nd = len(shape)
    return pl.BlockSpec(shape, lambda *_: (0,) * nd, pipeline_mode=pl.Buffered(1))


def _mem_kv_kernel(mem_ref, g_ref, w_ref, mk_ref, mv_ref):
    hn = _rms(mem_ref[0], g_ref[0]).astype(BF16)
    kv = _dot(hn, w_ref[0])
    mk = kv[:, :MEM_W] * (MEM_HEAD_DIM ** -0.5 * LOG2E)
    mv = kv[:, MEM_W:]
    lane_head = lax.broadcasted_iota(jnp.int32, (MEM_TOKENS, MEM_W), 1) >> 6
    for h in range(MEM_HEADS):
        sel = lane_head == h
        mk_ref[0, 0, h] = jnp.where(sel, mk, 0.0).astype(BF16)
        mv_ref[0, 0, h] = jnp.where(sel, mv, 0.0).astype(BF16)


def _mem_kv(mem, g_mem, w_mem_kv_bf16):
    B = mem.shape[0]
    out_sds = jax.ShapeDtypeStruct((DEPTH, B, MEM_HEADS, MEM_TOKENS, MEM_W), BF16)
    out_spec = pl.BlockSpec((1, 1, MEM_HEADS, MEM_TOKENS, MEM_W), lambda l, b: (l, b, 0, 0, 0))
    return pl.pallas_call(
        _mem_kv_kernel,
        grid=(DEPTH, B),
        in_specs=[
            pl.BlockSpec((1, MEM_TOKENS, D_MODEL), lambda l, b: (b, 0, 0)),
            pl.BlockSpec((1, 1, D_MODEL), lambda l, b: (l, 0, 0)),
            pl.BlockSpec((1, D_MODEL, 2 * MEM_W), lambda l, b: (l, 0, 0)),
        ],
        out_specs=[out_spec, out_spec],
        out_shape=[out_sds, out_sds],
        compiler_params=pltpu.CompilerParams(dimension_semantics=("arbitrary", "arbitrary")),
        name="mem_kv",
    )(mem, g_mem.reshape(DEPTH, 1, D_MODEL), w_mem_kv_bf16)


def _mem_attention(mq, mk_ref, mv_ref):
    out = None
    for h in range(MEM_HEADS):
        s = _dot_nt(mq, mk_ref[0, 0, h])
        e = jnp.exp2(s - jnp.max(s, axis=-1, keepdims=True))
        p = e * (1.0 / jnp.sum(e, axis=-1, keepdims=True))
        o = _dot(p.astype(BF16), mv_ref[0, 0, h])
        out = o if out is None else out + o
    return out


def _proj_a_kernel(x_ref, g_ref, wq_ref, wk_ref, wv_ref, wo_ref, wmq_ref, wg_ref, bg_ref, mk_ref, mv_ref,
                   q_ref, k_ref, v_ref, o_ref, gt_ref, mo_ref):
    hn = _rms(x_ref[...], g_ref[...]).astype(BF16)
    q_ref[...] = _dot(hn, wq_ref[...]).astype(BF16)
    k_ref[...] = (_dot(hn, wk_ref[...]) * (MLSTM_DQK ** -0.5)).astype(BF16)
    v_ref[...] = _dot_nt(wv_ref[...], hn).astype(BF16)
    o_ref[...] = _dot_nt(wo_ref[...], hn).astype(BF16)
    gates = _dot(hn, wg_ref[...]) + bg_ref[...]
    gates = GATE_SOFTCAP * jnp.tanh(gates * (1.0 / GATE_SOFTCAP))
    log_sig = jnp.minimum(gates, 0.0) - jnp.log1p(jnp.exp(-jnp.abs(gates)))
    lane = lax.broadcasted_iota(jnp.int32, gates.shape, 1)
    gt_ref[...] = jnp.where(lane < MLSTM_HEADS, gates, log_sig) * LOG2E
    mq = _dot(hn, wmq_ref[...]).astype(BF16)
    mo_ref[...] = _mem_attention(mq, mk_ref, mv_ref).astype(BF16)


def _proj_a(x2d, g_pre, wq, wk, wv, wo, wmq, wg, bg, mkm, mvm, rows_per_batch):
    N = x2d.shape[0]
    tm = ROW_TILE
    steps_per_batch = rows_per_batch // tm
    row = lambda w: pl.BlockSpec((tm, w), lambda i: (i, 0))
    col = pl.BlockSpec((MLSTM_V_W, tm), lambda i: (0, i))
    mem_spec = pl.BlockSpec((1, 1, MEM_HEADS, MEM_TOKENS, MEM_W),
                            lambda i: (0, i // steps_per_batch, 0, 0, 0))
    return pl.pallas_call(
        _proj_a_kernel,
        grid=(N // tm,),
        in_specs=[row(D_MODEL), _resident((1, D_MODEL)),
                  _resident(wq.shape), _resident(wk.shape), _resident(wv.shape), _resident(wo.shape),
                  _resident(wmq.shape), _resident(wg.shape), _resident((1, LANES)),
                  mem_spec, mem_spec],
        out_specs=[row(QK_PAD_W), row(QK_PAD_W), col, col, row(LANES), row(MEM_W)],
        out_shape=[jax.ShapeDtypeStruct((N, QK_PAD_W), BF16), jax.ShapeDtypeStruct((N, QK_PAD_W), BF16),
                   jax.ShapeDtypeStruct((MLSTM_V_W, N), BF16), jax.ShapeDtypeStruct((MLSTM_V_W, N), BF16),
                   jax.ShapeDtypeStruct((N, LANES), F32), jax.ShapeDtypeStruct((N, MEM_W), BF16)],
        compiler_params=pltpu.CompilerParams(dimension_semantics=("arbitrary",),
                                             vmem_limit_bytes=VMEM_LIMIT),
        name="proj_a",
    )(x2d, g_pre, wq, wk, wv, wo, wmq, wg, bg, mkm, mvm)


def _mlstm_kernel(q_ref, k_ref, vt_ref, ot_ref, gt_ref, goutt_ref, hmt_ref, c_scr, m_scr):
    L = q_ref.shape[0]

    @pl.when(pl.program_id(1) == 0)
    def _():
        c_scr[...] = jnp.zeros(c_scr.shape, F32)
        m_scr[...] = jnp.full(m_scr.shape, M_INIT, F32)

    src = lax.broadcasted_iota(jnp.int32, (L, L), 0)
    tgt = lax.broadcasted_iota(jnp.int32, (L, L), 1)
    causal = src <= tgt
    gates = gt_ref[...]
    csum = jnp.dot((tgt <= src).astype(F32), gates, precision=lax.Precision.HIGHEST,
                   preferred_element_type=F32)
    lane = lax.broadcasted_iota(jnp.int32, (L, LANES), 1)
    y = jnp.where(lane < MLSTM_HEADS, gates, csum)
    yt = y.T
    feat = lax.broadcasted_iota(jnp.int32, (V_WIN, L), 0)

    for h in range(MLSTM_HEADS):
        ws, off, ones_row = _V_WIN_START[h], _V_WIN_OFF[h], _V_ONES_COL[h]
        lo = h * MLSTM_DV
        b_row = yt[MLSTM_HEADS + h:MLSTM_HEADS + h + 1, :]
        c_row = b_row - yt[h:h + 1, :]
        c_col = y[:, MLSTM_HEADS + h:MLSTM_HEADS + h + 1] - y[:, h:h + 1]
        m_st = m_scr[h][0:1, 0:1]

        dmat = jnp.where(causal, b_row - c_col, -jnp.inf)
        inter_log = b_row + m_st
        m_row = jnp.maximum(inter_log, jnp.max(dmat, axis=0, keepdims=True))
        qh = q_ref[:, h * DQK_PAD:(h + 1) * DQK_PAD]
        kh = k_ref[:, h * DQK_PAD:(h + 1) * DQK_PAD]
        scores_t = (_dot_nt(kh, qh) * jnp.exp2(dmat - m_row)).astype(BF16)
        vt_aug = jnp.where(feat == ones_row, jnp.ones((), BF16), vt_ref[ws:ws + V_WIN, :])
        c_h = c_scr[h]
        inter_w = jnp.exp2(inter_log - m_row)
        num_t = _dot(vt_aug, scores_t) + inter_w * _dot_nt(c_h.astype(BF16), qh)
        den = num_t[ones_row:ones_row + 1, :]
        inv_dd = 1.0 / jnp.maximum(jnp.abs(den), jnp.exp2(-m_row))
        h_t = num_t[off:off + MLSTM_DV, :]
        ms = jnp.sum(h_t * h_t, axis=0, keepdims=True) * (1.0 / MLSTM_DV)
        scale = inv_dd * lax.rsqrt(ms * inv_dd * inv_dd + EPS)
        og = _sigmoid(ot_ref[lo:lo + MLSTM_DV, :].astype(F32))
        hmt_ref[lo:lo + MLSTM_DV, :] = (h_t * scale * goutt_ref[lo:lo + MLSTM_DV, :] * og).astype(BF16)

        g_tot = y[L - 1:L, MLSTM_HEADS + h:MLSTM_HEADS + h + 1]
        a = g_tot - c_row
        m_new = jnp.maximum(g_tot + m_st, jnp.max(a, axis=1, keepdims=True))
        vtw = (vt_aug.astype(F32) * jnp.exp2(a - m_new)).astype(BF16)
        c_scr[h] = jnp.exp2(g_tot + m_st - m_new) * c_h + _dot(vtw, kh)
        m_scr[h] = jnp.broadcast_to(m_new, m_scr.shape[1:])


def _mlstm(q, k, vt, ot, gt, gout_t, B, S):
    L = MLSTM_CHUNK
    nc = S // L
    N = B * S
    row = lambda w: pl.BlockSpec((L, w), lambda b, c: (b * nc + c, 0))
    col = pl.BlockSpec((MLSTM_V_W, L), lambda b, c: (0, b * nc + c))
    return pl.pallas_call(
        _mlstm_kernel,
        grid=(B, nc),
        in_specs=[row(QK_PAD_W), row(QK_PAD_W), col, col, row(LANES), _resident((MLSTM_V_W, L))],
        out_specs=col,
        out_shape=jax.ShapeDtypeStruct((MLSTM_V_W, N), BF16),
        scratch_shapes=[pltpu.VMEM((MLSTM_HEADS, V_WIN, DQK_PAD), F32),
                        pltpu.VMEM((MLSTM_HEADS, 8, LANES), F32)],
        compiler_params=pltpu.CompilerParams(dimension_semantics=("arbitrary", "arbitrary"),
                                             vmem_limit_bytes=VMEM_LIMIT),
        name="mlstm",
    )(q, k, vt, ot, gt, gout_t)


def _mix_ffn_core(x_ref, main_ref, mo_ref, wom_ref, woe_ref, gpost_ref, gfpre_ref, gfpost_ref,
                  wi_ref, wd_ref, main_feature_major):
    main_dot = _dot_tn if main_feature_major else _dot
    mix = main_dot(main_ref[...], wom_ref[...]) + _dot(mo_ref[...], woe_ref[...])
    x1 = x_ref[...] + _rms(mix, gpost_ref[...])
    hf = _rms(x1, gfpre_ref[...]).astype(BF16)
    acc = None
    for c in range(N_FF_CHUNKS):
        lo = c * FF_CHUNK
        g = _dot(hf, wi_ref[:, lo:lo + FF_CHUNK])
        u = _dot(hf, wi_ref[:, D_FF + lo:D_FF + lo + FF_CHUNK])
        act = (g * _sigmoid(g) * u).astype(BF16)
        d = _dot(act, wd_ref[lo:lo + FF_CHUNK, :])
        acc = d if acc is None else acc + d
    return x1 + _rms(acc, gfpost_ref[...])


def _mix_ffn_last_kernel(x_ref, main_ref, mo_ref, wom_ref, woe_ref, gpost_ref, gfpre_ref, gfpost_ref,
                         wi_ref, wd_ref, xo_ref):
    xo_ref[...] = _mix_ffn_core(x_ref, main_ref, mo_ref, wom_ref, woe_ref, gpost_ref, gfpre_ref,
                                gfpost_ref, wi_ref, wd_ref, main_feature_major=False)


def _rope(t1, t2, cos, sin):
    return t1 * cos - t2 * sin, t2 * cos + t1 * sin


def _rope_tables(pos_ref, inv_ref):
    ang = pos_ref[...].astype(F32) * inv_ref[...]
    cos_d = jnp.cos(ang)
    sin_d = jnp.sin(ang)
    grp = lax.broadcasted_iota(jnp.int32, ang.shape, 1) >> 5

    def spread(t, g):
        y = jnp.where(grp == g, t, 0.0)
        y = y + pltpu.roll(y, HEAD_DIM // 2, 1)
        return y + pltpu.roll(y, HEAD_DIM, 1)

    n_grp = LANES // (HEAD_DIM // 2)
    cos = jnp.concatenate([spread(cos_d, g) for g in range(n_grp)], axis=0)
    sin = jnp.concatenate([spread(sin_d, g) for g in range(n_grp)], axis=0)
    return cos, sin


def _mix_ffn_first_kernel(x_ref, main_ref, mo_ref, wom_ref, woe_ref, gpost_ref, gfpre_ref, gfpost_ref,
                          wi_ref, wd_ref,
                          pos_ref, inv_ref, gkv_ref, wkv_ref, gpre1_ref, wb_ref, mk_ref, mv_ref,
                          xo_ref, ks_ref, vs_ref, q1_ref, mo1_ref):
    cos, sin = _rope_tables(pos_ref, inv_ref)
    x2 = _mix_ffn_core(x_ref, main_ref, mo_ref, wom_ref, woe_ref, gpost_ref, gfpre_ref, gfpost_ref,
                       wi_ref, wd_ref, main_feature_major=True)
    xo_ref[...] = x2
    xn = x2 * lax.rsqrt(jnp.mean(x2 * x2, axis=-1, keepdims=True) + EPS)

    kv = _dot((xn * gkv_ref[...]).astype(BF16), wkv_ref[...])
    k1, k2 = _rope(kv[:, 0:LANES], kv[:, LANES:2 * LANES], cos, sin)
    scale = HEAD_DIM ** -0.5 * LOG2E
    ks_ref[:, 0:LANES] = (k1 * scale).astype(BF16)
    ks_ref[:, LANES:2 * LANES] = (k2 * scale).astype(BF16)
    vs_ref[...] = kv[:, SWA_KV_W:].astype(BF16)

    pb = _dot((xn * gpre1_ref[...]).astype(BF16), wb_ref[...])
    for gi in range(SWA_GROUPS):
        base = gi * 2 * LANES
        t1, t2 = _rope(pb[:, base:base + LANES], pb[:, base + LANES:base + 2 * LANES], cos, sin)
        q1_ref[:, base:base + LANES] = t1.astype(BF16)
        q1_ref[:, base + LANES:base + 2 * LANES] = t2.astype(BF16)
    mq = pb[:, SWA_Q_W:].astype(BF16)
    mo1_ref[...] = _mem_attention(mq, mk_ref, mv_ref).astype(BF16)


def _mix_ffn(layer, x2d, main, mo, wom, woe, g_post, g_fpre, g_fpost, wi, wd, first=None, rows_per_batch=None):
    N = x2d.shape[0]
    tm = ROW_TILE
    row = lambda w: pl.BlockSpec((tm, w), lambda i: (i, 0))
    slab = lambda w: pl.BlockSpec((None,) + w.shape[1:], lambda i: (layer, 0, 0), pipeline_mode=pl.Buffered(1))
    main_spec = row(main.shape[1]) if first is None else pl.BlockSpec((main.shape[0], tm), lambda i: (0, i))
    in_specs = [row(D_MODEL), main_spec, row(MEM_W),
                _resident(wom.shape), _resident(woe.shape),
                _resident((1, D_MODEL)), _resident((1, D_MODEL)), _resident((1, D_MODEL)),
                slab(wi), slab(wd)]
    args = [x2d, main, mo, wom, woe, g_post, g_fpre, g_fpost, wi, wd]
    out_specs = [row(D_MODEL)]
    out_shape = [jax.ShapeDtypeStruct((N, D_MODEL), F32)]
    if first is None:
        body = _mix_ffn_last_kernel
        name = "mix_ffn_last"
    else:
        body = _mix_ffn_first_kernel
        name = "mix_ffn_first"
        pos, inv, g_kv, wkv, g_pre1, wb, mkm, mvm = first
        steps_per_batch = rows_per_batch // tm
        mem_spec = pl.BlockSpec((1, 1, MEM_HEADS, MEM_TOKENS, MEM_W),
                                lambda i: (1, i // steps_per_batch, 0, 0, 0))
        pos_spec = pl.BlockSpec((tm // (LANES // (HEAD_DIM // 2)), LANES), lambda i: (i, 0))
        in_specs += [pos_spec, _resident((1, LANES)), _resident((1, D_MODEL)), _resident(wkv.shape),
                     _resident((1, D_MODEL)), _resident(wb.shape), mem_spec, mem_spec]
        args += [pos, inv, g_kv, wkv, g_pre1, wb, mkm, mvm]
        out_specs += [row(SWA_KV_W), row(SWA_KV_W), row(SWA_Q_W), row(MEM_W)]
        out_shape += [jax.ShapeDtypeStruct((N, SWA_KV_W), BF16), jax.ShapeDtypeStruct((N, SWA_KV_W), BF16),
                      jax.ShapeDtypeStruct((N, SWA_Q_W), BF16), jax.ShapeDtypeStruct((N, MEM_W), BF16)]
    return pl.pallas_call(
        body,
        grid=(N // tm,),
        in_specs=in_specs,
        out_specs=out_specs,
        out_shape=out_shape,
        compiler_params=pltpu.CompilerParams(dimension_semantics=("arbitrary",),
                                             vmem_limit_bytes=VMEM_LIMIT),
        name=name,
    )(*args)


def _swa_kernel(sink_ref, q_ref, kp_ref, kc_ref, vp_ref, vc_ref, o_ref):
    W = SWA_WINDOW
    G = SWA_GROUPS
    n = pl.program_id(1)
    n_sub = q_ref.shape[0] // W
    H = SWA_KV_HEADS
    kpos = lax.broadcasted_iota(jnp.int32, (2 * W, G * W), 0)
    qcol = lax.broadcasted_iota(jnp.int32, (2 * W, G * W), 1)
    diff = (qcol & (W - 1)) + W - kpos
    in_band = (diff >= 0) & (diff < W)
    k_lane = lax.broadcasted_iota(jnp.int32, (2 * W, SWA_KV_W), 1)
    sink = sink_ref[...]
    for sub in range(n_sub):
        q = jnp.concatenate([q_ref[sub * W:(sub + 1) * W, gi * 2 * LANES:(gi + 1) * 2 * LANES]
                             for gi in range(G)], axis=0)
        if sub == 0:
            k = jnp.concatenate([kp_ref[...], kc_ref[0:W, :]], axis=0)
            v = jnp.concatenate([vp_ref[...], vc_ref[0:W, :]], axis=0)
            valid = in_band & ((kpos >= W) | (n > 0))
        else:
            k = kc_ref[(sub - 1) * W:(sub + 1) * W, :]
            v = vc_ref[(sub - 1) * W:(sub + 1) * W, :]
            valid = in_band
        km = jnp.concatenate([jnp.where(((k_lane & (LANES - 1)) >> 5) == j, k, jnp.zeros((), BF16))
                              for j in range(H)], axis=0)
        vm = jnp.concatenate([jnp.where((k_lane >> 6) == j, v, jnp.zeros((), BF16))
                              for j in range(H)], axis=0)
        s = _dot_nt(km, q).reshape(H, 2 * W, G * W)
        s = jnp.where(valid[None], s, -jnp.inf)
        m = jnp.maximum(jnp.max(s, axis=1, keepdims=True), sink)
        e = jnp.exp2(s - m)
        p = e * (1.0 / (jnp.sum(e, axis=1, keepdims=True) + jnp.exp2(sink - m)))
        out = _dot_tn(p.astype(BF16).reshape(H * 2 * W, G * W), vm)
        for gi in range(G):
            o_ref[sub * W:(sub + 1) * W, gi * 2 * LANES:(gi + 1) * 2 * LANES] = (
                out[gi * W:(gi + 1) * W].astype(BF16))


def _swa(sink_cols, q1, ks, vs, B, S):
    W = SWA_WINDOW
    n_sub = SWA_BLOCKS_PER_STEP
    tq = n_sub * W
    steps = S // tq
    cur = lambda w: pl.BlockSpec((tq, w), lambda b, n: (b * steps + n, 0))
    prev = lambda w: pl.BlockSpec((W, w), lambda b, n: (b * (S // W) + jnp.maximum(n_sub * n - 1, 0), 0))
    return pl.pallas_call(
        _swa_kernel,
        grid=(B, steps),
        in_specs=[_resident(sink_cols.shape),
                  cur(SWA_Q_W), prev(SWA_KV_W), cur(SWA_KV_W), prev(SWA_KV_W), cur(SWA_KV_W)],
        out_specs=cur(SWA_Q_W),
        out_shape=jax.ShapeDtypeStruct((B * S, SWA_Q_W), BF16),
        compiler_params=pltpu.CompilerParams(dimension_semantics=("arbitrary", "arbitrary"),
                                             vmem_limit_bytes=VMEM_LIMIT),
        name="swa",
    )(sink_cols, q1, ks, ks, vs, vs)


def _pad_heads(w, heads, width, padded):
    w = w.reshape(w.shape[0], heads, width)
    return jnp.pad(w, ((0, 0), (0, 0), (0, padded - width))).reshape(w.shape[0], heads * padded)


def _swa_q_perm():
    idx = np.empty((SWA_GROUPS, 2, SWA_KV_HEADS, HEAD_DIM // 2), np.int32)
    for gi in range(SWA_GROUPS):
        for half in range(2):
            for j in range(SWA_KV_HEADS):
                idx[gi, half, j] = (j * SWA_GROUPS + gi) * HEAD_DIM + half * (HEAD_DIM // 2) + np.arange(HEAD_DIM // 2)
    return idx.reshape(-1)


def _swa_k_perm():
    idx = np.empty((2, SWA_KV_HEADS, HEAD_DIM // 2), np.int32)
    for half in range(2):
        for j in range(SWA_KV_HEADS):
            idx[half, j] = j * HEAD_DIM + half * (HEAD_DIM // 2) + np.arange(HEAD_DIM // 2)
    return idx.reshape(-1)


def _swa_out_perm():
    idx = np.empty((SWA_GROUPS, SWA_KV_HEADS, HEAD_DIM), np.int32)
    for gi in range(SWA_GROUPS):
        for j in range(SWA_KV_HEADS):
            idx[gi, j] = (j * SWA_GROUPS + gi) * HEAD_DIM + np.arange(HEAD_DIM)
    return idx.reshape(-1)


def kernel(x, mem, positions, g_mix_pre, g_mix_post, g_ffn_pre, g_ffn_post, g_mem, w_mem_kv, w_out,
           w_ffn_in, w_ffn_out, w_in_a, b_gates_a, g_mlstm_out, g_kv, w_kv, w_in_b, sinks_b):
    B, S, _ = x.shape
    N = B * S
    assert S % ROW_TILE == 0 and S % MLSTM_CHUNK == 0 and S % (SWA_WINDOW * SWA_BLOCKS_PER_STEP) == 0
    x2d = x.reshape(N, D_MODEL)
    vec = lambda g: g.reshape(1, -1).astype(F32)

    mkm, mvm = _mem_kv(mem, g_mem, w_mem_kv.astype(BF16))

    wa = w_in_a[0]
    c0 = 0
    wq = _pad_heads(wa[:, c0:c0 + MLSTM_QK_W], MLSTM_HEADS, MLSTM_DQK, DQK_PAD).astype(BF16)
    c0 += MLSTM_QK_W
    wk = _pad_heads(wa[:, c0:c0 + MLSTM_QK_W], MLSTM_HEADS, MLSTM_DQK, DQK_PAD).astype(BF16)
    c0 += MLSTM_QK_W
    wv = wa[:, c0:c0 + MLSTM_V_W].T.astype(BF16)
    c0 += MLSTM_V_W
    wo = wa[:, c0:c0 + MLSTM_V_W].T.astype(BF16)
    c0 += MLSTM_V_W
    wgt = jnp.pad(wa[:, c0:c0 + 2 * MLSTM_HEADS], ((0, 0), (0, LANES - 2 * MLSTM_HEADS))).astype(BF16)
    c0 += 2 * MLSTM_HEADS
    wmq = wa[:, c0:c0 + MEM_W].astype(BF16)
    bg = jnp.pad(b_gates_a[0].astype(F32), (0, LANES - 2 * MLSTM_HEADS)).reshape(1, LANES)

    q, k, v, o, gt, mo0 = _proj_a(x2d, vec(g_mix_pre[0]), wq, wk, wv, wo, wmq, wgt, bg, mkm, mvm, S)
    gout_t = jnp.broadcast_to(g_mlstm_out[0].astype(F32)[:, None], (MLSTM_V_W, MLSTM_CHUNK))
    hm = _mlstm(q, k, v, o, gt, gout_t, B, S)

    w_ffn_in_bf = w_ffn_in.astype(BF16)
    w_ffn_out_bf = w_ffn_out.astype(BF16)
    wo0 = w_out[0].astype(BF16)
    inv = 1.0 / (ROPE_THETA ** (jnp.arange(0, HEAD_DIM, 2, dtype=F32) / HEAD_DIM))
    inv = jnp.tile(inv, SWA_KV_HEADS).reshape(1, LANES)
    wkv = jnp.concatenate([w_kv[:, :SWA_KV_W][:, _swa_k_perm()], w_kv[:, SWA_KV_W:]], axis=1).astype(BF16)
    wb = jnp.concatenate([w_in_b[0][:, :SWA_Q_W][:, _swa_q_perm()], w_in_b[0][:, SWA_Q_W:]], axis=1).astype(BF16)
    n_grp = LANES // (HEAD_DIM // 2)
    pos_dense = positions.reshape(N // ROW_TILE, n_grp, ROW_TILE // n_grp).transpose(0, 2, 1)
    pos_dense = jnp.repeat(pos_dense, HEAD_DIM // 2, axis=2).reshape(N // n_grp, LANES)
    first = (pos_dense, inv, vec(g_kv), wkv, vec(g_mix_pre[1]), wb, mkm, mvm)
    x1, ks, vs, q1, mo1 = _mix_ffn(0, x2d, hm, mo0, wo0[:MLSTM_V_W], wo0[MLSTM_V_W:],
                                   vec(g_mix_post[0]), vec(g_ffn_pre[0]), vec(g_ffn_post[0]),
                                   w_ffn_in_bf, w_ffn_out_bf, first=first, rows_per_batch=S)

    sink_rows = jnp.repeat((sinks_b[0].astype(F32) * LOG2E).reshape(SWA_KV_HEADS, SWA_GROUPS), SWA_WINDOW, axis=1)
    attn = _swa(sink_rows.reshape(SWA_KV_HEADS, 1, SWA_GROUPS * SWA_WINDOW), q1, ks, vs, B, S)
    wo1 = w_out[1]
    (xo,) = _mix_ffn(1, x1, attn, mo1, wo1[:SWA_Q_W][_swa_out_perm()].astype(BF16), wo1[SWA_Q_W:].astype(BF16),
                     vec(g_mix_post[1]), vec(g_ffn_pre[1]), vec(g_ffn_post[1]),
                     w_ffn_in_bf, w_ffn_out_bf)
    return xo.reshape(B, S, D_MODEL)
```

```python
import numpy as np
import jax
import jax.numpy as jnp
from jax import lax
from jax.experimental import pallas as pl
from jax.experimental.pallas import tpu as pltpu

F32 = jnp.float32
BF16 = jnp.bfloat16

D_MODEL = 1024
DEPTH = 2
HEAD_DIM = 64
EPS = 1e-6
ROPE_THETA = 10000.0
LOG2E = 1.4426950408889634

MLSTM_HEADS = 4
MLSTM_DV = 192
MLSTM_DQK = 96
MLSTM_QK_W = MLSTM_HEADS * MLSTM_DQK
MLSTM_V_W = MLSTM_HEADS * MLSTM_DV
GATE_SOFTCAP = 15.0
M_INIT = -1e30

SWA_Q_HEADS = 12
SWA_KV_HEADS = 4
SWA_GROUPS = SWA_Q_HEADS // SWA_KV_HEADS
SWA_Q_W = SWA_Q_HEADS * HEAD_DIM
SWA_KV_W = SWA_KV_HEADS * HEAD_DIM
SWA_WINDOW = 128

MEM_TOKENS = 256
MEM_HEADS = 4
MEM_HEAD_DIM = 64
MEM_W = MEM_HEADS * MEM_HEAD_DIM

D_FF = 2816

LANES = 128
MXU_TILE = 256

DQK_PAD = LANES
QK_PAD_W = MLSTM_HEADS * DQK_PAD
V_WIN = MXU_TILE
MLSTM_CHUNK = 256
FF_CHUNK = MXU_TILE
N_FF_CHUNKS = D_FF // FF_CHUNK
ROW_TILE = 512
SWA_BLOCKS_PER_STEP = 2
VMEM_LIMIT = 56 * 1024 * 1024

_V_WIN_START = (0, 128, 384, 512)
_V_WIN_OFF = (0, 64, 0, 64)
_V_ONES_COL = (192, 0, 192, 0)


def _rms_hat(x):
    return x * lax.rsqrt(jnp.mean(x * x, axis=-1, keepdims=True) + EPS)


def _rms(x, g):
    return _rms_hat(x) * g


def _dot(a, b):
    return jnp.dot(a, b, preferred_element_type=F32)


def _dot_nt(a, b):
    return lax.dot_general(a, b, (((1,), (1,)), ((), ())), preferred_element_type=F32)


def _dot_tn(a, b):
    return lax.dot_general(a, b, (((0,), (0,)), ((), ())), preferred_element_type=F32)


def _sigmoid(x):
    return 1.0 / (1.0 + jnp.exp(-x))


def _resident(shape):
    nd = len(shape)
    return pl.BlockSpec(shape, lambda *_: (0,) * nd, pipeline_mode=pl.Buffered(1))


def _mem_kv_kernel(mem_ref, g_ref, w_ref, mk_ref, mv_ref):
    hn = _rms(mem_ref[0], g_ref[0]).astype(BF16)
    kv = _dot(hn, w_ref[0])
    mk = kv[:, :MEM_W] * (MEM_HEAD_DIM ** -0.5 * LOG2E)
    mv = kv[:, MEM_W:]
    lane_head = lax.broadcasted_iota(jnp.int32, (MEM_TOKENS, MEM_W), 1) >> 6
    for h in range(MEM_HEADS):
        sel = lane_head == h
        mk_ref[0, 0, h] = jnp.where(sel, mk, 0.0).astype(BF16)
        mv_ref[0, 0, h] = jnp.where(sel, mv, 0.0).astype(BF16)


def _mem_kv(mem, g_mem, w_mem_kv_bf16):
    B = mem.shape[0]
    out_sds = jax.ShapeDtypeStruct((DEPTH, B, MEM_HEADS, MEM_TOKENS, MEM_W), BF16)
    out_spec = pl.BlockSpec((1, 1, MEM_HEADS, MEM_TOKENS, MEM_W), lambda l, b: (l, b, 0, 0, 0))
    return pl.pallas_call(
        _mem_kv_kernel,
        grid=(DEPTH, B),
        in_specs=[
            pl.BlockSpec((1, MEM_TOKENS, D_MODEL), lambda l, b: (b, 0, 0)),
            pl.BlockSpec((1, 1, D_MODEL), lambda l, b: (l, 0, 0)),
            pl.BlockSpec((1, D_MODEL, 2 * MEM_W), lambda l, b: (l, 0, 0)),
        ],
        out_specs=[out_spec, out_spec],
        out_shape=[out_sds, out_sds],
        compiler_params=pltpu.CompilerParams(dimension_semantics=("arbitrary", "arbitrary")),
        name="mem_kv",
    )(mem, g_mem.reshape(DEPTH, 1, D_MODEL), w_mem_kv_bf16)


def _mem_scores(mq, mk_ref):
    return [_dot_nt(mq, mk_ref[0, 0, h]) for h in range(MEM_HEADS)]


def _mem_probs(s):
    e = [jnp.exp2(sh - jnp.max(sh, axis=-1, keepdims=True)) for sh in s]
    return [(eh * (1.0 / jnp.sum(eh, axis=-1, keepdims=True))).astype(BF16) for eh in e]


def _mem_out(p, mv_ref):
    out = _dot(p[0], mv_ref[0, 0, 0])
    for h in range(1, MEM_HEADS):
        out = out + _dot(p[h], mv_ref[0, 0, h])
    return out


def _proj_a_kernel(x_ref, wq_ref, wk_ref, wv_ref, wo_ref, wmq_ref, wg_ref, bg_ref, mk_ref, mv_ref,
                   q_ref, k_ref, v_ref, o_ref, gt_ref, mo_ref):
    hn = _rms_hat(x_ref[...]).astype(BF16)
    mq = _dot(hn, wmq_ref[...]).astype(BF16)
    s = _mem_scores(mq, mk_ref)
    q_ref[...] = _dot(hn, wq_ref[...]).astype(BF16)
    gates = _dot(hn, wg_ref[...]) + bg_ref[...]
    p = _mem_probs(s)
    k_ref[...] = (_dot(hn, wk_ref[...]) * (MLSTM_DQK ** -0.5)).astype(BF16)
    gates = GATE_SOFTCAP * jnp.tanh(gates * (1.0 / GATE_SOFTCAP))
    log_sig = jnp.minimum(gates, 0.0) - jnp.log1p(jnp.exp(-jnp.abs(gates)))
    lane = lax.broadcasted_iota(jnp.int32, gates.shape, 1)
    gt_ref[...] = jnp.where(lane < MLSTM_HEADS, gates, log_sig) * LOG2E
    v_ref[...] = _dot_nt(wv_ref[...], hn).astype(BF16)
    mo_ref[...] = _mem_out(p, mv_ref).astype(BF16)
    o_ref[...] = _dot_nt(wo_ref[...], hn).astype(BF16)


def _proj_a(x2d, wq, wk, wv, wo, wmq, wg, bg, mkm, mvm, rows_per_batch):
    N = x2d.shape[0]
    tm = ROW_TILE
    steps_per_batch = rows_per_batch // tm
    row = lambda w: pl.BlockSpec((tm, w), lambda i: (i, 0))
    col = pl.BlockSpec((MLSTM_V_W, tm), lambda i: (0, i))
    mem_spec = pl.BlockSpec((1, 1, MEM_HEADS, MEM_TOKENS, MEM_W),
                            lambda i: (0, i // steps_per_batch, 0, 0, 0))
    return pl.pallas_call(
        _proj_a_kernel,
        grid=(N // tm,),
        in_specs=[row(D_MODEL),
                  _resident(wq.shape), _resident(wk.shape), _resident(wv.shape), _resident(wo.shape),
                  _resident(wmq.shape), _resident(wg.shape), _resident((1, LANES)),
                  mem_spec, mem_spec],
        out_specs=[row(QK_PAD_W), row(QK_PAD_W), col, col, row(LANES), row(MEM_W)],
        out_shape=[jax.ShapeDtypeStruct((N, QK_PAD_W), BF16), jax.ShapeDtypeStruct((N, QK_PAD_W), BF16),
                   jax.ShapeDtypeStruct((MLSTM_V_W, N), BF16), jax.ShapeDtypeStruct((MLSTM_V_W, N), BF16),
                   jax.ShapeDtypeStruct((N, LANES), F32), jax.ShapeDtypeStruct((N, MEM_W), BF16)],
        compiler_params=pltpu.CompilerParams(dimension_semantics=("arbitrary",),
                                             vmem_limit_bytes=VMEM_LIMIT),
        name="proj_a",
    )(x2d, wq, wk, wv, wo, wmq, wg, bg, mkm, mvm)


def _mlstm_kernel(q_ref, k_ref, vt_ref, ot_ref, gt_ref, goutt_ref, hmt_ref, c_scr, m_scr):
    L = q_ref.shape[0]

    @pl.when(pl.program_id(1) == 0)
    def _():
        c_scr[...] = jnp.zeros(c_scr.shape, F32)
        m_scr[...] = jnp.full(m_scr.shape, M_INIT, F32)

    src = lax.broadcasted_iota(jnp.int32, (L, L), 0)
    tgt = lax.broadcasted_iota(jnp.int32, (L, L), 1)
    causal = src <= tgt
    gates = gt_ref[...]
    csum = jnp.dot((tgt <= src).astype(F32), gates, precision=lax.Precision.HIGHEST,
                   preferred_element_type=F32)
    lane = lax.broadcasted_iota(jnp.int32, (L, LANES), 1)
    y = jnp.where(lane < MLSTM_HEADS, gates, csum)
    yt = y.T
    feat = lax.broadcasted_iota(jnp.int32, (V_WIN, L), 0)

    heads = range(MLSTM_HEADS)
    b_row = [yt[MLSTM_HEADS + h:MLSTM_HEADS + h + 1, :] for h in heads]
    c_row = [b_row[h] - yt[h:h + 1, :] for h in heads]
    c_col = [y[:, MLSTM_HEADS + h:MLSTM_HEADS + h + 1] - y[:, h:h + 1] for h in heads]
    g_tot = [y[L - 1:L, MLSTM_HEADS + h:MLSTM_HEADS + h + 1] for h in heads]
    m_st = [m_scr[h][0:1, 0:1] for h in heads]
    c_st = [c_scr[h] for h in heads]
    qh = [q_ref[:, h * DQK_PAD:(h + 1) * DQK_PAD] for h in heads]
    kh = [k_ref[:, h * DQK_PAD:(h + 1) * DQK_PAD] for h in heads]
    vt_aug = [jnp.where(feat == _V_ONES_COL[h], jnp.ones((), BF16),
                        vt_ref[_V_WIN_START[h]:_V_WIN_START[h] + V_WIN, :]) for h in heads]

    qk_t = [_dot_nt(kh[h], qh[h]) for h in heads]
    inter_t = [_dot_nt(c_st[h].astype(BF16), qh[h]) for h in heads]
    dmat = [jnp.where(causal, b_row[h] - c_col[h], -jnp.inf) for h in heads]
    inter_log = [b_row[h] + m_st[h] for h in heads]
    m_row = [jnp.maximum(inter_log[h], jnp.max(dmat[h], axis=0, keepdims=True)) for h in heads]
    scores_t = [(qk_t[h] * jnp.exp2(dmat[h] - m_row[h])).astype(BF16) for h in heads]
    num_t = [_dot(vt_aug[h], scores_t[h]) + jnp.exp2(inter_log[h] - m_row[h]) * inter_t[h] for h in heads]

    a = [g_tot[h] - c_row[h] for h in heads]
    m_new = [jnp.maximum(g_tot[h] + m_st[h], jnp.max(a[h], axis=1, keepdims=True)) for h in heads]
    vtw = [(vt_aug[h].astype(F32) * jnp.exp2(a[h] - m_new[h])).astype(BF16) for h in heads]
    c_upd = [_dot(vtw[h], kh[h]) for h in heads]

    for h in heads:
        off, ones_row, lo = _V_WIN_OFF[h], _V_ONES_COL[h], h * MLSTM_DV
        den = num_t[h][ones_row:ones_row + 1, :]
        inv_dd = 1.0 / jnp.maximum(jnp.abs(den), jnp.exp2(-m_row[h]))
        h_t = num_t[h][off:off + MLSTM_DV, :]
        ms = jnp.sum(h_t * h_t, axis=0, keepdims=True) * (1.0 / MLSTM_DV)
        scale = inv_dd * lax.rsqrt(ms * inv_dd * inv_dd + EPS)
        og = _sigmoid(ot_ref[lo:lo + MLSTM_DV, :].astype(F32))
        hmt_ref[lo:lo + MLSTM_DV, :] = (h_t * scale * goutt_ref[lo:lo + MLSTM_DV, :] * og).astype(BF16)

    for h in heads:
        c_scr[h] = jnp.exp2(g_tot[h] + m_st[h] - m_new[h]) * c_st[h] + c_upd[h]
        m_scr[h] = jnp.broadcast_to(m_new[h], m_scr.shape[1:])


def _mlstm(q, k, vt, ot, gt, gout_t, B, S):
    L = MLSTM_CHUNK
    nc = S // L
    N = B * S
    row = lambda w: pl.BlockSpec((L, w), lambda b, c: (b * nc + c, 0))
    col = pl.BlockSpec((MLSTM_V_W, L), lambda b, c: (0, b * nc + c))
    return pl.pallas_call(
        _mlstm_kernel,
        grid=(B, nc),
        in_specs=[row(QK_PAD_W), row(QK_PAD_W), col, col, row(LANES), _resident((MLSTM_V_W, L))],
        out_specs=col,
        out_shape=jax.ShapeDtypeStruct((MLSTM_V_W, N), BF16),
        scratch_shapes=[pltpu.VMEM((MLSTM_HEADS, V_WIN, DQK_PAD), F32),
                        pltpu.VMEM((MLSTM_HEADS, 8, LANES), F32)],
        compiler_params=pltpu.CompilerParams(dimension_semantics=("arbitrary", "arbitrary"),
                                             vmem_limit_bytes=VMEM_LIMIT),
        name="mlstm",
    )(q, k, vt, ot, gt, gout_t)


def _mix_ffn_core(x_ref, main_ref, mo_ref, wom_ref, woe_ref, gpost_ref, gfpost_ref,
                  wi_ref, wd_ref, main_feature_major):
    main_dot = _dot_tn if main_feature_major else _dot
    mix = main_dot(main_ref[...], wom_ref[...]) + _dot(mo_ref[...], woe_ref[...])
    x1 = x_ref[...] + _rms(mix, gpost_ref[...])
    hf = _rms_hat(x1).astype(BF16)
    acc = None
    for c in range(N_FF_CHUNKS):
        lo = c * FF_CHUNK
        g = _dot(hf, wi_ref[:, lo:lo + FF_CHUNK])
        u = _dot(hf, wi_ref[:, D_FF + lo:D_FF + lo + FF_CHUNK])
        act = (g * _sigmoid(g) * u).astype(BF16)
        d = _dot(act, wd_ref[lo:lo + FF_CHUNK, :])
        acc = d if acc is None else acc + d
    return x1 + _rms(acc, gfpost_ref[...])


def _mix_ffn_last_kernel(x_ref, main_ref, mo_ref, wom_ref, woe_ref, gpost_ref, gfpost_ref,
                         wi_ref, wd_ref, xo_ref):
    xo_ref[...] = _mix_ffn_core(x_ref, main_ref, mo_ref, wom_ref, woe_ref, gpost_ref,
                                gfpost_ref, wi_ref, wd_ref, main_feature_major=False)


def _rope(t1, t2, cos, sin):
    return t1 * cos - t2 * sin, t2 * cos + t1 * sin


def _rope_tables(pos_ref, inv_ref):
    ang = pos_ref[...].astype(F32) * inv_ref[...]
    cos_d = jnp.cos(ang)
    sin_d = jnp.sin(ang)
    grp = lax.broadcasted_iota(jnp.int32, ang.shape, 1) >> 5

    def spread(t, g):
        y = jnp.where(grp == g, t, 0.0)
        y = y + pltpu.roll(y, HEAD_DIM // 2, 1)
        return y + pltpu.roll(y, HEAD_DIM, 1)

    n_grp = LANES // (HEAD_DIM // 2)
    cos = jnp.concatenate([spread(cos_d, g) for g in range(n_grp)], axis=0)
    sin = jnp.concatenate([spread(sin_d, g) for g in range(n_grp)], axis=0)
    return cos, sin


def _mix_ffn_first_kernel(x_ref, main_ref, mo_ref, wom_ref, woe_ref, gpost_ref, gfpost_ref,
                          wi_ref, wd_ref,
                          pos_ref, inv_ref, wkv_ref, wb_ref, mk_ref, mv_ref,
                          xo_ref, ks_ref, vs_ref, q1_ref, mo1_ref):
    cos, sin = _rope_tables(pos_ref, inv_ref)
    x2 = _mix_ffn_core(x_ref, main_ref, mo_ref, wom_ref, woe_ref, gpost_ref, gfpost_ref,
                       wi_ref, wd_ref, main_feature_major=True)
    xo_ref[...] = x2
    xn = _rms_hat(x2).astype(BF16)

    pb = _dot(xn, wb_ref[...])
    s = _mem_scores(pb[:, SWA_Q_W:].astype(BF16), mk_ref)
    kv = _dot(xn, wkv_ref[...])
    p = _mem_probs(s)

    for gi in range(SWA_GROUPS):
        base = gi * 2 * LANES
        t1, t2 = _rope(pb[:, base:base + LANES], pb[:, base + LANES:base + 2 * LANES], cos, sin)
        q1_ref[:, base:base + LANES] = t1.astype(BF16)
        q1_ref[:, base + LANES:base + 2 * LANES] = t2.astype(BF16)
    mo1_ref[...] = _mem_out(p, mv_ref).astype(BF16)

    k1, k2 = _rope(kv[:, 0:LANES], kv[:, LANES:2 * LANES], cos, sin)
    scale = HEAD_DIM ** -0.5 * LOG2E
    ks_ref[:, 0:LANES] = (k1 * scale).astype(BF16)
    ks_ref[:, LANES:2 * LANES] = (k2 * scale).astype(BF16)
    vs_ref[...] = kv[:, SWA_KV_W:].astype(BF16)


def _mix_ffn(layer, x2d, main, mo, wom, woe, g_post, g_fpost, wi, wd, first=None, rows_per_batch=None):
    N = x2d.shape[0]
    tm = ROW_TILE
    row = lambda w: pl.BlockSpec((tm, w), lambda i: (i, 0))
    slab = lambda w: pl.BlockSpec((None,) + w.shape[1:], lambda i: (layer, 0, 0), pipeline_mode=pl.Buffered(1))
    main_spec = row(main.shape[1]) if first is None else pl.BlockSpec((main.shape[0], tm), lambda i: (0, i))
    in_specs = [row(D_MODEL), main_spec, row(MEM_W),
                _resident(wom.shape), _resident(woe.shape),
                _resident((1, D_MODEL)), _resident((1, D_MODEL)),
                slab(wi), slab(wd)]
    args = [x2d, main, mo, wom, woe, g_post, g_fpost, wi, wd]
    out_specs = [row(D_MODEL)]
    out_shape = [jax.ShapeDtypeStruct((N, D_MODEL), F32)]
    if first is None:
        body = _mix_ffn_last_kernel
        name = "mix_ffn_last"
    else:
        body = _mix_ffn_first_kernel
        name = "mix_ffn_first"
        pos, inv, wkv, wb, mkm, mvm = first
        steps_per_batch = rows_per_batch // tm
        mem_spec = pl.BlockSpec((1, 1, MEM_HEADS, MEM_TOKENS, MEM_W),
                                lambda i: (1, i // steps_per_batch, 0, 0, 0))
        pos_spec = pl.BlockSpec((tm // (LANES // (HEAD_DIM // 2)), LANES), lambda i: (i, 0))
        in_specs += [pos_spec, _resident((1, LANES)), _resident(wkv.shape), _resident(wb.shape),
                     mem_spec, mem_spec]
        args += [pos, inv, wkv, wb, mkm, mvm]
        out_specs += [row(SWA_KV_W), row(SWA_KV_W), row(SWA_Q_W), row(MEM_W)]
        out_shape += [jax.ShapeDtypeStruct((N, SWA_KV_W), BF16), jax.ShapeDtypeStruct((N, SWA_KV_W), BF16),
                      jax.ShapeDtypeStruct((N, SWA_Q_W), BF16), jax.ShapeDtypeStruct((N, MEM_W), BF16)]
    return pl.pallas_call(
        body,
        grid=(N // tm,),
        in_specs=in_specs,
        out_specs=out_specs,
        out_shape=out_shape,
        compiler_params=pltpu.CompilerParams(dimension_semantics=("arbitrary",),
                                             vmem_limit_bytes=VMEM_LIMIT),
        name=name,
    )(*args)


def _swa_kernel(sink_ref, q_ref, kp_ref, kc_ref, vp_ref, vc_ref, o_ref):
    W = SWA_WINDOW
    G = SWA_GROUPS
    n = pl.program_id(1)
    n_sub = q_ref.shape[0] // W
    H = SWA_KV_HEADS
    kpos = lax.broadcasted_iota(jnp.int32, (2 * W, G * W), 0)
    qcol = lax.broadcasted_iota(jnp.int32, (2 * W, G * W), 1)
    diff = (qcol & (W - 1)) + W - kpos
    in_band = (diff >= 0) & (diff < W)
    k_lane = lax.broadcasted_iota(jnp.int32, (2 * W, SWA_KV_W), 1)
    sink = sink_ref[...]
    subs = range(n_sub)
    q, km, vm, valid = [], [], [], []
    for sub in subs:
        q.append(jnp.concatenate([q_ref[sub * W:(sub + 1) * W, gi * 2 * LANES:(gi + 1) * 2 * LANES]
                                  for gi in range(G)], axis=0))
        if sub == 0:
            k = jnp.concatenate([kp_ref[...], kc_ref[0:W, :]], axis=0)
            v = jnp.concatenate([vp_ref[...], vc_ref[0:W, :]], axis=0)
            valid.append(in_band & ((kpos >= W) | (n > 0)))
        else:
            k = kc_ref[(sub - 1) * W:(sub + 1) * W, :]
            v = vc_ref[(sub - 1) * W:(sub + 1) * W, :]
            valid.append(in_band)
        km.append(jnp.concatenate([jnp.where(((k_lane & (LANES - 1)) >> 5) == j, k, jnp.zeros((), BF16))
                                   for j in range(H)], axis=0))
        vm.append(jnp.concatenate([jnp.where((k_lane >> 6) == j, v, jnp.zeros((), BF16))
                                   for j in range(H)], axis=0))
    s = [_dot_nt(km[i], q[i]).reshape(H, 2 * W, G * W) for i in subs]
    s = [jnp.where(valid[i][None], s[i], -jnp.inf) for i in subs]
    m = [jnp.maximum(jnp.max(s[i], axis=1, keepdims=True), sink) for i in subs]
    e = [jnp.exp2(s[i] - m[i]) for i in subs]
    p = [e[i] * (1.0 / (jnp.sum(e[i], axis=1, keepdims=True) + jnp.exp2(sink - m[i]))) for i in subs]
    out = [_dot_tn(p[i].astype(BF16).reshape(H * 2 * W, G * W), vm[i]) for i in subs]
    for sub in subs:
        for gi in range(G):
            o_ref[sub * W:(sub + 1) * W, gi * 2 * LANES:(gi + 1) * 2 * LANES] = (
                out[sub][gi * W:(gi + 1) * W].astype(BF16))


def _swa(sink_cols, q1, ks, vs, B, S):
    W = SWA_WINDOW
    n_sub = SWA_BLOCKS_PER_STEP
    tq = n_sub * W
    steps = S // tq
    cur = lambda w: pl.BlockSpec((tq, w), lambda b, n: (b * steps + n, 0))
    prev = lambda w: pl.BlockSpec((W, w), lambda b, n: (b * (S // W) + jnp.maximum(n_sub * n - 1, 0), 0))
    return pl.pallas_call(
        _swa_kernel,
        grid=(B, steps),
        in_specs=[_resident(sink_cols.shape),
                  cur(SWA_Q_W), prev(SWA_KV_W), cur(SWA_KV_W), prev(SWA_KV_W), cur(SWA_KV_W)],
        out_specs=cur(SWA_Q_W),
        out_shape=jax.ShapeDtypeStruct((B * S, SWA_Q_W), BF16),
        compiler_params=pltpu.CompilerParams(dimension_semantics=("arbitrary", "arbitrary"),
                                             vmem_limit_bytes=VMEM_LIMIT),
        name="swa",
    )(sink_cols, q1, ks, ks, vs, vs)


def _pad_heads(w, heads, width, padded):
    w = w.reshape(w.shape[0], heads, width)
    return jnp.pad(w, ((0, 0), (0, 0), (0, padded - width))).reshape(w.shape[0], heads * padded)


def _swa_q_perm():
    idx = np.empty((SWA_GROUPS, 2, SWA_KV_HEADS, HEAD_DIM // 2), np.int32)
    for gi in range(SWA_GROUPS):
        for half in range(2):
            for j in range(SWA_KV_HEADS):
                idx[gi, half, j] = (j * SWA_GROUPS + gi) * HEAD_DIM + half * (HEAD_DIM // 2) + np.arange(HEAD_DIM // 2)
    return idx.reshape(-1)


def _swa_k_perm():
    idx = np.empty((2, SWA_KV_HEADS, HEAD_DIM // 2), np.int32)
    for half in range(2):
        for j in range(SWA_KV_HEADS):
            idx[half, j] = j * HEAD_DIM + half * (HEAD_DIM // 2) + np.arange(HEAD_DIM // 2)
    return idx.reshape(-1)


def _swa_out_perm():
    idx = np.empty((SWA_GROUPS, SWA_KV_HEADS, HEAD_DIM), np.int32)
    for gi in range(SWA_GROUPS):
        for j in range(SWA_KV_HEADS):
            idx[gi, j] = (j * SWA_GROUPS + gi) * HEAD_DIM + np.arange(HEAD_DIM)
    return idx.reshape(-1)


def kernel(x, mem, positions, g_mix_pre, g_mix_post, g_ffn_pre, g_ffn_post, g_mem, w_mem_kv, w_out,
           w_ffn_in, w_ffn_out, w_in_a, b_gates_a, g_mlstm_out, g_kv, w_kv, w_in_b, sinks_b):
    B, S, _ = x.shape
    N = B * S
    assert S % ROW_TILE == 0 and S % MLSTM_CHUNK == 0 and S % (SWA_WINDOW * SWA_BLOCKS_PER_STEP) == 0
    x2d = x.reshape(N, D_MODEL)
    vec = lambda g: g.reshape(1, -1).astype(F32)

    mkm, mvm = _mem_kv(mem, g_mem, w_mem_kv.astype(BF16))

    fold = lambda g, w: g.astype(F32)[:, None] * w

    wa = fold(g_mix_pre[0], w_in_a[0])
    c0 = 0
    wq = _pad_heads(wa[:, c0:c0 + MLSTM_QK_W], MLSTM_HEADS, MLSTM_DQK, DQK_PAD).astype(BF16)
    c0 += MLSTM_QK_W
    wk = _pad_heads(wa[:, c0:c0 + MLSTM_QK_W], MLSTM_HEADS, MLSTM_DQK, DQK_PAD).astype(BF16)
    c0 += MLSTM_QK_W
    wv = wa[:, c0:c0 + MLSTM_V_W].T.astype(BF16)
    c0 += MLSTM_V_W
    wo = wa[:, c0:c0 + MLSTM_V_W].T.astype(BF16)
    c0 += MLSTM_V_W
    wgt = jnp.pad(wa[:, c0:c0 + 2 * MLSTM_HEADS], ((0, 0), (0, LANES - 2 * MLSTM_HEADS))).astype(BF16)
    c0 += 2 * MLSTM_HEADS
    wmq = wa[:, c0:c0 + MEM_W].astype(BF16)
    bg = jnp.pad(b_gates_a[0].astype(F32), (0, LANES - 2 * MLSTM_HEADS)).reshape(1, LANES)

    q, k, v, o, gt, mo0 = _proj_a(x2d, wq, wk, wv, wo, wmq, wgt, bg, mkm, mvm, S)
    gout_t = jnp.broadcast_to(g_mlstm_out[0].astype(F32)[:, None], (MLSTM_V_W, MLSTM_CHUNK))
    hm = _mlstm(q, k, v, o, gt, gout_t, B, S)

    w_ffn_in_bf = (g_ffn_pre.astype(F32)[:, :, None] * w_ffn_in).astype(BF16)
    w_ffn_out_bf = w_ffn_out.astype(BF16)
    wo0 = w_out[0].astype(BF16)
    inv = 1.0 / (ROPE_THETA ** (jnp.arange(0, HEAD_DIM, 2, dtype=F32) / HEAD_DIM))
    inv = jnp.tile(inv, SWA_KV_HEADS).reshape(1, LANES)
    wkv = fold(g_kv, w_kv)
    wkv = jnp.concatenate([wkv[:, :SWA_KV_W][:, _swa_k_perm()], wkv[:, SWA_KV_W:]], axis=1).astype(BF16)
    wb = fold(g_mix_pre[1], w_in_b[0])
    wb = jnp.concatenate([wb[:, :SWA_Q_W][:, _swa_q_perm()], wb[:, SWA_Q_W:]], axis=1).astype(BF16)
    n_grp = LANES // (HEAD_DIM // 2)
    pos_dense = positions.reshape(N // ROW_TILE, n_grp, ROW_TILE // n_grp).transpose(0, 2, 1)
    pos_dense = jnp.repeat(pos_dense, HEAD_DIM // 2, axis=2).reshape(N // n_grp, LANES)
    first = (pos_dense, inv, wkv, wb, mkm, mvm)
    x1, ks, vs, q1, mo1 = _mix_ffn(0, x2d, hm, mo0, wo0[:MLSTM_V_W], wo0[MLSTM_V_W:],
                                   vec(g_mix_post[0]), vec(g_ffn_post[0]),
                                   w_ffn_in_bf, w_ffn_out_bf, first=first, rows_per_batch=S)

    sink_rows = jnp.repeat((sinks_b[0].astype(F32) * LOG2E).reshape(SWA_KV_HEADS, SWA_GROUPS), SWA_WINDOW, axis=1)
    attn = _swa(sink_rows.reshape(SWA_KV_HEADS, 1, SWA_GROUPS * SWA_WINDOW), q1, ks, vs, B, S)
    wo1 = w_out[1]
    (xo,) = _mix_ffn(1, x1, attn, mo1, wo1[:SWA_Q_W][_swa_out_perm()].astype(BF16), wo1[SWA_Q_W:].astype(BF16),
                     vec(g_mix_post[1]), vec(g_ffn_post[1]),
                     w_ffn_in_bf, w_ffn_out_bf)
    return xo.reshape(B, S, D_MODEL)
```

```python
import numpy as np
import jax
import jax.numpy as jnp
from jax import lax
from jax.experimental import pallas as pl
from jax.experimental.pallas import tpu as pltpu

F32 = jnp.float32
BF16 = jnp.bfloat16

D_MODEL = 1024
DEPTH = 2
HEAD_DIM = 64
EPS = 1e-6
ROPE_THETA = 10000.0
LOG2E = 1.4426950408889634

MLSTM_HEADS = 4
MLSTM_DV = 192
MLSTM_DQK = 96
MLSTM_QK_W = MLSTM_HEADS * MLSTM_DQK
MLSTM_V_W = MLSTM_HEADS * MLSTM_DV
GATE_SOFTCAP = 15.0
M_INIT = -1e30

SWA_Q_HEADS = 12
SWA_KV_HEADS = 4
SWA_GROUPS = SWA_Q_HEADS // SWA_KV_HEADS
SWA_Q_W = SWA_Q_HEADS * HEAD_DIM
SWA_KV_W = SWA_KV_HEADS * HEAD_DIM
SWA_WINDOW = 128

MEM_TOKENS = 256
MEM_HEADS = 4
MEM_HEAD_DIM = 64
MEM_W = MEM_HEADS * MEM_HEAD_DIM

D_FF = 2816

LANES = 128
MXU_TILE = 256

DQK_PAD = LANES
QK_PAD_W = MLSTM_HEADS * DQK_PAD
V_WIN = MXU_TILE
MLSTM_CHUNK = 256
FF_CHUNK = MXU_TILE
N_FF_CHUNKS = D_FF // FF_CHUNK
ROW_TILE = 512
SWA_BLOCKS_PER_STEP = 2
VMEM_LIMIT = 56 * 1024 * 1024

_V_WIN_START = (0, 128, 384, 512)
_V_WIN_OFF = (0, 64, 0, 64)
_V_ONES_COL = (192, 0, 192, 0)


def _rms_hat(x):
    return x * lax.rsqrt(jnp.mean(x * x, axis=-1, keepdims=True) + EPS)


def _rms(x, g):
    return _rms_hat(x) * g


def _dot(a, b):
    return jnp.dot(a, b, preferred_element_type=F32)


def _dot_nt(a, b):
    return lax.dot_general(a, b, (((1,), (1,)), ((), ())), preferred_element_type=F32)


def _dot_tn(a, b):
    return lax.dot_general(a, b, (((0,), (0,)), ((), ())), preferred_element_type=F32)


def _sigmoid(x):
    return 1.0 / (1.0 + jnp.exp(-x))


def _resident(shape):
    nd = len(shape)
    return pl.BlockSpec(shape, lambda *_: (0,) * nd, pipeline_mode=pl.Buffered(1))


def _mem_kv_kernel(mem_ref, g_ref, w_ref, mk_ref, mv_ref):
    hn = _rms(mem_ref[0], g_ref[0]).astype(BF16)
    kv = _dot(hn, w_ref[0])
    mk = kv[:, :MEM_W] * (MEM_HEAD_DIM ** -0.5 * LOG2E)
    mv = kv[:, MEM_W:]
    lane_head = lax.broadcasted_iota(jnp.int32, (MEM_TOKENS, MEM_W), 1) >> 6
    for h in range(MEM_HEADS):
        sel = lane_head == h
        mk_ref[0, 0, h] = jnp.where(sel, mk, 0.0).astype(BF16)
        mv_ref[0, 0, h] = jnp.where(sel, mv, 0.0).astype(BF16)


def _mem_kv(mem, g_mem, w_mem_kv_bf16):
    B = mem.shape[0]
    out_sds = jax.ShapeDtypeStruct((DEPTH, B, MEM_HEADS, MEM_TOKENS, MEM_W), BF16)
    out_spec = pl.BlockSpec((1, 1, MEM_HEADS, MEM_TOKENS, MEM_W), lambda l, b: (l, b, 0, 0, 0))
    return pl.pallas_call(
        _mem_kv_kernel,
        grid=(DEPTH, B),
        in_specs=[
            pl.BlockSpec((1, MEM_TOKENS, D_MODEL), lambda l, b: (b, 0, 0)),
            pl.BlockSpec((1, 1, D_MODEL), lambda l, b: (l, 0, 0)),
            pl.BlockSpec((1, D_MODEL, 2 * MEM_W), lambda l, b: (l, 0, 0)),
        ],
        out_specs=[out_spec, out_spec],
        out_shape=[out_sds, out_sds],
        compiler_params=pltpu.CompilerParams(dimension_semantics=("arbitrary", "arbitrary")),
        name="mem_kv",
    )(mem, g_mem.reshape(DEPTH, 1, D_MODEL), w_mem_kv_bf16)


def _mem_scores(mq, mk_ref):
    return [_dot_nt(mq, mk_ref[0, 0, h]) for h in range(MEM_HEADS)]


def _mem_probs(s):
    e = [jnp.exp2(sh - jnp.max(sh, axis=-1, keepdims=True)) for sh in s]
    return [(eh * (1.0 / jnp.sum(eh, axis=-1, keepdims=True))).astype(BF16) for eh in e]


def _mem_out(p, mv_ref):
    out = _dot(p[0], mv_ref[0, 0, 0])
    for h in range(1, MEM_HEADS):
        out = out + _dot(p[h], mv_ref[0, 0, h])
    return out


def _proj_a_kernel(x_ref, wq_ref, wk_ref, wv_ref, wo_ref, wmq_ref, wg_ref, bg_ref, mk_ref, mv_ref,
                   q_ref, k_ref, v_ref, o_ref, gt_ref, gtt_ref, mo_ref):
    hn = _rms_hat(x_ref[...]).astype(BF16)
    mq = _dot(hn, wmq_ref[...]).astype(BF16)
    s = _mem_scores(mq, mk_ref)
    gates = _dot(hn, wg_ref[...]) + bg_ref[...]
    q_ref[...] = _dot(hn, wq_ref[...]).astype(BF16)
    gates = GATE_SOFTCAP * jnp.tanh(gates * (1.0 / GATE_SOFTCAP))
    log_sig = jnp.minimum(gates, 0.0) - jnp.log1p(jnp.exp(-jnp.abs(gates)))
    lane = lax.broadcasted_iota(jnp.int32, gates.shape, 1)
    gl = jnp.where(lane < MLSTM_HEADS, gates, log_sig) * LOG2E
    p = _mem_probs(s)
    k_ref[...] = _dot(hn, wk_ref[...]).astype(BF16)
    g_hi = gl.astype(BF16)
    rest = gl - g_hi.astype(F32)
    g_mid = rest.astype(BF16)
    g_lo = (rest - g_mid.astype(F32)).astype(BF16)
    L = MLSTM_CHUNK
    tri = (lax.broadcasted_iota(jnp.int32, (L, L), 1) <= lax.broadcasted_iota(jnp.int32, (L, L), 0)).astype(BF16)
    csum = jnp.concatenate(
        [_dot(tri, g_hi[r:r + L]) + _dot(tri, g_mid[r:r + L]) + _dot(tri, g_lo[r:r + L])
         for r in range(0, gl.shape[0], L)], axis=0)
    v_ref[...] = _dot_nt(wv_ref[...], hn).astype(BF16)
    y = jnp.where(lane < MLSTM_HEADS, gl, csum)
    gt_ref[...] = y
    gtt_ref[...] = y.T[0:2 * MLSTM_HEADS, :]
    mo_ref[...] = _mem_out(p, mv_ref).astype(BF16)
    o_ref[...] = _dot_nt(wo_ref[...], hn).astype(BF16)


def _proj_a(x2d, wq, wk, wv, wo, wmq, wg, bg, mkm, mvm, rows_per_batch):
    N = x2d.shape[0]
    tm = ROW_TILE
    steps_per_batch = rows_per_batch // tm
    row = lambda w: pl.BlockSpec((tm, w), lambda i: (i, 0))
    col = pl.BlockSpec((MLSTM_V_W, tm), lambda i: (0, i))
    mem_spec = pl.BlockSpec((1, 1, MEM_HEADS, MEM_TOKENS, MEM_W),
                            lambda i: (0, i // steps_per_batch, 0, 0, 0))
    return pl.pallas_call(
        _proj_a_kernel,
        grid=(N // tm,),
        in_specs=[row(D_MODEL),
                  _resident(wq.shape), _resident(wk.shape), _resident(wv.shape), _resident(wo.shape),
                  _resident(wmq.shape), _resident(wg.shape), _resident((1, LANES)),
                  mem_spec, mem_spec],
        out_specs=[row(QK_PAD_W), row(QK_PAD_W), col, col, row(LANES),
                   pl.BlockSpec((2 * MLSTM_HEADS, tm), lambda i: (0, i)), row(MEM_W)],
        out_shape=[jax.ShapeDtypeStruct((N, QK_PAD_W), BF16), jax.ShapeDtypeStruct((N, QK_PAD_W), BF16),
                   jax.ShapeDtypeStruct((MLSTM_V_W, N), BF16), jax.ShapeDtypeStruct((MLSTM_V_W, N), BF16),
                   jax.ShapeDtypeStruct((N, LANES), F32), jax.ShapeDtypeStruct((2 * MLSTM_HEADS, N), F32),
                   jax.ShapeDtypeStruct((N, MEM_W), BF16)],
        compiler_params=pltpu.CompilerParams(dimension_semantics=("arbitrary",),
                                             vmem_limit_bytes=VMEM_LIMIT),
        name="proj_a",
    )(x2d, wq, wk, wv, wo, wmq, wg, bg, mkm, mvm)


def _mlstm_kernel(q_ref, k_ref, vt_ref, ot_ref, gt_ref, gtt_ref, hmt_ref, c_scr, m_scr):
    L = q_ref.shape[0]

    @pl.when(pl.program_id(1) == 0)
    def _():
        c_scr[...] = jnp.zeros(c_scr.shape, F32)
        m_scr[...] = jnp.full(m_scr.shape, M_INIT, F32)

    causal = lax.broadcasted_iota(jnp.int32, (L, L), 0) <= lax.broadcasted_iota(jnp.int32, (L, L), 1)
    y = gt_ref[...]
    yt = gtt_ref[...]
    feat = lax.broadcasted_iota(jnp.int32, (V_WIN, L), 0)

    heads = range(MLSTM_HEADS)
    b_row = [yt[MLSTM_HEADS + h:MLSTM_HEADS + h + 1, :] for h in heads]
    c_row = [b_row[h] - yt[h:h + 1, :] for h in heads]
    c_col = [y[:, MLSTM_HEADS + h:MLSTM_HEADS + h + 1] - y[:, h:h + 1] for h in heads]
    g_tot = [y[L - 1:L, MLSTM_HEADS + h:MLSTM_HEADS + h + 1] for h in heads]
    m_st = [m_scr[h][0:1, 0:1] for h in heads]
    c_st = [c_scr[h] for h in heads]
    qh = [q_ref[:, h * DQK_PAD:(h + 1) * DQK_PAD] for h in heads]
    kh = [k_ref[:, h * DQK_PAD:(h + 1) * DQK_PAD] for h in heads]
    vt_aug = [jnp.where(feat == _V_ONES_COL[h], jnp.ones((), BF16),
                        vt_ref[_V_WIN_START[h]:_V_WIN_START[h] + V_WIN, :]) for h in heads]

    qk_t = [_dot_nt(kh[h], qh[h]) for h in heads]
    inter_t = [_dot_nt(c_st[h].astype(BF16), qh[h]) for h in heads]
    dmat = [jnp.where(causal, b_row[h] - c_col[h], -jnp.inf) for h in heads]
    inter_log = [b_row[h] + m_st[h] for h in heads]
    m_row = [jnp.maximum(inter_log[h], jnp.max(dmat[h], axis=0, keepdims=True)) for h in heads]
    scores_t = [(qk_t[h] * jnp.exp2(dmat[h] - m_row[h])).astype(BF16) for h in heads]
    num_t = [_dot(vt_aug[h], scores_t[h]) + jnp.exp2(inter_log[h] - m_row[h]) * inter_t[h] for h in heads]

    a = [g_tot[h] - c_row[h] for h in heads]
    m_new = [jnp.maximum(g_tot[h] + m_st[h], jnp.max(a[h], axis=1, keepdims=True)) for h in heads]
    vtw = [vt_aug[h] * jnp.exp2(a[h] - m_new[h]).astype(BF16) for h in heads]
    c_upd = [_dot(vtw[h], kh[h]) for h in heads]

    for h in heads:
        off, ones_row, lo = _V_WIN_OFF[h], _V_ONES_COL[h], h * MLSTM_DV
        den = num_t[h][ones_row:ones_row + 1, :]
        inv_dd = 1.0 / jnp.maximum(jnp.abs(den), jnp.exp2(-m_row[h]))
        h_t = num_t[h][off:off + MLSTM_DV, :]
        ms = jnp.sum(h_t * h_t, axis=0, keepdims=True) * (1.0 / MLSTM_DV)
        scale = inv_dd * lax.rsqrt(ms * inv_dd * inv_dd + EPS)
        og = _sigmoid(ot_ref[lo:lo + MLSTM_DV, :].astype(F32))
        hmt_ref[lo:lo + MLSTM_DV, :] = (h_t * (scale * og)).astype(BF16)

    for h in heads:
        c_scr[h] = jnp.exp2(g_tot[h] + m_st[h] - m_new[h]) * c_st[h] + c_upd[h]
        m_scr[h] = jnp.broadcast_to(m_new[h], m_scr.shape[1:])


def _mlstm(q, k, vt, ot, gt, gtt, B, S):
    L = MLSTM_CHUNK
    nc = S // L
    N = B * S
    row = lambda w: pl.BlockSpec((L, w), lambda b, c: (b * nc + c, 0))
    col = pl.BlockSpec((MLSTM_V_W, L), lambda b, c: (0, b * nc + c))
    return pl.pallas_call(
        _mlstm_kernel,
        grid=(B, nc),
        in_specs=[row(QK_PAD_W), row(QK_PAD_W), col, col, row(LANES),
                  pl.BlockSpec((2 * MLSTM_HEADS, L), lambda b, c: (0, b * nc + c))],
        out_specs=col,
        out_shape=jax.ShapeDtypeStruct((MLSTM_V_W, N), BF16),
        scratch_shapes=[pltpu.VMEM((MLSTM_HEADS, V_WIN, DQK_PAD), F32),
                        pltpu.VMEM((MLSTM_HEADS, 8, LANES), F32)],
        compiler_params=pltpu.CompilerParams(dimension_semantics=("arbitrary", "arbitrary"),
                                             vmem_limit_bytes=VMEM_LIMIT),
        name="mlstm",
    )(q, k, vt, ot, gt, gtt)


def _mix_ffn_core(x_ref, main_ref, mo_ref, wom_ref, woe_ref, gpost_ref, gfpost_ref,
                  wi_ref, wd_ref, main_feature_major):
    main_dot = _dot_tn if main_feature_major else _dot
    mix = main_dot(main_ref[...], wom_ref[...]) + _dot(mo_ref[...], woe_ref[...])
    x1 = x_ref[...] + _rms(mix, gpost_ref[...])
    hf = _rms_hat(x1).astype(BF16)
    acc = None
    for c in range(N_FF_CHUNKS):
        lo = c * FF_CHUNK
        g = _dot(hf, wi_ref[:, lo:lo + FF_CHUNK])
        u = _dot(hf, wi_ref[:, D_FF + lo:D_FF + lo + FF_CHUNK])
        act = (g * _sigmoid(g) * u).astype(BF16)
        d = _dot(act, wd_ref[lo:lo + FF_CHUNK, :])
        acc = d if acc is None else acc + d
    return x1 + _rms(acc, gfpost_ref[...])


def _mix_ffn_last_kernel(x_ref, main_ref, mo_ref, wom_ref, woe_ref, gpost_ref, gfpost_ref,
                         wi_ref, wd_ref, xo_ref):
    xo_ref[...] = _mix_ffn_core(x_ref, main_ref, mo_ref, wom_ref, woe_ref, gpost_ref,
                                gfpost_ref, wi_ref, wd_ref, main_feature_major=False)


def _rope(t1, t2, cos, sin):
    return t1 * cos - t2 * sin, t2 * cos + t1 * sin


def _rope_tables(pos_ref, inv_ref):
    ang = pos_ref[...].astype(F32) * inv_ref[...]
    cos_d = jnp.cos(ang)
    sin_d = jnp.sin(ang)
    grp = lax.broadcasted_iota(jnp.int32, ang.shape, 1) >> 5

    def spread(t, g):
        y = jnp.where(grp == g, t, 0.0)
        y = y + pltpu.roll(y, HEAD_DIM // 2, 1)
        return y + pltpu.roll(y, HEAD_DIM, 1)

    n_grp = LANES // (HEAD_DIM // 2)
    cos = jnp.concatenate([spread(cos_d, g) for g in range(n_grp)], axis=0)
    sin = jnp.concatenate([spread(sin_d, g) for g in range(n_grp)], axis=0)
    return cos, sin


def _mix_ffn_first_kernel(x_ref, main_ref, mo_ref, wom_ref, woe_ref, gpost_ref, gfpost_ref,
                          wi_ref, wd_ref,
                          pos_ref, inv_ref, wkv_ref, wb_ref, mk_ref, mv_ref,
                          xo_ref, ks_ref, vs_ref, q1_ref, mo1_ref):
    cos, sin = _rope_tables(pos_ref, inv_ref)
    x2 = _mix_ffn_core(x_ref, main_ref, mo_ref, wom_ref, woe_ref, gpost_ref, gfpost_ref,
                       wi_ref, wd_ref, main_feature_major=True)
    xo_ref[...] = x2
    xn = _rms_hat(x2).astype(BF16)

    pb = _dot(xn, wb_ref[...])
    s = _mem_scores(pb[:, SWA_Q_W:].astype(BF16), mk_ref)
    kv = _dot(xn, wkv_ref[...])
    p = _mem_probs(s)

    for gi in range(SWA_GROUPS):
        base = gi * 2 * LANES
        t1, t2 = _rope(pb[:, base:base + LANES], pb[:, base + LANES:base + 2 * LANES], cos, sin)
        q1_ref[:, base:base + LANES] = t1.astype(BF16)
        q1_ref[:, base + LANES:base + 2 * LANES] = t2.astype(BF16)
    mo1_ref[...] = _mem_out(p, mv_ref).astype(BF16)

    k1, k2 = _rope(kv[:, 0:LANES], kv[:, LANES:2 * LANES], cos, sin)
    scale = HEAD_DIM ** -0.5 * LOG2E
    ks_ref[:, 0:LANES] = (k1 * scale).astype(BF16)
    ks_ref[:, LANES:2 * LANES] = (k2 * scale).astype(BF16)
    vs_ref[...] = kv[:, SWA_KV_W:].astype(BF16)


def _mix_ffn(layer, x2d, main, mo, wom, woe, g_post, g_fpost, wi, wd, first=None, rows_per_batch=None):
    N = x2d.shape[0]
    tm = ROW_TILE
    row = lambda w: pl.BlockSpec((tm, w), lambda i: (i, 0))
    slab = lambda w: pl.BlockSpec((None,) + w.shape[1:], lambda i: (layer, 0, 0), pipeline_mode=pl.Buffered(1))
    main_spec = row(main.shape[1]) if first is None else pl.BlockSpec((main.shape[0], tm), lambda i: (0, i))
    in_specs = [row(D_MODEL), main_spec, row(MEM_W),
                _resident(wom.shape), _resident(woe.shape),
                _resident((1, D_MODEL)), _resident((1, D_MODEL)),
                slab(wi), slab(wd)]
    args = [x2d, main, mo, wom, woe, g_post, g_fpost, wi, wd]
    out_specs = [row(D_MODEL)]
    out_shape = [jax.ShapeDtypeStruct((N, D_MODEL), F32)]
    if first is None:
        body = _mix_ffn_last_kernel
        name = "mix_ffn_last"
    else:
        body = _mix_ffn_first_kernel
        name = "mix_ffn_first"
        pos, inv, wkv, wb, mkm, mvm = first
        steps_per_batch = rows_per_batch // tm
        mem_spec = pl.BlockSpec((1, 1, MEM_HEADS, MEM_TOKENS, MEM_W),
                                lambda i: (1, i // steps_per_batch, 0, 0, 0))
        pos_spec = pl.BlockSpec((tm // (LANES // (HEAD_DIM // 2)), LANES), lambda i: (i, 0))
        in_specs += [pos_spec, _resident((1, LANES)), _resident(wkv.shape), _resident(wb.shape),
                     mem_spec, mem_spec]
        args += [pos, inv, wkv, wb, mkm, mvm]
        out_specs += [row(SWA_KV_W), row(SWA_KV_W), row(SWA_Q_W), row(MEM_W)]
        out_shape += [jax.ShapeDtypeStruct((N, SWA_KV_W), BF16), jax.ShapeDtypeStruct((N, SWA_KV_W), BF16),
                      jax.ShapeDtypeStruct((N, SWA_Q_W), BF16), jax.ShapeDtypeStruct((N, MEM_W), BF16)]
    return pl.pallas_call(
        body,
        grid=(N // tm,),
        in_specs=in_specs,
        out_specs=out_specs,
        out_shape=out_shape,
        compiler_params=pltpu.CompilerParams(dimension_semantics=("arbitrary",),
                                             vmem_limit_bytes=VMEM_LIMIT),
        name=name,
    )(*args)


def _swa_kernel(sink_ref, q_ref, kp_ref, kc_ref, vp_ref, vc_ref, o_ref):
    W = SWA_WINDOW
    G = SWA_GROUPS
    n = pl.program_id(1)
    n_sub = q_ref.shape[0] // W
    H = SWA_KV_HEADS
    kpos = lax.broadcasted_iota(jnp.int32, (2 * W, G * W), 0)
    qcol = lax.broadcasted_iota(jnp.int32, (2 * W, G * W), 1)
    diff = (qcol & (W - 1)) + W - kpos
    in_band = (diff >= 0) & (diff < W)
    k_lane = lax.broadcasted_iota(jnp.int32, (2 * W, SWA_KV_W), 1)
    sink = sink_ref[...]
    subs = range(n_sub)
    q, km, vm, valid = [], [], [], []
    for sub in subs:
        q.append(jnp.concatenate([q_ref[sub * W:(sub + 1) * W, gi * 2 * LANES:(gi + 1) * 2 * LANES]
                                  for gi in range(G)], axis=0))
        if sub == 0:
            k = jnp.concatenate([kp_ref[...], kc_ref[0:W, :]], axis=0)
            v = jnp.concatenate([vp_ref[...], vc_ref[0:W, :]], axis=0)
            valid.append(in_band & ((kpos >= W) | (n > 0)))
        else:
            k = kc_ref[(sub - 1) * W:(sub + 1) * W, :]
            v = vc_ref[(sub - 1) * W:(sub + 1) * W, :]
            valid.append(in_band)
        km.append(jnp.concatenate([jnp.where(((k_lane & (LANES - 1)) >> 5) == j, k, jnp.zeros((), BF16))
                                   for j in range(H)], axis=0))
        vm.append(jnp.concatenate([jnp.where((k_lane >> 6) == j, v, jnp.zeros((), BF16))
                                   for j in range(H)], axis=0))
    s = [_dot_nt(km[i], q[i]).reshape(H, 2 * W, G * W) for i in subs]
    s = [jnp.where(valid[i][None], s[i], -jnp.inf) for i in subs]
    m = [jnp.maximum(jnp.max(s[i], axis=1, keepdims=True), sink) for i in subs]
    e = [jnp.exp2(s[i] - m[i]) for i in subs]
    p = [e[i] * (1.0 / (jnp.sum(e[i], axis=1, keepdims=True) + jnp.exp2(sink - m[i]))) for i in subs]
    out = [_dot_tn(p[i].astype(BF16).reshape(H * 2 * W, G * W), vm[i]) for i in subs]
    for sub in subs:
        for gi in range(G):
            o_ref[sub * W:(sub + 1) * W, gi * 2 * LANES:(gi + 1) * 2 * LANES] = (
                out[sub][gi * W:(gi + 1) * W].astype(BF16))


def _swa(sink_cols, q1, ks, vs, B, S):
    W = SWA_WINDOW
    n_sub = SWA_BLOCKS_PER_STEP
    tq = n_sub * W
    steps = S // tq
    cur = lambda w: pl.BlockSpec((tq, w), lambda b, n: (b * steps + n, 0))
    prev = lambda w: pl.BlockSpec((W, w), lambda b, n: (b * (S // W) + jnp.maximum(n_sub * n - 1, 0), 0))
    return pl.pallas_call(
        _swa_kernel,
        grid=(B, steps),
        in_specs=[_resident(sink_cols.shape),
                  cur(SWA_Q_W), prev(SWA_KV_W), cur(SWA_KV_W), prev(SWA_KV_W), cur(SWA_KV_W)],
        out_specs=cur(SWA_Q_W),
        out_shape=jax.ShapeDtypeStruct((B * S, SWA_Q_W), BF16),
        compiler_params=pltpu.CompilerParams(dimension_semantics=("arbitrary", "arbitrary"),
                                             vmem_limit_bytes=VMEM_LIMIT),
        name="swa",
    )(sink_cols, q1, ks, ks, vs, vs)


def _pad_heads(w, heads, width, padded):
    w = w.reshape(w.shape[0], heads, width)
    return jnp.pad(w, ((0, 0), (0, 0), (0, padded - width))).reshape(w.shape[0], heads * padded)


def _swa_q_perm():
    idx = np.empty((SWA_GROUPS, 2, SWA_KV_HEADS, HEAD_DIM // 2), np.int32)
    for gi in range(SWA_GROUPS):
        for half in range(2):
            for j in range(SWA_KV_HEADS):
                idx[gi, half, j] = (j * SWA_GROUPS + gi) * HEAD_DIM + half * (HEAD_DIM // 2) + np.arange(HEAD_DIM // 2)
    return idx.reshape(-1)


def _swa_k_perm():
    idx = np.empty((2, SWA_KV_HEADS, HEAD_DIM // 2), np.int32)
    for half in range(2):
        for j in range(SWA_KV_HEADS):
            idx[half, j] = j * HEAD_DIM + half * (HEAD_DIM // 2) + np.arange(HEAD_DIM // 2)
    return idx.reshape(-1)


def _swa_out_perm():
    idx = np.empty((SWA_GROUPS, SWA_KV_HEADS, HEAD_DIM), np.int32)
    for gi in range(SWA_GROUPS):
        for j in range(SWA_KV_HEADS):
            idx[gi, j] = (j * SWA_GROUPS + gi) * HEAD_DIM + np.arange(HEAD_DIM)
    return idx.reshape(-1)


def kernel(x, mem, positions, g_mix_pre, g_mix_post, g_ffn_pre, g_ffn_post, g_mem, w_mem_kv, w_out,
           w_ffn_in, w_ffn_out, w_in_a, b_gates_a, g_mlstm_out, g_kv, w_kv, w_in_b, sinks_b):
    B, S, _ = x.shape
    N = B * S
    assert S % ROW_TILE == 0 and S % MLSTM_CHUNK == 0 and S % (SWA_WINDOW * SWA_BLOCKS_PER_STEP) == 0
    x2d = x.reshape(N, D_MODEL)
    vec = lambda g: g.reshape(1, -1).astype(F32)

    mkm, mvm = _mem_kv(mem, g_mem, w_mem_kv.astype(BF16))

    fold = lambda g, w: g.astype(F32)[:, None] * w

    wa = fold(g_mix_pre[0], w_in_a[0])
    c0 = 0
    wq = _pad_heads(wa[:, c0:c0 + MLSTM_QK_W], MLSTM_HEADS, MLSTM_DQK, DQK_PAD).astype(BF16)
    c0 += MLSTM_QK_W
    wk = _pad_heads(wa[:, c0:c0 + MLSTM_QK_W] * (MLSTM_DQK ** -0.5), MLSTM_HEADS, MLSTM_DQK, DQK_PAD).astype(BF16)
    c0 += MLSTM_QK_W
    wv = wa[:, c0:c0 + MLSTM_V_W].T.astype(BF16)
    c0 += MLSTM_V_W
    wo = wa[:, c0:c0 + MLSTM_V_W].T.astype(BF16)
    c0 += MLSTM_V_W
    wgt = jnp.pad(wa[:, c0:c0 + 2 * MLSTM_HEADS], ((0, 0), (0, LANES - 2 * MLSTM_HEADS))).astype(BF16)
    c0 += 2 * MLSTM_HEADS
    wmq = wa[:, c0:c0 + MEM_W].astype(BF16)
    bg = jnp.pad(b_gates_a[0].astype(F32), (0, LANES - 2 * MLSTM_HEADS)).reshape(1, LANES)

    q, k, v, o, gt, gtt, mo0 = _proj_a(x2d, wq, wk, wv, wo, wmq, wgt, bg, mkm, mvm, S)
    hm = _mlstm(q, k, v, o, gt, gtt, B, S)

    w_ffn_in_bf = (g_ffn_pre.astype(F32)[:, :, None] * w_ffn_in).astype(BF16)
    w_ffn_out_bf = w_ffn_out.astype(BF16)
    wo0 = jnp.concatenate([fold(g_mlstm_out[0], w_out[0][:MLSTM_V_W]), w_out[0][MLSTM_V_W:]], axis=0).astype(BF16)
    inv = 1.0 / (ROPE_THETA ** (jnp.arange(0, HEAD_DIM, 2, dtype=F32) / HEAD_DIM))
    inv = jnp.tile(inv, SWA_KV_HEADS).reshape(1, LANES)
    wkv = fold(g_kv, w_kv)
    wkv = jnp.concatenate([wkv[:, :SWA_KV_W][:, _swa_k_perm()], wkv[:, SWA_KV_W:]], axis=1).astype(BF16)
    wb = fold(g_mix_pre[1], w_in_b[0])
    wb = jnp.concatenate([wb[:, :SWA_Q_W][:, _swa_q_perm()], wb[:, SWA_Q_W:]], axis=1).astype(BF16)
    n_grp = LANES // (HEAD_DIM // 2)
    pos_dense = positions.reshape(N // ROW_TILE, n_grp, ROW_TILE // n_grp).transpose(0, 2, 1)
    pos_dense = jnp.repeat(pos_dense, HEAD_DIM // 2, axis=2).reshape(N // n_grp, LANES)
    first = (pos_dense, inv, wkv, wb, mkm, mvm)
    x1, ks, vs, q1, mo1 = _mix_ffn(0, x2d, hm, mo0, wo0[:MLSTM_V_W], wo0[MLSTM_V_W:],
                                   vec(g_mix_post[0]), vec(g_ffn_post[0]),
                                   w_ffn_in_bf, w_ffn_out_bf, first=first, rows_per_batch=S)

    sink_rows = jnp.repeat((sinks_b[0].astype(F32) * LOG2E).reshape(SWA_KV_HEADS, SWA_GROUPS), SWA_WINDOW, axis=1)
    attn = _swa(sink_rows.reshape(SWA_KV_HEADS, 1, SWA_GROUPS * SWA_WINDOW), q1, ks, vs, B, S)
    wo1 = w_out[1]
    (xo,) = _mix_ffn(1, x1, attn, mo1, wo1[:SWA_Q_W][_swa_out_perm()].astype(BF16), wo1[SWA_Q_W:].astype(BF16),
                     vec(g_mix_post[1]), vec(g_ffn_post[1]),
                     w_ffn_in_bf, w_ffn_out_bf)
    return xo.reshape(B, S, D_MODEL)
```

```python
import numpy as np
import jax
import jax.numpy as jnp
from jax import lax
from jax.experimental import pallas as pl
from jax.experimental.pallas import tpu as pltpu

F32 = jnp.float32
BF16 = jnp.bfloat16

D_MODEL = 1024
DEPTH = 2
HEAD_DIM = 64
EPS = 1e-6
ROPE_THETA = 10000.0
LOG2E = 1.4426950408889634

MLSTM_HEADS = 4
MLSTM_DV = 192
MLSTM_DQK = 96
MLSTM_QK_W = MLSTM_HEADS * MLSTM_DQK
MLSTM_V_W = MLSTM_HEADS * MLSTM_DV
GATE_SOFTCAP = 15.0
M_INIT = -1e30

SWA_Q_HEADS = 12
SWA_KV_HEADS = 4
SWA_GROUPS = SWA_Q_HEADS // SWA_KV_HEADS
SWA_Q_W = SWA_Q_HEADS * HEAD_DIM
SWA_KV_W = SWA_KV_HEADS * HEAD_DIM
SWA_WINDOW = 128

MEM_TOKENS = 256
MEM_HEADS = 4
MEM_HEAD_DIM = 64
MEM_W = MEM_HEADS * MEM_HEAD_DIM

D_FF = 2816

LANES = 128
MXU_TILE = 256

DQK_PAD = LANES
QK_PAD_W = MLSTM_HEADS * DQK_PAD
V_WIN = MXU_TILE
MLSTM_CHUNK = 256
MLSTM_CHUNKS_PER_STEP = 4
FF_CHUNK = MXU_TILE
N_FF_CHUNKS = D_FF // FF_CHUNK
ROW_TILE = 512
FFN_LAST_ROW_TILE = 1024
SWA_BLOCKS_PER_STEP = 4
VMEM_LIMIT = 56 * 1024 * 1024

_V_WIN_START = (0, 128, 384, 512)
_V_WIN_OFF = (0, 64, 0, 64)
_V_ONES_COL = (192, 0, 192, 0)


def _rms_hat(x):
    return x * lax.rsqrt(jnp.mean(x * x, axis=-1, keepdims=True) + EPS)


def _rms(x, g):
    return _rms_hat(x) * g


def _dot(a, b):
    return jnp.dot(a, b, preferred_element_type=F32)


def _dot_nt(a, b):
    return lax.dot_general(a, b, (((1,), (1,)), ((), ())), preferred_element_type=F32)


def _dot_tn(a, b):
    return lax.dot_general(a, b, (((0,), (0,)), ((), ())), preferred_element_type=F32)


def _sigmoid(x):
    return 1.0 / (1.0 + jnp.exp(-x))


def _resident(shape):
    nd = len(shape)
    return pl.BlockSpec(shape, lambda *_: (0,) * nd, pipeline_mode=pl.Buffered(1))


def _mem_kv_kernel(mem_ref, g_ref, w_ref, mk_ref, mv_ref):
    B = mem_ref.shape[0]
    hn = _rms(mem_ref[...].reshape(B * MEM_TOKENS, D_MODEL), g_ref[0]).astype(BF16)
    kv = _dot(hn, w_ref[0])
    mk = kv[:, :MEM_W] * (MEM_HEAD_DIM ** -0.5 * LOG2E)
    mv = kv[:, MEM_W:]
    lane_head = lax.broadcasted_iota(jnp.int32, mk.shape, 1) >> 6
    for h in range(MEM_HEADS):
        sel = lane_head == h
        mk_ref[0, :, h] = jnp.where(sel, mk, 0.0).astype(BF16).reshape(B, MEM_TOKENS, MEM_W)
        mv_ref[0, :, h] = jnp.where(sel, mv, 0.0).astype(BF16).reshape(B, MEM_TOKENS, MEM_W)


def _mem_kv(mem, g_mem, w_mem_kv_bf16):
    B = mem.shape[0]
    out_sds = jax.ShapeDtypeStruct((DEPTH, B, MEM_HEADS, MEM_TOKENS, MEM_W), BF16)
    out_spec = pl.BlockSpec((1, B, MEM_HEADS, MEM_TOKENS, MEM_W), lambda l: (l, 0, 0, 0, 0))
    return pl.pallas_call(
        _mem_kv_kernel,
        grid=(DEPTH,),
        in_specs=[
            pl.BlockSpec((B, MEM_TOKENS, D_MODEL), lambda l: (0, 0, 0)),
            pl.BlockSpec((1, 1, D_MODEL), lambda l: (l, 0, 0)),
            pl.BlockSpec((1, D_MODEL, 2 * MEM_W), lambda l: (l, 0, 0)),
        ],
        out_specs=[out_spec, out_spec],
        out_shape=[out_sds, out_sds],
        compiler_params=pltpu.CompilerParams(dimension_semantics=("arbitrary",),
                                             vmem_limit_bytes=VMEM_LIMIT),
        name="mem_kv",
    )(mem, g_mem.reshape(DEPTH, 1, D_MODEL), w_mem_kv_bf16)


def _mem_scores(mq, mk_ref):
    return [_dot_nt(mq, mk_ref[0, 0, h]) for h in range(MEM_HEADS)]


def _mem_probs(s):
    e = [jnp.exp2(sh - jnp.max(sh, axis=-1, keepdims=True)) for sh in s]
    return [(eh * (1.0 / jnp.sum(eh, axis=-1, keepdims=True))).astype(BF16) for eh in e]


def _mem_out(p, mv_ref):
    out = _dot(p[0], mv_ref[0, 0, 0])
    for h in range(1, MEM_HEADS):
        out = out + _dot(p[h], mv_ref[0, 0, h])
    return out


def _proj_a_kernel(x_ref, wq_ref, wk_ref, wv_ref, wo_ref, wmq_ref, wg_ref, bg_ref, mk_ref, mv_ref,
                   q_ref, k_ref, v_ref, o_ref, gt_ref, gtt_ref, mo_ref):
    hn = _rms_hat(x_ref[...]).astype(BF16)
    mq = _dot(hn, wmq_ref[...]).astype(BF16)
    s = _mem_scores(mq, mk_ref)
    gates = _dot(hn, wg_ref[...]) + bg_ref[...]
    q_ref[...] = _dot(hn, wq_ref[...]).astype(BF16)
    gates = GATE_SOFTCAP * jnp.tanh(gates * (1.0 / GATE_SOFTCAP))
    log_sig = jnp.minimum(gates, 0.0) - jnp.log1p(jnp.exp(-jnp.abs(gates)))
    lane = lax.broadcasted_iota(jnp.int32, gates.shape, 1)
    gl = jnp.where(lane < MLSTM_HEADS, gates, log_sig) * LOG2E
    p = _mem_probs(s)
    k_ref[...] = _dot(hn, wk_ref[...]).astype(BF16)
    g_hi = gl.astype(BF16)
    rest = gl - g_hi.astype(F32)
    g_mid = rest.astype(BF16)
    g_lo = (rest - g_mid.astype(F32)).astype(BF16)
    L = MLSTM_CHUNK
    tri = (lax.broadcasted_iota(jnp.int32, (L, L), 1) <= lax.broadcasted_iota(jnp.int32, (L, L), 0)).astype(BF16)
    csum = jnp.concatenate(
        [_dot(tri, g_hi[r:r + L]) + _dot(tri, g_mid[r:r + L]) + _dot(tri, g_lo[r:r + L])
         for r in range(0, gl.shape[0], L)], axis=0)
    v_ref[...] = _dot_nt(wv_ref[...], hn).astype(BF16)
    y = jnp.where(lane < MLSTM_HEADS, gl, csum)
    gt_ref[...] = y
    gtt_ref[...] = y.T[0:2 * MLSTM_HEADS, :]
    mo_ref[...] = _mem_out(p, mv_ref).astype(BF16)
    o_ref[...] = _dot_nt(wo_ref[...], hn).astype(BF16)


def _proj_a(x2d, wq, wk, wv, wo, wmq, wg, bg, mkm, mvm, rows_per_batch):
    N = x2d.shape[0]
    tm = ROW_TILE
    steps_per_batch = rows_per_batch // tm
    row = lambda w: pl.BlockSpec((tm, w), lambda i: (i, 0))
    col = pl.BlockSpec((MLSTM_V_W, tm), lambda i: (0, i))
    mem_spec = pl.BlockSpec((1, 1, MEM_HEADS, MEM_TOKENS, MEM_W),
                            lambda i: (0, i // steps_per_batch, 0, 0, 0))
    return pl.pallas_call(
        _proj_a_kernel,
        grid=(N // tm,),
        in_specs=[row(D_MODEL),
                  _resident(wq.shape), _resident(wk.shape), _resident(wv.shape), _resident(wo.shape),
                  _resident(wmq.shape), _resident(wg.shape), _resident((1, LANES)),
                  mem_spec, mem_spec],
        out_specs=[row(QK_PAD_W), row(QK_PAD_W), col, col, row(LANES),
                   pl.BlockSpec((2 * MLSTM_HEADS, tm), lambda i: (0, i)), row(MEM_W)],
        out_shape=[jax.ShapeDtypeStruct((N, QK_PAD_W), BF16), jax.ShapeDtypeStruct((N, QK_PAD_W), BF16),
                   jax.ShapeDtypeStruct((MLSTM_V_W, N), BF16), jax.ShapeDtypeStruct((MLSTM_V_W, N), BF16),
                   jax.ShapeDtypeStruct((N, LANES), F32), jax.ShapeDtypeStruct((2 * MLSTM_HEADS, N), F32),
                   jax.ShapeDtypeStruct((N, MEM_W), BF16)],
        compiler_params=pltpu.CompilerParams(dimension_semantics=("arbitrary",),
                                             vmem_limit_bytes=VMEM_LIMIT),
        name="proj_a",
    )(x2d, wq, wk, wv, wo, wmq, wg, bg, mkm, mvm)


def _mlstm_kernel(q_ref, k_ref, vt_ref, ot_ref, gt_ref, gtt_ref, hmt_ref, c_scr, m_scr):
    L = MLSTM_CHUNK

    @pl.when(pl.program_id(1) == 0)
    def _():
        c_scr[...] = jnp.zeros(c_scr.shape, F32)
        m_scr[...] = jnp.full(m_scr.shape, M_INIT, F32)

    causal = lax.broadcasted_iota(jnp.int32, (L, L), 0) <= lax.broadcasted_iota(jnp.int32, (L, L), 1)
    feat = lax.broadcasted_iota(jnp.int32, (V_WIN, L), 0)
    heads = range(MLSTM_HEADS)
    m_st = [m_scr[h][0:1, 0:1] for h in heads]
    c_st = [c_scr[h] for h in heads]

    for r0 in range(0, q_ref.shape[0], L):
        y = gt_ref[r0:r0 + L, :]
        yt = gtt_ref[:, r0:r0 + L]
        b_row = [yt[MLSTM_HEADS + h:MLSTM_HEADS + h + 1, :] for h in heads]
        c_row = [b_row[h] - yt[h:h + 1, :] for h in heads]
        c_col = [y[:, MLSTM_HEADS + h:MLSTM_HEADS + h + 1] - y[:, h:h + 1] for h in heads]
        g_tot = [y[L - 1:L, MLSTM_HEADS + h:MLSTM_HEADS + h + 1] for h in heads]
        qh = [q_ref[r0:r0 + L, h * DQK_PAD:(h + 1) * DQK_PAD] for h in heads]
        kh = [k_ref[r0:r0 + L, h * DQK_PAD:(h + 1) * DQK_PAD] for h in heads]
        vt_aug = [jnp.where(feat == _V_ONES_COL[h], jnp.ones((), BF16),
                            vt_ref[_V_WIN_START[h]:_V_WIN_START[h] + V_WIN, r0:r0 + L]) for h in heads]

        qk_t = [_dot_nt(kh[h], qh[h]) for h in heads]
        inter_t = [_dot_nt(c_st[h].astype(BF16), qh[h]) for h in heads]
        dmat = [jnp.where(causal, b_row[h] - c_col[h], -jnp.inf) for h in heads]
        inter_log = [b_row[h] + m_st[h] for h in heads]
        m_row = [jnp.maximum(inter_log[h], jnp.max(dmat[h], axis=0, keepdims=True)) for h in heads]
        scores_t = [(qk_t[h] * jnp.exp2(dmat[h] - m_row[h])).astype(BF16) for h in heads]
        num_t = [_dot(vt_aug[h], scores_t[h]) + jnp.exp2(inter_log[h] - m_row[h]) * inter_t[h] for h in heads]

        a = [g_tot[h] - c_row[h] for h in heads]
        m_new = [jnp.maximum(g_tot[h] + m_st[h], jnp.max(a[h], axis=1, keepdims=True)) for h in heads]
        vtw = [vt_aug[h] * jnp.exp2(a[h] - m_new[h]).astype(BF16) for h in heads]
        c_upd = [_dot(vtw[h], kh[h]) for h in heads]

        for h in heads:
            off, ones_row, lo = _V_WIN_OFF[h], _V_ONES_COL[h], h * MLSTM_DV
            den = num_t[h][ones_row:ones_row + 1, :]
            inv_dd = 1.0 / jnp.maximum(jnp.abs(den), jnp.exp2(-m_row[h]))
            h_t = num_t[h][off:off + MLSTM_DV, :]
            ms = jnp.sum(h_t * h_t, axis=0, keepdims=True) * (1.0 / MLSTM_DV)
            scale = inv_dd * lax.rsqrt(ms * inv_dd * inv_dd + EPS)
            og = _sigmoid(ot_ref[lo:lo + MLSTM_DV, r0:r0 + L].astype(F32))
            hmt_ref[lo:lo + MLSTM_DV, r0:r0 + L] = (h_t * (scale * og)).astype(BF16)

        c_st = [jnp.exp2(g_tot[h] + m_st[h] - m_new[h]) * c_st[h] + c_upd[h] for h in heads]
        m_st = m_new

    for h in heads:
        c_scr[h] = c_st[h]
        m_scr[h] = jnp.broadcast_to(m_st[h], m_scr.shape[1:])


def _mlstm(q, k, vt, ot, gt, gtt, B, S):
    rows = MLSTM_CHUNK * MLSTM_CHUNKS_PER_STEP
    steps = S // rows
    N = B * S
    row = lambda w: pl.BlockSpec((rows, w), lambda b, c: (b * steps + c, 0))
    col = lambda h: pl.BlockSpec((h, rows), lambda b, c: (0, b * steps + c))
    return pl.pallas_call(
        _mlstm_kernel,
        grid=(B, steps),
        in_specs=[row(QK_PAD_W), row(QK_PAD_W), col(MLSTM_V_W), col(MLSTM_V_W), row(LANES),
                  col(2 * MLSTM_HEADS)],
        out_specs=col(MLSTM_V_W),
        out_shape=jax.ShapeDtypeStruct((MLSTM_V_W, N), BF16),
        scratch_shapes=[pltpu.VMEM((MLSTM_HEADS, V_WIN, DQK_PAD), F32),
                        pltpu.VMEM((MLSTM_HEADS, 8, LANES), F32)],
        compiler_params=pltpu.CompilerParams(dimension_semantics=("arbitrary", "arbitrary"),
                                             vmem_limit_bytes=VMEM_LIMIT),
        name="mlstm",
    )(q, k, vt, ot, gt, gtt)


def _mix_ffn_core(x_ref, main_ref, mo_ref, wom_ref, woe_ref, gpost_ref, gfpost_ref,
                  wi_ref, wd_ref, main_feature_major):
    main_dot = _dot_tn if main_feature_major else _dot
    mix = main_dot(main_ref[...], wom_ref[...]) + _dot(mo_ref[...], woe_ref[...])
    x1 = x_ref[...] + _rms(mix, gpost_ref[...])
    hf = _rms_hat(x1).astype(BF16)
    acc = None
    for c in range(N_FF_CHUNKS):
        lo = c * FF_CHUNK
        g = _dot(hf, wi_ref[:, lo:lo + FF_CHUNK])
        u = _dot(hf, wi_ref[:, D_FF + lo:D_FF + lo + FF_CHUNK])
        act = (g * _sigmoid(g) * u).astype(BF16)
        d = _dot(act, wd_ref[lo:lo + FF_CHUNK, :])
        acc = d if acc is None else acc + d
    return x1 + _rms(acc, gfpost_ref[...])


def _mix_ffn_last_kernel(x_ref, main_ref, mo_ref, wom_ref, woe_ref, gpost_ref, gfpost_ref,
                         wi_ref, wd_ref, xo_ref):
    xo_ref[...] = _mix_ffn_core(x_ref, main_ref, mo_ref, wom_ref, woe_ref, gpost_ref,
                                gfpost_ref, wi_ref, wd_ref, main_feature_major=False)


def _rope(t1, t2, cos, sin):
    return t1 * cos - t2 * sin, t2 * cos + t1 * sin


def _rope_tables(pos_ref, inv_ref):
    ang = pos_ref[...].astype(F32) * inv_ref[...]
    cos_d = jnp.cos(ang)
    sin_d = jnp.sin(ang)
    grp = lax.broadcasted_iota(jnp.int32, ang.shape, 1) >> 5

    def spread(t, g):
        y = jnp.where(grp == g, t, 0.0)
        y = y + pltpu.roll(y, HEAD_DIM // 2, 1)
        return y + pltpu.roll(y, HEAD_DIM, 1)

    n_grp = LANES // (HEAD_DIM // 2)
    cos = jnp.concatenate([spread(cos_d, g) for g in range(n_grp)], axis=0)
    sin = jnp.concatenate([spread(sin_d, g) for g in range(n_grp)], axis=0)
    return cos, sin


def _mix_ffn_first_kernel(x_ref, main_ref, mo_ref, wom_ref, woe_ref, gpost_ref, gfpost_ref,
                          wi_ref, wd_ref,
                          pos_ref, inv_ref, wkv_ref, wb_ref, mk_ref, mv_ref,
                          xo_ref, ks_ref, vs_ref, q1_ref, mo1_ref):
    cos, sin = _rope_tables(pos_ref, inv_ref)
    x2 = _mix_ffn_core(x_ref, main_ref, mo_ref, wom_ref, woe_ref, gpost_ref, gfpost_ref,
                       wi_ref, wd_ref, main_feature_major=True)
    xo_ref[...] = x2
    xn = _rms_hat(x2).astype(BF16)

    pb = _dot(xn, wb_ref[...])
    s = _mem_scores(pb[:, SWA_Q_W:].astype(BF16), mk_ref)
    kv = _dot(xn, wkv_ref[...])
    p = _mem_probs(s)

    for gi in range(SWA_GROUPS):
        base = gi * 2 * LANES
        t1, t2 = _rope(pb[:, base:base + LANES], pb[:, base + LANES:base + 2 * LANES], cos, sin)
        q1_ref[:, base:base + LANES] = t1.astype(BF16)
        q1_ref[:, base + LANES:base + 2 * LANES] = t2.astype(BF16)
    mo1_ref[...] = _mem_out(p, mv_ref).astype(BF16)

    k1, k2 = _rope(kv[:, 0:LANES], kv[:, LANES:2 * LANES], cos, sin)
    scale = HEAD_DIM ** -0.5 * LOG2E
    ks_ref[:, 0:LANES] = (k1 * scale).astype(BF16)
    ks_ref[:, LANES:2 * LANES] = (k2 * scale).astype(BF16)
    vs_ref[...] = kv[:, SWA_KV_W:].astype(BF16)


def _mix_ffn(layer, x2d, main, mo, wom, woe, g_post, g_fpost, wi, wd, first=None, rows_per_batch=None):
    N = x2d.shape[0]
    tm = ROW_TILE if first is not None else FFN_LAST_ROW_TILE
    row = lambda w: pl.BlockSpec((tm, w), lambda i: (i, 0))
    slab =lambda w: pl.BlockSpec((None,) + w.shape[1:], lambda i: (layer, 0, 0), pipeline_mode=pl.Buffered(1))
    main_spec = row(main.shape[1]) if first is None else pl.BlockSpec((main.shape[0], tm), lambda i: (0, i))
    in_specs = [row(D_MODEL), main_spec, row(MEM_W),
                _resident(wom.shape), _resident(woe.shape),
                _resident((1, D_MODEL)), _resident((1, D_MODEL)),
                slab(wi), slab(wd)]
    args = [x2d, main, mo, wom, woe, g_post, g_fpost, wi, wd]
    out_specs = [row(D_MODEL)]
    out_shape = [jax.ShapeDtypeStruct((N, D_MODEL), F32)]
    if first is None:
        body = _mix_ffn_last_kernel
        name = "mix_ffn_last"
    else:
        body = _mix_ffn_first_kernel
        name = "mix_ffn_first"
        pos, inv, wkv, wb, mkm, mvm = first
        steps_per_batch = rows_per_batch // tm
        mem_spec = pl.BlockSpec((1, 1, MEM_HEADS, MEM_TOKENS, MEM_W),
                                lambda i: (1, i // steps_per_batch, 0, 0, 0))
        pos_spec = pl.BlockSpec((tm // (LANES // (HEAD_DIM // 2)), LANES), lambda i: (i, 0))
        in_specs += [pos_spec, _resident((1, LANES)), _resident(wkv.shape), _resident(wb.shape),
                     mem_spec, mem_spec]
        args += [pos, inv, wkv, wb, mkm, mvm]
        out_specs += [row(SWA_KV_W), row(SWA_KV_W), row(SWA_Q_W), row(MEM_W)]
        out_shape += [jax.ShapeDtypeStruct((N, SWA_KV_W), BF16), jax.ShapeDtypeStruct((N, SWA_KV_W), BF16),
                      jax.ShapeDtypeStruct((N, SWA_Q_W), BF16), jax.ShapeDtypeStruct((N, MEM_W), BF16)]
    return pl.pallas_call(
        body,
        grid=(N // tm,),
        in_specs=in_specs,
        out_specs=out_specs,
        out_shape=out_shape,
        compiler_params=pltpu.CompilerParams(dimension_semantics=("arbitrary",),
                                             vmem_limit_bytes=VMEM_LIMIT),
        name=name,
    )(*args)


def _swa_kernel(sink_ref, q_ref, kp_ref, kc_ref, vp_ref, vc_ref, o_ref):
    W = SWA_WINDOW
    G = SWA_GROUPS
    n = pl.program_id(1)
    n_sub = q_ref.shape[0] // W
    H = SWA_KV_HEADS
    kpos = lax.broadcasted_iota(jnp.int32, (2 * W, G * W), 0)
    qcol = lax.broadcasted_iota(jnp.int32, (2 * W, G * W), 1)
    diff = (qcol & (W - 1)) + W - kpos
    in_band = (diff >= 0) & (diff < W)
    k_lane = lax.broadcasted_iota(jnp.int32, (2 * W, SWA_KV_W), 1)
    sink = sink_ref[...]
    subs = range(n_sub)
    q, km, vm, valid = [], [], [], []
    for sub in subs:
        q.append(jnp.concatenate([q_ref[sub * W:(sub + 1) * W, gi * 2 * LANES:(gi + 1) * 2 * LANES]
                                  for gi in range(G)], axis=0))
        if sub == 0:
            k = jnp.concatenate([kp_ref[...], kc_ref[0:W, :]], axis=0)
            v = jnp.concatenate([vp_ref[...], vc_ref[0:W, :]], axis=0)
            valid.append(in_band & ((kpos >= W) | (n > 0)))
        else:
            k = kc_ref[(sub - 1) * W:(sub + 1) * W, :]
            v = vc_ref[(sub - 1) * W:(sub + 1) * W, :]
            valid.append(in_band)
        km.append(jnp.concatenate([jnp.where(((k_lane & (LANES - 1)) >> 5) == j, k, jnp.zeros((), BF16))
                                   for j in range(H)], axis=0))
        vm.append(jnp.concatenate([jnp.where((k_lane >> 6) == j, v, jnp.zeros((), BF16))
                                   for j in range(H)], axis=0))
    s = [_dot_nt(km[i], q[i]).reshape(H, 2 * W, G * W) for i in subs]
    s = [jnp.where(valid[i][None], s[i], -jnp.inf) for i in subs]
    m = [jnp.maximum(jnp.max(s[i], axis=1, keepdims=True), sink) for i in subs]
    e = [jnp.exp2(s[i] - m[i]) for i in subs]
    p = [e[i] * (1.0 / (jnp.sum(e[i], axis=1, keepdims=True) + jnp.exp2(sink - m[i]))) for i in subs]
    out = [_dot_tn(p[i].astype(BF16).reshape(H * 2 * W, G * W), vm[i]) for i in subs]
    for sub in subs:
        for gi in range(G):
            o_ref[sub * W:(sub + 1) * W, gi * 2 * LANES:(gi + 1) * 2 * LANES] = (
                out[sub][gi * W:(gi + 1) * W].astype(BF16))


def _swa(sink_cols, q1, ks, vs, B, S):
    W = SWA_WINDOW
    n_sub = SWA_BLOCKS_PER_STEP
    tq = n_sub * W
    steps = S // tq
    cur = lambda w: pl.BlockSpec((tq, w), lambda b, n: (b * steps + n, 0))
    prev = lambda w: pl.BlockSpec((W, w), lambda b, n: (b * (S // W) + jnp.maximum(n_sub * n - 1, 0), 0))
    return pl.pallas_call(
        _swa_kernel,
        grid=(B, steps),
        in_specs=[_resident(sink_cols.shape),
                  cur(SWA_Q_W), prev(SWA_KV_W), cur(SWA_KV_W), prev(SWA_KV_W), cur(SWA_KV_W)],
        out_specs=cur(SWA_Q_W),
        out_shape=jax.ShapeDtypeStruct((B * S, SWA_Q_W), BF16),
        compiler_params=pltpu.CompilerParams(dimension_semantics=("arbitrary", "arbitrary"),
                                             vmem_limit_bytes=VMEM_LIMIT),
        name="swa",
    )(sink_cols, q1, ks, ks, vs, vs)


def _pad_heads(w, heads, width, padded):
    w = w.reshape(w.shape[0], heads, width)
    return jnp.pad(w, ((0, 0), (0, 0), (0, padded - width))).reshape(w.shape[0], heads * padded)


def _swa_q_perm():
    idx = np.empty((SWA_GROUPS, 2, SWA_KV_HEADS, HEAD_DIM // 2), np.int32)
    for gi in range(SWA_GROUPS):
        for half in range(2):
            for j in range(SWA_KV_HEADS):
                idx[gi, half, j] = (j * SWA_GROUPS + gi) * HEAD_DIM + half * (HEAD_DIM // 2) + np.arange(HEAD_DIM // 2)
    return idx.reshape(-1)


def _swa_k_perm():
    idx = np.empty((2, SWA_KV_HEADS, HEAD_DIM // 2), np.int32)
    for half in range(2):
        for j in range(SWA_KV_HEADS):
            idx[half, j] = j * HEAD_DIM + half * (HEAD_DIM // 2) + np.arange(HEAD_DIM // 2)
    return idx.reshape(-1)


def _swa_out_perm():
    idx = np.empty((SWA_GROUPS, SWA_KV_HEADS, HEAD_DIM), np.int32)
    for gi in range(SWA_GROUPS):
        for j in range(SWA_KV_HEADS):
            idx[gi, j] = (j * SWA_GROUPS + gi) * HEAD_DIM + np.arange(HEAD_DIM)
    return idx.reshape(-1)


def kernel(x, mem, positions, g_mix_pre, g_mix_post, g_ffn_pre, g_ffn_post, g_mem, w_mem_kv, w_out,
           w_ffn_in, w_ffn_out, w_in_a, b_gates_a, g_mlstm_out, g_kv, w_kv, w_in_b, sinks_b):
    B, S, _ = x.shape
    N = B * S
    assert S % ROW_TILE == 0 and S % FFN_LAST_ROW_TILE == 0 and S % (MLSTM_CHUNK * MLSTM_CHUNKS_PER_STEP) == 0
    assert S % (SWA_WINDOW * SWA_BLOCKS_PER_STEP) == 0
    x2d = x.reshape(N, D_MODEL)
    vec = lambda g: g.reshape(1, -1).astype(F32)

    mkm, mvm = _mem_kv(mem, g_mem, w_mem_kv.astype(BF16))

    fold = lambda g, w: g.astype(F32)[:, None] * w

    wa = fold(g_mix_pre[0], w_in_a[0])
    c0 = 0
    wq = _pad_heads(wa[:, c0:c0 + MLSTM_QK_W], MLSTM_HEADS, MLSTM_DQK, DQK_PAD).astype(BF16)
    c0 += MLSTM_QK_W
    wk = _pad_heads(wa[:, c0:c0 + MLSTM_QK_W] * (MLSTM_DQK ** -0.5), MLSTM_HEADS, MLSTM_DQK, DQK_PAD).astype(BF16)
    c0 += MLSTM_QK_W
    wv = wa[:, c0:c0 + MLSTM_V_W].T.astype(BF16)
    c0 += MLSTM_V_W
    wo = wa[:, c0:c0 + MLSTM_V_W].T.astype(BF16)
    c0 += MLSTM_V_W
    wgt = jnp.pad(wa[:, c0:c0 + 2 * MLSTM_HEADS], ((0, 0), (0, LANES - 2 * MLSTM_HEADS))).astype(BF16)
    c0 += 2 * MLSTM_HEADS
    wmq = wa[:, c0:c0 + MEM_W].astype(BF16)
    bg = jnp.pad(b_gates_a[0].astype(F32), (0, LANES - 2 * MLSTM_HEADS)).reshape(1, LANES)

    q, k, v, o, gt, gtt, mo0 = _proj_a(x2d, wq, wk, wv, wo, wmq, wgt, bg, mkm, mvm, S)
    hm = _mlstm(q, k, v, o, gt, gtt, B, S)

    w_ffn_in_bf = (g_ffn_pre.astype(F32)[:, :, None] * w_ffn_in).astype(BF16)
    w_ffn_out_bf = w_ffn_out.astype(BF16)
    wo0 = jnp.concatenate([fold(g_mlstm_out[0], w_out[0][:MLSTM_V_W]), w_out[0][MLSTM_V_W:]], axis=0).astype(BF16)
    inv = 1.0 / (ROPE_THETA ** (jnp.arange(0, HEAD_DIM, 2, dtype=F32) / HEAD_DIM))
    inv = jnp.tile(inv, SWA_KV_HEADS).reshape(1, LANES)
    wkv = fold(g_kv, w_kv)
    wkv = jnp.concatenate([wkv[:, :SWA_KV_W][:, _swa_k_perm()], wkv[:, SWA_KV_W:]], axis=1).astype(BF16)
    wb = fold(g_mix_pre[1], w_in_b[0])
    wb = jnp.concatenate([wb[:, :SWA_Q_W][:, _swa_q_perm()], wb[:, SWA_Q_W:]], axis=1).astype(BF16)
    n_grp = LANES // (HEAD_DIM // 2)
    pos_dense = positions.reshape(N // ROW_TILE, n_grp, ROW_TILE // n_grp).transpose(0, 2, 1)
    pos_dense = jnp.repeat(pos_dense, HEAD_DIM // 2, axis=2).reshape(N // n_grp, LANES)
    first = (pos_dense, inv, wkv, wb, mkm, mvm)
    x1, ks, vs, q1, mo1 = _mix_ffn(0, x2d, hm, mo0, wo0[:MLSTM_V_W], wo0[MLSTM_V_W:],
                                   vec(g_mix_post[0]), vec(g_ffn_post[0]),
                                   w_ffn_in_bf, w_ffn_out_bf, first=first, rows_per_batch=S)

    sink_rows = jnp.repeat((sinks_b[0].astype(F32) * LOG2E).reshape(SWA_KV_HEADS, SWA_GROUPS), SWA_WINDOW, axis=1)
    attn = _swa(sink_rows.reshape(SWA_KV_HEADS, 1, SWA_GROUPS * SWA_WINDOW), q1, ks, vs, B, S)
    wo1 = w_out[1]
    (xo,) = _mix_ffn(1, x1, attn, mo1, wo1[:SWA_Q_W][_swa_out_perm()].astype(BF16), wo1[SWA_Q_W:].astype(BF16),
                     vec(g_mix_post[1]), vec(g_ffn_post[1]),
                     w_ffn_in_bf, w_ffn_out_bf)
    return xo.reshape(B, S, D_MODEL)
```

```python
import functools

import numpy as np
import jax
import jax.numpy as jnp
from jax import lax
from jax.experimental import pallas as pl
from jax.experimental.pallas import tpu as pltpu

F32 = jnp.float32
BF16 = jnp.bfloat16

D_MODEL = 1024
DEPTH = 2
HEAD_DIM = 64
EPS = 1e-6
ROPE_THETA = 10000.0
LOG2E = 1.4426950408889634

MLSTM_HEADS = 4
MLSTM_DV = 192
MLSTM_DQK = 96
MLSTM_QK_W = MLSTM_HEADS * MLSTM_DQK
MLSTM_V_W = MLSTM_HEADS * MLSTM_DV
GATE_SOFTCAP = 15.0
M_INIT = -1e30

SWA_Q_HEADS = 12
SWA_KV_HEADS = 4
SWA_GROUPS = SWA_Q_HEADS // SWA_KV_HEADS
SWA_Q_W = SWA_Q_HEADS * HEAD_DIM
SWA_KV_W = SWA_KV_HEADS * HEAD_DIM
SWA_WINDOW = 128

MEM_TOKENS = 256
MEM_HEADS = 4
MEM_HEAD_DIM = 64
MEM_W = MEM_HEADS * MEM_HEAD_DIM

D_FF = 2816

LANES = 128
MXU_TILE = 256

DQK_PAD = LANES
QK_PAD_W = MLSTM_HEADS * DQK_PAD
V_WIN = MXU_TILE
MLSTM_CHUNK = 256
MLSTM_CHUNKS_PER_STEP = 8
FF_CHUNK = MXU_TILE
N_FF_CHUNKS = D_FF // FF_CHUNK
PROJ_A_ROW_TILE = 1024
PROJ_B_ROW_TILE = 1024
PROJ_B_ROW_GROUP = 512
FFN_ROW_TILE = 1024
FFN_ROW_GROUP = 512
SWA_BLOCKS_PER_STEP = 4
VMEM_LIMIT = 56 * 1024 * 1024

_V_WIN_START = (0, 128, 384, 512)
_V_WIN_OFF = (0, 64, 0, 64)
_V_ONES_COL = (192, 0, 192, 0)


def _rms_hat(x):
    return x * lax.rsqrt(jnp.mean(x * x, axis=-1, keepdims=True) + EPS)


def _rms(x, g):
    return _rms_hat(x) * g


def _dot(a, b):
    return jnp.dot(a, b, preferred_element_type=F32)


def _dot_nt(a, b):
    return lax.dot_general(a, b, (((1,), (1,)), ((), ())), preferred_element_type=F32)


def _dot_tn(a, b):
    return lax.dot_general(a, b, (((0,), (0,)), ((), ())), preferred_element_type=F32)


def _sigmoid(x):
    return 1.0 / (1.0 + jnp.exp(-x))


def _resident(shape):
    nd = len(shape)
    return pl.BlockSpec(shape, lambda *_: (0,) * nd, pipeline_mode=pl.Buffered(1))


def _mem_kv_kernel(mem_ref, g_ref, w_ref, mk_ref, mv_ref):
    B = mem_ref.shape[0]
    hn = _rms(mem_ref[...].reshape(B * MEM_TOKENS, D_MODEL), g_ref[0]).astype(BF16)
    kv = _dot(hn, w_ref[0])
    mk = kv[:, :MEM_W] * (MEM_HEAD_DIM ** -0.5 * LOG2E)
    mv = kv[:, MEM_W:]
    lane_head = lax.broadcasted_iota(jnp.int32, mk.shape, 1) >> 6
    for h in range(MEM_HEADS):
        sel = lane_head == h
        mk_ref[0, :, h] = jnp.where(sel, mk, 0.0).astype(BF16).reshape(B, MEM_TOKENS, MEM_W)
        mv_ref[0, :, h] = jnp.where(sel, mv, 0.0).astype(BF16).reshape(B, MEM_TOKENS, MEM_W)


def _mem_kv(mem, g_mem, w_mem_kv_bf16):
    B = mem.shape[0]
    out_sds = jax.ShapeDtypeStruct((DEPTH, B, MEM_HEADS, MEM_TOKENS, MEM_W), BF16)
    out_spec = pl.BlockSpec((1, B, MEM_HEADS, MEM_TOKENS, MEM_W), lambda l: (l, 0, 0, 0, 0))
    return pl.pallas_call(
        _mem_kv_kernel,
        grid=(DEPTH,),
        in_specs=[
            pl.BlockSpec((B, MEM_TOKENS, D_MODEL), lambda l: (0, 0, 0)),
            pl.BlockSpec((1, 1, D_MODEL), lambda l: (l, 0, 0)),
            pl.BlockSpec((1, D_MODEL, 2 * MEM_W), lambda l: (l, 0, 0)),
        ],
        out_specs=[out_spec, out_spec],
        out_shape=[out_sds, out_sds],
        compiler_params=pltpu.CompilerParams(dimension_semantics=("arbitrary",),
                                             vmem_limit_bytes=VMEM_LIMIT),
        name="mem_kv",
    )(mem, g_mem.reshape(DEPTH, 1, D_MODEL), w_mem_kv_bf16)


def _mem_scores(mq, mk_ref):
    return [_dot_nt(mq, mk_ref[0, 0, h]) for h in range(MEM_HEADS)]


def _mem_probs(s):
    e = [jnp.exp2(sh - jnp.max(sh, axis=-1, keepdims=True)) for sh in s]
    return [(eh * (1.0 / jnp.sum(eh, axis=-1, keepdims=True))).astype(BF16) for eh in e]


def _mem_out(p, mv_ref):
    out = _dot(p[0], mv_ref[0, 0, 0])
    for h in range(1, MEM_HEADS):
        out = out + _dot(p[h], mv_ref[0, 0, h])
    return out


def _proj_a_kernel(x_ref, wq_ref, wk_ref, wv_ref, wo_ref, wmq_ref, wg_ref, bg_ref, mk_ref, mv_ref,
                   q_ref, k_ref, v_ref, o_ref, gt_ref, gtt_ref, mo_ref):
    hn = _rms_hat(x_ref[...]).astype(BF16)
    mq = _dot(hn, wmq_ref[...]).astype(BF16)
    s = _mem_scores(mq, mk_ref)
    gates = _dot(hn, wg_ref[...]) + bg_ref[...]
    q_ref[...] = _dot(hn, wq_ref[...]).astype(BF16)
    gates = GATE_SOFTCAP * jnp.tanh(gates * (1.0 / GATE_SOFTCAP))
    log_sig = jnp.minimum(gates, 0.0) - jnp.log1p(jnp.exp(-jnp.abs(gates)))
    lane = lax.broadcasted_iota(jnp.int32, gates.shape, 1)
    gl = jnp.where(lane < MLSTM_HEADS, gates, log_sig) * LOG2E
    p = _mem_probs(s)
    k_ref[...] = _dot(hn, wk_ref[...]).astype(BF16)
    g_hi = gl.astype(BF16)
    rest = gl - g_hi.astype(F32)
    g_mid = rest.astype(BF16)
    g_lo = (rest - g_mid.astype(F32)).astype(BF16)
    L = MLSTM_CHUNK
    tri = (lax.broadcasted_iota(jnp.int32, (L, L), 1) <= lax.broadcasted_iota(jnp.int32, (L, L), 0)).astype(BF16)
    csum = jnp.concatenate(
        [_dot(tri, g_hi[r:r + L]) + _dot(tri, g_mid[r:r + L]) + _dot(tri, g_lo[r:r + L])
         for r in range(0, gl.shape[0], L)], axis=0)
    v_ref[...] = _dot_nt(wv_ref[...], hn).astype(BF16)
    y = jnp.where(lane < MLSTM_HEADS, gl, csum)
    gt_ref[...] = y
    gtt_ref[...] = y.T[0:2 * MLSTM_HEADS, :]
    mo_ref[...] = _mem_out(p, mv_ref).astype(BF16)
    o_ref[...] = _dot_nt(wo_ref[...], hn).astype(BF16)


def _proj_a(x2d, wq, wk, wv, wo, wmq, wg, bg, mkm, mvm, rows_per_batch):
    N = x2d.shape[0]
    tm = PROJ_A_ROW_TILE
    steps_per_batch = rows_per_batch // tm
    row = lambda w: pl.BlockSpec((tm, w), lambda i: (i, 0))
    col = pl.BlockSpec((MLSTM_V_W, tm), lambda i: (0, i))
    mem_spec = pl.BlockSpec((1, 1, MEM_HEADS, MEM_TOKENS, MEM_W),
                            lambda i: (0, i // steps_per_batch, 0, 0, 0))
    return pl.pallas_call(
        _proj_a_kernel,
        grid=(N // tm,),
        in_specs=[row(D_MODEL),
                  _resident(wq.shape), _resident(wk.shape), _resident(wv.shape), _resident(wo.shape),
                  _resident(wmq.shape), _resident(wg.shape), _resident((1, LANES)),
                  mem_spec, mem_spec],
        out_specs=[row(QK_PAD_W), row(QK_PAD_W), col, col, row(LANES),
                   pl.BlockSpec((2 * MLSTM_HEADS, tm), lambda i: (0, i)), row(MEM_W)],
        out_shape=[jax.ShapeDtypeStruct((N, QK_PAD_W), BF16), jax.ShapeDtypeStruct((N, QK_PAD_W), BF16),
                   jax.ShapeDtypeStruct((MLSTM_V_W, N), BF16), jax.ShapeDtypeStruct((MLSTM_V_W, N), BF16),
                   jax.ShapeDtypeStruct((N, LANES), F32), jax.ShapeDtypeStruct((2 * MLSTM_HEADS, N), F32),
                   jax.ShapeDtypeStruct((N, MEM_W), BF16)],
        compiler_params=pltpu.CompilerParams(dimension_semantics=("arbitrary",),
                                             vmem_limit_bytes=VMEM_LIMIT),
        name="proj_a",
    )(x2d, wq, wk, wv, wo, wmq, wg, bg, mkm, mvm)


def _mlstm_kernel(q_ref, k_ref, vt_ref, ot_ref, gt_ref, gtt_ref, hmt_ref, c_scr, m_scr):
    L = MLSTM_CHUNK

    @pl.when(pl.program_id(1) == 0)
    def _():
        c_scr[...] = jnp.zeros(c_scr.shape, F32)
        m_scr[...] = jnp.full(m_scr.shape, M_INIT, F32)

    causal = lax.broadcasted_iota(jnp.int32, (L, L), 0) <= lax.broadcasted_iota(jnp.int32, (L, L), 1)
    feat = lax.broadcasted_iota(jnp.int32, (V_WIN, L), 0)
    heads = range(MLSTM_HEADS)
    m_st = [m_scr[h][0:1, 0:1] for h in heads]
    c_st = [c_scr[h] for h in heads]

    for r0 in range(0, q_ref.shape[0], L):
        y = gt_ref[r0:r0 + L, :]
        yt = gtt_ref[:, r0:r0 + L]
        b_row = [yt[MLSTM_HEADS + h:MLSTM_HEADS + h + 1, :] for h in heads]
        c_row = [b_row[h] - yt[h:h + 1, :] for h in heads]
        c_col = [y[:, MLSTM_HEADS + h:MLSTM_HEADS + h + 1] - y[:, h:h + 1] for h in heads]
        g_tot = [y[L - 1:L, MLSTM_HEADS + h:MLSTM_HEADS + h + 1] for h in heads]
        qh = [q_ref[r0:r0 + L, h * DQK_PAD:(h + 1) * DQK_PAD] for h in heads]
        kh = [k_ref[r0:r0 + L, h * DQK_PAD:(h + 1) * DQK_PAD] for h in heads]
        vt_aug = [jnp.where(feat == _V_ONES_COL[h], jnp.ones((), BF16),
                            vt_ref[_V_WIN_START[h]:_V_WIN_START[h] + V_WIN, r0:r0 + L]) for h in heads]

        qk_t = [_dot_nt(kh[h], qh[h]) for h in heads]
        inter_t = [_dot_nt(c_st[h].astype(BF16), qh[h]) for h in heads]
        dmat = [jnp.where(causal, b_row[h] - c_col[h], -jnp.inf) for h in heads]
        inter_log = [b_row[h] + m_st[h] for h in heads]
        m_row = [jnp.maximum(inter_log[h], jnp.max(dmat[h], axis=0, keepdims=True)) for h in heads]
        scores_t = [(qk_t[h] * jnp.exp2(dmat[h] - m_row[h])).astype(BF16) for h in heads]
        num_t = [_dot(vt_aug[h], scores_t[h]) + jnp.exp2(inter_log[h] - m_row[h]) * inter_t[h] for h in heads]

        a = [g_tot[h] - c_row[h] for h in heads]
        m_new = [jnp.maximum(g_tot[h] + m_st[h], jnp.max(a[h], axis=1, keepdims=True)) for h in heads]
        vtw = [vt_aug[h] * jnp.exp2(a[h] - m_new[h]).astype(BF16) for h in heads]
        c_upd = [_dot(vtw[h], kh[h]) for h in heads]

        for h in heads:
            off, ones_row, lo = _V_WIN_OFF[h], _V_ONES_COL[h], h * MLSTM_DV
            den = num_t[h][ones_row:ones_row + 1, :]
            inv_dd = 1.0 / jnp.maximum(jnp.abs(den), jnp.exp2(-m_row[h]))
            h_t = num_t[h][off:off + MLSTM_DV, :]
            ms = jnp.sum(h_t * h_t, axis=0, keepdims=True) * (1.0 / MLSTM_DV)
            scale = inv_dd * lax.rsqrt(ms * inv_dd * inv_dd + EPS)
            og = _sigmoid(ot_ref[lo:lo + MLSTM_DV, r0:r0 + L].astype(F32))
            hmt_ref[lo:lo + MLSTM_DV, r0:r0 + L] = (h_t * (scale * og)).astype(BF16)

        c_st = [jnp.exp2(g_tot[h] + m_st[h] - m_new[h]) * c_st[h] + c_upd[h] for h in heads]
        m_st = m_new

    for h in heads:
        c_scr[h] = c_st[h]
        m_scr[h] = jnp.broadcast_to(m_st[h], m_scr.shape[1:])


def _mlstm(q, k, vt, ot, gt, gtt, B, S):
    rows = MLSTM_CHUNK * MLSTM_CHUNKS_PER_STEP
    steps = S // rows
    N = B * S
    row = lambda w: pl.BlockSpec((rows, w), lambda b, c: (b * steps + c, 0))
    col = lambda h: pl.BlockSpec((h, rows), lambda b, c: (0, b * steps + c))
    return pl.pallas_call(
        _mlstm_kernel,
        grid=(B, steps),
        in_specs=[row(QK_PAD_W), row(QK_PAD_W), col(MLSTM_V_W), col(MLSTM_V_W), row(LANES),
                  col(2 * MLSTM_HEADS)],
        out_specs=col(MLSTM_V_W),
        out_shape=jax.ShapeDtypeStruct((MLSTM_V_W, N), BF16),
        scratch_shapes=[pltpu.VMEM((MLSTM_HEADS, V_WIN, DQK_PAD), F32),
                        pltpu.VMEM((MLSTM_HEADS, 8, LANES), F32)],
        compiler_params=pltpu.CompilerParams(dimension_semantics=("arbitrary", "arbitrary"),
                                             vmem_limit_bytes=VMEM_LIMIT),
        name="mlstm",
    )(q, k, vt, ot, gt, gtt)


def _mix_ffn_core(x_ref, main_ref, mo_ref, wom_ref, woe_ref, gpost_ref, gfpost_ref,
                  wi_ref, wd_ref, main_feature_major):
    main_dot = _dot_tn if main_feature_major else _dot
    groups = [slice(r, r + FFN_ROW_GROUP) for r in range(0, x_ref.shape[0], FFN_ROW_GROUP)]
    mix = [main_dot(main_ref[:, g] if main_feature_major else main_ref[g, :], wom_ref[...])
           + _dot(mo_ref[g, :], woe_ref[...]) for g in groups]
    x1 = [x_ref[g, :] + _rms(m, gpost_ref[...]) for g, m in zip(groups, mix)]
    hf = [_rms_hat(t).astype(BF16) for t in x1]
    acc = [None] * len(groups)
    for c in range(N_FF_CHUNKS):
        lo = c * FF_CHUNK
        for i, h in enumerate(hf):
            g = _dot(h, wi_ref[:, lo:lo + FF_CHUNK])
            u = _dot(h, wi_ref[:, D_FF + lo:D_FF + lo + FF_CHUNK])
            act = (g * _sigmoid(g) * u).astype(BF16)
            d = _dot(act, wd_ref[lo:lo + FF_CHUNK, :])
            acc[i] = d if acc[i] is None else acc[i] + d
    return jnp.concatenate([t + _rms(a, gfpost_ref[...]) for t, a in zip(x1, acc)], axis=0)


def _mix_ffn_kernel(x_ref, main_ref, mo_ref, wom_ref, woe_ref, gpost_ref, gfpost_ref,
                    wi_ref, wd_ref, xo_ref, *, main_feature_major):
    xo_ref[...] = _mix_ffn_core(x_ref, main_ref, mo_ref, wom_ref, woe_ref, gpost_ref,
                                gfpost_ref, wi_ref, wd_ref, main_feature_major)


def _rope(t1, t2, cos, sin):
    return t1 * cos - t2 * sin, t2 * cos + t1 * sin


def _rope_tables(pos_ref, inv_ref):
    ang = pos_ref[...].astype(F32) * inv_ref[...]
    cos_d = jnp.cos(ang)
    sin_d = jnp.sin(ang)
    grp = lax.broadcasted_iota(jnp.int32, ang.shape, 1) >> 5

    def spread(t, g):
        y = jnp.where(grp == g, t, 0.0)
        y = y + pltpu.roll(y, HEAD_DIM // 2, 1)
        return y + pltpu.roll(y, HEAD_DIM, 1)

    n_grp = LANES // (HEAD_DIM // 2)
    cos = jnp.concatenate([spread(cos_d, g) for g in range(n_grp)], axis=0)
    sin = jnp.concatenate([spread(sin_d, g) for g in range(n_grp)], axis=0)
    return cos, sin


def _mix_ffn(layer, x2d, main, mo, wom, woe, g_post, g_fpost, wi, wd, main_feature_major):
    N = x2d.shape[0]
    tm = FFN_ROW_TILE
    row = lambda w: pl.BlockSpec((tm, w), lambda i: (i, 0))
    slab = lambda w: pl.BlockSpec((None,) + w.shape[1:], lambda i: (layer, 0, 0), pipeline_mode=pl.Buffered(1))
    main_spec = pl.BlockSpec((main.shape[0], tm), lambda i: (0, i)) if main_feature_major else row(main.shape[1])
    return pl.pallas_call(
        functools.partial(_mix_ffn_kernel, main_feature_major=main_feature_major),
        grid=(N // tm,),
        in_specs=[row(D_MODEL), main_spec, row(MEM_W),
                  _resident(wom.shape), _resident(woe.shape),
                  _resident((1, D_MODEL)), _resident((1, D_MODEL)),
                  slab(wi), slab(wd)],
        out_specs=row(D_MODEL),
        out_shape=jax.ShapeDtypeStruct((N, D_MODEL), F32),
        compiler_params=pltpu.CompilerParams(dimension_semantics=("arbitrary",),
                                             vmem_limit_bytes=VMEM_LIMIT),
        name="mix_ffn_%d" % layer,
    )(x2d, main, mo, wom, woe, g_post, g_fpost, wi, wd)


def _proj_b_kernel(x_ref, pos_ref, inv_ref, wkv_ref, wb_ref, mk_ref, mv_ref,
                   ks_ref, vs_ref, q1_ref, mo1_ref):
    groups = [slice(r, r + PROJ_B_ROW_GROUP) for r in range(0, x_ref.shape[0], PROJ_B_ROW_GROUP)]
    xn = [_rms_hat(x_ref[g, :]).astype(BF16) for g in groups]
    pb = [_dot(t, wb_ref[...]) for t in xn]
    s = [_mem_scores(t[:, SWA_Q_W:].astype(BF16), mk_ref) for t in pb]
    cos, sin = _rope_tables(pos_ref, inv_ref)
    kv = [_dot(t, wkv_ref[...]) for t in xn]
    p = [_mem_probs(t) for t in s]

    for g, t in zip(groups, pb):
        for gi in range(SWA_GROUPS):
            base = gi * 2 * LANES
            t1, t2 = _rope(t[:, base:base + LANES], t[:, base + LANES:base + 2 * LANES], cos[g], sin[g])
            q1_ref[g, base:base + LANES] = t1.astype(BF16)
            q1_ref[g, base + LANES:base + 2 * LANES] = t2.astype(BF16)
    for g, t in zip(groups, p):
        mo1_ref[g, :] = _mem_out(t, mv_ref).astype(BF16)

    scale = HEAD_DIM ** -0.5 * LOG2E
    for g, t in zip(groups, kv):
        k1, k2 = _rope(t[:, 0:LANES], t[:, LANES:2 * LANES], cos[g], sin[g])
        ks_ref[g, 0:LANES] = (k1 * scale).astype(BF16)
        ks_ref[g, LANES:2 * LANES] = (k2 * scale).astype(BF16)
        vs_ref[g, :] = t[:, SWA_KV_W:].astype(BF16)


def _proj_b(x2d, pos, inv, wkv, wb, mkm, mvm, rows_per_batch):
    N = x2d.shape[0]
    tm = PROJ_B_ROW_TILE
    steps_per_batch = rows_per_batch // tm
    row = lambda w: pl.BlockSpec((tm, w), lambda i: (i, 0))
    mem_spec = pl.BlockSpec((1, 1, MEM_HEADS, MEM_TOKENS, MEM_W),
                            lambda i: (1, i // steps_per_batch, 0, 0, 0))
    pos_spec = pl.BlockSpec((tm // (LANES // (HEAD_DIM // 2)), LANES), lambda i: (i, 0))
    return pl.pallas_call(
        _proj_b_kernel,
        grid=(N // tm,),
        in_specs=[row(D_MODEL), pos_spec, _resident((1, LANES)), _resident(wkv.shape), _resident(wb.shape),
                  mem_spec, mem_spec],
        out_specs=[row(SWA_KV_W), row(SWA_KV_W), row(SWA_Q_W), row(MEM_W)],
        out_shape=[jax.ShapeDtypeStruct((N, SWA_KV_W), BF16), jax.ShapeDtypeStruct((N, SWA_KV_W), BF16),
                   jax.ShapeDtypeStruct((N, SWA_Q_W), BF16), jax.ShapeDtypeStruct((N, MEM_W), BF16)],
        compiler_params=pltpu.CompilerParams(dimension_semantics=("arbitrary",),
                                             vmem_limit_bytes=VMEM_LIMIT),
        name="proj_b",
    )(x2d, pos, inv, wkv, wb, mkm, mvm)


def _swa_kernel(sink_ref, q_ref, kp_ref, kc_ref, vp_ref, vc_ref, o_ref):
    W = SWA_WINDOW
    G = SWA_GROUPS
    n = pl.program_id(1)
    n_sub = q_ref.shape[0] // W
    H = SWA_KV_HEADS
    kpos = lax.broadcasted_iota(jnp.int32, (2 * W, G * W), 0)
    qcol = lax.broadcasted_iota(jnp.int32, (2 * W, G * W), 1)
    diff = (qcol & (W - 1)) + W - kpos
    in_band = (diff >= 0) & (diff < W)
    k_lane = lax.broadcasted_iota(jnp.int32, (2 * W, SWA_KV_W), 1)
    sink = sink_ref[...]
    subs = range(n_sub)
    q, km, vm, valid = [], [], [], []
    for sub in subs:
        q.append(jnp.concatenate([q_ref[sub * W:(sub + 1) * W, gi * 2 * LANES:(gi + 1) * 2 * LANES]
                                  for gi in range(G)], axis=0))
        if sub == 0:
            k = jnp.concatenate([kp_ref[...], kc_ref[0:W, :]], axis=0)
            v = jnp.concatenate([vp_ref[...], vc_ref[0:W, :]], axis=0)
            valid.append(in_band & ((kpos >= W) | (n > 0)))
        else:
            k = kc_ref[(sub - 1) * W:(sub + 1) * W, :]
            v = vc_ref[(sub - 1) * W:(sub + 1) * W, :]
            valid.append(in_band)
        km.append(jnp.concatenate([jnp.where(((k_lane & (LANES - 1)) >> 5) == j, k, jnp.zeros((), BF16))
                                   for j in range(H)], axis=0))
        vm.append(jnp.concatenate([jnp.where((k_lane >> 6) == j, v, jnp.zeros((), BF16))
                                   for j in range(H)], axis=0))
    s = [_dot_nt(km[i], q[i]).reshape(H, 2 * W, G * W) for i in subs]
    s = [jnp.where(valid[i][None], s[i], -jnp.inf) for i in subs]
    m = [jnp.maximum(jnp.max(s[i], axis=1, keepdims=True), sink) for i in subs]
    e = [jnp.exp2(s[i] - m[i]) for i in subs]
    p = [e[i] * (1.0 / (jnp.sum(e[i], axis=1, keepdims=True) + jnp.exp2(sink - m[i]))) for i in subs]
    out = [_dot_tn(p[i].astype(BF16).reshape(H * 2 * W, G * W), vm[i]) for i in subs]
    for sub in subs:
        for gi in range(G):
            o_ref[sub * W:(sub + 1) * W, gi * 2 * LANES:(gi + 1) * 2 * LANES] = (
                out[sub][gi * W:(gi + 1) * W].astype(BF16))


def _swa(sink_cols, q1, ks, vs, B, S):
    W = SWA_WINDOW
    n_sub = SWA_BLOCKS_PER_STEP
    tq = n_sub * W
    steps = S // tq
    cur = lambda w: pl.BlockSpec((tq, w), lambda b, n: (b * steps + n, 0))
    prev = lambda w: pl.BlockSpec((W, w), lambda b, n: (b * (S // W) + jnp.maximum(n_sub * n - 1, 0), 0))
    return pl.pallas_call(
        _swa_kernel,
        grid=(B, steps),
        in_specs=[_resident(sink_cols.shape),
                  cur(SWA_Q_W), prev(SWA_KV_W), cur(SWA_KV_W), prev(SWA_KV_W), cur(SWA_KV_W)],
        out_specs=cur(SWA_Q_W),
        out_shape=jax.ShapeDtypeStruct((B * S, SWA_Q_W), BF16),
        compiler_params=pltpu.CompilerParams(dimension_semantics=("arbitrary", "arbitrary"),
                                             vmem_limit_bytes=VMEM_LIMIT),
        name="swa",
    )(sink_cols, q1, ks, ks, vs, vs)


def _pad_heads(w, heads, width, padded):
    w = w.reshape(w.shape[0], heads, width)
    return jnp.pad(w, ((0, 0), (0, 0), (0, padded - width))).reshape(w.shape[0], heads * padded)


def _swa_q_perm():
    idx = np.empty((SWA_GROUPS, 2, SWA_KV_HEADS, HEAD_DIM // 2), np.int32)
    for gi in range(SWA_GROUPS):
        for half in range(2):
            for j in range(SWA_KV_HEADS):
                idx[gi, half, j] = (j * SWA_GROUPS + gi) * HEAD_DIM + half * (HEAD_DIM // 2) + np.arange(HEAD_DIM // 2)
    return idx.reshape(-1)


def _swa_k_perm():
    idx = np.empty((2, SWA_KV_HEADS, HEAD_DIM // 2), np.int32)
    for half in range(2):
        for j in range(SWA_KV_HEADS):
            idx[half, j] = j * HEAD_DIM + half * (HEAD_DIM // 2) + np.arange(HEAD_DIM // 2)
    return idx.reshape(-1)


def _swa_out_perm():
    idx = np.empty((SWA_GROUPS, SWA_KV_HEADS, HEAD_DIM), np.int32)
    for gi in range(SWA_GROUPS):
        for j in range(SWA_KV_HEADS):
            idx[gi, j] = (j * SWA_GROUPS + gi) * HEAD_DIM + np.arange(HEAD_DIM)
    return idx.reshape(-1)


def kernel(x, mem, positions, g_mix_pre, g_mix_post, g_ffn_pre, g_ffn_post, g_mem, w_mem_kv, w_out,
           w_ffn_in, w_ffn_out, w_in_a, b_gates_a, g_mlstm_out, g_kv, w_kv, w_in_b, sinks_b):
    B, S, _ = x.shape
    N = B * S
    assert S % PROJ_A_ROW_TILE == 0 and S % PROJ_B_ROW_TILE == 0 and S % FFN_ROW_TILE == 0
    assert S % (MLSTM_CHUNK * MLSTM_CHUNKS_PER_STEP) == 0
    assert S % (SWA_WINDOW * SWA_BLOCKS_PER_STEP) == 0
    x2d = x.reshape(N, D_MODEL)
    vec = lambda g: g.reshape(1, -1).astype(F32)

    mkm, mvm = _mem_kv(mem, g_mem, w_mem_kv.astype(BF16))

    fold = lambda g, w: g.astype(F32)[:, None] * w

    wa = fold(g_mix_pre[0], w_in_a[0])
    c0 = 0
    wq = _pad_heads(wa[:, c0:c0 + MLSTM_QK_W], MLSTM_HEADS, MLSTM_DQK, DQK_PAD).astype(BF16)
    c0 += MLSTM_QK_W
    wk = _pad_heads(wa[:, c0:c0 + MLSTM_QK_W] * (MLSTM_DQK ** -0.5), MLSTM_HEADS, MLSTM_DQK, DQK_PAD).astype(BF16)
    c0 += MLSTM_QK_W
    wv = wa[:, c0:c0 + MLSTM_V_W].T.astype(BF16)
    c0 += MLSTM_V_W
    wo = wa[:, c0:c0 + MLSTM_V_W].T.astype(BF16)
    c0 += MLSTM_V_W
    wgt = jnp.pad(wa[:, c0:c0 + 2 * MLSTM_HEADS], ((0, 0), (0, LANES - 2 * MLSTM_HEADS))).astype(BF16)
    c0 += 2 * MLSTM_HEADS
    wmq = wa[:, c0:c0 + MEM_W].astype(BF16)
    bg = jnp.pad(b_gates_a[0].astype(F32), (0, LANES - 2 * MLSTM_HEADS)).reshape(1, LANES)

    q, k, v, o, gt, gtt, mo0 = _proj_a(x2d, wq, wk, wv, wo, wmq, wgt, bg, mkm, mvm, S)
    hm = _mlstm(q, k, v, o, gt, gtt, B, S)

    w_ffn_in_bf = (g_ffn_pre.astype(F32)[:, :, None] * w_ffn_in).astype(BF16)
    w_ffn_out_bf = w_ffn_out.astype(BF16)
    wo0 = jnp.concatenate([fold(g_mlstm_out[0], w_out[0][:MLSTM_V_W]), w_out[0][MLSTM_V_W:]], axis=0).astype(BF16)
    inv = 1.0 / (ROPE_THETA ** (jnp.arange(0, HEAD_DIM, 2, dtype=F32) / HEAD_DIM))
    inv = jnp.tile(inv, SWA_KV_HEADS).reshape(1, LANES)
    wkv = fold(g_kv, w_kv)
    wkv = jnp.concatenate([wkv[:, :SWA_KV_W][:, _swa_k_perm()], wkv[:, SWA_KV_W:]], axis=1).astype(BF16)
    wb = fold(g_mix_pre[1], w_in_b[0])
    wb = jnp.concatenate([wb[:, :SWA_Q_W][:, _swa_q_perm()], wb[:, SWA_Q_W:]], axis=1).astype(BF16)
    n_grp = LANES // (HEAD_DIM // 2)
    pos_dense = positions.reshape(N // PROJ_B_ROW_TILE, n_grp, PROJ_B_ROW_TILE // n_grp).transpose(0, 2, 1)
    pos_dense = jnp.repeat(pos_dense, HEAD_DIM // 2, axis=2).reshape(N // n_grp, LANES)
    x1 = _mix_ffn(0, x2d, hm, mo0, wo0[:MLSTM_V_W], wo0[MLSTM_V_W:], vec(g_mix_post[0]), vec(g_ffn_post[0]),
                  w_ffn_in_bf, w_ffn_out_bf, main_feature_major=True)
    ks, vs, q1, mo1 = _proj_b(x1, pos_dense, inv, wkv, wb, mkm, mvm, S)

    sink_rows = jnp.repeat((sinks_b[0].astype(F32) * LOG2E).reshape(SWA_KV_HEADS, SWA_GROUPS), SWA_WINDOW, axis=1)
    attn = _swa(sink_rows.reshape(SWA_KV_HEADS, 1, SWA_GROUPS * SWA_WINDOW), q1, ks, vs, B, S)
    wo1 = w_out[1]
    xo = _mix_ffn(1, x1, attn, mo1, wo1[:SWA_Q_W][_swa_out_perm()].astype(BF16), wo1[SWA_Q_W:].astype(BF16),
                  vec(g_mix_post[1]), vec(g_ffn_post[1]), w_ffn_in_bf, w_ffn_out_bf, main_feature_major=False)
    return xo.reshape(B, S, D_MODEL)
```

```python
import functools

import numpy as np
import jax
import jax.numpy as jnp
from jax import lax
from jax.experimental import pallas as pl
from jax.experimental.pallas import tpu as pltpu

F32 = jnp.float32
BF16 = jnp.bfloat16

D_MODEL = 1024
DEPTH = 2
HEAD_DIM = 64
EPS = 1e-6
ROPE_THETA = 10000.0
LOG2E = 1.4426950408889634

MLSTM_HEADS = 4
MLSTM_DV = 192
MLSTM_DQK = 96
MLSTM_QK_W = MLSTM_HEADS * MLSTM_DQK
MLSTM_V_W = MLSTM_HEADS * MLSTM_DV
GATE_SOFTCAP = 15.0
M_INIT = -1e30

SWA_Q_HEADS = 12
SWA_KV_HEADS = 4
SWA_GROUPS = SWA_Q_HEADS // SWA_KV_HEADS
SWA_Q_W = SWA_Q_HEADS * HEAD_DIM
SWA_KV_W = SWA_KV_HEADS * HEAD_DIM
SWA_WINDOW = 128

MEM_TOKENS = 256
MEM_HEADS = 4
MEM_HEAD_DIM = 64
MEM_W = MEM_HEADS * MEM_HEAD_DIM

D_FF = 2816

LANES = 128
MXU_TILE = 256

DQK_PAD = LANES
QK_PAD_W = MLSTM_HEADS * DQK_PAD
V_WIN = MXU_TILE
MLSTM_CHUNK = 256
MLSTM_CHUNKS_PER_STEP = 8
FF_CHUNK = MXU_TILE
N_FF_CHUNKS = D_FF // FF_CHUNK
PROJ_A_ROW_TILE = 1024
PROJ_B_ROW_TILE = 1024
PROJ_B_ROW_GROUP = 512
FFN_ROW_TILE = 1024
FFN_ROW_GROUP = 512
SWA_BLOCKS_PER_STEP = 8
VMEM_LIMIT = 56 * 1024 * 1024

_V_WIN_START = (0, 128, 384, 512)
_V_WIN_OFF = (0, 64, 0, 64)
_V_ONES_COL = (192, 0, 192, 0)


def _rms_hat(x):
    return x * lax.rsqrt(jnp.mean(x * x, axis=-1, keepdims=True) + EPS)


def _rms(x, g):
    return _rms_hat(x) * g


def _dot(a, b):
    return jnp.dot(a, b, preferred_element_type=F32)


def _dot_nt(a, b):
    return lax.dot_general(a, b, (((1,), (1,)), ((), ())), preferred_element_type=F32)


def _dot_tn(a, b):
    return lax.dot_general(a, b, (((0,), (0,)), ((), ())), preferred_element_type=F32)


def _sigmoid(x):
    return 1.0 / (1.0 + jnp.exp(-x))


def _resident(shape):
    nd = len(shape)
    return pl.BlockSpec(shape, lambda *_: (0,) * nd, pipeline_mode=pl.Buffered(1))


def _mem_kv_kernel(mem_ref, g_ref, w_ref, mk_ref, mv_ref):
    B = mem_ref.shape[0]
    hn = _rms(mem_ref[...].reshape(B * MEM_TOKENS, D_MODEL), g_ref[0]).astype(BF16)
    kv = _dot(hn, w_ref[0])
    mk = kv[:, :MEM_W] * (MEM_HEAD_DIM ** -0.5 * LOG2E)
    mv = kv[:, MEM_W:]
    lane_head = lax.broadcasted_iota(jnp.int32, mk.shape, 1) >> 6
    for h in range(MEM_HEADS):
        sel = lane_head == h
        mk_ref[0, :, h] = jnp.where(sel, mk, 0.0).astype(BF16).reshape(B, MEM_TOKENS, MEM_W)
        mv_ref[0, :, h] = jnp.where(sel, mv, 0.0).astype(BF16).reshape(B, MEM_TOKENS, MEM_W)


def _mem_kv(mem, g_mem, w_mem_kv_bf16):
    B = mem.shape[0]
    out_sds = jax.ShapeDtypeStruct((DEPTH, B, MEM_HEADS, MEM_TOKENS, MEM_W), BF16)
    out_spec = pl.BlockSpec((1, B, MEM_HEADS, MEM_TOKENS, MEM_W), lambda l: (l, 0, 0, 0, 0))
    return pl.pallas_call(
        _mem_kv_kernel,
        grid=(DEPTH,),
        in_specs=[
            pl.BlockSpec((B, MEM_TOKENS, D_MODEL), lambda l: (0, 0, 0)),
            pl.BlockSpec((1, 1, D_MODEL), lambda l: (l, 0, 0)),
            pl.BlockSpec((1, D_MODEL, 2 * MEM_W), lambda l: (l, 0, 0)),
        ],
        out_specs=[out_spec, out_spec],
        out_shape=[out_sds, out_sds],
        compiler_params=pltpu.CompilerParams(dimension_semantics=("arbitrary",),
                                             vmem_limit_bytes=VMEM_LIMIT),
        name="mem_kv",
    )(mem, g_mem.reshape(DEPTH, 1, D_MODEL), w_mem_kv_bf16)


def _mem_scores(mq, mk_ref):
    return [_dot_nt(mq, mk_ref[0, 0, h]) for h in range(MEM_HEADS)]


def _mem_probs(s):
    e = [jnp.exp2(sh - jnp.max(sh, axis=-1, keepdims=True)) for sh in s]
    return [(eh * (1.0 / jnp.sum(eh, axis=-1, keepdims=True))).astype(BF16) for eh in e]


def _mem_out(p, mv_ref):
    out = _dot(p[0], mv_ref[0, 0, 0])
    for h in range(1, MEM_HEADS):
        out = out + _dot(p[h], mv_ref[0, 0, h])
    return out


def _proj_a_kernel(x_ref, wq_ref, wk_ref, wv_ref, wo_ref, wmq_ref, wg_ref, bg_ref, mk_ref, mv_ref,
                   q_ref, k_ref, v_ref, o_ref, gt_ref, gtt_ref, mo_ref):
    hn = _rms_hat(x_ref[...]).astype(BF16)
    mq = _dot(hn, wmq_ref[...]).astype(BF16)
    s = _mem_scores(mq, mk_ref)
    gates = _dot(hn, wg_ref[...]) + bg_ref[...]
    q_ref[...] = _dot(hn, wq_ref[...]).astype(BF16)
    gates = GATE_SOFTCAP * jnp.tanh(gates * (1.0 / GATE_SOFTCAP))
    log_sig = jnp.minimum(gates, 0.0) - jnp.log1p(jnp.exp(-jnp.abs(gates)))
    lane = lax.broadcasted_iota(jnp.int32, gates.shape, 1)
    gl = jnp.where(lane < MLSTM_HEADS, gates, log_sig) * LOG2E
    p = _mem_probs(s)
    k_ref[...] = _dot(hn, wk_ref[...]).astype(BF16)
    g_hi = gl.astype(BF16)
    rest = gl - g_hi.astype(F32)
    g_mid = rest.astype(BF16)
    g_lo = (rest - g_mid.astype(F32)).astype(BF16)
    L = MLSTM_CHUNK
    tri = (lax.broadcasted_iota(jnp.int32, (L, L), 1) <= lax.broadcasted_iota(jnp.int32, (L, L), 0)).astype(BF16)
    csum = jnp.concatenate(
        [_dot(tri, g_hi[r:r + L]) + _dot(tri, g_mid[r:r + L]) + _dot(tri, g_lo[r:r + L])
         for r in range(0, gl.shape[0], L)], axis=0)
    v_ref[...] = _dot_nt(wv_ref[...], hn).astype(BF16)
    y = jnp.where(lane < MLSTM_HEADS, gl, csum)
    gt_ref[...] = y
    gtt_ref[...] = y.T[0:2 * MLSTM_HEADS, :]
    mo_ref[...] = _mem_out(p, mv_ref).astype(BF16)
    o_ref[...] = _dot_nt(wo_ref[...], hn).astype(BF16)


def _proj_a(x2d, wq, wk, wv, wo, wmq, wg, bg, mkm, mvm, rows_per_batch):
    N = x2d.shape[0]
    tm = PROJ_A_ROW_TILE
    steps_per_batch = rows_per_batch // tm
    row = lambda w: pl.BlockSpec((tm, w), lambda i: (i, 0))
    col = pl.BlockSpec((MLSTM_V_W, tm), lambda i: (0, i))
    mem_spec = pl.BlockSpec((1, 1, MEM_HEADS, MEM_TOKENS, MEM_W),
                            lambda i: (0, i // steps_per_batch, 0, 0, 0))
    return pl.pallas_call(
        _proj_a_kernel,
        grid=(N // tm,),
        in_specs=[row(D_MODEL),
                  _resident(wq.shape), _resident(wk.shape), _resident(wv.shape), _resident(wo.shape),
                  _resident(wmq.shape), _resident(wg.shape), _resident((1, LANES)),
                  mem_spec, mem_spec],
        out_specs=[row(QK_PAD_W), row(QK_PAD_W), col, col, row(LANES),
                   pl.BlockSpec((2 * MLSTM_HEADS, tm), lambda i: (0, i)), row(MEM_W)],
        out_shape=[jax.ShapeDtypeStruct((N, QK_PAD_W), BF16), jax.ShapeDtypeStruct((N, QK_PAD_W), BF16),
                   jax.ShapeDtypeStruct((MLSTM_V_W, N), BF16), jax.ShapeDtypeStruct((MLSTM_V_W, N), BF16),
                   jax.ShapeDtypeStruct((N, LANES), F32), jax.ShapeDtypeStruct((2 * MLSTM_HEADS, N), F32),
                   jax.ShapeDtypeStruct((N, MEM_W), BF16)],
        compiler_params=pltpu.CompilerParams(dimension_semantics=("arbitrary",),
                                             vmem_limit_bytes=VMEM_LIMIT),
        name="proj_a",
    )(x2d, wq, wk, wv, wo, wmq, wg, bg, mkm, mvm)


def _mlstm_kernel(q_ref, k_ref, vt_ref, ot_ref, gt_ref, gtt_ref, hmt_ref, c_scr, m_scr):
    L = MLSTM_CHUNK

    @pl.when(pl.program_id(1) == 0)
    def _():
        c_scr[...] = jnp.zeros(c_scr.shape, F32)
        m_scr[...] = jnp.full(m_scr.shape, M_INIT, F32)

    causal = lax.broadcasted_iota(jnp.int32, (L, L), 0) <= lax.broadcasted_iota(jnp.int32, (L, L), 1)
    feat = lax.broadcasted_iota(jnp.int32, (V_WIN, L), 0)
    heads = range(MLSTM_HEADS)
    m_st = [m_scr[h][0:1, 0:1] for h in heads]
    c_st = [c_scr[h] for h in heads]

    for r0 in range(0, q_ref.shape[0], L):
        y = gt_ref[r0:r0 + L, :]
        yt = gtt_ref[:, r0:r0 + L]
        b_row = [yt[MLSTM_HEADS + h:MLSTM_HEADS + h + 1, :] for h in heads]
        c_row = [b_row[h] - yt[h:h + 1, :] for h in heads]
        c_col = [y[:, MLSTM_HEADS + h:MLSTM_HEADS + h + 1] - y[:, h:h + 1] for h in heads]
        g_tot = [y[L - 1:L, MLSTM_HEADS + h:MLSTM_HEADS + h + 1] for h in heads]
        qh = [q_ref[r0:r0 + L, h * DQK_PAD:(h + 1) * DQK_PAD] for h in heads]
        kh = [k_ref[r0:r0 + L, h * DQK_PAD:(h + 1) * DQK_PAD] for h in heads]
        vt_aug = [jnp.where(feat == _V_ONES_COL[h], jnp.ones((), BF16),
                            vt_ref[_V_WIN_START[h]:_V_WIN_START[h] + V_WIN, r0:r0 + L]) for h in heads]

        qk_t = [_dot_nt(kh[h], qh[h]) for h in heads]
        inter_t = [_dot_nt(c_st[h].astype(BF16), qh[h]) for h in heads]
        dmat = [jnp.where(causal, b_row[h] - c_col[h], -jnp.inf) for h in heads]
        inter_log = [b_row[h] + m_st[h] for h in heads]
        m_row = [jnp.maximum(inter_log[h], jnp.max(dmat[h], axis=0, keepdims=True)) for h in heads]
        scores_t = [(qk_t[h] * jnp.exp2(dmat[h] - m_row[h])).astype(BF16) for h in heads]
        num_t = [_dot(vt_aug[h], scores_t[h]) + jnp.exp2(inter_log[h] - m_row[h]) * inter_t[h] for h in heads]

        a = [g_tot[h] - c_row[h] for h in heads]
        m_new = [jnp.maximum(g_tot[h] + m_st[h], jnp.max(a[h], axis=1, keepdims=True)) for h in heads]
        vtw = [vt_aug[h] * jnp.exp2(a[h] - m_new[h]).astype(BF16) for h in heads]
        c_upd = [_dot(vtw[h], kh[h]) for h in heads]

        for h in heads:
            off, ones_row, lo = _V_WIN_OFF[h], _V_ONES_COL[h], h * MLSTM_DV
            den = num_t[h][ones_row:ones_row + 1, :]
            inv_dd = 1.0 / jnp.maximum(jnp.abs(den), jnp.exp2(-m_row[h]))
            h_t = num_t[h][off:off + MLSTM_DV, :]
            ms = jnp.sum(h_t * h_t, axis=0, keepdims=True) * (1.0 / MLSTM_DV)
            scale = inv_dd * lax.rsqrt(ms * inv_dd * inv_dd + EPS)
            og = _sigmoid(ot_ref[lo:lo + MLSTM_DV, r0:r0 + L].astype(F32))
            hmt_ref[lo:lo + MLSTM_DV, r0:r0 + L] = (h_t * (scale * og)).astype(BF16)

        c_st = [jnp.exp2(g_tot[h] + m_st[h] - m_new[h]) * c_st[h] + c_upd[h] for h in heads]
        m_st = m_new

    for h in heads:
        c_scr[h] = c_st[h]
        m_scr[h] = jnp.broadcast_to(m_st[h], m_scr.shape[1:])


def _mlstm(q, k, vt, ot, gt, gtt, B, S):
    rows = MLSTM_CHUNK * MLSTM_CHUNKS_PER_STEP
    steps = S // rows
    N = B * S
    row = lambda w: pl.BlockSpec((rows, w), lambda b, c: (b * steps + c, 0))
    col = lambda h: pl.BlockSpec((h, rows), lambda b, c: (0, b * steps + c))
    return pl.pallas_call(
        _mlstm_kernel,
        grid=(B, steps),
        in_specs=[row(QK_PAD_W), row(QK_PAD_W), col(MLSTM_V_W), col(MLSTM_V_W), row(LANES),
                  col(2 * MLSTM_HEADS)],
        out_specs=col(MLSTM_V_W),
        out_shape=jax.ShapeDtypeStruct((MLSTM_V_W, N), BF16),
        scratch_shapes=[pltpu.VMEM((MLSTM_HEADS, V_WIN, DQK_PAD), F32),
                        pltpu.VMEM((MLSTM_HEADS, 8, LANES), F32)],
        compiler_params=pltpu.CompilerParams(dimension_semantics=("arbitrary", "arbitrary"),
                                             vmem_limit_bytes=VMEM_LIMIT),
        name="mlstm",
    )(q, k, vt, ot, gt, gtt)


def _mix_ffn_core(x_ref, main_ref, mo_ref, wom_ref, woe_ref, gpost_ref, gfpost_ref,
                  wi_ref, wd_ref, main_feature_major):
    main_dot = _dot_tn if main_feature_major else _dot
    groups = [slice(r, r + FFN_ROW_GROUP) for r in range(0, x_ref.shape[0], FFN_ROW_GROUP)]
    mix = [main_dot(main_ref[:, g] if main_feature_major else main_ref[g, :], wom_ref[...])
           + _dot(mo_ref[g, :], woe_ref[...]) for g in groups]
    x1 = [x_ref[g, :] + _rms(m, gpost_ref[...]) for g, m in zip(groups, mix)]
    hf = [_rms_hat(t).astype(BF16) for t in x1]
    acc = [None] * len(groups)
    for c in range(N_FF_CHUNKS):
        lo = c * FF_CHUNK
        for i, h in enumerate(hf):
            g = _dot(h, wi_ref[:, lo:lo + FF_CHUNK])
            u = _dot(h, wi_ref[:, D_FF + lo:D_FF + lo + FF_CHUNK])
            act = (g * _sigmoid(g) * u).astype(BF16)
            d = _dot(act, wd_ref[lo:lo + FF_CHUNK, :])
            acc[i] = d if acc[i] is None else acc[i] + d
    return jnp.concatenate([t + _rms(a, gfpost_ref[...]) for t, a in zip(x1, acc)], axis=0)


def _mix_ffn_kernel(x_ref, main_ref, mo_ref, wom_ref, woe_ref, gpost_ref, gfpost_ref,
                    wi_ref, wd_ref, xo_ref, *, main_feature_major):
    xo_ref[...] = _mix_ffn_core(x_ref, main_ref, mo_ref, wom_ref, woe_ref, gpost_ref,
                                gfpost_ref, wi_ref, wd_ref, main_feature_major)


def _rope(t1, t2, cos, sin):
    return t1 * cos - t2 * sin, t2 * cos + t1 * sin


def _rope_tables(pos_ref, inv_ref):
    ang = pos_ref[...].astype(F32) * inv_ref[...]
    cos_d = jnp.cos(ang)
    sin_d = jnp.sin(ang)
    grp = lax.broadcasted_iota(jnp.int32, ang.shape, 1) >> 5

    def spread(t, g):
        y = jnp.where(grp == g, t, 0.0)
        y = y + pltpu.roll(y, HEAD_DIM // 2, 1)
        return y + pltpu.roll(y, HEAD_DIM, 1)

    n_grp = LANES // (HEAD_DIM // 2)
    cos = jnp.concatenate([spread(cos_d, g) for g in range(n_grp)], axis=0)
    sin = jnp.concatenate([spread(sin_d, g) for g in range(n_grp)], axis=0)
    return cos, sin


def _mix_ffn(layer, x2d, main, mo, wom, woe, g_post, g_fpost, wi, wd, main_feature_major):
    N = x2d.shape[0]
    tm = FFN_ROW_TILE
    row = lambda w: pl.BlockSpec((tm, w), lambda i: (i, 0))
    slab = lambda w: pl.BlockSpec((None,) + w.shape[1:], lambda i: (layer, 0, 0), pipeline_mode=pl.Buffered(1))
    main_spec = pl.BlockSpec((main.shape[0], tm), lambda i: (0, i)) if main_feature_major else row(main.shape[1])
    return pl.pallas_call(
        functools.partial(_mix_ffn_kernel, main_feature_major=main_feature_major),
        grid=(N // tm,),
        in_specs=[row(D_MODEL), main_spec, row(MEM_W),
                  _resident(wom.shape), _resident(woe.shape),
                  _resident((1, D_MODEL)), _resident((1, D_MODEL)),
                  slab(wi), slab(wd)],
        out_specs=row(D_MODEL),
        out_shape=jax.ShapeDtypeStruct((N, D_MODEL), F32),
        compiler_params=pltpu.CompilerParams(dimension_semantics=("arbitrary",),
                                             vmem_limit_bytes=VMEM_LIMIT),
        name="mix_ffn_%d" % layer,
    )(x2d, main, mo, wom, woe, g_post, g_fpost, wi, wd)


def _proj_b_kernel(x_ref, pos_ref, inv_ref, wkv_ref, wb_ref, mk_ref, mv_ref,
                   ks_ref, vs_ref, q1_ref, mo1_ref):
    groups = [slice(r, r + PROJ_B_ROW_GROUP) for r in range(0, x_ref.shape[0], PROJ_B_ROW_GROUP)]
    xn = [_rms_hat(x_ref[g, :]).astype(BF16) for g in groups]
    pb = [_dot(t, wb_ref[...]) for t in xn]
    s = [_mem_scores(t[:, SWA_Q_W:].astype(BF16), mk_ref) for t in pb]
    cos, sin = _rope_tables(pos_ref, inv_ref)
    kv = [_dot(t, wkv_ref[...]) for t in xn]
    p = [_mem_probs(t) for t in s]

    for g, t in zip(groups, pb):
        for gi in range(SWA_GROUPS):
            base = gi * 2 * LANES
            t1, t2 = _rope(t[:, base:base + LANES], t[:, base + LANES:base + 2 * LANES], cos[g], sin[g])
            q1_ref[g, base:base + LANES] = t1.astype(BF16)
            q1_ref[g, base + LANES:base + 2 * LANES] = t2.astype(BF16)
    for g, t in zip(groups, p):
        mo1_ref[g, :] = _mem_out(t, mv_ref).astype(BF16)

    scale = HEAD_DIM ** -0.5 * LOG2E
    for g, t in zip(groups, kv):
        k1, k2 = _rope(t[:, 0:LANES], t[:, LANES:2 * LANES], cos[g], sin[g])
        ks_ref[g, 0:LANES] = (k1 * scale).astype(BF16)
        ks_ref[g, LANES:2 * LANES] = (k2 * scale).astype(BF16)
        vs_ref[g, :] = t[:, SWA_KV_W:].astype(BF16)


def _proj_b(x2d, pos, inv, wkv, wb, mkm, mvm, rows_per_batch):
    N = x2d.shape[0]
    tm = PROJ_B_ROW_TILE
    steps_per_batch = rows_per_batch // tm
    row = lambda w: pl.BlockSpec((tm, w), lambda i: (i, 0))
    mem_spec = pl.BlockSpec((1, 1, MEM_HEADS, MEM_TOKENS, MEM_W),
                            lambda i: (1, i // steps_per_batch, 0, 0, 0))
    pos_spec = pl.BlockSpec((tm // (LANES // (HEAD_DIM // 2)), LANES), lambda i: (i, 0))
    return pl.pallas_call(
        _proj_b_kernel,
        grid=(N // tm,),
        in_specs=[row(D_MODEL), pos_spec, _resident((1, LANES)), _resident(wkv.shape), _resident(wb.shape),
                  mem_spec, mem_spec],
        out_specs=[row(SWA_KV_W), row(SWA_KV_W), row(SWA_Q_W), row(MEM_W)],
        out_shape=[jax.ShapeDtypeStruct((N, SWA_KV_W), BF16), jax.ShapeDtypeStruct((N, SWA_KV_W), BF16),
                   jax.ShapeDtypeStruct((N, SWA_Q_W), BF16), jax.ShapeDtypeStruct((N, MEM_W), BF16)],
        compiler_params=pltpu.CompilerParams(dimension_semantics=("arbitrary",),
                                             vmem_limit_bytes=VMEM_LIMIT),
        name="proj_b",
    )(x2d, pos, inv, wkv, wb, mkm, mvm)


def _swa_kernel(sink_ref, q_ref, kp_ref, kc_ref, vp_ref, vc_ref, o_ref):
    W = SWA_WINDOW
    G = SWA_GROUPS
    n = pl.program_id(1)
    n_sub = q_ref.shape[0] // W
    H = SWA_KV_HEADS
    kpos = lax.broadcasted_iota(jnp.int32, (2 * W, G * W), 0)
    qcol = lax.broadcasted_iota(jnp.int32, (2 * W, G * W), 1)
    diff = (qcol & (W - 1)) + W - kpos
    in_band = (diff >= 0) & (diff < W)
    k_lane = lax.broadcasted_iota(jnp.int32, (2 * W, SWA_KV_W), 1)
    sink = sink_ref[...]
    subs = range(n_sub)
    q, km, vm, valid = [], [], [], []
    for sub in subs:
        q.append(jnp.concatenate([q_ref[sub * W:(sub + 1) * W, gi * 2 * LANES:(gi + 1) * 2 * LANES]
                                  for gi in range(G)], axis=0))
        if sub == 0:
            k = jnp.concatenate([kp_ref[...], kc_ref[0:W, :]], axis=0)
            v = jnp.concatenate([vp_ref[...], vc_ref[0:W, :]], axis=0)
            valid.append(in_band & ((kpos >= W) | (n > 0)))
        else:
            k = kc_ref[(sub - 1) * W:(sub + 1) * W, :]
            v = vc_ref[(sub - 1) * W:(sub + 1) * W, :]
            valid.append(in_band)
        km.append(jnp.concatenate([jnp.where(((k_lane & (LANES - 1)) >> 5) == j, k, jnp.zeros((), BF16))
                                   for j in range(H)], axis=0))
        vm.append(jnp.concatenate([jnp.where((k_lane >> 6) == j, v, jnp.zeros((), BF16))
                                   for j in range(H)], axis=0))
    s = [_dot_nt(km[i], q[i]).reshape(H, 2 * W, G * W) for i in subs]
    s = [jnp.where(valid[i][None], s[i], -jnp.inf) for i in subs]
    m = [jnp.maximum(jnp.max(s[i], axis=1, keepdims=True), sink) for i in subs]
    e = [jnp.exp2(s[i] - m[i]) for i in subs]
    p = [e[i] * (1.0 / (jnp.sum(e[i], axis=1, keepdims=True) + jnp.exp2(sink - m[i]))) for i in subs]
    out = [_dot_tn(p[i].astype(BF16).reshape(H * 2 * W, G * W), vm[i]) for i in subs]
    for sub in subs:
        for gi in range(G):
            o_ref[sub * W:(sub + 1) * W, gi * 2 * LANES:(gi + 1) * 2 * LANES] = (
                out[sub][gi * W:(gi + 1) * W].astype(BF16))


def _swa(sink_cols, q1, ks, vs, B, S):
    W = SWA_WINDOW
    n_sub = SWA_BLOCKS_PER_STEP
    tq = n_sub * W
    steps = S // tq
    cur = lambda w: pl.BlockSpec((tq, w), lambda b, n: (b * steps + n, 0))
    prev = lambda w: pl.BlockSpec((W, w), lambda b, n: (b * (S // W) + jnp.maximum(n_sub * n - 1, 0), 0))
    return pl.pallas_call(
        _swa_kernel,
        grid=(B, steps),
        in_specs=[_resident(sink_cols.shape),
                  cur(SWA_Q_W), prev(SWA_KV_W), cur(SWA_KV_W), prev(SWA_KV_W), cur(SWA_KV_W)],
        out_specs=cur(SWA_Q_W),
        out_shape=jax.ShapeDtypeStruct((B * S, SWA_Q_W), BF16),
        compiler_params=pltpu.CompilerParams(dimension_semantics=("arbitrary", "arbitrary"),
                                             vmem_limit_bytes=VMEM_LIMIT),
        name="swa",
    )(sink_cols, q1, ks, ks, vs, vs)


def _pad_heads(w, heads, width, padded):
    w = w.reshape(w.shape[0], heads, width)
    return jnp.pad(w, ((0, 0), (0, 0), (0, padded - width))).reshape(w.shape[0], heads * padded)


def _swa_q_perm():
    idx = np.empty((SWA_GROUPS, 2, SWA_KV_HEADS, HEAD_DIM // 2), np.int32)
    for gi in range(SWA_GROUPS):
        for half in range(2):
            for j in range(SWA_KV_HEADS):
                idx[gi, half, j] = (j * SWA_GROUPS + gi) * HEAD_DIM + half * (HEAD_DIM // 2) + np.arange(HEAD_DIM // 2)
    return idx.reshape(-1)


def _swa_k_perm():
    idx = np.empty((2, SWA_KV_HEADS, HEAD_DIM // 2), np.int32)
    for half in range(2):
        for j in range(SWA_KV_HEADS):
            idx[half, j] = j * HEAD_DIM + half * (HEAD_DIM // 2) + np.arange(HEAD_DIM // 2)
    return idx.reshape(-1)


def _swa_out_perm():
    idx = np.empty((SWA_GROUPS, SWA_KV_HEADS, HEAD_DIM), np.int32)
    for gi in range(SWA_GROUPS):
        for j in range(SWA_KV_HEADS):
            idx[gi, j] = (j * SWA_GROUPS + gi) * HEAD_DIM + np.arange(HEAD_DIM)
    return idx.reshape(-1)


def kernel(x, mem, positions, g_mix_pre, g_mix_post, g_ffn_pre, g_ffn_post, g_mem, w_mem_kv, w_out,
           w_ffn_in, w_ffn_out, w_in_a, b_gates_a, g_mlstm_out, g_kv, w_kv, w_in_b, sinks_b):
    B, S, _ = x.shape
    N = B * S
    assert S % PROJ_A_ROW_TILE == 0 and S % PROJ_B_ROW_TILE == 0 and S % FFN_ROW_TILE == 0
    assert S % (MLSTM_CHUNK * MLSTM_CHUNKS_PER_STEP) == 0
    assert S % (SWA_WINDOW * SWA_BLOCKS_PER_STEP) == 0
    x2d = x.reshape(N, D_MODEL)
    vec = lambda g: g.reshape(1, -1).astype(F32)

    mkm, mvm = _mem_kv(mem, g_mem, w_mem_kv.astype(BF16))

    fold = lambda g, w: g.astype(F32)[:, None] * w

    piece = lambda c0, width: fold(g_mix_pre[0], w_in_a[0][:, c0:c0 + width])
    c0 = 0
    wq = _pad_heads(piece(c0, MLSTM_QK_W), MLSTM_HEADS, MLSTM_DQK, DQK_PAD).astype(BF16)
    c0 += MLSTM_QK_W
    wk = _pad_heads(piece(c0, MLSTM_QK_W) * (MLSTM_DQK ** -0.5), MLSTM_HEADS, MLSTM_DQK, DQK_PAD).astype(BF16)
    c0 += MLSTM_QK_W
    wv = piece(c0, MLSTM_V_W).astype(BF16).T
    c0 += MLSTM_V_W
    wo = piece(c0, MLSTM_V_W).astype(BF16).T
    c0 += MLSTM_V_W
    wgt = jnp.pad(piece(c0, 2 * MLSTM_HEADS), ((0, 0), (0, LANES - 2 * MLSTM_HEADS))).astype(BF16)
    c0 += 2 * MLSTM_HEADS
    wmq = piece(c0, MEM_W).astype(BF16)
    bg = jnp.pad(b_gates_a[0].astype(F32), (0, LANES - 2 * MLSTM_HEADS)).reshape(1, LANES)

    q, k, v, o, gt, gtt, mo0 = _proj_a(x2d, wq, wk, wv, wo, wmq, wgt, bg, mkm, mvm, S)
    hm = _mlstm(q, k, v, o, gt, gtt, B, S)

    w_ffn_in_bf = (g_ffn_pre.astype(F32)[:, :, None] * w_ffn_in).astype(BF16)
    w_ffn_out_bf = w_ffn_out.astype(BF16)
    wom0 = fold(g_mlstm_out[0], w_out[0][:MLSTM_V_W]).astype(BF16)
    woe0 = w_out[0][MLSTM_V_W:].astype(BF16)
    inv = 1.0 / (ROPE_THETA ** (jnp.arange(0, HEAD_DIM, 2, dtype=F32) / HEAD_DIM))
    inv = jnp.tile(inv, SWA_KV_HEADS).reshape(1, LANES)
    wkv = fold(g_kv, w_kv)
    wkv = jnp.concatenate([wkv[:, :SWA_KV_W][:, _swa_k_perm()], wkv[:, SWA_KV_W:]], axis=1).astype(BF16)
    wb = fold(g_mix_pre[1], w_in_b[0])
    wb = jnp.concatenate([wb[:, :SWA_Q_W][:, _swa_q_perm()], wb[:, SWA_Q_W:]], axis=1).astype(BF16)
    n_grp = LANES // (HEAD_DIM // 2)
    pos_tiles = positions.reshape(N // PROJ_B_ROW_TILE, n_grp, PROJ_B_ROW_TILE // n_grp)
    pos_dense = jnp.concatenate(
        [jnp.broadcast_to(pos_tiles[:, g, :, None], pos_tiles.shape[:1] + pos_tiles.shape[2:] + (HEAD_DIM // 2,))
         for g in range(n_grp)], axis=-1).reshape(N // n_grp, LANES)
    x1 = _mix_ffn(0, x2d, hm, mo0, wom0, woe0, vec(g_mix_post[0]), vec(g_ffn_post[0]),
                  w_ffn_in_bf, w_ffn_out_bf, main_feature_major=True)
    ks, vs, q1, mo1 = _proj_b(x1, pos_dense, inv, wkv, wb, mkm, mvm, S)

    sink_rows = jnp.repeat((sinks_b[0].astype(F32) * LOG2E).reshape(SWA_KV_HEADS, SWA_GROUPS), SWA_WINDOW, axis=1)
    attn = _swa(sink_rows.reshape(SWA_KV_HEADS, 1, SWA_GROUPS * SWA_WINDOW), q1, ks, vs, B, S)
    wo1 = w_out[1]
    xo = _mix_ffn(1, x1, attn, mo1, wo1[:SWA_Q_W][_swa_out_perm()].astype(BF16), wo1[SWA_Q_W:].astype(BF16),
                  vec(g_mix_post[1]), vec(g_ffn_post[1]), w_ffn_in_bf, w_ffn_out_bf, main_feature_major=False)
    return xo.reshape(B, S, D_MODEL)
```

```python
import functools

import numpy as np
import jax
import jax.numpy as jnp
from jax import lax
from jax.experimental import pallas as pl
from jax.experimental.pallas import tpu as pltpu

F32 = jnp.float32
BF16 = jnp.bfloat16

D_MODEL = 1024
DEPTH = 2
HEAD_DIM = 64
EPS = 1e-6
ROPE_THETA = 10000.0
LOG2E = 1.4426950408889634

MLSTM_HEADS = 4
MLSTM_DV = 192
MLSTM_DQK = 96
MLSTM_QK_W = MLSTM_HEADS * MLSTM_DQK
MLSTM_V_W = MLSTM_HEADS * MLSTM_DV
GATE_SOFTCAP = 15.0
M_INIT = -1e30

SWA_Q_HEADS = 12
SWA_KV_HEADS = 4
SWA_GROUPS = SWA_Q_HEADS // SWA_KV_HEADS
SWA_Q_W = SWA_Q_HEADS * HEAD_DIM
SWA_KV_W = SWA_KV_HEADS * HEAD_DIM
SWA_WINDOW = 128

MEM_TOKENS = 256
MEM_HEADS = 4
MEM_HEAD_DIM = 64
MEM_W = MEM_HEADS * MEM_HEAD_DIM

D_FF = 2816

LANES = 128
MXU_TILE = 256

DQK_PAD = LANES
QK_PAD_W = MLSTM_HEADS * DQK_PAD
V_WIN = MXU_TILE
MLSTM_CHUNK = 256
MLSTM_CHUNKS_PER_STEP = 8
FF_CHUNK = MXU_TILE
N_FF_CHUNKS = D_FF // FF_CHUNK
PROJ_A_ROW_TILE = 1024
PROJ_B_ROW_TILE = 2048
PROJ_B_ROW_GROUP = 512
FFN_ROW_TILE = 1024
FFN_ROW_GROUP = 512
SWA_BLOCKS_PER_STEP = 8
VMEM_LIMIT = 56 * 1024 * 1024

_V_WIN_START = (0, 128, 384, 512)
_V_WIN_OFF = (0, 64, 0, 64)
_V_ONES_COL = (192, 0, 192, 0)


def _rms_hat(x):
    return x * lax.rsqrt(jnp.mean(x * x, axis=-1, keepdims=True) + EPS)


def _rms(x, g):
    return _rms_hat(x) * g


def _dot(a, b):
    return jnp.dot(a, b, preferred_element_type=F32)


def _dot_nt(a, b):
    return lax.dot_general(a, b, (((1,), (1,)), ((), ())), preferred_element_type=F32)


def _dot_tn(a, b):
    return lax.dot_general(a, b, (((0,), (0,)), ((), ())), preferred_element_type=F32)


def _sigmoid(x):
    return 1.0 / (1.0 + jnp.exp(-x))


def _resident(shape):
    nd = len(shape)
    return pl.BlockSpec(shape, lambda *_: (0,) * nd, pipeline_mode=pl.Buffered(1))


def _mem_kv_kernel(mem_ref, g_ref, w_ref, mk_ref, mv_ref):
    B = mem_ref.shape[0]
    hn = _rms(mem_ref[...].reshape(B * MEM_TOKENS, D_MODEL), g_ref[0]).astype(BF16)
    kv = _dot(hn, w_ref[0])
    mk = kv[:, :MEM_W] * (MEM_HEAD_DIM ** -0.5 * LOG2E)
    mv = kv[:, MEM_W:]
    lane_head = lax.broadcasted_iota(jnp.int32, mk.shape, 1) >> 6
    for h in range(MEM_HEADS):
        sel = lane_head == h
        mk_ref[0, :, h] = jnp.where(sel, mk, 0.0).astype(BF16).reshape(B, MEM_TOKENS, MEM_W)
        mv_ref[0, :, h] = jnp.where(sel, mv, 0.0).astype(BF16).reshape(B, MEM_TOKENS, MEM_W)


def _mem_kv(mem, g_mem, w_mem_kv_bf16):
    B = mem.shape[0]
    out_sds = jax.ShapeDtypeStruct((DEPTH, B, MEM_HEADS, MEM_TOKENS, MEM_W), BF16)
    out_spec = pl.BlockSpec((1, B, MEM_HEADS, MEM_TOKENS, MEM_W), lambda l: (l, 0, 0, 0, 0))
    return pl.pallas_call(
        _mem_kv_kernel,
        grid=(DEPTH,),
        in_specs=[
            pl.BlockSpec((B, MEM_TOKENS, D_MODEL), lambda l: (0, 0, 0)),
            pl.BlockSpec((1, 1, D_MODEL), lambda l: (l, 0, 0)),
            pl.BlockSpec((1, D_MODEL, 2 * MEM_W), lambda l: (l, 0, 0)),
        ],
        out_specs=[out_spec, out_spec],
        out_shape=[out_sds, out_sds],
        compiler_params=pltpu.CompilerParams(dimension_semantics=("arbitrary",),
                                             vmem_limit_bytes=VMEM_LIMIT),
        name="mem_kv",
    )(mem, g_mem.reshape(DEPTH, 1, D_MODEL), w_mem_kv_bf16)


def _mem_scores(mq, mk_ref):
    return [_dot_nt(mq, mk_ref[0, 0, h]) for h in range(MEM_HEADS)]


def _mem_probs(s):
    e = [jnp.exp2(sh - jnp.max(sh, axis=-1, keepdims=True)) for sh in s]
    return [(eh * (1.0 / jnp.sum(eh, axis=-1, keepdims=True))).astype(BF16) for eh in e]


def _mem_out(p, mv_ref):
    out = _dot(p[0], mv_ref[0, 0, 0])
    for h in range(1, MEM_HEADS):
        out = out + _dot(p[h], mv_ref[0, 0, h])
    return out


def _proj_a_kernel(x_ref, wq_ref, wk_ref, wv_ref, wo_ref, wmq_ref, wg_ref, bg_ref, mk_ref, mv_ref,
                   q_ref, k_ref, v_ref, o_ref, gt_ref, gtt_ref, mo_ref):
    hn = _rms_hat(x_ref[...]).astype(BF16)
    mq = _dot(hn, wmq_ref[...]).astype(BF16)
    s = _mem_scores(mq, mk_ref)
    gates = _dot(hn, wg_ref[...]) + bg_ref[...]
    q_ref[...] = _dot(hn, wq_ref[...]).astype(BF16)
    gates = GATE_SOFTCAP * jnp.tanh(gates * (1.0 / GATE_SOFTCAP))
    log_sig = jnp.minimum(gates, 0.0) - jnp.log1p(jnp.exp(-jnp.abs(gates)))
    lane = lax.broadcasted_iota(jnp.int32, gates.shape, 1)
    gl = jnp.where(lane < MLSTM_HEADS, gates, log_sig) * LOG2E
    p = _mem_probs(s)
    k_ref[...] = _dot(hn, wk_ref[...]).astype(BF16)
    g_hi = gl.astype(BF16)
    rest = gl - g_hi.astype(F32)
    g_mid = rest.astype(BF16)
    g_lo = (rest - g_mid.astype(F32)).astype(BF16)
    L = MLSTM_CHUNK
    tri = (lax.broadcasted_iota(jnp.int32, (L, L), 1) <= lax.broadcasted_iota(jnp.int32, (L, L), 0)).astype(BF16)
    csum = jnp.concatenate(
        [_dot(tri, g_hi[r:r + L]) + _dot(tri, g_mid[r:r + L]) + _dot(tri, g_lo[r:r + L])
         for r in range(0, gl.shape[0], L)], axis=0)
    v_ref[...] = _dot_nt(wv_ref[...], hn).astype(BF16)
    y = jnp.where(lane < MLSTM_HEADS, gl, csum)
    gt_ref[...] = y
    gtt_ref[...] = y.T[0:2 * MLSTM_HEADS, :]
    mo_ref[...] = _mem_out(p, mv_ref).astype(BF16)
    o_ref[...] = _dot_nt(wo_ref[...], hn).astype(BF16)


def _proj_a(x2d, wq, wk, wv, wo, wmq, wg, bg, mkm, mvm, rows_per_batch):
    N = x2d.shape[0]
    tm = PROJ_A_ROW_TILE
    steps_per_batch = rows_per_batch // tm
    row = lambda w: pl.BlockSpec((tm, w), lambda i: (i, 0))
    col = pl.BlockSpec((MLSTM_V_W, tm), lambda i: (0, i))
    mem_spec = pl.BlockSpec((1, 1, MEM_HEADS, MEM_TOKENS, MEM_W),
                            lambda i: (0, i // steps_per_batch, 0, 0, 0))
    return pl.pallas_call(
        _proj_a_kernel,
        grid=(N // tm,),
        in_specs=[row(D_MODEL),
                  _resident(wq.shape), _resident(wk.shape), _resident(wv.shape), _resident(wo.shape),
                  _resident(wmq.shape), _resident(wg.shape), _resident((1, LANES)),
                  mem_spec, mem_spec],
        out_specs=[row(QK_PAD_W), row(QK_PAD_W), col, col, row(LANES),
                   pl.BlockSpec((2 * MLSTM_HEADS, tm), lambda i: (0, i)), row(MEM_W)],
        out_shape=[jax.ShapeDtypeStruct((N, QK_PAD_W), BF16), jax.ShapeDtypeStruct((N, QK_PAD_W), BF16),
                   jax.ShapeDtypeStruct((MLSTM_V_W, N), BF16), jax.ShapeDtypeStruct((MLSTM_V_W, N), BF16),
                   jax.ShapeDtypeStruct((N, LANES), F32), jax.ShapeDtypeStruct((2 * MLSTM_HEADS, N), F32),
                   jax.ShapeDtypeStruct((N, MEM_W), BF16)],
        compiler_params=pltpu.CompilerParams(dimension_semantics=("arbitrary",),
                                             vmem_limit_bytes=VMEM_LIMIT),
        name="proj_a",
    )(x2d, wq, wk, wv, wo, wmq, wg, bg, mkm, mvm)


def _mlstm_kernel(q_ref, k_ref, vt_ref, ot_ref, gt_ref, gtt_ref, hmt_ref, c_scr, m_scr):
    L = MLSTM_CHUNK

    @pl.when(pl.program_id(1) == 0)
    def _():
        c_scr[...] = jnp.zeros(c_scr.shape, F32)
        m_scr[...] = jnp.full(m_scr.shape, M_INIT, F32)

    causal = lax.broadcasted_iota(jnp.int32, (L, L), 0) <= lax.broadcasted_iota(jnp.int32, (L, L), 1)
    feat = lax.broadcasted_iota(jnp.int32, (V_WIN, L), 0)
    heads = range(MLSTM_HEADS)
    m_st = [m_scr[h][0:1, 0:1] for h in heads]
    c_st = [c_scr[h] for h in heads]

    for r0 in range(0, q_ref.shape[0], L):
        y = gt_ref[r0:r0 + L, :]
        yt = gtt_ref[:, r0:r0 + L]
        b_row = [yt[MLSTM_HEADS + h:MLSTM_HEADS + h + 1, :] for h in heads]
        c_row = [b_row[h] - yt[h:h + 1, :] for h in heads]
        c_col = [y[:, MLSTM_HEADS + h:MLSTM_HEADS + h + 1] - y[:, h:h + 1] for h in heads]
        g_tot = [y[L - 1:L, MLSTM_HEADS + h:MLSTM_HEADS + h + 1] for h in heads]
        qh = [q_ref[r0:r0 + L, h * DQK_PAD:(h + 1) * DQK_PAD] for h in heads]
        kh = [k_ref[r0:r0 + L, h * DQK_PAD:(h + 1) * DQK_PAD] for h in heads]
        vt_aug = [jnp.where(feat == _V_ONES_COL[h], jnp.ones((), BF16),
                            vt_ref[_V_WIN_START[h]:_V_WIN_START[h] + V_WIN, r0:r0 + L]) for h in heads]

        qk_t = [_dot_nt(kh[h], qh[h]) for h in heads]
        inter_t = [_dot_nt(c_st[h].astype(BF16), qh[h]) for h in heads]
        dmat = [jnp.where(causal, b_row[h] - c_col[h], -jnp.inf) for h in heads]
        inter_log = [b_row[h] + m_st[h] for h in heads]
        m_row = [jnp.maximum(inter_log[h], jnp.max(dmat[h], axis=0, keepdims=True)) for h in heads]
        scores_t = [(qk_t[h] * jnp.exp2(dmat[h] - m_row[h])).astype(BF16) for h in heads]
        num_t = [_dot(vt_aug[h], scores_t[h]) + jnp.exp2(inter_log[h] - m_row[h]) * inter_t[h] for h in heads]

        a = [g_tot[h] - c_row[h] for h in heads]
        m_new = [jnp.maximum(g_tot[h] + m_st[h], jnp.max(a[h], axis=1, keepdims=True)) for h in heads]
        vtw = [vt_aug[h] * jnp.exp2(a[h] - m_new[h]).astype(BF16) for h in heads]
        c_upd = [_dot(vtw[h], kh[h]) for h in heads]

        for h in heads:
            off, ones_row, lo = _V_WIN_OFF[h], _V_ONES_COL[h], h * MLSTM_DV
            den = num_t[h][ones_row:ones_row + 1, :]
            inv_dd = 1.0 / jnp.maximum(jnp.abs(den), jnp.exp2(-m_row[h]))
            h_t = num_t[h][off:off + MLSTM_DV, :]
            ms = jnp.sum(h_t * h_t, axis=0, keepdims=True) * (1.0 / MLSTM_DV)
            scale = inv_dd * lax.rsqrt(ms * inv_dd * inv_dd + EPS)
            og = _sigmoid(ot_ref[lo:lo + MLSTM_DV, r0:r0 + L].astype(F32))
            hmt_ref[lo:lo + MLSTM_DV, r0:r0 + L] = (h_t * (scale * og)).astype(BF16)

        c_st = [jnp.exp2(g_tot[h] + m_st[h] - m_new[h]) * c_st[h] + c_upd[h] for h in heads]
        m_st = m_new

    for h in heads:
        c_scr[h] = c_st[h]
        m_scr[h] = jnp.broadcast_to(m_st[h], m_scr.shape[1:])


def _mlstm(q, k, vt, ot, gt, gtt, B, S):
    rows = MLSTM_CHUNK * MLSTM_CHUNKS_PER_STEP
    steps = S // rows
    N = B * S
    row = lambda w: pl.BlockSpec((rows, w), lambda b, c: (b * steps + c, 0))
    col = lambda h: pl.BlockSpec((h, rows), lambda b, c: (0, b * steps + c))
    return pl.pallas_call(
        _mlstm_kernel,
        grid=(B, steps),
        in_specs=[row(QK_PAD_W), row(QK_PAD_W), col(MLSTM_V_W), col(MLSTM_V_W), row(LANES),
                  col(2 * MLSTM_HEADS)],
        out_specs=col(MLSTM_V_W),
        out_shape=jax.ShapeDtypeStruct((MLSTM_V_W, N), BF16),
        scratch_shapes=[pltpu.VMEM((MLSTM_HEADS, V_WIN, DQK_PAD), F32),
                        pltpu.VMEM((MLSTM_HEADS, 8, LANES), F32)],
        compiler_params=pltpu.CompilerParams(dimension_semantics=("arbitrary", "arbitrary"),
                                             vmem_limit_bytes=VMEM_LIMIT),
        name="mlstm",
    )(q, k, vt, ot, gt, gtt)


def _mix_ffn_core(x_ref, main_ref, mo_ref, wo_ref, gpost_ref, gfpost_ref,
                  wi_ref, wd_ref, main_feature_major):
    main_dot = _dot_tn if main_feature_major else _dot
    mix_w = D_MODEL - MEM_W
    groups = [slice(r, r + FFN_ROW_GROUP) for r in range(0, x_ref.shape[0], FFN_ROW_GROUP)]
    mix = [main_dot(main_ref[:, g] if main_feature_major else main_ref[g, :], wo_ref[0:mix_w, :])
           + _dot(mo_ref[g, :], wo_ref[mix_w:, :]) for g in groups]
    x1 = [x_ref[g, :] + _rms(m, gpost_ref[...]) for g, m in zip(groups, mix)]
    hf = [_rms_hat(t).astype(BF16) for t in x1]
    acc = [None] * len(groups)
    for c in range(N_FF_CHUNKS):
        lo = c * FF_CHUNK
        for i, h in enumerate(hf):
            g = _dot(h, wi_ref[:, lo:lo + FF_CHUNK])
            u = _dot(h, wi_ref[:, D_FF + lo:D_FF + lo + FF_CHUNK])
            act = (g * _sigmoid(g) * u).astype(BF16)
            d = _dot(act, wd_ref[lo:lo + FF_CHUNK, :])
            acc[i] = d if acc[i] is None else acc[i] + d
    return jnp.concatenate([t + _rms(a, gfpost_ref[...]) for t, a in zip(x1, acc)], axis=0)


def _mix_ffn_kernel(x_ref, main_ref, mo_ref, wo_ref, gpost_ref, gfpost_ref,
                    wi_ref, wd_ref, xo_ref, *, main_feature_major):
    xo_ref[...] = _mix_ffn_core(x_ref, main_ref, mo_ref, wo_ref, gpost_ref,
                                gfpost_ref, wi_ref, wd_ref, main_feature_major)


def _rope(t1, t2, cos, sin):
    return t1 * cos - t2 * sin, t2 * cos + t1 * sin


def _rope_tables(pos_ref, inv_ref):
    ang = pos_ref[...].astype(F32) * inv_ref[...]
    cos_d = jnp.cos(ang)
    sin_d = jnp.sin(ang)
    grp = lax.broadcasted_iota(jnp.int32, ang.shape, 1) >> 5

    def spread(t, g):
        y = jnp.where(grp == g, t, 0.0)
        y = y + pltpu.roll(y, HEAD_DIM // 2, 1)
        return y + pltpu.roll(y, HEAD_DIM, 1)

    n_grp = LANES // (HEAD_DIM // 2)
    cos = jnp.concatenate([spread(cos_d, g) for g in range(n_grp)], axis=0)
    sin = jnp.concatenate([spread(sin_d, g) for g in range(n_grp)], axis=0)
    return cos, sin


def _mix_ffn(layer, x2d, main, mo, wo, g_post, g_fpost, wi, wd, main_feature_major):
    N = x2d.shape[0]
    tm = FFN_ROW_TILE
    row = lambda w: pl.BlockSpec((tm, w), lambda i: (i, 0))
    slab = lambda w: pl.BlockSpec((None,) + w.shape[1:], lambda i: (layer, 0, 0), pipeline_mode=pl.Buffered(1))
    main_spec = pl.BlockSpec((main.shape[0], tm), lambda i: (0, i)) if main_feature_major else row(main.shape[1])
    return pl.pallas_call(
        functools.partial(_mix_ffn_kernel, main_feature_major=main_feature_major),
        grid=(N // tm,),
        in_specs=[row(D_MODEL), main_spec, row(MEM_W),
                  _resident(wo.shape),
                  _resident((1, D_MODEL)), _resident((1, D_MODEL)),
                  slab(wi), slab(wd)],
        out_specs=row(D_MODEL),
        out_shape=jax.ShapeDtypeStruct((N, D_MODEL), F32),
        compiler_params=pltpu.CompilerParams(dimension_semantics=("arbitrary",),
                                             vmem_limit_bytes=VMEM_LIMIT),
        name="mix_ffn_%d" % layer,
    )(x2d, main, mo, wo, g_post, g_fpost, wi, wd)


def _proj_b_kernel(x_ref, pos_ref, inv_ref, wkv_ref, wb_ref, mk_ref, mv_ref,
                   ks_ref, vs_ref, q1_ref, mo1_ref):
    groups = [slice(r, r + PROJ_B_ROW_GROUP) for r in range(0, x_ref.shape[0], PROJ_B_ROW_GROUP)]
    xn = [_rms_hat(x_ref[g, :]).astype(BF16) for g in groups]
    pb = [_dot(t, wb_ref[...]) for t in xn]
    s = [_mem_scores(t[:, SWA_Q_W:].astype(BF16), mk_ref) for t in pb]
    cos, sin = _rope_tables(pos_ref, inv_ref)
    kv = [_dot(t, wkv_ref[...]) for t in xn]
    p = [_mem_probs(t) for t in s]

    for g, t in zip(groups, pb):
        for gi in range(SWA_GROUPS):
            base = gi * 2 * LANES
            t1, t2 = _rope(t[:, base:base + LANES], t[:, base + LANES:base + 2 * LANES], cos[g], sin[g])
            q1_ref[g, base:base + LANES] = t1.astype(BF16)
            q1_ref[g, base + LANES:base + 2 * LANES] = t2.astype(BF16)
    for g, t in zip(groups, p):
        mo1_ref[g, :] = _mem_out(t, mv_ref).astype(BF16)

    scale = HEAD_DIM ** -0.5 * LOG2E
    for g, t in zip(groups, kv):
        k1, k2 = _rope(t[:, 0:LANES], t[:, LANES:2 * LANES], cos[g], sin[g])
        ks_ref[g, 0:LANES] = (k1 * scale).astype(BF16)
        ks_ref[g, LANES:2 * LANES] = (k2 * scale).astype(BF16)
        vs_ref[g, :] = t[:, SWA_KV_W:].astype(BF16)


def _proj_b(x2d, pos, inv, wkv, wb, mkm, mvm, rows_per_batch):
    N = x2d.shape[0]
    tm = PROJ_B_ROW_TILE
    steps_per_batch = rows_per_batch // tm
    row = lambda w: pl.BlockSpec((tm, w), lambda i: (i, 0))
    mem_spec = pl.BlockSpec((1, 1, MEM_HEADS, MEM_TOKENS, MEM_W),
                            lambda i: (1, i // steps_per_batch, 0, 0, 0))
    pos_spec = pl.BlockSpec((tm // (LANES // (HEAD_DIM // 2)), LANES), lambda i: (i, 0))
    return pl.pallas_call(
        _proj_b_kernel,
        grid=(N // tm,),
        in_specs=[row(D_MODEL), pos_spec, _resident((1, LANES)), _resident(wkv.shape), _resident(wb.shape),
                  mem_spec, mem_spec],
        out_specs=[row(SWA_KV_W), row(SWA_KV_W), row(SWA_Q_W), row(MEM_W)],
        out_shape=[jax.ShapeDtypeStruct((N, SWA_KV_W), BF16), jax.ShapeDtypeStruct((N, SWA_KV_W), BF16),
                   jax.ShapeDtypeStruct((N, SWA_Q_W), BF16), jax.ShapeDtypeStruct((N, MEM_W), BF16)],
        compiler_params=pltpu.CompilerParams(dimension_semantics=("arbitrary",),
                                             vmem_limit_bytes=VMEM_LIMIT),
        name="proj_b",
    )(x2d, pos, inv, wkv, wb, mkm, mvm)


def _swa_kernel(sink_ref, q_ref, kp_ref, kc_ref, vp_ref, vc_ref, o_ref):
    W = SWA_WINDOW
    G = SWA_GROUPS
    n = pl.program_id(1)
    n_sub = q_ref.shape[0] // W
    H = SWA_KV_HEADS
    kpos = lax.broadcasted_iota(jnp.int32, (2 * W, G * W), 0)
    qcol = lax.broadcasted_iota(jnp.int32, (2 * W, G * W), 1)
    diff = (qcol & (W - 1)) + W - kpos
    in_band = (diff >= 0) & (diff < W)
    k_lane = lax.broadcasted_iota(jnp.int32, (2 * W, SWA_KV_W), 1)
    sink = sink_ref[...]
    subs = range(n_sub)
    q, km, vm, valid = [], [], [], []
    for sub in subs:
        q.append(jnp.concatenate([q_ref[sub * W:(sub + 1) * W, gi * 2 * LANES:(gi + 1) * 2 * LANES]
                                  for gi in range(G)], axis=0))
        if sub == 0:
            k = jnp.concatenate([kp_ref[...], kc_ref[0:W, :]], axis=0)
            v = jnp.concatenate([vp_ref[...], vc_ref[0:W, :]], axis=0)
            valid.append(in_band & ((kpos >= W) | (n > 0)))
        else:
            k = kc_ref[(sub - 1) * W:(sub + 1) * W, :]
            v = vc_ref[(sub - 1) * W:(sub + 1) * W, :]
            valid.append(in_band)
        km.append(jnp.concatenate([jnp.where(((k_lane & (LANES - 1)) >> 5) == j, k, jnp.zeros((), BF16))
                                   for j in range(H)], axis=0))
        vm.append(jnp.concatenate([jnp.where((k_lane >> 6) == j, v, jnp.zeros((), BF16))
                                   for j in range(H)], axis=0))
    s = [_dot_nt(km[i], q[i]).reshape(H, 2 * W, G * W) for i in subs]
    s = [jnp.where(valid[i][None], s[i], -jnp.inf) for i in subs]
    m = [jnp.maximum(jnp.max(s[i], axis=1, keepdims=True), sink) for i in subs]
    e = [jnp.exp2(s[i] - m[i]) for i in subs]
    p = [e[i] * (1.0 / (jnp.sum(e[i], axis=1, keepdims=True) + jnp.exp2(sink - m[i]))) for i in subs]
    out = [_dot_tn(p[i].astype(BF16).reshape(H * 2 * W, G * W), vm[i]) for i in subs]
    for sub in subs:
        for gi in range(G):
            o_ref[sub * W:(sub + 1) * W, gi * 2 * LANES:(gi + 1) * 2 * LANES] = (
                out[sub][gi * W:(gi + 1) * W].astype(BF16))


def _swa(sink_cols, q1, ks, vs, B, S):
    W = SWA_WINDOW
    n_sub = SWA_BLOCKS_PER_STEP
    tq = n_sub * W
    steps = S // tq
    cur = lambda w: pl.BlockSpec((tq, w), lambda b, n: (b * steps + n, 0))
    prev = lambda w: pl.BlockSpec((W, w), lambda b, n: (b * (S // W) + jnp.maximum(n_sub * n - 1, 0), 0))
    return pl.pallas_call(
        _swa_kernel,
        grid=(B, steps),
        in_specs=[_resident(sink_cols.shape),
                  cur(SWA_Q_W), prev(SWA_KV_W), cur(SWA_KV_W), prev(SWA_KV_W), cur(SWA_KV_W)],
        out_specs=cur(SWA_Q_W),
        out_shape=jax.ShapeDtypeStruct((B * S, SWA_Q_W), BF16),
        compiler_params=pltpu.CompilerParams(dimension_semantics=("arbitrary", "arbitrary"),
                                             vmem_limit_bytes=VMEM_LIMIT),
        name="swa",
    )(sink_cols, q1, ks, ks, vs, vs)


def _pad_heads(w, heads, width, padded):
    w = w.reshape(w.shape[0], heads, width)
    return jnp.pad(w, ((0, 0), (0, 0), (0, padded - width))).reshape(w.shape[0], heads * padded)


def _with_tail(perm, total):
    return np.concatenate([perm, np.arange(len(perm), total, dtype=np.int32)])


def _swa_q_perm():
    idx = np.empty((SWA_GROUPS, 2, SWA_KV_HEADS, HEAD_DIM // 2), np.int32)
    for gi in range(SWA_GROUPS):
        for half in range(2):
            for j in range(SWA_KV_HEADS):
                idx[gi, half, j] = (j * SWA_GROUPS + gi) * HEAD_DIM + half * (HEAD_DIM // 2) + np.arange(HEAD_DIM // 2)
    return idx.reshape(-1)


def _swa_k_perm():
    idx = np.empty((2, SWA_KV_HEADS, HEAD_DIM // 2), np.int32)
    for half in range(2):
        for j in range(SWA_KV_HEADS):
            idx[half, j] = j * HEAD_DIM + half * (HEAD_DIM // 2) + np.arange(HEAD_DIM // 2)
    return idx.reshape(-1)


def _swa_out_perm():
    idx = np.empty((SWA_GROUPS, SWA_KV_HEADS, HEAD_DIM), np.int32)
    for gi in range(SWA_GROUPS):
        for j in range(SWA_KV_HEADS):
            idx[gi, j] = (j * SWA_GROUPS + gi) * HEAD_DIM + np.arange(HEAD_DIM)
    return idx.reshape(-1)


def kernel(x, mem, positions, g_mix_pre, g_mix_post, g_ffn_pre, g_ffn_post, g_mem, w_mem_kv, w_out,
           w_ffn_in, w_ffn_out, w_in_a, b_gates_a, g_mlstm_out, g_kv, w_kv, w_in_b, sinks_b):
    B, S, _ = x.shape
    N = B * S
    assert S % PROJ_A_ROW_TILE == 0 and S % PROJ_B_ROW_TILE == 0 and S % FFN_ROW_TILE == 0
    assert S % (MLSTM_CHUNK * MLSTM_CHUNKS_PER_STEP) == 0
    assert S % (SWA_WINDOW * SWA_BLOCKS_PER_STEP) == 0
    x2d = x.reshape(N, D_MODEL)
    vec = lambda g: g.reshape(1, -1).astype(F32)

    mkm, mvm = _mem_kv(mem, g_mem, w_mem_kv.astype(BF16))

    fold = lambda g, w: g.astype(F32)[:, None] * w

    piece = lambda c0, width: fold(g_mix_pre[0], w_in_a[0][:, c0:c0 + width])
    c0 = 0
    wq = _pad_heads(piece(c0, MLSTM_QK_W), MLSTM_HEADS, MLSTM_DQK, DQK_PAD).astype(BF16)
    c0 += MLSTM_QK_W
    wk = _pad_heads(piece(c0, MLSTM_QK_W) * (MLSTM_DQK ** -0.5), MLSTM_HEADS, MLSTM_DQK, DQK_PAD).astype(BF16)
    c0 += MLSTM_QK_W
    wv = piece(c0, MLSTM_V_W).astype(BF16).T
    c0 += MLSTM_V_W
    wo = piece(c0, MLSTM_V_W).astype(BF16).T
    c0 += MLSTM_V_W
    wgt = jnp.pad(piece(c0, 2 * MLSTM_HEADS), ((0, 0), (0, LANES - 2 * MLSTM_HEADS))).astype(BF16)
    c0 += 2 * MLSTM_HEADS
    wmq = piece(c0, MEM_W).astype(BF16)
    bg = jnp.pad(b_gates_a[0].astype(F32), (0, LANES - 2 * MLSTM_HEADS)).reshape(1, LANES)

    q, k, v, o, gt, gtt, mo0 = _proj_a(x2d, wq, wk, wv, wo, wmq, wgt, bg, mkm, mvm, S)
    hm = _mlstm(q, k, v, o, gt, gtt, B, S)

    w_ffn_in_bf = (g_ffn_pre.astype(F32)[:, :, None] * w_ffn_in).astype(BF16)
    w_ffn_out_bf = w_ffn_out.astype(BF16)
    wo0 = fold(jnp.concatenate([g_mlstm_out[0].astype(F32), jnp.ones((MEM_W,), F32)]), w_out[0]).astype(BF16)
    inv = 1.0 / (ROPE_THETA ** (jnp.arange(0, HEAD_DIM, 2, dtype=F32) / HEAD_DIM))
    inv = jnp.tile(inv, SWA_KV_HEADS).reshape(1, LANES)
    wkv = fold(g_kv, w_kv)[:, _with_tail(_swa_k_perm(), 2 * SWA_KV_W)].astype(BF16)
    wb = fold(g_mix_pre[1], w_in_b[0])[:, _with_tail(_swa_q_perm(), SWA_Q_W + MEM_W)].astype(BF16)
    n_grp = LANES // (HEAD_DIM // 2)
    pos_dense = positions.reshape(N // PROJ_B_ROW_TILE, n_grp, PROJ_B_ROW_TILE // n_grp).transpose(0, 2, 1)
    pos_dense = jnp.repeat(pos_dense, HEAD_DIM // 2, axis=2).reshape(N // n_grp, LANES)
    x1 = _mix_ffn(0, x2d, hm, mo0, wo0, vec(g_mix_post[0]), vec(g_ffn_post[0]),
                  w_ffn_in_bf, w_ffn_out_bf, main_feature_major=True)
    ks, vs, q1, mo1 = _proj_b(x1, pos_dense, inv, wkv, wb, mkm, mvm, S)

    sink_rows = jnp.repeat((sinks_b[0].astype(F32) * LOG2E).reshape(SWA_KV_HEADS, SWA_GROUPS), SWA_WINDOW, axis=1)
    attn = _swa(sink_rows.reshape(SWA_KV_HEADS, 1, SWA_GROUPS * SWA_WINDOW), q1, ks, vs, B, S)
    wo1 = w_out[1][_with_tail(_swa_out_perm(), D_MODEL)].astype(BF16)
    xo = _mix_ffn(1, x1, attn, mo1, wo1, vec(g_mix_post[1]), vec(g_ffn_post[1]),
                  w_ffn_in_bf, w_ffn_out_bf, main_feature_major=False)
    return xo.reshape(B, S, D_MODEL)
```

```python
import functools

import numpy as np
import jax
import jax.numpy as jnp
from jax import lax
from jax.experimental import pallas as pl
from jax.experimental.pallas import tpu as pltpu

F32 = jnp.float32
BF16 = jnp.bfloat16

D_MODEL = 1024
DEPTH = 2
HEAD_DIM = 64
EPS = 1e-6
ROPE_THETA = 10000.0
LOG2E = 1.4426950408889634

MLSTM_HEADS = 4
MLSTM_DV = 192
MLSTM_DQK = 96
MLSTM_QK_W = MLSTM_HEADS * MLSTM_DQK
MLSTM_V_W = MLSTM_HEADS * MLSTM_DV
GATE_SOFTCAP = 15.0
M_INIT = -1e30

SWA_Q_HEADS = 12
SWA_KV_HEADS = 4
SWA_GROUPS = SWA_Q_HEADS // SWA_KV_HEADS
SWA_Q_W = SWA_Q_HEADS * HEAD_DIM
SWA_KV_W = SWA_KV_HEADS * HEAD_DIM
SWA_WINDOW = 128

MEM_TOKENS = 256
MEM_HEADS = 4
MEM_HEAD_DIM = 64
MEM_W = MEM_HEADS * MEM_HEAD_DIM

D_FF = 2816

LANES = 128
MXU_TILE = 256

DQK_PAD = LANES
QK_PAD_W = MLSTM_HEADS * DQK_PAD
V_WIN = MXU_TILE
MLSTM_CHUNK = 256
MLSTM_CHUNKS_PER_STEP = 8
FF_CHUNK = MXU_TILE
N_FF_CHUNKS = D_FF // FF_CHUNK
PROJ_A_ROW_TILE = 1024
PROJ_B_ROW_TILE = 2048
PROJ_B_ROW_GROUP = 512
FFN_ROW_TILE = 1024
FFN_ROW_GROUP = 512
SWA_BLOCKS_PER_STEP = 8
VMEM_LIMIT = 56 * 1024 * 1024

_V_WIN_START = (0, 128, 384, 512)
_V_WIN_OFF = (0, 64, 0, 64)
_V_ONES_COL = (192, 0, 192, 0)


def _rms_hat(x):
    return x * lax.rsqrt(jnp.mean(x * x, axis=-1, keepdims=True) + EPS)


def _rms(x, g):
    return _rms_hat(x) * g


def _dot(a, b):
    return jnp.dot(a, b, preferred_element_type=F32)


def _dot_nt(a, b):
    return lax.dot_general(a, b, (((1,), (1,)), ((), ())), preferred_element_type=F32)


def _dot_tn(a, b):
    return lax.dot_general(a, b, (((0,), (0,)), ((), ())), preferred_element_type=F32)


def _sigmoid(x):
    return 1.0 / (1.0 + jnp.exp(-x))


def _resident(shape):
    nd = len(shape)
    return pl.BlockSpec(shape, lambda *_: (0,) * nd, pipeline_mode=pl.Buffered(1))


def _mem_kv_kernel(mem_ref, g_ref, w_ref, mk_ref, mv_ref):
    B = mem_ref.shape[0]
    hn = _rms(mem_ref[...].reshape(B * MEM_TOKENS, D_MODEL), g_ref[0]).astype(BF16)
    kv = _dot(hn, w_ref[0])
    mk = kv[:, :MEM_W] * (MEM_HEAD_DIM ** -0.5 * LOG2E)
    mv = kv[:, MEM_W:]
    lane_head = lax.broadcasted_iota(jnp.int32, mk.shape, 1) >> 6
    for h in range(MEM_HEADS):
        sel = lane_head == h
        mk_ref[0, :, h] = jnp.where(sel, mk, 0.0).astype(BF16).reshape(B, MEM_TOKENS, MEM_W)
        mv_ref[0, :, h] = jnp.where(sel, mv, 0.0).astype(BF16).reshape(B, MEM_TOKENS, MEM_W)


def _mem_kv(mem, g_mem, w_mem_kv_bf16):
    B = mem.shape[0]
    out_sds = jax.ShapeDtypeStruct((DEPTH, B, MEM_HEADS, MEM_TOKENS, MEM_W), BF16)
    out_spec = pl.BlockSpec((1, B, MEM_HEADS, MEM_TOKENS, MEM_W), lambda l: (l, 0, 0, 0, 0))
    return pl.pallas_call(
        _mem_kv_kernel,
        grid=(DEPTH,),
        in_specs=[
            pl.BlockSpec((B, MEM_TOKENS, D_MODEL), lambda l: (0, 0, 0)),
            pl.BlockSpec((1, 1, D_MODEL), lambda l: (l, 0, 0)),
            pl.BlockSpec((1, D_MODEL, 2 * MEM_W), lambda l: (l, 0, 0)),
        ],
        out_specs=[out_spec, out_spec],
        out_shape=[out_sds, out_sds],
        compiler_params=pltpu.CompilerParams(dimension_semantics=("arbitrary",),
                                             vmem_limit_bytes=VMEM_LIMIT),
        name="mem_kv",
    )(mem, g_mem.reshape(DEPTH, 1, D_MODEL), w_mem_kv_bf16)


def _mem_scores(mq, mk_ref):
    return [_dot_nt(mq, mk_ref[0, 0, h]) for h in range(MEM_HEADS)]


def _mem_probs(s):
    e = [jnp.exp2(sh - jnp.max(sh, axis=-1, keepdims=True)) for sh in s]
    return [(eh * (1.0 / jnp.sum(eh, axis=-1, keepdims=True))).astype(BF16) for eh in e]


def _mem_out(p, mv_ref):
    out = _dot(p[0], mv_ref[0, 0, 0])
    for h in range(1, MEM_HEADS):
        out = out + _dot(p[h], mv_ref[0, 0, h])
    return out


def _proj_a_kernel(x_ref, wq_ref, wk_ref, wv_ref, wo_ref, wmq_ref, wg_ref, bg_ref, mk_ref, mv_ref,
                   q_ref, k_ref, v_ref, o_ref, gt_ref, gtt_ref, mo_ref):
    hn = _rms_hat(x_ref[...]).astype(BF16)
    mq = _dot(hn, wmq_ref[...]).astype(BF16)
    s = _mem_scores(mq, mk_ref)
    gates = _dot(hn, wg_ref[...]) + bg_ref[...]
    q_ref[...] = _dot(hn, wq_ref[...]).astype(BF16)
    gates = GATE_SOFTCAP * jnp.tanh(gates * (1.0 / GATE_SOFTCAP))
    log_sig = jnp.minimum(gates, 0.0) - jnp.log1p(jnp.exp(-jnp.abs(gates)))
    lane = lax.broadcasted_iota(jnp.int32, gates.shape, 1)
    gl = jnp.where(lane < MLSTM_HEADS, gates, log_sig) * LOG2E
    p = _mem_probs(s)
    k_ref[...] = _dot(hn, wk_ref[...]).astype(BF16)
    g_hi = gl.astype(BF16)
    rest = gl - g_hi.astype(F32)
    g_mid = rest.astype(BF16)
    g_lo = (rest - g_mid.astype(F32)).astype(BF16)
    L = MLSTM_CHUNK
    tri = (lax.broadcasted_iota(jnp.int32, (L, L), 1) <= lax.broadcasted_iota(jnp.int32, (L, L), 0)).astype(BF16)
    csum = jnp.concatenate(
        [_dot(tri, g_hi[r:r + L]) + _dot(tri, g_mid[r:r + L]) + _dot(tri, g_lo[r:r + L])
         for r in range(0, gl.shape[0], L)], axis=0)
    v_ref[...] = _dot_nt(wv_ref[...], hn).astype(BF16)
    y = jnp.where(lane < MLSTM_HEADS, gl, csum)
    gt_ref[...] = y
    gtt_ref[...] = y.T[0:2 * MLSTM_HEADS, :]
    mo_ref[...] = _mem_out(p, mv_ref).astype(BF16)
    o_ref[...] = _dot_nt(wo_ref[...], hn).astype(BF16)


def _proj_a(x2d, wq, wk, wv, wo, wmq, wg, bg, mkm, mvm, rows_per_batch):
    N = x2d.shape[0]
    tm = PROJ_A_ROW_TILE
    steps_per_batch = rows_per_batch // tm
    row = lambda w: pl.BlockSpec((tm, w), lambda i: (i, 0))
    col = pl.BlockSpec((MLSTM_V_W, tm), lambda i: (0, i))
    mem_spec = pl.BlockSpec((1, 1, MEM_HEADS, MEM_TOKENS, MEM_W),
                            lambda i: (0, i // steps_per_batch, 0, 0, 0))
    return pl.pallas_call(
        _proj_a_kernel,
        grid=(N // tm,),
        in_specs=[row(D_MODEL),
                  _resident(wq.shape), _resident(wk.shape), _resident(wv.shape), _resident(wo.shape),
                  _resident(wmq.shape), _resident(wg.shape), _resident((1, LANES)),
                  mem_spec, mem_spec],
        out_specs=[row(QK_PAD_W), row(QK_PAD_W), col, col, row(LANES),
                   pl.BlockSpec((2 * MLSTM_HEADS, tm), lambda i: (0, i)), row(MEM_W)],
        out_shape=[jax.ShapeDtypeStruct((N, QK_PAD_W), BF16), jax.ShapeDtypeStruct((N, QK_PAD_W), BF16),
                   jax.ShapeDtypeStruct((MLSTM_V_W, N), BF16), jax.ShapeDtypeStruct((MLSTM_V_W, N), BF16),
                   jax.ShapeDtypeStruct((N, LANES), F32), jax.ShapeDtypeStruct((2 * MLSTM_HEADS, N), F32),
                   jax.ShapeDtypeStruct((N, MEM_W), BF16)],
        compiler_params=pltpu.CompilerParams(dimension_semantics=("arbitrary",),
                                             vmem_limit_bytes=VMEM_LIMIT),
        name="proj_a",
    )(x2d, wq, wk, wv, wo, wmq, wg, bg, mkm, mvm)


def _mlstm_kernel(q_ref, k_ref, vt_ref, ot_ref, gt_ref, gtt_ref, hmt_ref, c_scr, m_scr):
    L = MLSTM_CHUNK

    @pl.when(pl.program_id(1) == 0)
    def _():
        c_scr[...] = jnp.zeros(c_scr.shape, F32)
        m_scr[...] = jnp.full(m_scr.shape, M_INIT, F32)

    causal = lax.broadcasted_iota(jnp.int32, (L, L), 0) <= lax.broadcasted_iota(jnp.int32, (L, L), 1)
    feat = lax.broadcasted_iota(jnp.int32, (V_WIN, L), 0)
    heads = range(MLSTM_HEADS)
    m_st = [m_scr[h][0:1, 0:1] for h in heads]
    c_st = [c_scr[h] for h in heads]

    for r0 in range(0, q_ref.shape[0], L):
        y = gt_ref[r0:r0 + L, :]
        yt = gtt_ref[:, r0:r0 + L]
        b_row = [yt[MLSTM_HEADS + h:MLSTM_HEADS + h + 1, :] for h in heads]
        c_row = [b_row[h] - yt[h:h + 1, :] for h in heads]
        c_col = [y[:, MLSTM_HEADS + h:MLSTM_HEADS + h + 1] - y[:, h:h + 1] for h in heads]
        g_tot = [y[L - 1:L, MLSTM_HEADS + h:MLSTM_HEADS + h + 1] for h in heads]
        qh = [q_ref[r0:r0 + L, h * DQK_PAD:(h + 1) * DQK_PAD] for h in heads]
        kh = [k_ref[r0:r0 + L, h * DQK_PAD:(h + 1) * DQK_PAD] for h in heads]
        vt_aug = [jnp.where(feat == _V_ONES_COL[h], jnp.ones((), BF16),
                            vt_ref[_V_WIN_START[h]:_V_WIN_START[h] + V_WIN, r0:r0 + L]) for h in heads]

        qk_t = [_dot_nt(kh[h], qh[h]) for h in heads]
        inter_t = [_dot_nt(c_st[h].astype(BF16), qh[h]) for h in heads]
        dmat = [jnp.where(causal, b_row[h] - c_col[h], -jnp.inf) for h in heads]
        inter_log = [b_row[h] + m_st[h] for h in heads]
        m_row = [jnp.maximum(inter_log[h], jnp.max(dmat[h], axis=0, keepdims=True)) for h in heads]
        scores_t = [(qk_t[h] * jnp.exp2(dmat[h] - m_row[h])).astype(BF16) for h in heads]
        num_t = [_dot(vt_aug[h], scores_t[h]) + jnp.exp2(inter_log[h] - m_row[h]) * inter_t[h] for h in heads]

        a = [g_tot[h] - c_row[h] for h in heads]
        m_new = [jnp.maximum(g_tot[h] + m_st[h], jnp.max(a[h], axis=1, keepdims=True)) for h in heads]
        vtw = [vt_aug[h] * jnp.exp2(a[h] - m_new[h]).astype(BF16) for h in heads]
        c_upd = [_dot(vtw[h], kh[h]) for h in heads]

        for h in heads:
            off, ones_row, lo = _V_WIN_OFF[h], _V_ONES_COL[h], h * MLSTM_DV
            den = num_t[h][ones_row:ones_row + 1, :]
            inv_dd = 1.0 / jnp.maximum(jnp.abs(den), jnp.exp2(-m_row[h]))
            h_t = num_t[h][off:off + MLSTM_DV, :]
            ms = jnp.sum(h_t * h_t, axis=0, keepdims=True) * (1.0 / MLSTM_DV)
            scale = inv_dd * lax.rsqrt(ms * inv_dd * inv_dd + EPS)
            og = _sigmoid(ot_ref[lo:lo + MLSTM_DV, r0:r0 + L].astype(F32))
            hmt_ref[lo:lo + MLSTM_DV, r0:r0 + L] = (h_t * (scale * og)).astype(BF16)

        c_st = [jnp.exp2(g_tot[h] + m_st[h] - m_new[h]) * c_st[h] + c_upd[h] for h in heads]
        m_st = m_new

    for h in heads:
        c_scr[h] = c_st[h]
        m_scr[h] = jnp.broadcast_to(m_st[h], m_scr.shape[1:])


def _mlstm(q, k, vt, ot, gt, gtt, B, S):
    rows = MLSTM_CHUNK * MLSTM_CHUNKS_PER_STEP
    steps = S // rows
    N = B * S
    row = lambda w: pl.BlockSpec((rows, w), lambda b, c: (b * steps + c, 0))
    col = lambda h: pl.BlockSpec((h, rows), lambda b, c: (0, b * steps + c))
    return pl.pallas_call(
        _mlstm_kernel,
        grid=(B, steps),
        in_specs=[row(QK_PAD_W), row(QK_PAD_W), col(MLSTM_V_W), col(MLSTM_V_W), row(LANES),
                  col(2 * MLSTM_HEADS)],
        out_specs=col(MLSTM_V_W),
        out_shape=jax.ShapeDtypeStruct((MLSTM_V_W, N), BF16),
        scratch_shapes=[pltpu.VMEM((MLSTM_HEADS, V_WIN, DQK_PAD), F32),
                        pltpu.VMEM((MLSTM_HEADS, 8, LANES), F32)],
        compiler_params=pltpu.CompilerParams(dimension_semantics=("arbitrary", "arbitrary"),
                                             vmem_limit_bytes=VMEM_LIMIT),
        name="mlstm",
    )(q, k, vt, ot, gt, gtt)


def _mix_ffn_core(x_ref, main_ref, mo_ref, wo_ref, gpost_ref, gfpost_ref,
                  wi_ref, wd_ref, main_feature_major):
    main_dot = _dot_tn if main_feature_major else _dot
    mix_w = D_MODEL - MEM_W
    groups = [slice(r, r + FFN_ROW_GROUP) for r in range(0, x_ref.shape[0], FFN_ROW_GROUP)]
    mix = [main_dot(main_ref[:, g] if main_feature_major else main_ref[g, :], wo_ref[0:mix_w, :])
           + _dot(mo_ref[g, :], wo_ref[mix_w:, :]) for g in groups]
    x1 = [x_ref[g, :] + _rms(m, gpost_ref[...]) for g, m in zip(groups, mix)]
    hf = [_rms_hat(t).astype(BF16) for t in x1]
    act = [[] for _ in groups]
    for c in range(N_FF_CHUNKS):
        lo = c * FF_CHUNK
        for i, h in enumerate(hf):
            g = _dot(h, wi_ref[:, lo:lo + FF_CHUNK])
            u = _dot(h, wi_ref[:, D_FF + lo:D_FF + lo + FF_CHUNK])
            act[i].append((g * _sigmoid(g) * u).astype(BF16))
    acc = [_dot(jnp.concatenate(a, axis=1), wd_ref[...]) for a in act]
    return jnp.concatenate([t + _rms(a, gfpost_ref[...]) for t, a in zip(x1, acc)], axis=0)


def _mix_ffn_kernel(x_ref, main_ref, mo_ref, wo_ref, gpost_ref, gfpost_ref,
                    wi_ref, wd_ref, xo_ref, *, main_feature_major):
    xo_ref[...] = _mix_ffn_core(x_ref, main_ref, mo_ref, wo_ref, gpost_ref,
                                gfpost_ref, wi_ref, wd_ref, main_feature_major)


def _rope(t1, t2, cos, sin):
    return t1 * cos - t2 * sin, t2 * cos + t1 * sin


def _rope_tables(pos_ref, inv_ref):
    ang = pos_ref[...].astype(F32) * inv_ref[...]
    cos_d = jnp.cos(ang)
    sin_d = jnp.sin(ang)
    grp = lax.broadcasted_iota(jnp.int32, ang.shape, 1) >> 5

    def spread(t, g):
        y = jnp.where(grp == g, t, 0.0)
        y = y + pltpu.roll(y, HEAD_DIM // 2, 1)
        return y + pltpu.roll(y, HEAD_DIM, 1)

    n_grp = LANES // (HEAD_DIM // 2)
    cos = jnp.concatenate([spread(cos_d, g) for g in range(n_grp)], axis=0)
    sin = jnp.concatenate([spread(sin_d, g) for g in range(n_grp)], axis=0)
    return cos, sin


def _mix_ffn(layer, x2d, main, mo, wo, g_post, g_fpost, wi, wd, main_feature_major):
    N = x2d.shape[0]
    tm = FFN_ROW_TILE
    row = lambda w: pl.BlockSpec((tm, w), lambda i: (i, 0))
    slab = lambda w: pl.BlockSpec((None,) + w.shape[1:], lambda i: (layer, 0, 0), pipeline_mode=pl.Buffered(1))
    main_spec = pl.BlockSpec((main.shape[0], tm), lambda i: (0, i)) if main_feature_major else row(main.shape[1])
    return pl.pallas_call(
        functools.partial(_mix_ffn_kernel, main_feature_major=main_feature_major),
        grid=(N // tm,),
        in_specs=[row(D_MODEL), main_spec, row(MEM_W),
                  _resident(wo.shape),
                  _resident((1, D_MODEL)), _resident((1, D_MODEL)),
                  slab(wi), slab(wd)],
        out_specs=row(D_MODEL),
        out_shape=jax.ShapeDtypeStruct((N, D_MODEL), F32),
        compiler_params=pltpu.CompilerParams(dimension_semantics=("arbitrary",),
                                             vmem_limit_bytes=VMEM_LIMIT),
        name="mix_ffn_%d" % layer,
    )(x2d, main, mo, wo, g_post, g_fpost, wi, wd)


def _proj_b_kernel(x_ref, pos_ref, inv_ref, wkv_ref, wb_ref, mk_ref, mv_ref,
                   ks_ref, vs_ref, q1_ref, mo1_ref):
    groups = [slice(r, r + PROJ_B_ROW_GROUP) for r in range(0, x_ref.shape[0], PROJ_B_ROW_GROUP)]
    xn = [_rms_hat(x_ref[g, :]).astype(BF16) for g in groups]
    pb = [_dot(t, wb_ref[...]) for t in xn]
    s = [_mem_scores(t[:, SWA_Q_W:].astype(BF16), mk_ref) for t in pb]
    cos, sin = _rope_tables(pos_ref, inv_ref)
    kv = [_dot(t, wkv_ref[...]) for t in xn]
    p = [_mem_probs(t) for t in s]

    for g, t in zip(groups, pb):
        for gi in range(SWA_GROUPS):
            base = gi * 2 * LANES
            t1, t2 = _rope(t[:, base:base + LANES], t[:, base + LANES:base + 2 * LANES], cos[g], sin[g])
            q1_ref[g, base:base + LANES] = t1.astype(BF16)
            q1_ref[g, base + LANES:base + 2 * LANES] = t2.astype(BF16)
    for g, t in zip(groups, p):
        mo1_ref[g, :] = _mem_out(t, mv_ref).astype(BF16)

    scale = HEAD_DIM ** -0.5 * LOG2E
    for g, t in zip(groups, kv):
        k1, k2 = _rope(t[:, 0:LANES], t[:, LANES:2 * LANES], cos[g], sin[g])
        ks_ref[g, 0:LANES] = (k1 * scale).astype(BF16)
        ks_ref[g, LANES:2 * LANES] = (k2 * scale).astype(BF16)
        vs_ref[g, :] = t[:, SWA_KV_W:].astype(BF16)


def _proj_b(x2d, pos, inv, wkv, wb, mkm, mvm, rows_per_batch):
    N = x2d.shape[0]
    tm = PROJ_B_ROW_TILE
    steps_per_batch = rows_per_batch // tm
    row = lambda w: pl.BlockSpec((tm, w), lambda i: (i, 0))
    mem_spec = pl.BlockSpec((1, 1, MEM_HEADS, MEM_TOKENS, MEM_W),
                            lambda i: (1, i // steps_per_batch, 0, 0, 0))
    pos_spec = pl.BlockSpec((tm // (LANES // (HEAD_DIM // 2)), LANES), lambda i: (i, 0))
    return pl.pallas_call(
        _proj_b_kernel,
        grid=(N // tm,),
        in_specs=[row(D_MODEL), pos_spec, _resident((1, LANES)), _resident(wkv.shape), _resident(wb.shape),
                  mem_spec, mem_spec],
        out_specs=[row(SWA_KV_W), row(SWA_KV_W), row(SWA_Q_W), row(MEM_W)],
        out_shape=[jax.ShapeDtypeStruct((N, SWA_KV_W), BF16), jax.ShapeDtypeStruct((N, SWA_KV_W), BF16),
                   jax.ShapeDtypeStruct((N, SWA_Q_W), BF16), jax.ShapeDtypeStruct((N, MEM_W), BF16)],
        compiler_params=pltpu.CompilerParams(dimension_semantics=("arbitrary",),
                                             vmem_limit_bytes=VMEM_LIMIT),
        name="proj_b",
    )(x2d, pos, inv, wkv, wb, mkm, mvm)


def _swa_kernel(sink_ref, q_ref, kp_ref, kc_ref, vp_ref, vc_ref, o_ref):
    W = SWA_WINDOW
    G = SWA_GROUPS
    n = pl.program_id(1)
    n_sub = q_ref.shape[0] // W
    H = SWA_KV_HEADS
    kpos = lax.broadcasted_iota(jnp.int32, (2 * W, G * W), 0)
    qcol = lax.broadcasted_iota(jnp.int32, (2 * W, G * W), 1)
    diff = (qcol & (W - 1)) + W - kpos
    in_band = (diff >= 0) & (diff < W)
    k_lane = lax.broadcasted_iota(jnp.int32, (2 * W, SWA_KV_W), 1)
    sink = sink_ref[...]
    subs = range(n_sub)
    q, km, vm, valid = [], [], [], []
    for sub in subs:
        q.append(jnp.concatenate([q_ref[sub * W:(sub + 1) * W, gi * 2 * LANES:(gi + 1) * 2 * LANES]
                                  for gi in range(G)], axis=0))
        if sub == 0:
            k = jnp.concatenate([kp_ref[...], kc_ref[0:W, :]], axis=0)
            v = jnp.concatenate([vp_ref[...], vc_ref[0:W, :]], axis=0)
            valid.append(in_band & ((kpos >= W) | (n > 0)))
        else:
            k = kc_ref[(sub - 1) * W:(sub + 1) * W, :]
            v = vc_ref[(sub - 1) * W:(sub + 1) * W, :]
            valid.append(in_band)
        km.append(jnp.concatenate([jnp.where(((k_lane & (LANES - 1)) >> 5) == j, k, jnp.zeros((), BF16))
                                   for j in range(H)], axis=0))
        vm.append(jnp.concatenate([jnp.where((k_lane >> 6) == j, v, jnp.zeros((), BF16))
                                   for j in range(H)], axis=0))
    s = [_dot_nt(km[i], q[i]).reshape(H, 2 * W, G * W) for i in subs]
    s = [jnp.where(valid[i][None], s[i], -jnp.inf) for i in subs]
    m = [jnp.maximum(jnp.max(s[i], axis=1, keepdims=True), sink) for i in subs]
    e = [jnp.exp2(s[i] - m[i]) for i in subs]
    inv = [1.0 / (jnp.sum(e[i], axis=1, keepdims=True) + jnp.exp2(sink - m[i])) for i in subs]
    p = [e[i].astype(BF16) * inv[i].astype(BF16) for i in subs]
    out = [_dot_tn(p[i].reshape(H * 2 * W, G * W), vm[i]) for i in subs]
    for sub in subs:
        for gi in range(G):
            o_ref[sub * W:(sub + 1) * W, gi * 2 * LANES:(gi + 1) * 2 * LANES] = (
                out[sub][gi * W:(gi + 1) * W].astype(BF16))


def _swa(sink_cols, q1, ks, vs, B, S):
    W = SWA_WINDOW
    n_sub = SWA_BLOCKS_PER_STEP
    tq = n_sub * W
    steps = S // tq
    cur = lambda w: pl.BlockSpec((tq, w), lambda b, n: (b * steps + n, 0))
    prev = lambda w: pl.BlockSpec((W, w), lambda b, n: (b * (S // W) + jnp.maximum(n_sub * n - 1, 0), 0))
    return pl.pallas_call(
        _swa_kernel,
        grid=(B, steps),
        in_specs=[_resident(sink_cols.shape),
                  cur(SWA_Q_W), prev(SWA_KV_W), cur(SWA_KV_W), prev(SWA_KV_W), cur(SWA_KV_W)],
        out_specs=cur(SWA_Q_W),
        out_shape=jax.ShapeDtypeStruct((B * S, SWA_Q_W), BF16),
        compiler_params=pltpu.CompilerParams(dimension_semantics=("arbitrary", "arbitrary"),
                                             vmem_limit_bytes=VMEM_LIMIT),
        name="swa",
    )(sink_cols, q1, ks, ks, vs, vs)


def _pad_heads(w, heads, width, padded):
    w = w.reshape(w.shape[0], heads, width)
    return jnp.pad(w, ((0, 0), (0, 0), (0, padded - width))).reshape(w.shape[0], heads * padded)


def _with_tail(perm, total):
    return np.concatenate([perm, np.arange(len(perm), total, dtype=np.int32)])


def _swa_q_perm():
    idx = np.empty((SWA_GROUPS, 2, SWA_KV_HEADS, HEAD_DIM // 2), np.int32)
    for gi in range(SWA_GROUPS):
        for half in range(2):
            for j in range(SWA_KV_HEADS):
                idx[gi, half, j] = (j * SWA_GROUPS + gi) * HEAD_DIM + half * (HEAD_DIM // 2) + np.arange(HEAD_DIM // 2)
    return idx.reshape(-1)


def _swa_k_perm():
    idx = np.empty((2, SWA_KV_HEADS, HEAD_DIM // 2), np.int32)
    for half in range(2):
        for j in range(SWA_KV_HEADS):
            idx[half, j] = j * HEAD_DIM + half * (HEAD_DIM // 2) + np.arange(HEAD_DIM // 2)
    return idx.reshape(-1)


def _swa_out_perm():
    idx = np.empty((SWA_GROUPS, SWA_KV_HEADS, HEAD_DIM), np.int32)
    for gi in range(SWA_GROUPS):
        for j in range(SWA_KV_HEADS):
            idx[gi, j] = (j * SWA_GROUPS + gi) * HEAD_DIM + np.arange(HEAD_DIM)
    return idx.reshape(-1)


def kernel(x, mem, positions, g_mix_pre, g_mix_post, g_ffn_pre, g_ffn_post, g_mem, w_mem_kv, w_out,
           w_ffn_in, w_ffn_out, w_in_a, b_gates_a, g_mlstm_out, g_kv, w_kv, w_in_b, sinks_b):
    B, S, _ = x.shape
    N = B * S
    assert S % PROJ_A_ROW_TILE == 0 and S % PROJ_B_ROW_TILE == 0 and S % FFN_ROW_TILE == 0
    assert S % (MLSTM_CHUNK * MLSTM_CHUNKS_PER_STEP) == 0
    assert S % (SWA_WINDOW * SWA_BLOCKS_PER_STEP) == 0
    x2d = x.reshape(N, D_MODEL)
    vec = lambda g: g.reshape(1, -1).astype(F32)

    mkm, mvm = _mem_kv(mem, g_mem, w_mem_kv.astype(BF16))

    fold = lambda g, w: g.astype(F32)[:, None] * w

    piece = lambda c0, width: fold(g_mix_pre[0], w_in_a[0][:, c0:c0 + width])
    c0 = 0
    wq = _pad_heads(piece(c0, MLSTM_QK_W), MLSTM_HEADS, MLSTM_DQK, DQK_PAD).astype(BF16)
    c0 += MLSTM_QK_W
    wk = _pad_heads(piece(c0, MLSTM_QK_W) * (MLSTM_DQK ** -0.5), MLSTM_HEADS, MLSTM_DQK, DQK_PAD).astype(BF16)
    c0 += MLSTM_QK_W
    wv = piece(c0, MLSTM_V_W).astype(BF16).T
    c0 += MLSTM_V_W
    wo = piece(c0, MLSTM_V_W).astype(BF16).T
    c0 += MLSTM_V_W
    wgt = jnp.pad(piece(c0, 2 * MLSTM_HEADS), ((0, 0), (0, LANES - 2 * MLSTM_HEADS))).astype(BF16)
    c0 += 2 * MLSTM_HEADS
    wmq = piece(c0, MEM_W).astype(BF16)
    bg = jnp.pad(b_gates_a[0].astype(F32), (0, LANES - 2 * MLSTM_HEADS)).reshape(1, LANES)

    q, k, v, o, gt, gtt, mo0 = _proj_a(x2d, wq, wk, wv, wo, wmq, wgt, bg, mkm, mvm, S)
    hm = _mlstm(q, k, v, o, gt, gtt, B, S)

    w_ffn_in_bf = (g_ffn_pre.astype(F32)[:, :, None] * w_ffn_in).astype(BF16)
    w_ffn_out_bf = w_ffn_out.astype(BF16)
    wo0 = fold(jnp.concatenate([g_mlstm_out[0].astype(F32), jnp.ones((MEM_W,), F32)]), w_out[0]).astype(BF16)
    inv = 1.0 / (ROPE_THETA ** (jnp.arange(0, HEAD_DIM, 2, dtype=F32) / HEAD_DIM))
    inv = jnp.tile(inv, SWA_KV_HEADS).reshape(1, LANES)
    wkv = fold(g_kv, w_kv)[:, _with_tail(_swa_k_perm(), 2 * SWA_KV_W)].astype(BF16)
    wb = fold(g_mix_pre[1], w_in_b[0])[:, _with_tail(_swa_q_perm(), SWA_Q_W + MEM_W)].astype(BF16)
    n_grp = LANES // (HEAD_DIM // 2)
    pos_dense = positions.reshape(N // PROJ_B_ROW_TILE, n_grp, PROJ_B_ROW_TILE // n_grp).transpose(0, 2, 1)
    pos_dense = jnp.repeat(pos_dense, HEAD_DIM // 2, axis=2).reshape(N // n_grp, LANES)
    x1 = _mix_ffn(0, x2d, hm, mo0, wo0, vec(g_mix_post[0]), vec(g_ffn_post[0]),
                  w_ffn_in_bf, w_ffn_out_bf, main_feature_major=True)
    ks, vs, q1, mo1 = _proj_b(x1, pos_dense, inv, wkv, wb, mkm, mvm, S)

    sink_rows = jnp.repeat((sinks_b[0].astype(F32) * LOG2E).reshape(SWA_KV_HEADS, SWA_GROUPS), SWA_WINDOW, axis=1)
    attn = _swa(sink_rows.reshape(SWA_KV_HEADS, 1, SWA_GROUPS * SWA_WINDOW), q1, ks, vs, B, S)
    wo1 = w_out[1][_with_tail(_swa_out_perm(), D_MODEL)].astype(BF16)
    xo = _mix_ffn(1, x1, attn, mo1, wo1, vec(g_mix_post[1]), vec(g_ffn_post[1]),
                  w_ffn_in_bf, w_ffn_out_bf, main_feature_major=False)
    return xo.reshape(B, S, D_MODEL)
```

```python
import functools

import numpy as np
import jax
import jax.numpy as jnp
from jax import lax
from jax.experimental import pallas as pl
from jax.experimental.pallas import tpu as pltpu

F32 = jnp.float32
BF16 = jnp.bfloat16

D_MODEL = 1024
DEPTH = 2
HEAD_DIM = 64
EPS = 1e-6
ROPE_THETA = 10000.0
LOG2E = 1.4426950408889634

MLSTM_HEADS = 4
MLSTM_DV = 192
MLSTM_DQK = 96
MLSTM_QK_W = MLSTM_HEADS * MLSTM_DQK
MLSTM_V_W = MLSTM_HEADS * MLSTM_DV
GATE_SOFTCAP = 15.0
M_INIT = -1e30

SWA_Q_HEADS = 12
SWA_KV_HEADS = 4
SWA_GROUPS = SWA_Q_HEADS // SWA_KV_HEADS
SWA_Q_W = SWA_Q_HEADS * HEAD_DIM
SWA_KV_W = SWA_KV_HEADS * HEAD_DIM
SWA_WINDOW = 128

MEM_TOKENS = 256
MEM_HEADS = 4
MEM_HEAD_DIM = 64
MEM_W = MEM_HEADS * MEM_HEAD_DIM

D_FF = 2816

LANES = 128
MXU_TILE = 256

DQK_PAD = LANES
QK_PAD_W = MLSTM_HEADS * DQK_PAD
V_WIN = MXU_TILE
MLSTM_CHUNK = 256
MLSTM_CHUNKS_PER_STEP = 8
FF_CHUNK = MXU_TILE
N_FF_CHUNKS = D_FF // FF_CHUNK
PROJ_A_ROW_TILE = 1024
PROJ_B_ROW_TILE = 2048
PROJ_B_ROW_GROUP = 512
FFN_ROW_TILE = 1024
FFN_ROW_GROUP = 512
SWA_BLOCKS_PER_STEP = 8
VMEM_LIMIT = 56 * 1024 * 1024

_V_WIN_START = (0, 128, 384, 512)
_V_WIN_OFF = (0, 64, 0, 64)
_V_ONES_COL = (192, 0, 192, 0)


def _rms_hat(x):
    return x * lax.rsqrt(jnp.mean(x * x, axis=-1, keepdims=True) + EPS)


def _rms(x, g):
    return _rms_hat(x) * g


def _dot(a, b):
    return jnp.dot(a, b, preferred_element_type=F32)


def _dot_nt(a, b):
    return lax.dot_general(a, b, (((1,), (1,)), ((), ())), preferred_element_type=F32)


def _dot_tn(a, b):
    return lax.dot_general(a, b, (((0,), (0,)), ((), ())), preferred_element_type=F32)


def _sigmoid(x):
    return 1.0 / (1.0 + jnp.exp(-x))


def _resident(shape):
    nd = len(shape)
    return pl.BlockSpec(shape, lambda *_: (0,) * nd, pipeline_mode=pl.Buffered(1))


def _mem_kv_kernel(mem_ref, g_ref, w_ref, mk_ref, mv_ref):
    B = mem_ref.shape[0]
    hn = _rms(mem_ref[...].reshape(B * MEM_TOKENS, D_MODEL), g_ref[0]).astype(BF16)
    kv = _dot(hn, w_ref[0])
    mk = kv[:, :MEM_W] * (MEM_HEAD_DIM ** -0.5 * LOG2E)
    mv = kv[:, MEM_W:]
    lane_head = lax.broadcasted_iota(jnp.int32, mk.shape, 1) >> 6
    for h in range(MEM_HEADS):
        sel = lane_head == h
        mk_ref[0, :, h] = jnp.where(sel, mk, 0.0).astype(BF16).reshape(B, MEM_TOKENS, MEM_W)
        mv_ref[0, :, h] = jnp.where(sel, mv, 0.0).astype(BF16).reshape(B, MEM_TOKENS, MEM_W)


def _mem_kv(mem, g_mem, w_mem_kv_bf16):
    B = mem.shape[0]
    out_sds = jax.ShapeDtypeStruct((DEPTH, B, MEM_HEADS, MEM_TOKENS, MEM_W), BF16)
    out_spec = pl.BlockSpec((1, B, MEM_HEADS, MEM_TOKENS, MEM_W), lambda l: (l, 0, 0, 0, 0))
    return pl.pallas_call(
        _mem_kv_kernel,
        grid=(DEPTH,),
        in_specs=[
            pl.BlockSpec((B, MEM_TOKENS, D_MODEL), lambda l: (0, 0, 0)),
            pl.BlockSpec((1, 1, D_MODEL), lambda l: (l, 0, 0)),
            pl.BlockSpec((1, D_MODEL, 2 * MEM_W), lambda l: (l, 0, 0)),
        ],
        out_specs=[out_spec, out_spec],
        out_shape=[out_sds, out_sds],
        compiler_params=pltpu.CompilerParams(dimension_semantics=("arbitrary",),
                                             vmem_limit_bytes=VMEM_LIMIT),
        name="mem_kv",
    )(mem, g_mem.reshape(DEPTH, 1, D_MODEL), w_mem_kv_bf16)


def _mem_scores(mq, mk_ref):
    return [_dot_nt(mq, mk_ref[0, 0, h]) for h in range(MEM_HEADS)]


def _mem_probs(s):
    e = [jnp.exp2(sh - jnp.max(sh, axis=-1, keepdims=True)) for sh in s]
    return [(eh * (1.0 / jnp.sum(eh, axis=-1, keepdims=True))).astype(BF16) for eh in e]


def _mem_out(p, mv_ref):
    out = _dot(p[0], mv_ref[0, 0, 0])
    for h in range(1, MEM_HEADS):
        out = out + _dot(p[h], mv_ref[0, 0, h])
    return out


def _proj_a_kernel(x_ref, wq_ref, wk_ref, wv_ref, wo_ref, wmq_ref, wg_ref, bg_ref, mk_ref, mv_ref, pos_ref, inv_ref,
                   q_ref, k_ref, v_ref, o_ref, gt_ref, gtt_ref, mo_ref, cos_ref, sin_ref):
    cos_ref[...], sin_ref[...] = _rope_tables(pos_ref, inv_ref)
    hn = _rms_hat(x_ref[...]).astype(BF16)
    mq = _dot(hn, wmq_ref[...]).astype(BF16)
    s = _mem_scores(mq, mk_ref)
    gates = _dot(hn, wg_ref[...]) + bg_ref[...]
    q_ref[...] = _dot(hn, wq_ref[...]).astype(BF16)
    gates = GATE_SOFTCAP * jnp.tanh(gates * (1.0 / GATE_SOFTCAP))
    log_sig = jnp.minimum(gates, 0.0) - jnp.log1p(jnp.exp(-jnp.abs(gates)))
    lane = lax.broadcasted_iota(jnp.int32, gates.shape, 1)
    gl = jnp.where(lane < MLSTM_HEADS, gates, log_sig) * LOG2E
    p = _mem_probs(s)
    k_ref[...] = _dot(hn, wk_ref[...]).astype(BF16)
    g_hi = gl.astype(BF16)
    rest = gl - g_hi.astype(F32)
    g_mid = rest.astype(BF16)
    g_lo = (rest - g_mid.astype(F32)).astype(BF16)
    L = MLSTM_CHUNK
    tri = (lax.broadcasted_iota(jnp.int32, (L, L), 1) <= lax.broadcasted_iota(jnp.int32, (L, L), 0)).astype(BF16)
    csum = jnp.concatenate(
        [_dot(tri, g_hi[r:r + L]) + _dot(tri, g_mid[r:r + L]) + _dot(tri, g_lo[r:r + L])
         for r in range(0, gl.shape[0], L)], axis=0)
    v_ref[...] = _dot_nt(wv_ref[...], hn).astype(BF16)
    y = jnp.where(lane < MLSTM_HEADS, gl, csum)
    gt_ref[...] = y
    gtt_ref[...] = y.T[0:2 * MLSTM_HEADS, :]
    mo_ref[...] = _mem_out(p, mv_ref).astype(BF16)
    o_ref[...] = _dot_nt(wo_ref[...], hn).astype(BF16)


def _proj_a(x2d, wq, wk, wv, wo, wmq, wg, bg, mkm, mvm, pos, inv, rows_per_batch):
    N = x2d.shape[0]
    tm = PROJ_A_ROW_TILE
    steps_per_batch = rows_per_batch // tm
    row = lambda w: pl.BlockSpec((tm, w), lambda i: (i, 0))
    col = pl.BlockSpec((MLSTM_V_W, tm), lambda i: (0, i))
    mem_spec = pl.BlockSpec((1, 1, MEM_HEADS, MEM_TOKENS, MEM_W),
                            lambda i: (0, i // steps_per_batch, 0, 0, 0))
    return pl.pallas_call(
        _proj_a_kernel,
        grid=(N // tm,),
        in_specs=[row(D_MODEL),
                  _resident(wq.shape), _resident(wk.shape), _resident(wv.shape), _resident(wo.shape),
                  _resident(wmq.shape), _resident(wg.shape), _resident((1, LANES)),
                  mem_spec, mem_spec,
                  pl.BlockSpec((tm // (LANES // (HEAD_DIM // 2)), LANES), lambda i: (i, 0)), _resident((1, LANES))],
        out_specs=[row(QK_PAD_W), row(QK_PAD_W), col, col, row(LANES),
                   pl.BlockSpec((2 * MLSTM_HEADS, tm), lambda i: (0, i)), row(MEM_W), row(LANES), row(LANES)],
        out_shape=[jax.ShapeDtypeStruct((N, QK_PAD_W), BF16), jax.ShapeDtypeStruct((N, QK_PAD_W), BF16),
                   jax.ShapeDtypeStruct((MLSTM_V_W, N), BF16), jax.ShapeDtypeStruct((MLSTM_V_W, N), BF16),
                   jax.ShapeDtypeStruct((N, LANES), F32), jax.ShapeDtypeStruct((2 * MLSTM_HEADS, N), F32),
                   jax.ShapeDtypeStruct((N, MEM_W), BF16),
                   jax.ShapeDtypeStruct((N, LANES), F32), jax.ShapeDtypeStruct((N, LANES), F32)],
        compiler_params=pltpu.CompilerParams(dimension_semantics=("arbitrary",),
                                             vmem_limit_bytes=VMEM_LIMIT),
        name="proj_a",
    )(x2d, wq, wk, wv, wo, wmq, wg, bg, mkm, mvm, pos, inv)


def _mlstm_kernel(q_ref, k_ref, vt_ref, ot_ref, gt_ref, gtt_ref, hmt_ref, c_scr, m_scr):
    L = MLSTM_CHUNK

    @pl.when(pl.program_id(1) == 0)
    def _():
        c_scr[...] = jnp.zeros(c_scr.shape, F32)
        m_scr[...] = jnp.full(m_scr.shape, M_INIT, F32)

    causal = lax.broadcasted_iota(jnp.int32, (L, L), 0) <= lax.broadcasted_iota(jnp.int32, (L, L), 1)
    feat = lax.broadcasted_iota(jnp.int32, (V_WIN, L), 0)
    heads = range(MLSTM_HEADS)
    m_st = [m_scr[h][0:1, 0:1] for h in heads]
    c_st = [c_scr[h] for h in heads]

    for r0 in range(0, q_ref.shape[0], L):
        y = gt_ref[r0:r0 + L, :]
        yt = gtt_ref[:, r0:r0 + L]
        b_row = [yt[MLSTM_HEADS + h:MLSTM_HEADS + h + 1, :] for h in heads]
        c_row = [b_row[h] - yt[h:h + 1, :] for h in heads]
        c_col = [y[:, MLSTM_HEADS + h:MLSTM_HEADS + h + 1] - y[:, h:h + 1] for h in heads]
        g_tot = [y[L - 1:L, MLSTM_HEADS + h:MLSTM_HEADS + h + 1] for h in heads]
        qh = [q_ref[r0:r0 + L, h * DQK_PAD:(h + 1) * DQK_PAD] for h in heads]
        kh = [k_ref[r0:r0 + L, h * DQK_PAD:(h + 1) * DQK_PAD] for h in heads]
        vt_aug = [jnp.where(feat == _V_ONES_COL[h], jnp.ones((), BF16),
                            vt_ref[_V_WIN_START[h]:_V_WIN_START[h] + V_WIN, r0:r0 + L]) for h in heads]

        qk_t = [_dot_nt(kh[h], qh[h]) for h in heads]
        inter_t = [_dot_nt(c_st[h].astype(BF16), qh[h]) for h in heads]
        dmat = [jnp.where(causal, b_row[h] - c_col[h], -jnp.inf) for h in heads]
        inter_log = [b_row[h] + m_st[h] for h in heads]
        m_row = [jnp.maximum(inter_log[h], jnp.max(dmat[h], axis=0, keepdims=True)) for h in heads]
        scores_t = [(qk_t[h] * jnp.exp2(dmat[h] - m_row[h])).astype(BF16) for h in heads]
        num_t = [_dot(vt_aug[h], scores_t[h]) + jnp.exp2(inter_log[h] - m_row[h]) * inter_t[h] for h in heads]

        a = [g_tot[h] - c_row[h] for h in heads]
        m_new = [jnp.maximum(g_tot[h] + m_st[h], jnp.max(a[h], axis=1, keepdims=True)) for h in heads]
        vtw = [vt_aug[h] * jnp.exp2(a[h] - m_new[h]).astype(BF16) for h in heads]
        c_upd = [_dot(vtw[h], kh[h]) for h in heads]

        for h in heads:
            off, ones_row, lo = _V_WIN_OFF[h], _V_ONES_COL[h], h * MLSTM_DV
            den = num_t[h][ones_row:ones_row + 1, :]
            inv_dd = 1.0 / jnp.maximum(jnp.abs(den), jnp.exp2(-m_row[h]))
            h_t = num_t[h][off:off + MLSTM_DV, :]
            ms = jnp.sum(h_t * h_t, axis=0, keepdims=True) * (1.0 / MLSTM_DV)
            scale = inv_dd * lax.rsqrt(ms * inv_dd * inv_dd + EPS)
            og = _sigmoid(ot_ref[lo:lo + MLSTM_DV, r0:r0 + L].astype(F32))
            hmt_ref[lo:lo + MLSTM_DV, r0:r0 + L] = (h_t * (scale * og)).astype(BF16)

        c_st = [jnp.exp2(g_tot[h] + m_st[h] - m_new[h]) * c_st[h] + c_upd[h] for h in heads]
        m_st = m_new

    for h in heads:
        c_scr[h] = c_st[h]
        m_scr[h] = jnp.broadcast_to(m_st[h], m_scr.shape[1:])


def _mlstm(q, k, vt, ot, gt, gtt, B, S):
    rows = MLSTM_CHUNK * MLSTM_CHUNKS_PER_STEP
    steps = S // rows
    N = B * S
    row = lambda w: pl.BlockSpec((rows, w), lambda b, c: (b * steps + c, 0))
    col = lambda h: pl.BlockSpec((h, rows), lambda b, c: (0, b * steps + c))
    return pl.pallas_call(
        _mlstm_kernel,
        grid=(B, steps),
        in_specs=[row(QK_PAD_W), row(QK_PAD_W), col(MLSTM_V_W), col(MLSTM_V_W), row(LANES),
                  col(2 * MLSTM_HEADS)],
        out_specs=col(MLSTM_V_W),
        out_shape=jax.ShapeDtypeStruct((MLSTM_V_W, N), BF16),
        scratch_shapes=[pltpu.VMEM((MLSTM_HEADS, V_WIN, DQK_PAD), F32),
                        pltpu.VMEM((MLSTM_HEADS, 8, LANES), F32)],
        compiler_params=pltpu.CompilerParams(dimension_semantics=("arbitrary", "arbitrary"),
                                             vmem_limit_bytes=VMEM_LIMIT),
        name="mlstm",
    )(q, k, vt, ot, gt, gtt)


def _mix_ffn_core(x_ref, main_ref, mo_ref, wo_ref, gpost_ref, gfpost_ref,
                  wi_ref, wd_ref, main_feature_major):
    main_dot = _dot_tn if main_feature_major else _dot
    mix_w = D_MODEL - MEM_W
    groups = [slice(r, r + FFN_ROW_GROUP) for r in range(0, x_ref.shape[0], FFN_ROW_GROUP)]
    mix = [main_dot(main_ref[:, g] if main_feature_major else main_ref[g, :], wo_ref[0:mix_w, :])
           + _dot(mo_ref[g, :], wo_ref[mix_w:, :]) for g in groups]
    x1 = [x_ref[g, :] + _rms(m, gpost_ref[...]) for g, m in zip(groups, mix)]
    hf = [_rms_hat(t).astype(BF16) for t in x1]
    act = [[] for _ in groups]
    for c in range(N_FF_CHUNKS):
        lo = c * FF_CHUNK
        for i, h in enumerate(hf):
            g = _dot(h, wi_ref[:, lo:lo + FF_CHUNK])
            u = _dot(h, wi_ref[:, D_FF + lo:D_FF + lo + FF_CHUNK])
            act[i].append((g * _sigmoid(g) * u).astype(BF16))
    acc = [_dot(jnp.concatenate(a, axis=1), wd_ref[...]) for a in act]
    return jnp.concatenate([t + _rms(a, gfpost_ref[...]) for t, a in zip(x1, acc)], axis=0)


def _mix_ffn_kernel(x_ref, main_ref, mo_ref, wo_ref, gpost_ref, gfpost_ref,
                    wi_ref, wd_ref, xo_ref, *, main_feature_major):
    xo_ref[...] = _mix_ffn_core(x_ref, main_ref, mo_ref, wo_ref, gpost_ref,
                                gfpost_ref, wi_ref, wd_ref, main_feature_major)


def _rope(t1, t2, cos, sin):
    return t1 * cos - t2 * sin, t2 * cos + t1 * sin


def _rope_tables(pos_ref, inv_ref):
    ang = pos_ref[...].astype(F32) * inv_ref[...]
    cos_d = jnp.cos(ang)
    sin_d = jnp.sin(ang)
    grp = lax.broadcasted_iota(jnp.int32, ang.shape, 1) >> 5

    def spread(t, g):
        y = jnp.where(grp == g, t, 0.0)
        y = y + pltpu.roll(y, HEAD_DIM // 2, 1)
        return y + pltpu.roll(y, HEAD_DIM, 1)

    n_grp = LANES // (HEAD_DIM // 2)
    cos = jnp.concatenate([spread(cos_d, g) for g in range(n_grp)], axis=0)
    sin = jnp.concatenate([spread(sin_d, g) for g in range(n_grp)], axis=0)
    return cos, sin


def _mix_ffn(layer, x2d, main, mo, wo, g_post, g_fpost, wi, wd, main_feature_major):
    N = x2d.shape[0]
    tm = FFN_ROW_TILE
    row = lambda w: pl.BlockSpec((tm, w), lambda i: (i, 0))
    slab = lambda w: pl.BlockSpec((None,) + w.shape[1:], lambda i: (layer, 0, 0), pipeline_mode=pl.Buffered(1))
    main_spec = pl.BlockSpec((main.shape[0], tm), lambda i: (0, i)) if main_feature_major else row(main.shape[1])
    return pl.pallas_call(
        functools.partial(_mix_ffn_kernel, main_feature_major=main_feature_major),
        grid=(N // tm,),
        in_specs=[row(D_MODEL), main_spec, row(MEM_W),
                  _resident(wo.shape),
                  _resident((1, D_MODEL)), _resident((1, D_MODEL)),
                  slab(wi), slab(wd)],
        out_specs=row(D_MODEL),
        out_shape=jax.ShapeDtypeStruct((N, D_MODEL), F32),
        compiler_params=pltpu.CompilerParams(dimension_semantics=("arbitrary",),
                                             vmem_limit_bytes=VMEM_LIMIT),
        name="mix_ffn_%d" % layer,
    )(x2d, main, mo, wo, g_post, g_fpost, wi, wd)


def _proj_b_kernel(x_ref, cos_ref, sin_ref, wkv_ref, wb_ref, mk_ref, mv_ref,
                   ks_ref, vs_ref, q1_ref, mo1_ref):
    groups = [slice(r, r + PROJ_B_ROW_GROUP) for r in range(0, x_ref.shape[0], PROJ_B_ROW_GROUP)]
    xn = [_rms_hat(x_ref[g, :]).astype(BF16) for g in groups]
    pb = [_dot(t, wb_ref[...]) for t in xn]
    s = [_mem_scores(t[:, SWA_Q_W:].astype(BF16), mk_ref) for t in pb]
    cos, sin = cos_ref[...], sin_ref[...]
    kv = [_dot(t, wkv_ref[...]) for t in xn]
    p = [_mem_probs(t) for t in s]

    for g, t in zip(groups, pb):
        for gi in range(SWA_GROUPS):
            base = gi * 2 * LANES
            t1, t2 = _rope(t[:, base:base + LANES], t[:, base + LANES:base + 2 * LANES], cos[g], sin[g])
            q1_ref[g, base:base + LANES] = t1.astype(BF16)
            q1_ref[g, base + LANES:base + 2 * LANES] = t2.astype(BF16)
    for g, t in zip(groups, p):
        mo1_ref[g, :] = _mem_out(t, mv_ref).astype(BF16)

    scale = HEAD_DIM ** -0.5 * LOG2E
    for g, t in zip(groups, kv):
        k1, k2 = _rope(t[:, 0:LANES], t[:, LANES:2 * LANES], cos[g], sin[g])
        ks_ref[g, 0:LANES] = (k1 * scale).astype(BF16)
        ks_ref[g, LANES:2 * LANES] = (k2 * scale).astype(BF16)
        vs_ref[g, :] = t[:, SWA_KV_W:].astype(BF16)


def _proj_b(x2d, cos, sin, wkv, wb, mkm, mvm, rows_per_batch):
    N = x2d.shape[0]
    tm = PROJ_B_ROW_TILE
    steps_per_batch = rows_per_batch // tm
    row = lambda w: pl.BlockSpec((tm, w), lambda i: (i, 0))
    mem_spec = pl.BlockSpec((1, 1, MEM_HEADS, MEM_TOKENS, MEM_W),
                            lambda i: (1, i // steps_per_batch, 0, 0, 0))
    return pl.pallas_call(
        _proj_b_kernel,
        grid=(N // tm,),
        in_specs=[row(D_MODEL), row(LANES), row(LANES), _resident(wkv.shape), _resident(wb.shape),
                  mem_spec, mem_spec],
        out_specs=[row(SWA_KV_W), row(SWA_KV_W), row(SWA_Q_W), row(MEM_W)],
        out_shape=[jax.ShapeDtypeStruct((N, SWA_KV_W), BF16), jax.ShapeDtypeStruct((N, SWA_KV_W), BF16),
                   jax.ShapeDtypeStruct((N, SWA_Q_W), BF16), jax.ShapeDtypeStruct((N, MEM_W), BF16)],
        compiler_params=pltpu.CompilerParams(dimension_semantics=("arbitrary",),
                                             vmem_limit_bytes=VMEM_LIMIT),
        name="proj_b",
    )(x2d, cos, sin, wkv, wb, mkm, mvm)


def _swa_kernel(sink_ref, q_ref, kp_ref, kc_ref, vp_ref, vc_ref, o_ref):
    W = SWA_WINDOW
    G = SWA_GROUPS
    n = pl.program_id(1)
    n_sub = q_ref.shape[0] // W
    H = SWA_KV_HEADS
    kpos = lax.broadcasted_iota(jnp.int32, (2 * W, G * W), 0)
    qcol = lax.broadcasted_iota(jnp.int32, (2 * W, G * W), 1)
    diff = (qcol & (W - 1)) + W - kpos
    in_band = (diff >= 0) & (diff < W)
    k_lane = lax.broadcasted_iota(jnp.int32, (2 * W, SWA_KV_W), 1)
    sink = sink_ref[...]
    subs = range(n_sub)
    q, km, vm, valid = [], [], [], []
    for sub in subs:
        q.append(jnp.concatenate([q_ref[sub * W:(sub + 1) * W, gi * 2 * LANES:(gi + 1) * 2 * LANES]
                                  for gi in range(G)], axis=0))
        if sub == 0:
            k = jnp.concatenate([kp_ref[...], kc_ref[0:W, :]], axis=0)
            v = jnp.concatenate([vp_ref[...], vc_ref[0:W, :]], axis=0)
            valid.append(in_band & ((kpos >= W) | (n > 0)))
        else:
            k = kc_ref[(sub - 1) * W:(sub + 1) * W, :]
            v = vc_ref[(sub - 1) * W:(sub + 1) * W, :]
            valid.append(in_band)
        km.append(jnp.concatenate([jnp.where(((k_lane & (LANES - 1)) >> 5) == j, k, jnp.zeros((), BF16))
                                   for j in range(H)], axis=0))
        vm.append(jnp.concatenate([jnp.where((k_lane >> 6) == j, v, jnp.zeros((), BF16))
                                   for j in range(H)], axis=0))
    s = [_dot_nt(km[i], q[i]).reshape(H, 2 * W, G * W) for i in subs]
    s = [jnp.where(valid[i][None], s[i], -jnp.inf) for i in subs]
    m = [jnp.maximum(jnp.max(s[i], axis=1, keepdims=True), sink) for i in subs]
    e = [jnp.exp2(s[i] - m[i]) for i in subs]
    inv = [1.0 / (jnp.sum(e[i], axis=1, keepdims=True) + jnp.exp2(sink - m[i])) for i in subs]
    p = [e[i].astype(BF16) * inv[i].astype(BF16) for i in subs]
    out = [_dot_tn(p[i].reshape(H * 2 * W, G * W), vm[i]) for i in subs]
    for sub in subs:
        for gi in range(G):
            o_ref[sub * W:(sub + 1) * W, gi * 2 * LANES:(gi + 1) * 2 * LANES] = (
                out[sub][gi * W:(gi + 1) * W].astype(BF16))


def _swa(sink_cols, q1, ks, vs, B, S):
    W = SWA_WINDOW
    n_sub = SWA_BLOCKS_PER_STEP
    tq = n_sub * W
    steps = S // tq
    cur = lambda w: pl.BlockSpec((tq, w), lambda b, n: (b * steps + n, 0))
    prev = lambda w: pl.BlockSpec((W, w), lambda b, n: (b * (S // W) + jnp.maximum(n_sub * n - 1, 0), 0))
    return pl.pallas_call(
        _swa_kernel,
        grid=(B, steps),
        in_specs=[_resident(sink_cols.shape),
                  cur(SWA_Q_W), prev(SWA_KV_W), cur(SWA_KV_W), prev(SWA_KV_W), cur(SWA_KV_W)],
        out_specs=cur(SWA_Q_W),
        out_shape=jax.ShapeDtypeStruct((B * S, SWA_Q_W), BF16),
        compiler_params=pltpu.CompilerParams(dimension_semantics=("arbitrary", "arbitrary"),
                                             vmem_limit_bytes=VMEM_LIMIT),
        name="swa",
    )(sink_cols, q1, ks, ks, vs, vs)


def _pad_heads(w, heads, width, padded):
    w = w.reshape(w.shape[0], heads, width)
    return jnp.pad(w, ((0, 0), (0, 0), (0, padded - width))).reshape(w.shape[0], heads * padded)


def _with_tail(perm, total):
    return np.concatenate([perm, np.arange(len(perm), total, dtype=np.int32)])


def _swa_q_perm():
    idx = np.empty((SWA_GROUPS, 2, SWA_KV_HEADS, HEAD_DIM // 2), np.int32)
    for gi in range(SWA_GROUPS):
        for half in range(2):
            for j in range(SWA_KV_HEADS):
                idx[gi, half, j] = (j * SWA_GROUPS + gi) * HEAD_DIM + half * (HEAD_DIM // 2) + np.arange(HEAD_DIM // 2)
    return idx.reshape(-1)


def _swa_k_perm():
    idx = np.empty((2, SWA_KV_HEADS, HEAD_DIM // 2), np.int32)
    for half in range(2):
        for j in range(SWA_KV_HEADS):
            idx[half, j] = j * HEAD_DIM + half * (HEAD_DIM // 2) + np.arange(HEAD_DIM // 2)
    return idx.reshape(-1)


def _swa_out_perm():
    idx = np.empty((SWA_GROUPS, SWA_KV_HEADS, HEAD_DIM), np.int32)
    for gi in range(SWA_GROUPS):
        for j in range(SWA_KV_HEADS):
            idx[gi, j] = (j * SWA_GROUPS + gi) * HEAD_DIM + np.arange(HEAD_DIM)
    return idx.reshape(-1)


def kernel(x, mem, positions, g_mix_pre, g_mix_post, g_ffn_pre, g_ffn_post, g_mem, w_mem_kv, w_out,
           w_ffn_in, w_ffn_out, w_in_a, b_gates_a, g_mlstm_out, g_kv, w_kv, w_in_b, sinks_b):
    B, S, _ = x.shape
    N = B * S
    assert S % PROJ_A_ROW_TILE == 0 and S % PROJ_B_ROW_TILE == 0 and S % FFN_ROW_TILE == 0
    assert S % (MLSTM_CHUNK * MLSTM_CHUNKS_PER_STEP) == 0
    assert S % (SWA_WINDOW * SWA_BLOCKS_PER_STEP) == 0
    x2d = x.reshape(N, D_MODEL)
    vec = lambda g: g.reshape(1, -1).astype(F32)

    mkm, mvm = _mem_kv(mem, g_mem, w_mem_kv.astype(BF16))

    fold = lambda g, w: g.astype(F32)[:, None] * w

    piece = lambda c0, width: fold(g_mix_pre[0], w_in_a[0][:, c0:c0 + width])
    c0 = 0
    wq = _pad_heads(piece(c0, MLSTM_QK_W), MLSTM_HEADS, MLSTM_DQK, DQK_PAD).astype(BF16)
    c0 += MLSTM_QK_W
    wk = _pad_heads(piece(c0, MLSTM_QK_W) * (MLSTM_DQK ** -0.5), MLSTM_HEADS, MLSTM_DQK, DQK_PAD).astype(BF16)
    c0 += MLSTM_QK_W
    wv = piece(c0, MLSTM_V_W).astype(BF16).T
    c0 += MLSTM_V_W
    wo = piece(c0, MLSTM_V_W).astype(BF16).T
    c0 += MLSTM_V_W
    wgt = jnp.pad(piece(c0, 2 * MLSTM_HEADS), ((0, 0), (0, LANES - 2 * MLSTM_HEADS))).astype(BF16)
    c0 += 2 * MLSTM_HEADS
    wmq = piece(c0, MEM_W).astype(BF16)
    bg = jnp.pad(b_gates_a[0].astype(F32), (0, LANES - 2 * MLSTM_HEADS)).reshape(1, LANES)

    inv = 1.0 / (ROPE_THETA ** (jnp.arange(0, HEAD_DIM, 2, dtype=F32) / HEAD_DIM))
    inv = jnp.tile(inv, SWA_KV_HEADS).reshape(1, LANES)
    n_grp = LANES // (HEAD_DIM // 2)
    pos_dense = positions.reshape(N // PROJ_A_ROW_TILE, n_grp, PROJ_A_ROW_TILE // n_grp).transpose(0, 2, 1)
    pos_dense = jnp.repeat(pos_dense, HEAD_DIM // 2, axis=2).reshape(N // n_grp, LANES)
    q, k, v, o, gt, gtt, mo0, cos, sin = _proj_a(x2d, wq, wk, wv, wo, wmq, wgt, bg, mkm, mvm, pos_dense, inv, S)
    hm = _mlstm(q, k, v, o, gt, gtt, B, S)

    w_ffn_in_bf = (g_ffn_pre.astype(F32)[:, :, None] * w_ffn_in).astype(BF16)
    w_ffn_out_bf = w_ffn_out.astype(BF16)
    wo0 = fold(jnp.concatenate([g_mlstm_out[0].astype(F32), jnp.ones((MEM_W,), F32)]), w_out[0]).astype(BF16)
    wkv = fold(g_kv, w_kv)[:, _with_tail(_swa_k_perm(), 2 * SWA_KV_W)].astype(BF16)
    wb = fold(g_mix_pre[1], w_in_b[0])[:, _with_tail(_swa_q_perm(), SWA_Q_W + MEM_W)].astype(BF16)
    x1 = _mix_ffn(0, x2d, hm, mo0, wo0, vec(g_mix_post[0]), vec(g_ffn_post[0]),
                  w_ffn_in_bf, w_ffn_out_bf, main_feature_major=True)
    ks, vs, q1, mo1 = _proj_b(x1, cos, sin, wkv, wb, mkm, mvm, S)

    sink_rows = jnp.repeat((sinks_b[0].astype(F32) * LOG2E).reshape(SWA_KV_HEADS, SWA_GROUPS), SWA_WINDOW, axis=1)
    attn = _swa(sink_rows.reshape(SWA_KV_HEADS, 1, SWA_GROUPS * SWA_WINDOW), q1, ks, vs, B, S)
    wo1 = w_out[1][_with_tail(_swa_out_perm(), D_MODEL)].astype(BF16)
    xo = _mix_ffn(1, x1, attn, mo1, wo1, vec(g_mix_post[1]), vec(g_ffn_post[1]),
                  w_ffn_in_bf, w_ffn_out_bf, main_feature_major=False)
    return xo.reshape(B, S, D_MODEL)
```

```python
import numpy as np
import jax
import jax.numpy as jnp
from jax import lax
from jax.experimental import pallas as pl
from jax.experimental.pallas import tpu as pltpu

F32 = jnp.float32
BF16 = jnp.bfloat16

D_MODEL = 1024
DEPTH = 2
HEAD_DIM = 64
EPS = 1e-6
ROPE_THETA = 10000.0
LOG2E = 1.4426950408889634

MLSTM_HEADS = 4
MLSTM_DV = 192
MLSTM_DQK = 96
MLSTM_QK_W = MLSTM_HEADS * MLSTM_DQK
MLSTM_V_W = MLSTM_HEADS * MLSTM_DV
GATE_SOFTCAP = 15.0
M_INIT = -1e30

SWA_Q_HEADS = 12
SWA_KV_HEADS = 4
SWA_GROUPS = SWA_Q_HEADS // SWA_KV_HEADS
SWA_Q_W = SWA_Q_HEADS * HEAD_DIM
SWA_KV_W = SWA_KV_HEADS * HEAD_DIM
SWA_WINDOW = 128

MEM_TOKENS = 256
MEM_HEADS = 4
MEM_HEAD_DIM = 64
MEM_W = MEM_HEADS * MEM_HEAD_DIM

D_FF = 2816

LANES = 128
MXU_TILE = 256

DQK_PAD = LANES
QK_PAD_W = MLSTM_HEADS * DQK_PAD
V_WIN = MXU_TILE
MLSTM_CHUNK = 256
MLSTM_CHUNKS_PER_STEP = 8
FF_CHUNK = MXU_TILE
N_FF_CHUNKS = D_FF // FF_CHUNK
PROJ_A_ROW_TILE = 1024
PROJ_B_ROW_TILE = 2048
PROJ_B_ROW_GROUP = 512
FFN_ROW_TILE = 1024
FFN_ROW_GROUP = 512
SWA_BLOCKS_PER_STEP = 8
VMEM_LIMIT = 56 * 1024 * 1024

_V_WIN_START = (0, 128, 384, 512)
_V_WIN_OFF = (0, 64, 0, 64)
_V_ONES_COL = (192, 0, 192, 0)


def _rms_hat(x):
    return x * lax.rsqrt(jnp.mean(x * x, axis=-1, keepdims=True) + EPS)


def _rms(x, g):
    return _rms_hat(x) * g


def _dot(a, b):
    return jnp.dot(a, b, preferred_element_type=F32)


def _dot_nt(a, b):
    return lax.dot_general(a, b, (((1,), (1,)), ((), ())), preferred_element_type=F32)


def _dot_tn(a, b):
    return lax.dot_general(a, b, (((0,), (0,)), ((), ())), preferred_element_type=F32)


def _sigmoid(x):
    return 1.0 / (1.0 + jnp.exp(-x))


def _resident(shape):
    nd = len(shape)
    return pl.BlockSpec(shape, lambda *_: (0,) * nd, pipeline_mode=pl.Buffered(1))


def _mem_kv_kernel(mem_ref, g_ref, w_ref, mk_ref, mv_ref):
    B = mem_ref.shape[0]
    hn = _rms(mem_ref[...].reshape(B * MEM_TOKENS, D_MODEL), g_ref[0]).astype(BF16)
    kv = _dot(hn, w_ref[0])
    mk = kv[:, :MEM_W] * (MEM_HEAD_DIM ** -0.5 * LOG2E)
    mv = kv[:, MEM_W:]
    lane_head = lax.broadcasted_iota(jnp.int32, mk.shape, 1) >> 6
    for h in range(MEM_HEADS):
        sel = lane_head == h
        mk_ref[0, :, h] = jnp.where(sel, mk, 0.0).astype(BF16).reshape(B, MEM_TOKENS, MEM_W)
        mv_ref[0, :, h] = jnp.where(sel, mv, 0.0).astype(BF16).reshape(B, MEM_TOKENS, MEM_W)


def _mem_kv(mem, g_mem, w_mem_kv_bf16):
    B = mem.shape[0]
    out_sds = jax.ShapeDtypeStruct((DEPTH, B, MEM_HEADS, MEM_TOKENS, MEM_W), BF16)
    out_spec = pl.BlockSpec((1, B, MEM_HEADS, MEM_TOKENS, MEM_W), lambda l: (l, 0, 0, 0, 0))
    return pl.pallas_call(
        _mem_kv_kernel,
        grid=(DEPTH,),
        in_specs=[
            pl.BlockSpec((B, MEM_TOKENS, D_MODEL), lambda l: (0, 0, 0)),
            pl.BlockSpec((1, 1, D_MODEL), lambda l: (l, 0, 0)),
            pl.BlockSpec((1, D_MODEL, 2 * MEM_W), lambda l: (l, 0, 0)),
        ],
        out_specs=[out_spec, out_spec],
        out_shape=[out_sds, out_sds],
        compiler_params=pltpu.CompilerParams(dimension_semantics=("arbitrary",),
                                             vmem_limit_bytes=VMEM_LIMIT),
        name="mem_kv",
    )(mem, g_mem.reshape(DEPTH, 1, D_MODEL), w_mem_kv_bf16)


def _mem_scores(mq, mk_ref):
    return [_dot_nt(mq, mk_ref[0, 0, h]) for h in range(MEM_HEADS)]


def _mem_probs(s):
    e = [jnp.exp2(sh - jnp.max(sh, axis=-1, keepdims=True)) for sh in s]
    return [(eh * (1.0 / jnp.sum(eh, axis=-1, keepdims=True))).astype(BF16) for eh in e]


def _mem_out(p, mv_ref):
    out = _dot(p[0], mv_ref[0, 0, 0])
    for h in range(1, MEM_HEADS):
        out = out + _dot(p[h], mv_ref[0, 0, h])
    return out


def _proj_a_kernel(x_ref, wq_ref, wk_ref, wv_ref, wo_ref, wmq_ref, wg_ref, bg_ref, mk_ref, mv_ref, pos_ref, inv_ref,
                   q_ref, k_ref, v_ref, o_ref, gt_ref, gtt_ref, mo_ref, cos_ref, sin_ref):
    cos_ref[...], sin_ref[...] = _rope_tables(pos_ref, inv_ref)
    hn = _rms_hat(x_ref[...]).astype(BF16)
    mq = _dot(hn, wmq_ref[...]).astype(BF16)
    s = _mem_scores(mq, mk_ref)
    gates = _dot(hn, wg_ref[...]) + bg_ref[...]
    q_ref[...] = _dot(hn, wq_ref[...]).astype(BF16)
    gates = GATE_SOFTCAP * jnp.tanh(gates * (1.0 / GATE_SOFTCAP))
    log_sig = jnp.minimum(gates, 0.0) - jnp.log1p(jnp.exp(-jnp.abs(gates)))
    lane = lax.broadcasted_iota(jnp.int32, gates.shape, 1)
    gl = jnp.where(lane < MLSTM_HEADS, gates, log_sig) * LOG2E
    p = _mem_probs(s)
    k_ref[...] = _dot(hn, wk_ref[...]).astype(BF16)
    g_hi = gl.astype(BF16)
    rest = gl - g_hi.astype(F32)
    g_mid = rest.astype(BF16)
    g_lo = (rest - g_mid.astype(F32)).astype(BF16)
    L = MLSTM_CHUNK
    tri = (lax.broadcasted_iota(jnp.int32, (L, L), 1) <= lax.broadcasted_iota(jnp.int32, (L, L), 0)).astype(BF16)
    csum = jnp.concatenate(
        [_dot(tri, g_hi[r:r + L]) + _dot(tri, g_mid[r:r + L]) + _dot(tri, g_lo[r:r + L])
         for r in range(0, gl.shape[0], L)], axis=0)
    v_ref[...] = _dot_nt(wv_ref[...], hn).astype(BF16)
    y = jnp.where(lane < MLSTM_HEADS, gl, csum)
    gt_ref[...] = y
    gtt_ref[...] = y.T[0:2 * MLSTM_HEADS, :]
    mo_ref[...] = _mem_out(p, mv_ref).astype(BF16)
    o_ref[...] = _sigmoid(_dot_nt(wo_ref[...], hn)).astype(BF16)


def _proj_a(x2d, wq, wk, wv, wo, wmq, wg, bg, mkm, mvm, pos, inv, rows_per_batch):
    N = x2d.shape[0]
    tm = PROJ_A_ROW_TILE
    steps_per_batch = rows_per_batch // tm
    row = lambda w: pl.BlockSpec((tm, w), lambda i: (i, 0))
    col = pl.BlockSpec((MLSTM_V_W, tm), lambda i: (0, i))
    mem_spec = pl.BlockSpec((1, 1, MEM_HEADS, MEM_TOKENS, MEM_W),
                            lambda i: (0, i // steps_per_batch, 0, 0, 0))
    return pl.pallas_call(
        _proj_a_kernel,
        grid=(N // tm,),
        in_specs=[row(D_MODEL),
                  _resident(wq.shape), _resident(wk.shape), _resident(wv.shape), _resident(wo.shape),
                  _resident(wmq.shape), _resident(wg.shape), _resident((1, LANES)),
                  mem_spec, mem_spec,
                  pl.BlockSpec((tm // (LANES // (HEAD_DIM // 2)), LANES), lambda i: (i, 0)), _resident((1, LANES))],
        out_specs=[row(QK_PAD_W), row(QK_PAD_W), col, col, row(LANES),
                   pl.BlockSpec((2 * MLSTM_HEADS, tm), lambda i: (0, i)), row(MEM_W), row(LANES), row(LANES)],
        out_shape=[jax.ShapeDtypeStruct((N, QK_PAD_W), BF16), jax.ShapeDtypeStruct((N, QK_PAD_W), BF16),
                   jax.ShapeDtypeStruct((MLSTM_V_W, N), BF16), jax.ShapeDtypeStruct((MLSTM_V_W, N), BF16),
                   jax.ShapeDtypeStruct((N, LANES), F32), jax.ShapeDtypeStruct((2 * MLSTM_HEADS, N), F32),
                   jax.ShapeDtypeStruct((N, MEM_W), BF16),
                   jax.ShapeDtypeStruct((N, LANES), F32), jax.ShapeDtypeStruct((N, LANES), F32)],
        compiler_params=pltpu.CompilerParams(dimension_semantics=("arbitrary",),
                                             vmem_limit_bytes=VMEM_LIMIT),
        name="proj_a",
    )(x2d, wq, wk, wv, wo, wmq, wg, bg, mkm, mvm, pos, inv)


def _mlstm_kernel(q_ref, k_ref, vt_ref, gt_ref, gtt_ref, hmt_ref, c_scr, m_scr):
    L = MLSTM_CHUNK

    @pl.when(pl.program_id(1) == 0)
    def _():
        c_scr[...] = jnp.zeros(c_scr.shape, F32)
        m_scr[...] = jnp.full(m_scr.shape, M_INIT, F32)

    causal = lax.broadcasted_iota(jnp.int32, (L, L), 0) <= lax.broadcasted_iota(jnp.int32, (L, L), 1)
    feat = lax.broadcasted_iota(jnp.int32, (V_WIN, L), 0)
    heads = range(MLSTM_HEADS)
    m_st = [m_scr[h][0:1, 0:1] for h in heads]
    c_st = [c_scr[h] for h in heads]

    for r0 in range(0, q_ref.shape[0], L):
        y = gt_ref[r0:r0 + L, :]
        yt = gtt_ref[:, r0:r0 + L]
        b_row = [yt[MLSTM_HEADS + h:MLSTM_HEADS + h + 1, :] for h in heads]
        c_row = [b_row[h] - yt[h:h + 1, :] for h in heads]
        c_col = [y[:, MLSTM_HEADS + h:MLSTM_HEADS + h + 1] - y[:, h:h + 1] for h in heads]
        g_tot = [y[L - 1:L, MLSTM_HEADS + h:MLSTM_HEADS + h + 1] for h in heads]
        qh = [q_ref[r0:r0 + L, h * DQK_PAD:(h + 1) * DQK_PAD] for h in heads]
        kh = [k_ref[r0:r0 + L, h * DQK_PAD:(h + 1) * DQK_PAD] for h in heads]
        vt_aug = [jnp.where(feat == _V_ONES_COL[h], jnp.ones((), BF16),
                            vt_ref[_V_WIN_START[h]:_V_WIN_START[h] + V_WIN, r0:r0 + L]) for h in heads]

        qk_t = [_dot_nt(kh[h], qh[h]) for h in heads]
        inter_t = [_dot_nt(c_st[h].astype(BF16), qh[h]) for h in heads]
        dmat = [jnp.where(causal, b_row[h] - c_col[h], -jnp.inf) for h in heads]
        inter_log = [b_row[h] + m_st[h] for h in heads]
        m_row = [jnp.maximum(inter_log[h], jnp.max(dmat[h], axis=0, keepdims=True)) for h in heads]
        scores_t = [(qk_t[h] * jnp.exp2(dmat[h] - m_row[h])).astype(BF16) for h in heads]
        num_t = [_dot(vt_aug[h], scores_t[h]) + jnp.exp2(inter_log[h] - m_row[h]) * inter_t[h] for h in heads]

        a = [g_tot[h] - c_row[h] for h in heads]
        m_new = [jnp.maximum(g_tot[h] + m_st[h], jnp.max(a[h], axis=1, keepdims=True)) for h in heads]
        vtw = [vt_aug[h] * jnp.exp2(a[h] - m_new[h]).astype(BF16) for h in heads]
        c_upd = [_dot(vtw[h], kh[h]) for h in heads]

        for h in heads:
            off, ones_row, lo = _V_WIN_OFF[h], _V_ONES_COL[h], h * MLSTM_DV
            den = num_t[h][ones_row:ones_row + 1, :]
            inv_dd = 1.0 / jnp.maximum(jnp.abs(den), jnp.exp2(-m_row[h]))
            h_t = num_t[h][off:off + MLSTM_DV, :]
            ms = jnp.sum(h_t * h_t, axis=0, keepdims=True) * (1.0 / MLSTM_DV)
            scale = inv_dd * lax.rsqrt(ms * inv_dd * inv_dd + EPS)
            hmt_ref[lo:lo + MLSTM_DV, r0:r0 + L] = (h_t * scale).astype(BF16)

        c_st = [jnp.exp2(g_tot[h] + m_st[h] - m_new[h]) * c_st[h] + c_upd[h] for h in heads]
        m_st = m_new

    for h in heads:
        c_scr[h] = c_st[h]
        m_scr[h] = jnp.broadcast_to(m_st[h], m_scr.shape[1:])


def _mlstm(q, k, vt, gt, gtt, B, S):
    rows = MLSTM_CHUNK * MLSTM_CHUNKS_PER_STEP
    steps = S // rows
    N = B * S
    row = lambda w: pl.BlockSpec((rows, w), lambda b, c: (b * steps + c, 0))
    col = lambda h: pl.BlockSpec((h, rows), lambda b, c: (0, b * steps + c))
    return pl.pallas_call(
        _mlstm_kernel,
        grid=(B, steps),
        in_specs=[row(QK_PAD_W), row(QK_PAD_W), col(MLSTM_V_W), row(LANES), col(2 * MLSTM_HEADS)],
        out_specs=col(MLSTM_V_W),
        out_shape=jax.ShapeDtypeStruct((MLSTM_V_W, N), BF16),
        scratch_shapes=[pltpu.VMEM((MLSTM_HEADS, V_WIN, DQK_PAD), F32),
                        pltpu.VMEM((MLSTM_HEADS, 8, LANES), F32)],
        compiler_params=pltpu.CompilerParams(dimension_semantics=("arbitrary", "arbitrary"),
                                             vmem_limit_bytes=VMEM_LIMIT),
        name="mlstm",
    )(q, k, vt, gt, gtt)


def _mix_ffn_core(x_ref, main_ref, gate_ref, mo_ref, wo_ref, gpost_ref, gfpost_ref, wi_ref, wd_ref):
    mix_w = D_MODEL - MEM_W
    groups = [slice(r, r + FFN_ROW_GROUP) for r in range(0, x_ref.shape[0], FFN_ROW_GROUP)]
    if gate_ref is None:
        main = [_dot(main_ref[g, :], wo_ref[0:mix_w, :]) for g in groups]
    else:
        main = [_dot_tn(main_ref[:, g] * gate_ref[:, g], wo_ref[0:mix_w, :]) for g in groups]
    mix = [m + _dot(mo_ref[g, :], wo_ref[mix_w:, :]) for g, m in zip(groups, main)]
    x1 = [x_ref[g, :] + _rms(m, gpost_ref[...]) for g, m in zip(groups, mix)]
    hf = [_rms_hat(t).astype(BF16) for t in x1]
    act = [[] for _ in groups]
    for c in range(N_FF_CHUNKS):
        lo = c * FF_CHUNK
        for i, h in enumerate(hf):
            g = _dot(h, wi_ref[:, lo:lo + FF_CHUNK])
            u = _dot(h, wi_ref[:, D_FF + lo:D_FF + lo + FF_CHUNK])
            act[i].append((g * _sigmoid(g) * u).astype(BF16))
    acc = [_dot(jnp.concatenate(a, axis=1), wd_ref[...]) for a in act]
    return jnp.concatenate([t + _rms(a, gfpost_ref[...]) for t, a in zip(x1, acc)], axis=0)


def _mix_ffn_kernel(x_ref, main_ref, mo_ref, wo_ref, gpost_ref, gfpost_ref, wi_ref, wd_ref, xo_ref):
    xo_ref[...] = _mix_ffn_core(x_ref, main_ref, None, mo_ref, wo_ref, gpost_ref, gfpost_ref, wi_ref, wd_ref)


def _mix_ffn_gated_kernel(x_ref, main_ref, gate_ref, mo_ref, wo_ref, gpost_ref, gfpost_ref, wi_ref, wd_ref, xo_ref):
    xo_ref[...] = _mix_ffn_core(x_ref, main_ref, gate_ref, mo_ref, wo_ref, gpost_ref, gfpost_ref, wi_ref, wd_ref)


def _rope(t1, t2, cos, sin):
    return t1 * cos - t2 * sin, t2 * cos + t1 * sin


def _rope_tables(pos_ref, inv_ref):
    ang = pos_ref[...].astype(F32) * inv_ref[...]
    cos_d = jnp.cos(ang)
    sin_d = jnp.sin(ang)
    grp = lax.broadcasted_iota(jnp.int32, ang.shape, 1) >> 5

    def spread(t, g):
        y = jnp.where(grp == g, t, 0.0)
        y = y + pltpu.roll(y, HEAD_DIM // 2, 1)
        return y + pltpu.roll(y, HEAD_DIM, 1)

    n_grp = LANES // (HEAD_DIM // 2)
    cos = jnp.concatenate([spread(cos_d, g) for g in range(n_grp)], axis=0)
    sin = jnp.concatenate([spread(sin_d, g) for g in range(n_grp)], axis=0)
    return cos, sin


def _mix_ffn(layer, x2d, main, mo, wo, g_post, g_fpost, wi, wd, gate=None):
    N = x2d.shape[0]
    tm = FFN_ROW_TILE
    row = lambda w: pl.BlockSpec((tm, w), lambda i: (i, 0))
    slab = lambda w: pl.BlockSpec((None,) + w.shape[1:], lambda i: (layer, 0, 0), pipeline_mode=pl.Buffered(1))
    if gate is None:
        body, main_specs, mains = _mix_ffn_kernel, [row(main.shape[1])], [main]
    else:
        col = pl.BlockSpec((main.shape[0], tm), lambda i: (0, i))
        body, main_specs, mains = _mix_ffn_gated_kernel, [col, col], [main, gate]
    return pl.pallas_call(
        body,
        grid=(N // tm,),
        in_specs=[row(D_MODEL)] + main_specs + [row(MEM_W),
                  _resident(wo.shape),
                  _resident((1, D_MODEL)), _resident((1, D_MODEL)),
                  slab(wi), slab(wd)],
        out_specs=row(D_MODEL),
        out_shape=jax.ShapeDtypeStruct((N, D_MODEL), F32),
        compiler_params=pltpu.CompilerParams(dimension_semantics=("arbitrary",),
                                             vmem_limit_bytes=VMEM_LIMIT),
        name="mix_ffn_%d" % layer,
    )(x2d, *mains, mo, wo, g_post, g_fpost, wi, wd)


def _proj_b_kernel(x_ref, cos_ref, sin_ref, wkv_ref, wb_ref, mk_ref, mv_ref,
                   ks_ref, vs_ref, q1_ref, mo1_ref):
    groups = [slice(r, r + PROJ_B_ROW_GROUP) for r in range(0, x_ref.shape[0], PROJ_B_ROW_GROUP)]
    xn = [_rms_hat(x_ref[g, :]).astype(BF16) for g in groups]
    pb = [_dot(t, wb_ref[...]) for t in xn]
    s = [_mem_scores(t[:, SWA_Q_W:].astype(BF16), mk_ref) for t in pb]
    cos, sin = cos_ref[...], sin_ref[...]
    kv = [_dot(t, wkv_ref[...]) for t in xn]
    p = [_mem_probs(t) for t in s]

    for g, t in zip(groups, pb):
        for gi in range(SWA_GROUPS):
            base = gi * 2 * LANES
            t1, t2 = _rope(t[:, base:base + LANES], t[:, base + LANES:base + 2 * LANES], cos[g], sin[g])
            q1_ref[g, base:base + LANES] = t1.astype(BF16)
            q1_ref[g, base + LANES:base + 2 * LANES] = t2.astype(BF16)
    for g, t in zip(groups, p):
        mo1_ref[g, :] = _mem_out(t, mv_ref).astype(BF16)

    scale = HEAD_DIM ** -0.5 * LOG2E
    for g, t in zip(groups, kv):
        k1, k2 = _rope(t[:, 0:LANES], t[:, LANES:2 * LANES], cos[g], sin[g])
        ks_ref[g, 0:LANES] = (k1 * scale).astype(BF16)
        ks_ref[g, LANES:2 * LANES] = (k2 * scale).astype(BF16)
        vs_ref[g, :] = t[:, SWA_KV_W:].astype(BF16)


def _proj_b(x2d, cos, sin, wkv, wb, mkm, mvm, rows_per_batch):
    N = x2d.shape[0]
    tm = PROJ_B_ROW_TILE
    steps_per_batch = rows_per_batch // tm
    row = lambda w: pl.BlockSpec((tm, w), lambda i: (i, 0))
    mem_spec = pl.BlockSpec((1, 1, MEM_HEADS, MEM_TOKENS, MEM_W),
                            lambda i: (1, i // steps_per_batch, 0, 0, 0))
    return pl.pallas_call(
        _proj_b_kernel,
        grid=(N // tm,),
        in_specs=[row(D_MODEL), row(LANES), row(LANES), _resident(wkv.shape), _resident(wb.shape),
                  mem_spec, mem_spec],
        out_specs=[row(SWA_KV_W), row(SWA_KV_W), row(SWA_Q_W), row(MEM_W)],
        out_shape=[jax.ShapeDtypeStruct((N, SWA_KV_W), BF16), jax.ShapeDtypeStruct((N, SWA_KV_W), BF16),
                   jax.ShapeDtypeStruct((N, SWA_Q_W), BF16), jax.ShapeDtypeStruct((N, MEM_W), BF16)],
        compiler_params=pltpu.CompilerParams(dimension_semantics=("arbitrary",),
                                             vmem_limit_bytes=VMEM_LIMIT),
        name="proj_b",
    )(x2d, cos, sin, wkv, wb, mkm, mvm)


def _swa_kernel(sink_ref, q_ref, kp_ref, kc_ref, vp_ref, vc_ref, o_ref):
    W = SWA_WINDOW
    G = SWA_GROUPS
    n = pl.program_id(1)
    n_sub = q_ref.shape[0] // W
    H = SWA_KV_HEADS
    kpos = lax.broadcasted_iota(jnp.int32, (2 * W, G * W), 0)
    qcol = lax.broadcasted_iota(jnp.int32, (2 * W, G * W), 1)
    diff = (qcol & (W - 1)) + W - kpos
    in_band = (diff >= 0) & (diff < W)
    k_lane = lax.broadcasted_iota(jnp.int32, (2 * W, SWA_KV_W), 1)
    sink = sink_ref[...]
    subs = range(n_sub)
    q, km, vm, valid = [], [], [], []
    for sub in subs:
        q.append(jnp.concatenate([q_ref[sub * W:(sub + 1) * W, gi * 2 * LANES:(gi + 1) * 2 * LANES]
                                  for gi in range(G)], axis=0))
        if sub == 0:
            k = jnp.concatenate([kp_ref[...], kc_ref[0:W, :]], axis=0)
            v = jnp.concatenate([vp_ref[...], vc_ref[0:W, :]], axis=0)
            valid.append(in_band & ((kpos >= W) | (n > 0)))
        else:
            k = kc_ref[(sub - 1) * W:(sub + 1) * W, :]
            v = vc_ref[(sub - 1) * W:(sub + 1) * W, :]
            valid.append(in_band)
        km.append(jnp.concatenate([jnp.where(((k_lane & (LANES - 1)) >> 5) == j, k, jnp.zeros((), BF16))
                                   for j in range(H)], axis=0))
        vm.append(jnp.concatenate([jnp.where((k_lane >> 6) == j, v, jnp.zeros((), BF16))
                                   for j in range(H)], axis=0))
    s = [_dot_nt(km[i], q[i]).reshape(H, 2 * W, G * W) for i in subs]
    s = [jnp.where(valid[i][None], s[i], -jnp.inf) for i in subs]
    m = [jnp.maximum(jnp.max(s[i], axis=1, keepdims=True), sink) for i in subs]
    e = [jnp.exp2(s[i] - m[i]) for i in subs]
    inv = [1.0 / (jnp.sum(e[i], axis=1, keepdims=True) + jnp.exp2(sink - m[i])) for i in subs]
    p = [e[i].astype(BF16) * inv[i].astype(BF16) for i in subs]
    out = [_dot_tn(p[i].reshape(H * 2 * W, G * W), vm[i]) for i in subs]
    for sub in subs:
        for gi in range(G):
            o_ref[sub * W:(sub + 1) * W, gi * 2 * LANES:(gi + 1) * 2 * LANES] = (
                out[sub][gi * W:(gi + 1) * W].astype(BF16))


def _swa(sink_cols, q1, ks, vs, B, S):
    W = SWA_WINDOW
    n_sub = SWA_BLOCKS_PER_STEP
    tq = n_sub * W
    steps = S // tq
    cur = lambda w: pl.BlockSpec((tq, w), lambda b, n: (b * steps + n, 0))
    prev = lambda w: pl.BlockSpec((W, w), lambda b, n: (b * (S // W) + jnp.maximum(n_sub * n - 1, 0), 0))
    return pl.pallas_call(
        _swa_kernel,
        grid=(B, steps),
        in_specs=[_resident(sink_cols.shape),
                  cur(SWA_Q_W), prev(SWA_KV_W), cur(SWA_KV_W), prev(SWA_KV_W), cur(SWA_KV_W)],
        out_specs=cur(SWA_Q_W),
        out_shape=jax.ShapeDtypeStruct((B * S, SWA_Q_W), BF16),
        compiler_params=pltpu.CompilerParams(dimension_semantics=("arbitrary", "arbitrary"),
                                             vmem_limit_bytes=VMEM_LIMIT),
        name="swa",
    )(sink_cols, q1, ks, ks, vs, vs)


def _pad_heads(w, heads, width, padded):
    w = w.reshape(w.shape[0], heads, width)
    return jnp.pad(w, ((0, 0), (0, 0), (0, padded - width))).reshape(w.shape[0], heads * padded)


def _with_tail(perm, total):
    return np.concatenate([perm, np.arange(len(perm), total, dtype=np.int32)])


def _swa_q_perm():
    idx = np.empty((SWA_GROUPS, 2, SWA_KV_HEADS, HEAD_DIM // 2), np.int32)
    for gi in range(SWA_GROUPS):
        for half in range(2):
            for j in range(SWA_KV_HEADS):
                idx[gi, half, j] = (j * SWA_GROUPS + gi) * HEAD_DIM + half * (HEAD_DIM // 2) + np.arange(HEAD_DIM // 2)
    return idx.reshape(-1)


def _swa_k_perm():
    idx = np.empty((2, SWA_KV_HEADS, HEAD_DIM // 2), np.int32)
    for half in range(2):
        for j in range(SWA_KV_HEADS):
            idx[half, j] = j * HEAD_DIM + half * (HEAD_DIM // 2) + np.arange(HEAD_DIM // 2)
    return idx.reshape(-1)


def _swa_out_perm():
    idx = np.empty((SWA_GROUPS, SWA_KV_HEADS, HEAD_DIM), np.int32)
    for gi in range(SWA_GROUPS):
        for j in range(SWA_KV_HEADS):
            idx[gi, j] = (j * SWA_GROUPS + gi) * HEAD_DIM + np.arange(HEAD_DIM)
    return idx.reshape(-1)


def kernel(x, mem, positions, g_mix_pre, g_mix_post, g_ffn_pre, g_ffn_post, g_mem, w_mem_kv, w_out,
           w_ffn_in, w_ffn_out, w_in_a, b_gates_a, g_mlstm_out, g_kv, w_kv, w_in_b, sinks_b):
    B, S, _ = x.shape
    N = B * S
    assert S % PROJ_A_ROW_TILE == 0 and S % PROJ_B_ROW_TILE == 0 and S % FFN_ROW_TILE == 0
    assert S % (MLSTM_CHUNK * MLSTM_CHUNKS_PER_STEP) == 0
    assert S % (SWA_WINDOW * SWA_BLOCKS_PER_STEP) == 0
    x2d = x.reshape(N, D_MODEL)
    vec = lambda g: g.reshape(1, -1).astype(F32)

    mkm, mvm = _mem_kv(mem, g_mem, w_mem_kv.astype(BF16))

    fold = lambda g, w: g.astype(F32)[:, None] * w

    piece = lambda c0, width: fold(g_mix_pre[0], w_in_a[0][:, c0:c0 + width])
    c0 = 0
    wq = _pad_heads(piece(c0, MLSTM_QK_W), MLSTM_HEADS, MLSTM_DQK, DQK_PAD).astype(BF16)
    c0 += MLSTM_QK_W
    wk = _pad_heads(piece(c0, MLSTM_QK_W) * (MLSTM_DQK ** -0.5), MLSTM_HEADS, MLSTM_DQK, DQK_PAD).astype(BF16)
    c0 += MLSTM_QK_W
    wv = piece(c0, MLSTM_V_W).astype(BF16).T
    c0 += MLSTM_V_W
    wo = piece(c0, MLSTM_V_W).astype(BF16).T
    c0 += MLSTM_V_W
    wgt = jnp.pad(piece(c0, 2 * MLSTM_HEADS), ((0, 0), (0, LANES - 2 * MLSTM_HEADS))).astype(BF16)
    c0 += 2 * MLSTM_HEADS
    wmq = piece(c0, MEM_W).astype(BF16)
    bg = jnp.pad(b_gates_a[0].astype(F32), (0, LANES - 2 * MLSTM_HEADS)).reshape(1, LANES)

    inv = 1.0 / (ROPE_THETA ** (jnp.arange(0, HEAD_DIM, 2, dtype=F32) / HEAD_DIM))
    inv = jnp.tile(inv, SWA_KV_HEADS).reshape(1, LANES)
    n_grp = LANES // (HEAD_DIM // 2)
    pos_dense = positions.reshape(N // PROJ_A_ROW_TILE, n_grp, PROJ_A_ROW_TILE // n_grp).transpose(0, 2, 1)
    pos_dense = jnp.repeat(pos_dense, HEAD_DIM // 2, axis=2).reshape(N // n_grp, LANES)
    q, k, v, o, gt, gtt, mo0, cos, sin = _proj_a(x2d, wq, wk, wv, wo, wmq, wgt, bg, mkm, mvm, pos_dense, inv, S)
    hm = _mlstm(q, k, v, gt, gtt, B, S)

    w_ffn_in_bf = (g_ffn_pre.astype(F32)[:, :, None] * w_ffn_in).astype(BF16)
    w_ffn_out_bf = w_ffn_out.astype(BF16)
    wo0 = fold(jnp.concatenate([g_mlstm_out[0].astype(F32), jnp.ones((MEM_W,), F32)]), w_out[0]).astype(BF16)
    wkv = fold(g_kv, w_kv)[:, _with_tail(_swa_k_perm(), 2 * SWA_KV_W)].astype(BF16)
    wb = fold(g_mix_pre[1], w_in_b[0])[:, _with_tail(_swa_q_perm(), SWA_Q_W + MEM_W)].astype(BF16)
    x1 = _mix_ffn(0, x2d, hm, mo0, wo0, vec(g_mix_post[0]), vec(g_ffn_post[0]),
                  w_ffn_in_bf, w_ffn_out_bf, gate=o)
    ks, vs, q1, mo1 = _proj_b(x1, cos, sin, wkv, wb, mkm, mvm, S)

    sink_rows = jnp.repeat((sinks_b[0].astype(F32) * LOG2E).reshape(SWA_KV_HEADS, SWA_GROUPS), SWA_WINDOW, axis=1)
    attn = _swa(sink_rows.reshape(SWA_KV_HEADS, 1, SWA_GROUPS * SWA_WINDOW), q1, ks, vs, B, S)
    wo1 = w_out[1][_with_tail(_swa_out_perm(), D_MODEL)].astype(BF16)
    xo = _mix_ffn(1, x1, attn, mo1, wo1, vec(g_mix_post[1]), vec(g_ffn_post[1]),
                  w_ffn_in_bf, w_ffn_out_bf)
    return xo.reshape(B, S, D_MODEL)
```

```python
import numpy as np
import jax
import jax.numpy as jnp
from jax import lax
from jax.experimental import pallas as pl
from jax.experimental.pallas import tpu as pltpu

F32 = jnp.float32
BF16 = jnp.bfloat16

D_MODEL = 1024
DEPTH = 2
HEAD_DIM = 64
EPS = 1e-6
ROPE_THETA = 10000.0
LOG2E = 1.4426950408889634

MLSTM_HEADS = 4
MLSTM_DV = 192
MLSTM_DQK = 96
MLSTM_QK_W = MLSTM_HEADS * MLSTM_DQK
MLSTM_V_W = MLSTM_HEADS * MLSTM_DV
GATE_SOFTCAP = 15.0
M_INIT = -1e30

SWA_Q_HEADS = 12
SWA_KV_HEADS = 4
SWA_GROUPS = SWA_Q_HEADS // SWA_KV_HEADS
SWA_Q_W = SWA_Q_HEADS * HEAD_DIM
SWA_KV_W = SWA_KV_HEADS * HEAD_DIM
SWA_WINDOW = 128

MEM_TOKENS = 256
MEM_HEADS = 4
MEM_HEAD_DIM = 64
MEM_W = MEM_HEADS * MEM_HEAD_DIM

D_FF = 2816

LANES = 128
MXU_TILE = 256

DQK_PAD = LANES
QK_PAD_W = MLSTM_HEADS * DQK_PAD
V_WIN = MXU_TILE
MLSTM_CHUNK = 256
MLSTM_CHUNKS_PER_STEP = 8
FF_CHUNK = MXU_TILE
N_FF_CHUNKS = D_FF // FF_CHUNK
PROJ_A_ROW_TILE = 1024
PROJ_B_ROW_TILE = 2048
PROJ_B_ROW_GROUP = 512
FFN_ROW_TILE = 1024
FFN_ROW_GROUP = 512
SWA_BLOCKS_PER_STEP = 8
VMEM_LIMIT = 56 * 1024 * 1024

_V_WIN_START = (0, 128, 384, 512)
_V_WIN_OFF = (0, 64, 0, 64)
_V_ONES_COL = (192, 0, 192, 0)


def _rms_hat(x):
    return x * lax.rsqrt(jnp.mean(x * x, axis=-1, keepdims=True) + EPS)


def _rms(x, g):
    return _rms_hat(x) * g


def _dot(a, b):
    return jnp.dot(a, b, preferred_element_type=F32)


def _dot_nt(a, b):
    return lax.dot_general(a, b, (((1,), (1,)), ((), ())), preferred_element_type=F32)


def _dot_tn(a, b):
    return lax.dot_general(a, b, (((0,), (0,)), ((), ())), preferred_element_type=F32)


def _sigmoid(x):
    return 1.0 / (1.0 + jnp.exp(-x))


def _resident(shape):
    nd = len(shape)
    return pl.BlockSpec(shape, lambda *_: (0,) * nd, pipeline_mode=pl.Buffered(1))


def _mem_kv_kernel(mem_ref, g_ref, w_ref, mk_ref, mv_ref):
    B = mem_ref.shape[0]
    hn = _rms(mem_ref[...].reshape(B * MEM_TOKENS, D_MODEL), g_ref[0]).astype(BF16)
    kv = _dot(hn, w_ref[0])
    mk = kv[:, :MEM_W] * (MEM_HEAD_DIM ** -0.5 * LOG2E)
    mv = kv[:, MEM_W:]
    lane_head = lax.broadcasted_iota(jnp.int32, mk.shape, 1) >> 6
    for h in range(MEM_HEADS):
        sel = lane_head == h
        mk_ref[0, :, h] = jnp.where(sel, mk, 0.0).astype(BF16).reshape(B, MEM_TOKENS, MEM_W)
        mv_ref[0, :, h] = jnp.where(sel, mv, 0.0).astype(BF16).reshape(B, MEM_TOKENS, MEM_W)


def _mem_kv(mem, g_mem, w_mem_kv_bf16):
    B = mem.shape[0]
    out_sds = jax.ShapeDtypeStruct((DEPTH, B, MEM_HEADS, MEM_TOKENS, MEM_W), BF16)
    out_spec = pl.BlockSpec((1, B, MEM_HEADS, MEM_TOKENS, MEM_W), lambda l: (l, 0, 0, 0, 0))
    return pl.pallas_call(
        _mem_kv_kernel,
        grid=(DEPTH,),
        in_specs=[
            pl.BlockSpec((B, MEM_TOKENS, D_MODEL), lambda l: (0, 0, 0)),
            pl.BlockSpec((1, 1, D_MODEL), lambda l: (l, 0, 0)),
            pl.BlockSpec((1, D_MODEL, 2 * MEM_W), lambda l: (l, 0, 0)),
        ],
        out_specs=[out_spec, out_spec],
        out_shape=[out_sds, out_sds],
        compiler_params=pltpu.CompilerParams(dimension_semantics=("arbitrary",),
                                             vmem_limit_bytes=VMEM_LIMIT),
        name="mem_kv",
    )(mem, g_mem.reshape(DEPTH, 1, D_MODEL), w_mem_kv_bf16)


def _mem_scores(mq, mk_ref):
    return [_dot_nt(mq, mk_ref[0, 0, h]) for h in range(MEM_HEADS)]


def _mem_probs(s):
    e = [jnp.exp2(sh - jnp.max(sh, axis=-1, keepdims=True)) for sh in s]
    return [(eh * (1.0 / jnp.sum(eh, axis=-1, keepdims=True))).astype(BF16) for eh in e]


def _mem_out(p, mv_ref):
    out = _dot(p[0], mv_ref[0, 0, 0])
    for h in range(1, MEM_HEADS):
        out = out + _dot(p[h], mv_ref[0, 0, h])
    return out


def _proj_a_kernel(x_ref, wq_ref, wk_ref, wv_ref, wo_ref, wmq_ref, wg_ref, bg_ref, mk_ref, mv_ref, pos_ref, inv_ref,
                   q_ref, k_ref, v_ref, o_ref, gt_ref, gtt_ref, mo_ref, cos_ref, sin_ref):
    cos_ref[...], sin_ref[...] = _rope_tables(pos_ref, inv_ref)
    hn = _rms_hat(x_ref[...]).astype(BF16)
    mq = _dot(hn, wmq_ref[...]).astype(BF16)
    s = _mem_scores(mq, mk_ref)
    gates = _dot(hn, wg_ref[...]) + bg_ref[...]
    q_ref[...] = _dot(hn, wq_ref[...]).astype(BF16)
    gates = GATE_SOFTCAP * jnp.tanh(gates * (1.0 / GATE_SOFTCAP))
    log_sig = jnp.minimum(gates, 0.0) - jnp.log1p(jnp.exp(-jnp.abs(gates)))
    lane = lax.broadcasted_iota(jnp.int32, gates.shape, 1)
    gl = jnp.where(lane < MLSTM_HEADS, gates, log_sig) * LOG2E
    p = _mem_probs(s)
    k_ref[...] = _dot(hn, wk_ref[...]).astype(BF16)
    g_hi = gl.astype(BF16)
    rest = gl - g_hi.astype(F32)
    g_mid = rest.astype(BF16)
    g_lo = (rest - g_mid.astype(F32)).astype(BF16)
    L = MLSTM_CHUNK
    tri = (lax.broadcasted_iota(jnp.int32, (L, L), 1) <= lax.broadcasted_iota(jnp.int32, (L, L), 0)).astype(BF16)
    csum = jnp.concatenate(
        [_dot(tri, g_hi[r:r + L]) + _dot(tri, g_mid[r:r + L]) + _dot(tri, g_lo[r:r + L])
         for r in range(0, gl.shape[0], L)], axis=0)
    v_ref[...] = _dot_nt(wv_ref[...], hn).astype(BF16)
    y = jnp.where(lane < MLSTM_HEADS, gl, csum)
    gt_ref[...] = y
    gtt_ref[...] = y.T[0:2 * MLSTM_HEADS, :]
    mo_ref[...] = _mem_out(p, mv_ref).astype(BF16)
    o_ref[...] = _sigmoid(_dot_nt(wo_ref[...], hn)).astype(BF16)


def _proj_a(x2d, wq, wk, wv, wo, wmq, wg, bg, mkm, mvm, pos, inv, rows_per_batch):
    N = x2d.shape[0]
    tm = PROJ_A_ROW_TILE
    steps_per_batch = rows_per_batch // tm
    row = lambda w: pl.BlockSpec((tm, w), lambda i: (i, 0))
    col = pl.BlockSpec((MLSTM_V_W, tm), lambda i: (0, i))
    mem_spec = pl.BlockSpec((1, 1, MEM_HEADS, MEM_TOKENS, MEM_W),
                            lambda i: (0, i // steps_per_batch, 0, 0, 0))
    return pl.pallas_call(
        _proj_a_kernel,
        grid=(N // tm,),
        in_specs=[row(D_MODEL),
                  _resident(wq.shape), _resident(wk.shape), _resident(wv.shape), _resident(wo.shape),
                  _resident(wmq.shape), _resident(wg.shape), _resident((1, LANES)),
                  mem_spec, mem_spec,
                  pl.BlockSpec((tm // (LANES // (HEAD_DIM // 2)), LANES), lambda i: (i, 0)), _resident((1, LANES))],
        out_specs=[row(QK_PAD_W), row(QK_PAD_W), col, col, row(LANES),
                   pl.BlockSpec((2 * MLSTM_HEADS, tm), lambda i: (0, i)), row(MEM_W), row(LANES), row(LANES)],
        out_shape=[jax.ShapeDtypeStruct((N, QK_PAD_W), BF16), jax.ShapeDtypeStruct((N, QK_PAD_W), BF16),
                   jax.ShapeDtypeStruct((MLSTM_V_W, N), BF16), jax.ShapeDtypeStruct((MLSTM_V_W, N), BF16),
                   jax.ShapeDtypeStruct((N, LANES), F32), jax.ShapeDtypeStruct((2 * MLSTM_HEADS, N), F32),
                   jax.ShapeDtypeStruct((N, MEM_W), BF16),
                   jax.ShapeDtypeStruct((N, LANES), F32), jax.ShapeDtypeStruct((N, LANES), F32)],
        compiler_params=pltpu.CompilerParams(dimension_semantics=("arbitrary",),
                                             vmem_limit_bytes=VMEM_LIMIT),
        name="proj_a",
    )(x2d, wq, wk, wv, wo, wmq, wg, bg, mkm, mvm, pos, inv)


def _mlstm_kernel(q_ref, k_ref, vt_ref, gt_ref, gtt_ref, hmt_ref, c_scr, m_scr):
    L = MLSTM_CHUNK

    @pl.when(pl.program_id(1) == 0)
    def _():
        c_scr[...] = jnp.zeros(c_scr.shape, F32)
        m_scr[...] = jnp.full(m_scr.shape, M_INIT, F32)

    causal = lax.broadcasted_iota(jnp.int32, (L, L), 0) <= lax.broadcasted_iota(jnp.int32, (L, L), 1)
    feat = lax.broadcasted_iota(jnp.int32, (V_WIN, L), 0)
    heads = range(MLSTM_HEADS)
    m_st = [m_scr[h][0:1, 0:1] for h in heads]
    c_st = [c_scr[h] for h in heads]

    for r0 in range(0, q_ref.shape[0], L):
        y = gt_ref[r0:r0 + L, :]
        yt = gtt_ref[:, r0:r0 + L]
        b_row = [yt[MLSTM_HEADS + h:MLSTM_HEADS + h + 1, :] for h in heads]
        c_row = [b_row[h] - yt[h:h + 1, :] for h in heads]
        c_col = [y[:, MLSTM_HEADS + h:MLSTM_HEADS + h + 1] - y[:, h:h + 1] for h in heads]
        g_tot = [y[L - 1:L, MLSTM_HEADS + h:MLSTM_HEADS + h + 1] for h in heads]
        qh = [q_ref[r0:r0 + L, h * DQK_PAD:(h + 1) * DQK_PAD] for h in heads]
        kh = [k_ref[r0:r0 + L, h * DQK_PAD:(h + 1) * DQK_PAD] for h in heads]
        vt_aug = [jnp.where(feat == _V_ONES_COL[h], jnp.ones((), BF16),
                            vt_ref[_V_WIN_START[h]:_V_WIN_START[h] + V_WIN, r0:r0 + L]) for h in heads]

        both = [_dot_nt(jnp.concatenate([kh[h], c_st[h].astype(BF16)], axis=0), qh[h]) for h in heads]
        qk_t = [t[:L] for t in both]
        inter_t = [t[L:] for t in both]
        dmat = [jnp.where(causal, b_row[h] - c_col[h], -jnp.inf) for h in heads]
        inter_log = [b_row[h] + m_st[h] for h in heads]
        m_row = [jnp.maximum(inter_log[h], jnp.max(dmat[h], axis=0, keepdims=True)) for h in heads]
        scores_t = [(qk_t[h] * jnp.exp2(dmat[h] - m_row[h])).astype(BF16) for h in heads]
        num_t = [_dot(vt_aug[h], scores_t[h]) + jnp.exp2(inter_log[h] - m_row[h]) * inter_t[h] for h in heads]

        a = [g_tot[h] - c_row[h] for h in heads]
        m_new = [jnp.maximum(g_tot[h] + m_st[h], jnp.max(a[h], axis=1, keepdims=True)) for h in heads]
        vtw = [vt_aug[h] * jnp.exp2(a[h] - m_new[h]).astype(BF16) for h in heads]
        c_upd = [_dot(vtw[h], kh[h]) for h in heads]

        for h in heads:
            off, ones_row, lo = _V_WIN_OFF[h], _V_ONES_COL[h], h * MLSTM_DV
            den = num_t[h][ones_row:ones_row + 1, :]
            inv_dd = 1.0 / jnp.maximum(jnp.abs(den), jnp.exp2(-m_row[h]))
            h_t = num_t[h][off:off + MLSTM_DV, :]
            ms = jnp.sum(h_t * h_t, axis=0, keepdims=True) * (1.0 / MLSTM_DV)
            scale = inv_dd * lax.rsqrt(ms * inv_dd * inv_dd + EPS)
            hmt_ref[lo:lo + MLSTM_DV, r0:r0 + L] = (h_t * scale).astype(BF16)

        c_st = [jnp.exp2(g_tot[h] + m_st[h] - m_new[h]) * c_st[h] + c_upd[h] for h in heads]
        m_st = m_new

    for h in heads:
        c_scr[h] = c_st[h]
        m_scr[h] = jnp.broadcast_to(m_st[h], m_scr.shape[1:])


def _mlstm(q, k, vt, gt, gtt, B, S):
    rows = MLSTM_CHUNK * MLSTM_CHUNKS_PER_STEP
    steps = S // rows
    N = B * S
    row = lambda w: pl.BlockSpec((rows, w), lambda b, c: (b * steps + c, 0))
    col = lambda h: pl.BlockSpec((h, rows), lambda b, c: (0, b * steps + c))
    return pl.pallas_call(
        _mlstm_kernel,
        grid=(B, steps),
        in_specs=[row(QK_PAD_W), row(QK_PAD_W), col(MLSTM_V_W), row(LANES), col(2 * MLSTM_HEADS)],
        out_specs=col(MLSTM_V_W),
        out_shape=jax.ShapeDtypeStruct((MLSTM_V_W, N), BF16),
        scratch_shapes=[pltpu.VMEM((MLSTM_HEADS, V_WIN, DQK_PAD), F32),
                        pltpu.VMEM((MLSTM_HEADS, 8, LANES), F32)],
        compiler_params=pltpu.CompilerParams(dimension_semantics=("arbitrary", "arbitrary"),
                                             vmem_limit_bytes=VMEM_LIMIT),
        name="mlstm",
    )(q, k, vt, gt, gtt)


def _mix_ffn_core(x_ref, main_ref, gate_ref, mo_ref, wo_ref, gpost_ref, gfpost_ref, wi_ref, wd_ref):
    mix_w = D_MODEL - MEM_W
    groups = [slice(r, r + FFN_ROW_GROUP) for r in range(0, x_ref.shape[0], FFN_ROW_GROUP)]
    if gate_ref is None:
        main = [_dot(main_ref[g, :], wo_ref[0:mix_w, :]) for g in groups]
    else:
        main = [_dot_tn(main_ref[:, g] * gate_ref[:, g], wo_ref[0:mix_w, :]) for g in groups]
    mix = [m + _dot(mo_ref[g, :], wo_ref[mix_w:, :]) for g, m in zip(groups, main)]
    x1 = [x_ref[g, :] + _rms(m, gpost_ref[...]) for g, m in zip(groups, mix)]
    hf = [_rms_hat(t).astype(BF16) for t in x1]
    act = [[] for _ in groups]
    for c in range(N_FF_CHUNKS):
        lo = c * FF_CHUNK
        for i, h in enumerate(hf):
            g = _dot(h, wi_ref[:, lo:lo + FF_CHUNK])
            u = _dot(h, wi_ref[:, D_FF + lo:D_FF + lo + FF_CHUNK])
            act[i].append((g * _sigmoid(g) * u).astype(BF16))
    acc = [_dot(jnp.concatenate(a, axis=1), wd_ref[...]) for a in act]
    return jnp.concatenate([t + _rms(a, gfpost_ref[...]) for t, a in zip(x1, acc)], axis=0)


def _mix_ffn_kernel(x_ref, main_ref, mo_ref, wo_ref, gpost_ref, gfpost_ref, wi_ref, wd_ref, xo_ref):
    xo_ref[...] = _mix_ffn_core(x_ref, main_ref, None, mo_ref, wo_ref, gpost_ref, gfpost_ref, wi_ref, wd_ref)


def _mix_ffn_gated_kernel(x_ref, main_ref, gate_ref, mo_ref, wo_ref, gpost_ref, gfpost_ref, wi_ref, wd_ref, xo_ref):
    xo_ref[...] = _mix_ffn_core(x_ref, main_ref, gate_ref, mo_ref, wo_ref, gpost_ref, gfpost_ref, wi_ref, wd_ref)


def _rope(t1, t2, cos, sin):
    return t1 * cos - t2 * sin, t2 * cos + t1 * sin


def _rope_tables(pos_ref, inv_ref):
    ang = pos_ref[...].astype(F32) * inv_ref[...]
    cos_d = jnp.cos(ang)
    sin_d = jnp.sin(ang)
    grp = lax.broadcasted_iota(jnp.int32, ang.shape, 1) >> 5

    def spread(t, g):
        y = jnp.where(grp == g, t, 0.0)
        y = y + pltpu.roll(y, HEAD_DIM // 2, 1)
        return y + pltpu.roll(y, HEAD_DIM, 1)

    n_grp = LANES // (HEAD_DIM // 2)
    cos = jnp.concatenate([spread(cos_d, g) for g in range(n_grp)], axis=0)
    sin = jnp.concatenate([spread(sin_d, g) for g in range(n_grp)], axis=0)
    return cos, sin


def _mix_ffn(layer, x2d, main, mo, wo, g_post, g_fpost, wi, wd, gate=None):
    N = x2d.shape[0]
    tm = FFN_ROW_TILE
    row = lambda w: pl.BlockSpec((tm, w), lambda i: (i, 0))
    slab = lambda w: pl.BlockSpec((None,) + w.shape[1:], lambda i: (layer, 0, 0), pipeline_mode=pl.Buffered(1))
    if gate is None:
        body, main_specs, mains = _mix_ffn_kernel, [row(main.shape[1])], [main]
    else:
        col = pl.BlockSpec((main.shape[0], tm), lambda i: (0, i))
        body, main_specs, mains = _mix_ffn_gated_kernel, [col, col], [main, gate]
    return pl.pallas_call(
        body,
        grid=(N // tm,),
        in_specs=[row(D_MODEL)] + main_specs + [row(MEM_W),
                  _resident(wo.shape),
                  _resident((1, D_MODEL)), _resident((1, D_MODEL)),
                  slab(wi), slab(wd)],
        out_specs=row(D_MODEL),
        out_shape=jax.ShapeDtypeStruct((N, D_MODEL), F32),
        compiler_params=pltpu.CompilerParams(dimension_semantics=("arbitrary",),
                                             vmem_limit_bytes=VMEM_LIMIT),
        name="mix_ffn_%d" % layer,
    )(x2d, *mains, mo, wo, g_post, g_fpost, wi, wd)


def _proj_b_kernel(x_ref, cos_ref, sin_ref, wkv_ref, wb_ref, mk_ref, mv_ref,
                   ks_ref, vs_ref, q1_ref, mo1_ref):
    groups = [slice(r, r + PROJ_B_ROW_GROUP) for r in range(0, x_ref.shape[0], PROJ_B_ROW_GROUP)]
    xn = [_rms_hat(x_ref[g, :]).astype(BF16) for g in groups]
    pb = [_dot(t, wb_ref[...]) for t in xn]
    s = [_mem_scores(t[:, SWA_Q_W:].astype(BF16), mk_ref) for t in pb]
    cos, sin = cos_ref[...], sin_ref[...]
    kv = [_dot(t, wkv_ref[...]) for t in xn]
    p = [_mem_probs(t) for t in s]

    for g, t in zip(groups, pb):
        for gi in range(SWA_GROUPS):
            base = gi * 2 * LANES
            t1, t2 = _rope(t[:, base:base + LANES], t[:, base + LANES:base + 2 * LANES], cos[g], sin[g])
            q1_ref[g, base:base + LANES] = t1.astype(BF16)
            q1_ref[g, base + LANES:base + 2 * LANES] = t2.astype(BF16)
    for g, t in zip(groups, p):
        mo1_ref[g, :] = _mem_out(t, mv_ref).astype(BF16)

    scale = HEAD_DIM ** -0.5 * LOG2E
    for g, t in zip(groups, kv):
        k1, k2 = _rope(t[:, 0:LANES], t[:, LANES:2 * LANES], cos[g], sin[g])
        ks_ref[g, 0:LANES] = (k1 * scale).astype(BF16)
        ks_ref[g, LANES:2 * LANES] = (k2 * scale).astype(BF16)
        vs_ref[g, :] = t[:, SWA_KV_W:].astype(BF16)


def _proj_b(x2d, cos, sin, wkv, wb, mkm, mvm, rows_per_batch):
    N = x2d.shape[0]
    tm = PROJ_B_ROW_TILE
    steps_per_batch = rows_per_batch // tm
    row = lambda w: pl.BlockSpec((tm, w), lambda i: (i, 0))
    mem_spec = pl.BlockSpec((1, 1, MEM_HEADS, MEM_TOKENS, MEM_W),
                            lambda i: (1, i // steps_per_batch, 0, 0, 0))
    return pl.pallas_call(
        _proj_b_kernel,
        grid=(N // tm,),
        in_specs=[row(D_MODEL), row(LANES), row(LANES), _resident(wkv.shape), _resident(wb.shape),
                  mem_spec, mem_spec],
        out_specs=[row(SWA_KV_W), row(SWA_KV_W), row(SWA_Q_W), row(MEM_W)],
        out_shape=[jax.ShapeDtypeStruct((N, SWA_KV_W), BF16), jax.ShapeDtypeStruct((N, SWA_KV_W), BF16),
                   jax.ShapeDtypeStruct((N, SWA_Q_W), BF16), jax.ShapeDtypeStruct((N, MEM_W), BF16)],
        compiler_params=pltpu.CompilerParams(dimension_semantics=("arbitrary",),
                                             vmem_limit_bytes=VMEM_LIMIT),
        name="proj_b",
    )(x2d, cos, sin, wkv, wb, mkm, mvm)


def _swa_kernel(sink_ref, q_ref, kp_ref, kc_ref, vp_ref, vc_ref, o_ref):
    W = SWA_WINDOW
    G = SWA_GROUPS
    n = pl.program_id(1)
    n_sub = q_ref.shape[0] // W
    H = SWA_KV_HEADS
    kpos = lax.broadcasted_iota(jnp.int32, (2 * W, G * W), 0)
    qcol = lax.broadcasted_iota(jnp.int32, (2 * W, G * W), 1)
    diff = (qcol & (W - 1)) + W - kpos
    in_band = (diff >= 0) & (diff < W)
    k_lane = lax.broadcasted_iota(jnp.int32, (2 * W, SWA_KV_W), 1)
    sink = sink_ref[...]
    subs = range(n_sub)
    q, km, vm, valid = [], [], [], []
    for sub in subs:
        q.append(jnp.concatenate([q_ref[sub * W:(sub + 1) * W, gi * 2 * LANES:(gi + 1) * 2 * LANES]
                                  for gi in range(G)], axis=0))
        if sub == 0:
            k = jnp.concatenate([kp_ref[...], kc_ref[0:W, :]], axis=0)
            v = jnp.concatenate([vp_ref[...], vc_ref[0:W, :]], axis=0)
            valid.append(in_band & ((kpos >= W) | (n > 0)))
        else:
            k = kc_ref[(sub - 1) * W:(sub + 1) * W, :]
            v = vc_ref[(sub - 1) * W:(sub + 1) * W, :]
            valid.append(in_band)
        km.append(jnp.concatenate([jnp.where(((k_lane & (LANES - 1)) >> 5) == j, k, jnp.zeros((), BF16))
                                   for j in range(H)], axis=0))
        vm.append(jnp.concatenate([jnp.where((k_lane >> 6) == j, v, jnp.zeros((), BF16))
                                   for j in range(H)], axis=0))
    s = [_dot_nt(km[i], q[i]).reshape(H, 2 * W, G * W) for i in subs]
    s = [jnp.where(valid[i][None], s[i], -jnp.inf) for i in subs]
    m = [jnp.maximum(jnp.max(s[i], axis=1, keepdims=True), sink) for i in subs]
    e = [jnp.exp2(s[i] - m[i]) for i in subs]
    inv = [1.0 / (jnp.sum(e[i], axis=1, keepdims=True) + jnp.exp2(sink - m[i])) for i in subs]
    p = [e[i].astype(BF16) * inv[i].astype(BF16) for i in subs]
    out = [_dot_tn(p[i].reshape(H * 2 * W, G * W), vm[i]) for i in subs]
    for sub in subs:
        for gi in range(G):
            o_ref[sub * W:(sub + 1) * W, gi * 2 * LANES:(gi + 1) * 2 * LANES] = (
                out[sub][gi * W:(gi + 1) * W].astype(BF16))


def _swa(sink_cols, q1, ks, vs, B, S):
    W = SWA_WINDOW
    n_sub = SWA_BLOCKS_PER_STEP
    tq = n_sub * W
    steps = S // tq
    cur = lambda w: pl.BlockSpec((tq, w), lambda b, n: (b * steps + n, 0))
    prev = lambda w: pl.BlockSpec((W, w), lambda b, n: (b * (S // W) + jnp.maximum(n_sub * n - 1, 0), 0))
    return pl.pallas_call(
        _swa_kernel,
        grid=(B, steps),
        in_specs=[_resident(sink_cols.shape),
                  cur(SWA_Q_W), prev(SWA_KV_W), cur(SWA_KV_W), prev(SWA_KV_W), cur(SWA_KV_W)],
        out_specs=cur(SWA_Q_W),
        out_shape=jax.ShapeDtypeStruct((B * S, SWA_Q_W), BF16),
        compiler_params=pltpu.CompilerParams(dimension_semantics=("arbitrary", "arbitrary"),
                                             vmem_limit_bytes=VMEM_LIMIT),
        name="swa",
    )(sink_cols, q1, ks, ks, vs, vs)


def _pad_heads(w, heads, width, padded):
    w = w.reshape(w.shape[0], heads, width)
    return jnp.pad(w, ((0, 0), (0, 0), (0, padded - width))).reshape(w.shape[0], heads * padded)


def _with_tail(perm, total):
    return np.concatenate([perm, np.arange(len(perm), total, dtype=np.int32)])


def _swa_q_perm():
    idx = np.empty((SWA_GROUPS, 2, SWA_KV_HEADS, HEAD_DIM // 2), np.int32)
    for gi in range(SWA_GROUPS):
        for half in range(2):
            for j in range(SWA_KV_HEADS):
                idx[gi, half, j] = (j * SWA_GROUPS + gi) * HEAD_DIM + half * (HEAD_DIM // 2) + np.arange(HEAD_DIM // 2)
    return idx.reshape(-1)


def _swa_k_perm():
    idx = np.empty((2, SWA_KV_HEADS, HEAD_DIM // 2), np.int32)
    for half in range(2):
        for j in range(SWA_KV_HEADS):
            idx[half, j] = j * HEAD_DIM + half * (HEAD_DIM // 2) + np.arange(HEAD_DIM // 2)
    return idx.reshape(-1)


def _swa_out_perm():
    idx = np.empty((SWA_GROUPS, SWA_KV_HEADS, HEAD_DIM), np.int32)
    for gi in range(SWA_GROUPS):
        for j in range(SWA_KV_HEADS):
            idx[gi, j] = (j * SWA_GROUPS + gi) * HEAD_DIM + np.arange(HEAD_DIM)
    return idx.reshape(-1)


def kernel(x, mem, positions, g_mix_pre, g_mix_post, g_ffn_pre, g_ffn_post, g_mem, w_mem_kv, w_out,
           w_ffn_in, w_ffn_out, w_in_a, b_gates_a, g_mlstm_out, g_kv, w_kv, w_in_b, sinks_b):
    B, S, _ = x.shape
    N = B * S
    assert S % PROJ_A_ROW_TILE == 0 and S % PROJ_B_ROW_TILE == 0 and S % FFN_ROW_TILE == 0
    assert S % (MLSTM_CHUNK * MLSTM_CHUNKS_PER_STEP) == 0
    assert S % (SWA_WINDOW * SWA_BLOCKS_PER_STEP) == 0
    x2d = x.reshape(N, D_MODEL)
    vec = lambda g: g.reshape(1, -1).astype(F32)

    mkm, mvm = _mem_kv(mem, g_mem, w_mem_kv.astype(BF16))

    fold = lambda g, w: g.astype(F32)[:, None] * w

    piece = lambda c0, width: fold(g_mix_pre[0], w_in_a[0][:, c0:c0 + width])
    c0 = 0
    wq = _pad_heads(piece(c0, MLSTM_QK_W), MLSTM_HEADS, MLSTM_DQK, DQK_PAD).astype(BF16)
    c0 += MLSTM_QK_W
    wk = _pad_heads(piece(c0, MLSTM_QK_W) * (MLSTM_DQK ** -0.5), MLSTM_HEADS, MLSTM_DQK, DQK_PAD).astype(BF16)
    c0 += MLSTM_QK_W
    wv = piece(c0, MLSTM_V_W).astype(BF16).T
    c0 += MLSTM_V_W
    wo = piece(c0, MLSTM_V_W).astype(BF16).T
    c0 += MLSTM_V_W
    wgt = jnp.pad(piece(c0, 2 * MLSTM_HEADS), ((0, 0), (0, LANES - 2 * MLSTM_HEADS))).astype(BF16)
    c0 += 2 * MLSTM_HEADS
    wmq = piece(c0, MEM_W).astype(BF16)
    bg = jnp.pad(b_gates_a[0].astype(F32), (0, LANES - 2 * MLSTM_HEADS)).reshape(1, LANES)

    inv = 1.0 / (ROPE_THETA ** (jnp.arange(0, HEAD_DIM, 2, dtype=F32) / HEAD_DIM))
    inv = jnp.tile(inv, SWA_KV_HEADS).reshape(1, LANES)
    n_grp = LANES // (HEAD_DIM // 2)
    pos_dense = positions.reshape(N // PROJ_A_ROW_TILE, n_grp, PROJ_A_ROW_TILE // n_grp).transpose(0, 2, 1)
    pos_dense = jnp.repeat(pos_dense, HEAD_DIM // 2, axis=2).reshape(N // n_grp, LANES)
    q, k, v, o, gt, gtt, mo0, cos, sin = _proj_a(x2d, wq, wk, wv, wo, wmq, wgt, bg, mkm, mvm, pos_dense, inv, S)
    hm = _mlstm(q, k, v, gt, gtt, B, S)

    w_ffn_in_bf = (g_ffn_pre.astype(F32)[:, :, None] * w_ffn_in).astype(BF16)
    w_ffn_out_bf = w_ffn_out.astype(BF16)
    wo0 = fold(jnp.concatenate([g_mlstm_out[0].astype(F32), jnp.ones((MEM_W,), F32)]), w_out[0]).astype(BF16)
    wkv = fold(g_kv, w_kv)[:, _with_tail(_swa_k_perm(), 2 * SWA_KV_W)].astype(BF16)
    wb = fold(g_mix_pre[1], w_in_b[0])[:, _with_tail(_swa_q_perm(), SWA_Q_W + MEM_W)].astype(BF16)
    x1 = _mix_ffn(0, x2d, hm, mo0, wo0, vec(g_mix_post[0]), vec(g_ffn_post[0]),
                  w_ffn_in_bf, w_ffn_out_bf, gate=o)
    ks, vs, q1, mo1 = _proj_b(x1, cos, sin, wkv, wb, mkm, mvm, S)

    sink_rows = jnp.repeat((sinks_b[0].astype(F32) * LOG2E).reshape(SWA_KV_HEADS, SWA_GROUPS), SWA_WINDOW, axis=1)
    attn = _swa(sink_rows.reshape(SWA_KV_HEADS, 1, SWA_GROUPS * SWA_WINDOW), q1, ks, vs, B, S)
    wo1 = w_out[1][_with_tail(_swa_out_perm(), D_MODEL)].astype(BF16)
    xo = _mix_ffn(1, x1, attn, mo1, wo1, vec(g_mix_post[1]), vec(g_ffn_post[1]),
                  w_ffn_in_bf, w_ffn_out_bf)
    return xo.reshape(B, S, D_MODEL)
```

```python
import numpy as np
import jax
import jax.numpy as jnp
from jax import lax
from jax.experimental import pallas as pl
from jax.experimental.pallas import tpu as pltpu

F32 = jnp.float32
BF16 = jnp.bfloat16

D_MODEL = 1024
DEPTH = 2
HEAD_DIM = 64
EPS = 1e-6
ROPE_THETA = 10000.0
LOG2E = 1.4426950408889634

MLSTM_HEADS = 4
MLSTM_DV = 192
MLSTM_DQK = 96
MLSTM_QK_W = MLSTM_HEADS * MLSTM_DQK
MLSTM_V_W = MLSTM_HEADS * MLSTM_DV
GATE_SOFTCAP = 15.0
M_INIT = -1e30

SWA_Q_HEADS = 12
SWA_KV_HEADS = 4
SWA_GROUPS = SWA_Q_HEADS // SWA_KV_HEADS
SWA_Q_W = SWA_Q_HEADS * HEAD_DIM
SWA_KV_W = SWA_KV_HEADS * HEAD_DIM
SWA_WINDOW = 128

MEM_TOKENS = 256
MEM_HEADS = 4
MEM_HEAD_DIM = 64
MEM_W = MEM_HEADS * MEM_HEAD_DIM

D_FF = 2816

LANES = 128
MXU_TILE = 256

DQK_PAD = LANES
QK_PAD_W = MLSTM_HEADS * DQK_PAD
V_WIN = MXU_TILE
MLSTM_CHUNK = 256
MLSTM_CHUNKS_PER_STEP = 16
FF_CHUNK = MXU_TILE
N_FF_CHUNKS = D_FF // FF_CHUNK
PROJ_A_ROW_TILE = 1024
PROJ_B_ROW_TILE = 2048
PROJ_B_ROW_GROUP = 512
FFN_ROW_TILE = 1024
FFN_ROW_GROUP = 512
SWA_BLOCKS_PER_STEP = 16
VMEM_LIMIT = 56 * 1024 * 1024

_V_WIN_START = (0, 128, 384, 512)
_V_WIN_OFF = (0, 64, 0, 64)
_V_ONES_COL = (192, 0, 192, 0)


def _rms_hat(x):
    return x * lax.rsqrt(jnp.mean(x * x, axis=-1, keepdims=True) + EPS)


def _rms(x, g):
    return _rms_hat(x) * g


def _dot(a, b):
    return jnp.dot(a, b, preferred_element_type=F32)


def _dot_nt(a, b):
    return lax.dot_general(a, b, (((1,), (1,)), ((), ())), preferred_element_type=F32)


def _dot_tn(a, b):
    return lax.dot_general(a, b, (((0,), (0,)), ((), ())), preferred_element_type=F32)


def _sigmoid(x):
    return 1.0 / (1.0 + jnp.exp(-x))


def _resident(shape):
    nd = len(shape)
    return pl.BlockSpec(shape, lambda *_: (0,) * nd, pipeline_mode=pl.Buffered(1))


def _mem_kv_kernel(mem_ref, g_ref, w_ref, mk_ref, mv_ref):
    B = mem_ref.shape[0]
    hn = _rms(mem_ref[...].reshape(B * MEM_TOKENS, D_MODEL), g_ref[0]).astype(BF16)
    kv = _dot(hn, w_ref[0])
    mk = kv[:, :MEM_W] * (MEM_HEAD_DIM ** -0.5 * LOG2E)
    mv = kv[:, MEM_W:]
    lane_head = lax.broadcasted_iota(jnp.int32, mk.shape, 1) >> 6
    for h in range(MEM_HEADS):
        sel = lane_head == h
        mk_ref[0, :, h] = jnp.where(sel, mk, 0.0).astype(BF16).reshape(B, MEM_TOKENS, MEM_W)
        mv_ref[0, :, h] = jnp.where(sel, mv, 0.0).astype(BF16).reshape(B, MEM_TOKENS, MEM_W)


def _mem_kv(mem, g_mem, w_mem_kv_bf16):
    B = mem.shape[0]
    out_sds = jax.ShapeDtypeStruct((DEPTH, B, MEM_HEADS, MEM_TOKENS, MEM_W), BF16)
    out_spec = pl.BlockSpec((1, B, MEM_HEADS, MEM_TOKENS, MEM_W), lambda l: (l, 0, 0, 0, 0))
    return pl.pallas_call(
        _mem_kv_kernel,
        grid=(DEPTH,),
        in_specs=[
            pl.BlockSpec((B, MEM_TOKENS, D_MODEL), lambda l: (0, 0, 0)),
            pl.BlockSpec((1, 1, D_MODEL), lambda l: (l, 0, 0)),
            pl.BlockSpec((1, D_MODEL, 2 * MEM_W), lambda l: (l, 0, 0)),
        ],
        out_specs=[out_spec, out_spec],
        out_shape=[out_sds, out_sds],
        compiler_params=pltpu.CompilerParams(dimension_semantics=("arbitrary",),
                                             vmem_limit_bytes=VMEM_LIMIT),
        name="mem_kv",
    )(mem, g_mem.reshape(DEPTH, 1, D_MODEL), w_mem_kv_bf16)


def _mem_scores(mq, mk_ref):
    return [_dot_nt(mq, mk_ref[0, 0, h]) for h in range(MEM_HEADS)]


def _mem_probs(s):
    e = [jnp.exp2(sh - jnp.max(sh, axis=-1, keepdims=True)) for sh in s]
    return [(eh * (1.0 / jnp.sum(eh, axis=-1, keepdims=True))).astype(BF16) for eh in e]


def _mem_out(p, mv_ref):
    out = _dot(p[0], mv_ref[0, 0, 0])
    for h in range(1, MEM_HEADS):
        out = out + _dot(p[h], mv_ref[0, 0, h])
    return out


def _proj_a_kernel(x_ref, wq_ref, wk_ref, wv_ref, wo_ref, wmq_ref, wg_ref, bg_ref, mk_ref, mv_ref, pos_ref, inv_ref,
                   q_ref, k_ref, v_ref, o_ref, gt_ref, gtt_ref, mo_ref, cos_ref, sin_ref):
    cos_ref[...], sin_ref[...] = _rope_tables(pos_ref, inv_ref)
    hn = _rms_hat(x_ref[...]).astype(BF16)
    mq = _dot(hn, wmq_ref[...]).astype(BF16)
    s = _mem_scores(mq, mk_ref)
    gates = _dot(hn, wg_ref[...]) + bg_ref[...]
    q_ref[...] = _dot(hn, wq_ref[...]).astype(BF16)
    gates = GATE_SOFTCAP * jnp.tanh(gates * (1.0 / GATE_SOFTCAP))
    log_sig = jnp.minimum(gates, 0.0) - jnp.log1p(jnp.exp(-jnp.abs(gates)))
    lane = lax.broadcasted_iota(jnp.int32, gates.shape, 1)
    gl = jnp.where(lane < MLSTM_HEADS, gates, log_sig) * LOG2E
    p = _mem_probs(s)
    k_ref[...] = _dot(hn, wk_ref[...]).astype(BF16)
    g_hi = gl.astype(BF16)
    rest = gl - g_hi.astype(F32)
    g_mid = rest.astype(BF16)
    g_lo = (rest - g_mid.astype(F32)).astype(BF16)
    L = MLSTM_CHUNK
    tri = (lax.broadcasted_iota(jnp.int32, (L, L), 1) <= lax.broadcasted_iota(jnp.int32, (L, L), 0)).astype(BF16)
    csum = jnp.concatenate(
        [_dot(tri, g_hi[r:r + L]) + _dot(tri, g_mid[r:r + L]) + _dot(tri, g_lo[r:r + L])
         for r in range(0, gl.shape[0], L)], axis=0)
    v_ref[...] = _dot_nt(wv_ref[...], hn).astype(BF16)
    y = jnp.where(lane < MLSTM_HEADS, gl, csum)
    gt_ref[...] = y
    gtt_ref[...] = y.T[0:2 * MLSTM_HEADS, :]
    mo_ref[...] = _mem_out(p, mv_ref).astype(BF16)
    o_ref[...] = _sigmoid(_dot_nt(wo_ref[...], hn)).astype(BF16)


def _proj_a(x2d, wq, wk, wv, wo, wmq, wg, bg, mkm, mvm, pos, inv, rows_per_batch):
    N = x2d.shape[0]
    tm = PROJ_A_ROW_TILE
    steps_per_batch = rows_per_batch // tm
    row = lambda w: pl.BlockSpec((tm, w), lambda i: (i, 0))
    col = pl.BlockSpec((MLSTM_V_W, tm), lambda i: (0, i))
    mem_spec = pl.BlockSpec((1, 1, MEM_HEADS, MEM_TOKENS, MEM_W),
                            lambda i: (0, i // steps_per_batch, 0, 0, 0))
    return pl.pallas_call(
        _proj_a_kernel,
        grid=(N // tm,),
        in_specs=[row(D_MODEL),
                  _resident(wq.shape), _resident(wk.shape), _resident(wv.shape), _resident(wo.shape),
                  _resident(wmq.shape), _resident(wg.shape), _resident((1, LANES)),
                  mem_spec, mem_spec,
                  pl.BlockSpec((tm // (LANES // (HEAD_DIM // 2)), LANES), lambda i: (i, 0)), _resident((1, LANES))],
        out_specs=[row(QK_PAD_W), row(QK_PAD_W), col, col, row(LANES),
                   pl.BlockSpec((2 * MLSTM_HEADS, tm), lambda i: (0, i)), row(MEM_W), row(LANES), row(LANES)],
        out_shape=[jax.ShapeDtypeStruct((N, QK_PAD_W), BF16), jax.ShapeDtypeStruct((N, QK_PAD_W), BF16),
                   jax.ShapeDtypeStruct((MLSTM_V_W, N), BF16), jax.ShapeDtypeStruct((MLSTM_V_W, N), BF16),
                   jax.ShapeDtypeStruct((N, LANES), F32), jax.ShapeDtypeStruct((2 * MLSTM_HEADS, N), F32),
                   jax.ShapeDtypeStruct((N, MEM_W), BF16),
                   jax.ShapeDtypeStruct((N, LANES), F32), jax.ShapeDtypeStruct((N, LANES), F32)],
        compiler_params=pltpu.CompilerParams(dimension_semantics=("arbitrary",),
                                             vmem_limit_bytes=VMEM_LIMIT),
        name="proj_a",
    )(x2d, wq, wk, wv, wo, wmq, wg, bg, mkm, mvm, pos, inv)


def _mlstm_kernel(q_ref, k_ref, vt_ref, gt_ref, gtt_ref, hmt_ref, c_scr, m_scr):
    L = MLSTM_CHUNK

    @pl.when(pl.program_id(1) == 0)
    def _():
        c_scr[...] = jnp.zeros(c_scr.shape, F32)
        m_scr[...] = jnp.full(m_scr.shape, M_INIT, F32)

    causal = lax.broadcasted_iota(jnp.int32, (L, L), 0) <= lax.broadcasted_iota(jnp.int32, (L, L), 1)
    feat = lax.broadcasted_iota(jnp.int32, (V_WIN, L), 0)
    heads = range(MLSTM_HEADS)
    m_st = [m_scr[h][0:1, 0:1] for h in heads]
    c_st = [c_scr[h] for h in heads]

    for r0 in range(0, q_ref.shape[0], L):
        y = gt_ref[r0:r0 + L, :]
        yt = gtt_ref[:, r0:r0 + L]
        b_row = [yt[MLSTM_HEADS + h:MLSTM_HEADS + h + 1, :] for h in heads]
        c_row = [b_row[h] - yt[h:h + 1, :] for h in heads]
        c_col = [y[:, MLSTM_HEADS + h:MLSTM_HEADS + h + 1] - y[:, h:h + 1] for h in heads]
        g_tot = [y[L - 1:L, MLSTM_HEADS + h:MLSTM_HEADS + h + 1] for h in heads]
        qh = [q_ref[r0:r0 + L, h * DQK_PAD:(h + 1) * DQK_PAD] for h in heads]
        kh = [k_ref[r0:r0 + L, h * DQK_PAD:(h + 1) * DQK_PAD] for h in heads]
        vt_aug = [jnp.where(feat == _V_ONES_COL[h], jnp.ones((), BF16),
                            vt_ref[_V_WIN_START[h]:_V_WIN_START[h] + V_WIN, r0:r0 + L]) for h in heads]

        both = [_dot_nt(jnp.concatenate([kh[h], c_st[h].astype(BF16)], axis=0), qh[h]) for h in heads]
        qk_t = [t[:L] for t in both]
        inter_t = [t[L:] for t in both]
        dmat = [jnp.where(causal, b_row[h] - c_col[h], -jnp.inf) for h in heads]
        inter_log = [b_row[h] + m_st[h] for h in heads]
        m_row = [jnp.maximum(inter_log[h], jnp.max(dmat[h], axis=0, keepdims=True)) for h in heads]
        scores_t = [(qk_t[h] * jnp.exp2(dmat[h] - m_row[h])).astype(BF16) for h in heads]
        num_t = [_dot(vt_aug[h], scores_t[h]) + jnp.exp2(inter_log[h] - m_row[h]) * inter_t[h] for h in heads]

        a = [g_tot[h] - c_row[h] for h in heads]
        m_new = [jnp.maximum(g_tot[h] + m_st[h], jnp.max(a[h], axis=1, keepdims=True)) for h in heads]
        vtw = [vt_aug[h] * jnp.exp2(a[h] - m_new[h]).astype(BF16) for h in heads]
        c_upd = [_dot(vtw[h], kh[h]) for h in heads]

        for h in heads:
            off, ones_row, lo = _V_WIN_OFF[h], _V_ONES_COL[h], h * MLSTM_DV
            den = num_t[h][ones_row:ones_row + 1, :]
            inv_dd = 1.0 / jnp.maximum(jnp.abs(den), jnp.exp2(-m_row[h]))
            h_t = num_t[h][off:off + MLSTM_DV, :]
            ms = jnp.sum(h_t * h_t, axis=0, keepdims=True) * (1.0 / MLSTM_DV)
            scale = inv_dd * lax.rsqrt(ms * inv_dd * inv_dd + EPS)
            hmt_ref[lo:lo + MLSTM_DV, r0:r0 + L] = (h_t * scale).astype(BF16)

        c_st = [jnp.exp2(g_tot[h] + m_st[h] - m_new[h]) * c_st[h] + c_upd[h] for h in heads]
        m_st = m_new

    for h in heads:
        c_scr[h] = c_st[h]
        m_scr[h] = jnp.broadcast_to(m_st[h], m_scr.shape[1:])


def _mlstm(q, k, vt, gt, gtt, B, S):
    rows = MLSTM_CHUNK * MLSTM_CHUNKS_PER_STEP
    steps = S // rows
    N = B * S
    row = lambda w: pl.BlockSpec((rows, w), lambda b, c: (b * steps + c, 0))
    col = lambda h: pl.BlockSpec((h, rows), lambda b, c: (0, b * steps + c))
    return pl.pallas_call(
        _mlstm_kernel,
        grid=(B, steps),
        in_specs=[row(QK_PAD_W), row(QK_PAD_W), col(MLSTM_V_W), row(LANES), col(2 * MLSTM_HEADS)],
        out_specs=col(MLSTM_V_W),
        out_shape=jax.ShapeDtypeStruct((MLSTM_V_W, N), BF16),
        scratch_shapes=[pltpu.VMEM((MLSTM_HEADS, V_WIN, DQK_PAD), F32),
                        pltpu.VMEM((MLSTM_HEADS, 8, LANES), F32)],
        compiler_params=pltpu.CompilerParams(dimension_semantics=("arbitrary", "arbitrary"),
                                             vmem_limit_bytes=VMEM_LIMIT),
        name="mlstm",
    )(q, k, vt, gt, gtt)


def _mix_ffn_core(x_ref, main_ref, gate_ref, mo_ref, wo_ref, gpost_ref, gfpost_ref, wi_ref, wd_ref):
    mix_w = D_MODEL - MEM_W
    groups = [slice(r, r + FFN_ROW_GROUP) for r in range(0, x_ref.shape[0], FFN_ROW_GROUP)]
    if gate_ref is None:
        main = [_dot(main_ref[g, :], wo_ref[0:mix_w, :]) for g in groups]
    else:
        main = [_dot_tn(main_ref[:, g] * gate_ref[:, g], wo_ref[0:mix_w, :]) for g in groups]
    mix = [m + _dot(mo_ref[g, :], wo_ref[mix_w:, :]) for g, m in zip(groups, main)]
    x1 = [x_ref[g, :] + _rms(m, gpost_ref[...]) for g, m in zip(groups, mix)]
    hf = [_rms_hat(t).astype(BF16) for t in x1]
    act = [[] for _ in groups]
    for c in range(N_FF_CHUNKS):
        lo = c * FF_CHUNK
        for i, h in enumerate(hf):
            g = _dot(h, wi_ref[:, lo:lo + FF_CHUNK])
            u = _dot(h, wi_ref[:, D_FF + lo:D_FF + lo + FF_CHUNK])
            act[i].append((g * _sigmoid(g) * u).astype(BF16))
    acc = [_dot(jnp.concatenate(a, axis=1), wd_ref[...]) for a in act]
    return jnp.concatenate([t + _rms(a, gfpost_ref[...]) for t, a in zip(x1, acc)], axis=0)


def _mix_ffn_kernel(x_ref, main_ref, mo_ref, wo_ref, gpost_ref, gfpost_ref, wi_ref, wd_ref, xo_ref):
    xo_ref[...] = _mix_ffn_core(x_ref, main_ref, None, mo_ref, wo_ref, gpost_ref, gfpost_ref, wi_ref, wd_ref)


def _mix_ffn_gated_kernel(x_ref, main_ref, gate_ref, mo_ref, wo_ref, gpost_ref, gfpost_ref, wi_ref, wd_ref, xo_ref):
    xo_ref[...] = _mix_ffn_core(x_ref, main_ref, gate_ref, mo_ref, wo_ref, gpost_ref, gfpost_ref, wi_ref, wd_ref)


def _rope(t1, t2, cos, sin):
    return t1 * cos - t2 * sin, t2 * cos + t1 * sin


def _rope_tables(pos_ref, inv_ref):
    ang = pos_ref[...].astype(F32) * inv_ref[...]
    cos_d = jnp.cos(ang)
    sin_d = jnp.sin(ang)
    grp = lax.broadcasted_iota(jnp.int32, ang.shape, 1) >> 5

    def spread(t, g):
        y = jnp.where(grp == g, t, 0.0)
        y = y + pltpu.roll(y, HEAD_DIM // 2, 1)
        return y + pltpu.roll(y, HEAD_DIM, 1)

    n_grp = LANES // (HEAD_DIM // 2)
    cos = jnp.concatenate([spread(cos_d, g) for g in range(n_grp)], axis=0)
    sin = jnp.concatenate([spread(sin_d, g) for g in range(n_grp)], axis=0)
    return cos, sin


def _mix_ffn(layer, x2d, main, mo, wo, g_post, g_fpost, wi, wd, gate=None):
    N = x2d.shape[0]
    tm = FFN_ROW_TILE
    row = lambda w: pl.BlockSpec((tm, w), lambda i: (i, 0))
    slab = lambda w: pl.BlockSpec((None,) + w.shape[1:], lambda i: (layer, 0, 0), pipeline_mode=pl.Buffered(1))
    if gate is None:
        body, main_specs, mains = _mix_ffn_kernel, [row(main.shape[1])], [main]
    else:
        col = pl.BlockSpec((main.shape[0], tm), lambda i: (0, i))
        body, main_specs, mains = _mix_ffn_gated_kernel, [col, col], [main, gate]
    return pl.pallas_call(
        body,
        grid=(N // tm,),
        in_specs=[row(D_MODEL)] + main_specs + [row(MEM_W),
                  _resident(wo.shape),
                  _resident((1, D_MODEL)), _resident((1, D_MODEL)),
                  slab(wi), slab(wd)],
        out_specs=row(D_MODEL),
        out_shape=jax.ShapeDtypeStruct((N, D_MODEL), F32),
        compiler_params=pltpu.CompilerParams(dimension_semantics=("arbitrary",),
                                             vmem_limit_bytes=VMEM_LIMIT),
        name="mix_ffn_%d" % layer,
    )(x2d, *mains, mo, wo, g_post, g_fpost, wi, wd)


def _proj_b_kernel(x_ref, cos_ref, sin_ref, wkv_ref, wb_ref, mk_ref, mv_ref,
                   ks_ref, vs_ref, q1_ref, mo1_ref):
    groups = [slice(r, r + PROJ_B_ROW_GROUP) for r in range(0, x_ref.shape[0], PROJ_B_ROW_GROUP)]
    xn = [_rms_hat(x_ref[g, :]).astype(BF16) for g in groups]
    pb = [_dot(t, wb_ref[...]) for t in xn]
    s = [_mem_scores(t[:, SWA_Q_W:].astype(BF16), mk_ref) for t in pb]
    cos, sin = cos_ref[...], sin_ref[...]
    kv = [_dot(t, wkv_ref[...]) for t in xn]
    p = [_mem_probs(t) for t in s]

    for g, t in zip(groups, pb):
        for gi in range(SWA_GROUPS):
            base = gi * 2 * LANES
            t1, t2 = _rope(t[:, base:base + LANES], t[:, base + LANES:base + 2 * LANES], cos[g], sin[g])
            q1_ref[g, base:base + LANES] = t1.astype(BF16)
            q1_ref[g, base + LANES:base + 2 * LANES] = t2.astype(BF16)
    for g, t in zip(groups, p):
        mo1_ref[g, :] = _mem_out(t, mv_ref).astype(BF16)

    scale = HEAD_DIM ** -0.5 * LOG2E
    for g, t in zip(groups, kv):
        k1, k2 = _rope(t[:, 0:LANES], t[:, LANES:2 * LANES], cos[g], sin[g])
        ks_ref[g, 0:LANES] = (k1 * scale).astype(BF16)
        ks_ref[g, LANES:2 * LANES] = (k2 * scale).astype(BF16)
        vs_ref[g, :] = t[:, SWA_KV_W:].astype(BF16)


def _proj_b(x2d, cos, sin, wkv, wb, mkm, mvm, rows_per_batch):
    N = x2d.shape[0]
    tm = PROJ_B_ROW_TILE
    steps_per_batch = rows_per_batch // tm
    row = lambda w: pl.BlockSpec((tm, w), lambda i: (i, 0))
    mem_spec = pl.BlockSpec((1, 1, MEM_HEADS, MEM_TOKENS, MEM_W),
                            lambda i: (1, i // steps_per_batch, 0, 0, 0))
    return pl.pallas_call(
        _proj_b_kernel,
        grid=(N // tm,),
        in_specs=[row(D_MODEL), row(LANES), row(LANES), _resident(wkv.shape), _resident(wb.shape),
                  mem_spec, mem_spec],
        out_specs=[row(SWA_KV_W), row(SWA_KV_W), row(SWA_Q_W), row(MEM_W)],
        out_shape=[jax.ShapeDtypeStruct((N, SWA_KV_W), BF16), jax.ShapeDtypeStruct((N, SWA_KV_W), BF16),
                   jax.ShapeDtypeStruct((N, SWA_Q_W), BF16), jax.ShapeDtypeStruct((N, MEM_W), BF16)],
        compiler_params=pltpu.CompilerParams(dimension_semantics=("arbitrary",),
                                             vmem_limit_bytes=VMEM_LIMIT),
        name="proj_b",
    )(x2d, cos, sin, wkv, wb, mkm, mvm)


def _swa_kernel(sink_ref, q_ref, kp_ref, kc_ref, vp_ref, vc_ref, o_ref):
    W = SWA_WINDOW
    G = SWA_GROUPS
    n = pl.program_id(1)
    n_sub = q_ref.shape[0] // W
    H = SWA_KV_HEADS
    kpos = lax.broadcasted_iota(jnp.int32, (2 * W, G * W), 0)
    qcol = lax.broadcasted_iota(jnp.int32, (2 * W, G * W), 1)
    diff = (qcol & (W - 1)) + W - kpos
    in_band = (diff >= 0) & (diff < W)
    k_lane = lax.broadcasted_iota(jnp.int32, (2 * W, SWA_KV_W), 1)
    sink = sink_ref[...]
    subs = range(n_sub)
    q, km, vm, valid = [], [], [], []
    for sub in subs:
        q.append(jnp.concatenate([q_ref[sub * W:(sub + 1) * W, gi * 2 * LANES:(gi + 1) * 2 * LANES]
                                  for gi in range(G)], axis=0))
        if sub == 0:
            k = jnp.concatenate([kp_ref[...], kc_ref[0:W, :]], axis=0)
            v = jnp.concatenate([vp_ref[...], vc_ref[0:W, :]], axis=0)
            valid.append(in_band & ((kpos >= W) | (n > 0)))
        else:
            k = kc_ref[(sub - 1) * W:(sub + 1) * W, :]
            v = vc_ref[(sub - 1) * W:(sub + 1) * W, :]
            valid.append(in_band)
        km.append(jnp.concatenate([jnp.where(((k_lane & (LANES - 1)) >> 5) == j, k, jnp.zeros((), BF16))
                                   for j in range(H)], axis=0))
        vm.append(jnp.concatenate([jnp.where((k_lane >> 6) == j, v, jnp.zeros((), BF16))
                                   for j in range(H)], axis=0))
    s = [_dot_nt(km[i], q[i]).reshape(H, 2 * W, G * W) for i in subs]
    s = [jnp.where(valid[i][None], s[i], -jnp.inf) for i in subs]
    m = [jnp.maximum(jnp.max(s[i], axis=1, keepdims=True), sink) for i in subs]
    e = [jnp.exp2(s[i] - m[i]) for i in subs]
    inv = [1.0 / (jnp.sum(e[i], axis=1, keepdims=True) + jnp.exp2(sink - m[i])) for i in subs]
    p = [e[i].astype(BF16) * inv[i].astype(BF16) for i in subs]
    out = [_dot_tn(p[i].reshape(H * 2 * W, G * W), vm[i]) for i in subs]
    for sub in subs:
        for gi in range(G):
            o_ref[sub * W:(sub + 1) * W, gi * 2 * LANES:(gi + 1) * 2 * LANES] = (
                out[sub][gi * W:(gi + 1) * W].astype(BF16))


def _swa(sink_cols, q1, ks, vs, B, S):
    W = SWA_WINDOW
    n_sub = SWA_BLOCKS_PER_STEP
    tq = n_sub * W
    steps = S // tq
    cur = lambda w: pl.BlockSpec((tq, w), lambda b, n: (b * steps + n, 0))
    prev = lambda w: pl.BlockSpec((W, w), lambda b, n: (b * (S // W) + jnp.maximum(n_sub * n - 1, 0), 0))
    return pl.pallas_call(
        _swa_kernel,
        grid=(B, steps),
        in_specs=[_resident(sink_cols.shape),
                  cur(SWA_Q_W), prev(SWA_KV_W), cur(SWA_KV_W), prev(SWA_KV_W), cur(SWA_KV_W)],
        out_specs=cur(SWA_Q_W),
        out_shape=jax.ShapeDtypeStruct((B * S, SWA_Q_W), BF16),
        compiler_params=pltpu.CompilerParams(dimension_semantics=("arbitrary", "arbitrary"),
                                             vmem_limit_bytes=VMEM_LIMIT),
        name="swa",
    )(sink_cols, q1, ks, ks, vs, vs)


def _pad_heads(w, heads, width, padded):
    w = w.reshape(w.shape[0], heads, width)
    return jnp.pad(w, ((0, 0), (0, 0), (0, padded - width))).reshape(w.shape[0], heads * padded)


def _with_tail(perm, total):
    return np.concatenate([perm, np.arange(len(perm), total, dtype=np.int32)])


def _swa_q_perm():
    idx = np.empty((SWA_GROUPS, 2, SWA_KV_HEADS, HEAD_DIM // 2), np.int32)
    for gi in range(SWA_GROUPS):
        for half in range(2):
            for j in range(SWA_KV_HEADS):
                idx[gi, half, j] = (j * SWA_GROUPS + gi) * HEAD_DIM + half * (HEAD_DIM // 2) + np.arange(HEAD_DIM // 2)
    return idx.reshape(-1)


def _swa_k_perm():
    idx = np.empty((2, SWA_KV_HEADS, HEAD_DIM // 2), np.int32)
    for half in range(2):
        for j in range(SWA_KV_HEADS):
            idx[half, j] = j * HEAD_DIM + half * (HEAD_DIM // 2) + np.arange(HEAD_DIM // 2)
    return idx.reshape(-1)


def _swa_out_perm():
    idx = np.empty((SWA_GROUPS, SWA_KV_HEADS, HEAD_DIM), np.int32)
    for gi in range(SWA_GROUPS):
        for j in range(SWA_KV_HEADS):
            idx[gi, j] = (j * SWA_GROUPS + gi) * HEAD_DIM + np.arange(HEAD_DIM)
    return idx.reshape(-1)


def kernel(x, mem, positions, g_mix_pre, g_mix_post, g_ffn_pre, g_ffn_post, g_mem, w_mem_kv, w_out,
           w_ffn_in, w_ffn_out, w_in_a, b_gates_a, g_mlstm_out, g_kv, w_kv, w_in_b, sinks_b):
    B, S, _ = x.shape
    N = B * S
    assert S % PROJ_A_ROW_TILE == 0 and S % PROJ_B_ROW_TILE == 0 and S % FFN_ROW_TILE == 0
    assert S % (MLSTM_CHUNK * MLSTM_CHUNKS_PER_STEP) == 0
    assert S % (SWA_WINDOW * SWA_BLOCKS_PER_STEP) == 0
    x2d = x.reshape(N, D_MODEL)
    vec = lambda g: g.reshape(1, -1).astype(F32)

    mkm, mvm = _mem_kv(mem, g_mem, w_mem_kv.astype(BF16))

    fold = lambda g, w: g.astype(F32)[:, None] * w

    piece = lambda c0, width: fold(g_mix_pre[0], w_in_a[0][:, c0:c0 + width])
    c0 = 0
    wq = _pad_heads(piece(c0, MLSTM_QK_W), MLSTM_HEADS, MLSTM_DQK, DQK_PAD).astype(BF16)
    c0 += MLSTM_QK_W
    wk = _pad_heads(piece(c0, MLSTM_QK_W) * (MLSTM_DQK ** -0.5), MLSTM_HEADS, MLSTM_DQK, DQK_PAD).astype(BF16)
    c0 += MLSTM_QK_W
    wv = piece(c0, MLSTM_V_W).astype(BF16).T
    c0 += MLSTM_V_W
    wo = piece(c0, MLSTM_V_W).astype(BF16).T
    c0 += MLSTM_V_W
    wgt = jnp.pad(piece(c0, 2 * MLSTM_HEADS), ((0, 0), (0, LANES - 2 * MLSTM_HEADS))).astype(BF16)
    c0 += 2 * MLSTM_HEADS
    wmq = piece(c0, MEM_W).astype(BF16)
    bg = jnp.pad(b_gates_a[0].astype(F32), (0, LANES - 2 * MLSTM_HEADS)).reshape(1, LANES)

    inv = 1.0 / (ROPE_THETA ** (jnp.arange(0, HEAD_DIM, 2, dtype=F32) / HEAD_DIM))
    inv = jnp.tile(inv, SWA_KV_HEADS).reshape(1, LANES)
    n_grp = LANES // (HEAD_DIM // 2)
    pos_dense = positions.reshape(N // PROJ_A_ROW_TILE, n_grp, PROJ_A_ROW_TILE // n_grp).transpose(0, 2, 1)
    pos_dense = jnp.repeat(pos_dense, HEAD_DIM // 2, axis=2).reshape(N // n_grp, LANES)
    q, k, v, o, gt, gtt, mo0, cos, sin = _proj_a(x2d, wq, wk, wv, wo, wmq, wgt, bg, mkm, mvm, pos_dense, inv, S)
    hm = _mlstm(q, k, v, gt, gtt, B, S)

    w_ffn_in_bf = (g_ffn_pre.astype(F32)[:, :, None] * w_ffn_in).astype(BF16)
    w_ffn_out_bf = w_ffn_out.astype(BF16)
    wo0 = fold(jnp.concatenate([g_mlstm_out[0].astype(F32), jnp.ones((MEM_W,), F32)]), w_out[0]).astype(BF16)
    wkv = fold(g_kv, w_kv)[:, _with_tail(_swa_k_perm(), 2 * SWA_KV_W)].astype(BF16)
    wb = fold(g_mix_pre[1], w_in_b[0])[:, _with_tail(_swa_q_perm(), SWA_Q_W + MEM_W)].astype(BF16)
    x1 = _mix_ffn(0, x2d, hm, mo0, wo0, vec(g_mix_post[0]), vec(g_ffn_post[0]),
                  w_ffn_in_bf, w_ffn_out_bf, gate=o)
    ks, vs, q1, mo1 = _proj_b(x1, cos, sin, wkv, wb, mkm, mvm, S)

    sink_rows = jnp.repeat((sinks_b[0].astype(F32) * LOG2E).reshape(SWA_KV_HEADS, SWA_GROUPS), SWA_WINDOW, axis=1)
    attn = _swa(sink_rows.reshape(SWA_KV_HEADS, 1, SWA_GROUPS * SWA_WINDOW), q1, ks, vs, B, S)
    wo1 = w_out[1][_with_tail(_swa_out_perm(), D_MODEL)].astype(BF16)
    xo = _mix_ffn(1, x1, attn, mo1, wo1, vec(g_mix_post[1]), vec(g_ffn_post[1]),
                  w_ffn_in_bf, w_ffn_out_bf)
    return xo.reshape(B, S, D_MODEL)
```

```python
import numpy as np
import jax
import jax.numpy as jnp
from jax import lax
from jax.experimental import pallas as pl
from jax.experimental.pallas import tpu as pltpu

F32 = jnp.float32
BF16 = jnp.bfloat16

D_MODEL = 1024
DEPTH = 2
HEAD_DIM = 64
EPS = 1e-6
ROPE_THETA = 10000.0
LOG2E = 1.4426950408889634

MLSTM_HEADS = 4
MLSTM_DV = 192
MLSTM_DQK = 96
MLSTM_QK_W = MLSTM_HEADS * MLSTM_DQK
MLSTM_V_W = MLSTM_HEADS * MLSTM_DV
GATE_SOFTCAP = 15.0
M_INIT = -1e30

SWA_Q_HEADS = 12
SWA_KV_HEADS = 4
SWA_GROUPS = SWA_Q_HEADS // SWA_KV_HEADS
SWA_Q_W = SWA_Q_HEADS * HEAD_DIM
SWA_KV_W = SWA_KV_HEADS * HEAD_DIM
SWA_WINDOW = 128

MEM_TOKENS = 256
MEM_HEADS = 4
MEM_HEAD_DIM = 64
MEM_W = MEM_HEADS * MEM_HEAD_DIM

D_FF = 2816

LANES = 128
MXU_TILE = 256

DQK_PAD = LANES
QK_PAD_W = MLSTM_HEADS * DQK_PAD
V_WIN = MXU_TILE
MLSTM_CHUNK = 256
MLSTM_CHUNKS_PER_STEP = 8
FF_CHUNK = MXU_TILE
N_FF_CHUNKS = D_FF // FF_CHUNK
PROJ_A_ROW_TILE = 1024
PROJ_B_ROW_TILE = 2048
PROJ_B_ROW_GROUP = 512
FFN_ROW_TILE = 1024
FFN_ROW_GROUP = 512
SWA_BLOCKS_PER_STEP = 16
VMEM_LIMIT = 56 * 1024 * 1024

_V_WIN_START = (0, 128, 384, 512)
_V_WIN_OFF = (0, 64, 0, 64)
_V_ONES_COL = (192, 0, 192, 0)


def _rms_hat(x):
    return x * lax.rsqrt(jnp.mean(x * x, axis=-1, keepdims=True) + EPS)


def _rms(x, g):
    return _rms_hat(x) * g


def _dot(a, b):
    return jnp.dot(a, b, preferred_element_type=F32)


def _dot_nt(a, b):
    return lax.dot_general(a, b, (((1,), (1,)), ((), ())), preferred_element_type=F32)


def _dot_tn(a, b):
    return lax.dot_general(a, b, (((0,), (0,)), ((), ())), preferred_element_type=F32)


def _sigmoid(x):
    return 1.0 / (1.0 + jnp.exp(-x))


def _resident(shape):
    nd = len(shape)
    return pl.BlockSpec(shape, lambda *_: (0,) * nd, pipeline_mode=pl.Buffered(1))


def _mem_kv_kernel(mem_ref, g_ref, w_ref, mk_ref, mv_ref):
    B = mem_ref.shape[0]
    hn = _rms(mem_ref[...].reshape(B * MEM_TOKENS, D_MODEL), g_ref[0]).astype(BF16)
    kv = _dot(hn, w_ref[0])
    mk = kv[:, :MEM_W] * (MEM_HEAD_DIM ** -0.5 * LOG2E)
    mv = kv[:, MEM_W:]
    lane_head = lax.broadcasted_iota(jnp.int32, mk.shape, 1) >> 6
    for h in range(MEM_HEADS):
        sel = lane_head == h
        mk_ref[0, :, h] = jnp.where(sel, mk, 0.0).astype(BF16).reshape(B, MEM_TOKENS, MEM_W)
        mv_ref[0, :, h] = jnp.where(sel, mv, 0.0).astype(BF16).reshape(B, MEM_TOKENS, MEM_W)


def _mem_kv(mem, g_mem, w_mem_kv_bf16):
    B = mem.shape[0]
    out_sds = jax.ShapeDtypeStruct((DEPTH, B, MEM_HEADS, MEM_TOKENS, MEM_W), BF16)
    out_spec = pl.BlockSpec((1, B, MEM_HEADS, MEM_TOKENS, MEM_W), lambda l: (l, 0, 0, 0, 0))
    return pl.pallas_call(
        _mem_kv_kernel,
        grid=(DEPTH,),
        in_specs=[
            pl.BlockSpec((B, MEM_TOKENS, D_MODEL), lambda l: (0, 0, 0)),
            pl.BlockSpec((1, 1, D_MODEL), lambda l: (l, 0, 0)),
            pl.BlockSpec((1, D_MODEL, 2 * MEM_W), lambda l: (l, 0, 0)),
        ],
        out_specs=[out_spec, out_spec],
        out_shape=[out_sds, out_sds],
        compiler_params=pltpu.CompilerParams(dimension_semantics=("arbitrary",),
                                             vmem_limit_bytes=VMEM_LIMIT),
        name="mem_kv",
    )(mem, g_mem.reshape(DEPTH, 1, D_MODEL), w_mem_kv_bf16)


def _mem_scores(mq, mk_ref):
    return [_dot_nt(mq, mk_ref[0, 0, h]) for h in range(MEM_HEADS)]


def _mem_probs(s):
    e = [jnp.exp2(sh - jnp.max(sh, axis=-1, keepdims=True)) for sh in s]
    return [(eh * (1.0 / jnp.sum(eh, axis=-1, keepdims=True))).astype(BF16) for eh in e]


def _mem_out(p, mv_ref):
    out = _dot(p[0], mv_ref[0, 0, 0])
    for h in range(1, MEM_HEADS):
        out = out + _dot(p[h], mv_ref[0, 0, h])
    return out


def _proj_a_kernel(x_ref, wq_ref, wk_ref, wv_ref, wo_ref, wmq_ref, wg_ref, bg_ref, mk_ref, mv_ref, pos_ref, inv_ref,
                   q_ref, k_ref, v_ref, o_ref, gt_ref, gtt_ref, mo_ref, cos_ref, sin_ref):
    cos_ref[...], sin_ref[...] = _rope_tables(pos_ref, inv_ref)
    hn = _rms_hat(x_ref[...]).astype(BF16)
    mq = _dot(hn, wmq_ref[...]).astype(BF16)
    s = _mem_scores(mq, mk_ref)
    gates = _dot(hn, wg_ref[...]) + bg_ref[...]
    q_ref[...] = _dot(hn, wq_ref[...]).astype(BF16)
    gates = GATE_SOFTCAP * jnp.tanh(gates * (1.0 / GATE_SOFTCAP))
    log_sig = jnp.minimum(gates, 0.0) - jnp.log1p(jnp.exp(-jnp.abs(gates)))
    lane = lax.broadcasted_iota(jnp.int32, gates.shape, 1)
    gl = jnp.where(lane < MLSTM_HEADS, gates, log_sig) * LOG2E
    p = _mem_probs(s)
    k_ref[...] = _dot(hn, wk_ref[...]).astype(BF16)
    g_hi = gl.astype(BF16)
    rest = gl - g_hi.astype(F32)
    g_mid = rest.astype(BF16)
    g_lo = (rest - g_mid.astype(F32)).astype(BF16)
    L = MLSTM_CHUNK
    tri = (lax.broadcasted_iota(jnp.int32, (L, L), 1) <= lax.broadcasted_iota(jnp.int32, (L, L), 0)).astype(BF16)
    csum = jnp.concatenate(
        [_dot(tri, g_hi[r:r + L]) + _dot(tri, g_mid[r:r + L]) + _dot(tri, g_lo[r:r + L])
         for r in range(0, gl.shape[0], L)], axis=0)
    v_ref[...] = _dot_nt(wv_ref[...], hn).astype(BF16)
    y = jnp.where(lane < MLSTM_HEADS, gl, csum)
    gt_ref[...] = y
    gtt_ref[...] = y.T[0:2 * MLSTM_HEADS, :]
    mo_ref[...] = _mem_out(p, mv_ref).astype(BF16)
    o_ref[...] = _sigmoid(_dot_nt(wo_ref[...], hn)).astype(BF16)


def _proj_a(x2d, wq, wk, wv, wo, wmq, wg, bg, mkm, mvm, pos, inv, rows_per_batch):
    N = x2d.shape[0]
    tm = PROJ_A_ROW_TILE
    steps_per_batch = rows_per_batch // tm
    row = lambda w: pl.BlockSpec((tm, w), lambda i: (i, 0))
    col = pl.BlockSpec((MLSTM_V_W, tm), lambda i: (0, i))
    mem_spec = pl.BlockSpec((1, 1, MEM_HEADS, MEM_TOKENS, MEM_W),
                            lambda i: (0, i // steps_per_batch, 0, 0, 0))
    return pl.pallas_call(
        _proj_a_kernel,
        grid=(N // tm,),
        in_specs=[row(D_MODEL),
                  _resident(wq.shape), _resident(wk.shape), _resident(wv.shape), _resident(wo.shape),
                  _resident(wmq.shape), _resident(wg.shape), _resident((1, LANES)),
                  mem_spec, mem_spec,
                  pl.BlockSpec((tm // (LANES // (HEAD_DIM // 2)), LANES), lambda i: (i, 0)), _resident((1, LANES))],
        out_specs=[row(QK_PAD_W), row(QK_PAD_W), col, col, row(LANES),
                   pl.BlockSpec((2 * MLSTM_HEADS, tm), lambda i: (0, i)), row(MEM_W), row(LANES), row(LANES)],
        out_shape=[jax.ShapeDtypeStruct((N, QK_PAD_W), BF16), jax.ShapeDtypeStruct((N, QK_PAD_W), BF16),
                   jax.ShapeDtypeStruct((MLSTM_V_W, N), BF16), jax.ShapeDtypeStruct((MLSTM_V_W, N), BF16),
                   jax.ShapeDtypeStruct((N, LANES), F32), jax.ShapeDtypeStruct((2 * MLSTM_HEADS, N), F32),
                   jax.ShapeDtypeStruct((N, MEM_W), BF16),
                   jax.ShapeDtypeStruct((N, LANES), F32), jax.ShapeDtypeStruct((N, LANES), F32)],
        compiler_params=pltpu.CompilerParams(dimension_semantics=("arbitrary",),
                                             vmem_limit_bytes=VMEM_LIMIT),
        name="proj_a",
    )(x2d, wq, wk, wv, wo, wmq, wg, bg, mkm, mvm, pos, inv)


def _ffn_weight_specs(layer, steps, step_index):
    ri, rd = D_MODEL // steps, D_FF // steps
    assert ri * steps == D_MODEL and rd * steps == D_FF and ri % 16 == 0 and rd % 16 == 0
    src = lambda rows, w: pl.BlockSpec((None, rows, w), lambda *g: (layer, step_index(*g), 0))
    dst = lambda rows, w: pl.BlockSpec((rows, w), lambda *g: (step_index(*g), 0))
    return ([src(ri, 1), src(ri, 2 * D_FF), src(rd, D_MODEL)],
            [dst(ri, 2 * D_FF), dst(rd, D_MODEL)],
            [jax.ShapeDtypeStruct((D_MODEL, 2 * D_FF), BF16), jax.ShapeDtypeStruct((D_FF, D_MODEL), BF16)])


def _prepare_ffn_weights(g_ref, wi_ref, wd_ref, wi_out_ref, wd_out_ref):
    wi_out_ref[...] = (g_ref[...] * wi_ref[...]).astype(BF16)
    wd_out_ref[...] = wd_ref[...].astype(BF16)


def _mlstm_kernel(q_ref, k_ref, vt_ref, gt_ref, gtt_ref, g_ref, wi_ref, wd_ref,
                  hmt_ref, wi_out_ref, wd_out_ref, c_scr, m_scr):
    _prepare_ffn_weights(g_ref, wi_ref, wd_ref, wi_out_ref, wd_out_ref)
    L = MLSTM_CHUNK

    @pl.when(pl.program_id(1) == 0)
    def _():
        c_scr[...] = jnp.zeros(c_scr.shape, F32)
        m_scr[...] = jnp.full(m_scr.shape, M_INIT, F32)

    causal = lax.broadcasted_iota(jnp.int32, (L, L), 0) <= lax.broadcasted_iota(jnp.int32, (L, L), 1)
    feat = lax.broadcasted_iota(jnp.int32, (V_WIN, L), 0)
    heads = range(MLSTM_HEADS)
    m_st = [m_scr[h][0:1, 0:1] for h in heads]
    c_st = [c_scr[h] for h in heads]

    for r0 in range(0, q_ref.shape[0], L):
        y = gt_ref[r0:r0 + L, :]
        yt = gtt_ref[:, r0:r0 + L]
        b_row = [yt[MLSTM_HEADS + h:MLSTM_HEADS + h + 1, :] for h in heads]
        c_row = [b_row[h] - yt[h:h + 1, :] for h in heads]
        c_col = [y[:, MLSTM_HEADS + h:MLSTM_HEADS + h + 1] - y[:, h:h + 1] for h in heads]
        g_tot = [y[L - 1:L, MLSTM_HEADS + h:MLSTM_HEADS + h + 1] for h in heads]
        qh = [q_ref[r0:r0 + L, h * DQK_PAD:(h + 1) * DQK_PAD] for h in heads]
        kh = [k_ref[r0:r0 + L, h * DQK_PAD:(h + 1) * DQK_PAD] for h in heads]
        vt_aug = [jnp.where(feat == _V_ONES_COL[h], jnp.ones((), BF16),
                            vt_ref[_V_WIN_START[h]:_V_WIN_START[h] + V_WIN, r0:r0 + L]) for h in heads]

        both = [_dot_nt(jnp.concatenate([kh[h], c_st[h].astype(BF16)], axis=0), qh[h]) for h in heads]
        qk_t = [t[:L] for t in both]
        inter_t = [t[L:] for t in both]
        dmat = [jnp.where(causal, b_row[h] - c_col[h], -jnp.inf) for h in heads]
        inter_log = [b_row[h] + m_st[h] for h in heads]
        m_row = [jnp.maximum(inter_log[h], jnp.max(dmat[h], axis=0, keepdims=True)) for h in heads]
        scores_t = [(qk_t[h] * jnp.exp2(dmat[h] - m_row[h])).astype(BF16) for h in heads]
        num_t = [_dot(vt_aug[h], scores_t[h]) + jnp.exp2(inter_log[h] - m_row[h]) * inter_t[h] for h in heads]

        a = [g_tot[h] - c_row[h] for h in heads]
        m_new = [jnp.maximum(g_tot[h] + m_st[h], jnp.max(a[h], axis=1, keepdims=True)) for h in heads]
        vtw = [vt_aug[h] * jnp.exp2(a[h] - m_new[h]).astype(BF16) for h in heads]
        c_upd = [_dot(vtw[h], kh[h]) for h in heads]

        for h in heads:
            off, ones_row, lo = _V_WIN_OFF[h], _V_ONES_COL[h], h * MLSTM_DV
            den = num_t[h][ones_row:ones_row + 1, :]
            inv_dd = 1.0 / jnp.maximum(jnp.abs(den), jnp.exp2(-m_row[h]))
            h_t = num_t[h][off:off + MLSTM_DV, :]
            ms = jnp.sum(h_t * h_t, axis=0, keepdims=True) * (1.0 / MLSTM_DV)
            scale = inv_dd * lax.rsqrt(ms * inv_dd * inv_dd + EPS)
            hmt_ref[lo:lo + MLSTM_DV, r0:r0 + L] = (h_t * scale).astype(BF16)

        c_st = [jnp.exp2(g_tot[h] + m_st[h] - m_new[h]) * c_st[h] + c_upd[h] for h in heads]
        m_st = m_new

    for h in heads:
        c_scr[h] = c_st[h]
        m_scr[h] = jnp.broadcast_to(m_st[h], m_scr.shape[1:])


def _mlstm(q, k, vt, gt, gtt, g_ffn, w_ffn_in, w_ffn_out, B, S):
    rows = MLSTM_CHUNK * MLSTM_CHUNKS_PER_STEP
    steps = S // rows
    N = B * S
    row = lambda w: pl.BlockSpec((rows, w), lambda b, c: (b * steps + c, 0))
    col = lambda h: pl.BlockSpec((h, rows), lambda b, c: (0, b * steps + c))
    w_in, w_out, w_shape = _ffn_weight_specs(0, B * steps, lambda b, c: b * steps + c)
    return pl.pallas_call(
        _mlstm_kernel,
        grid=(B, steps),
        in_specs=[row(QK_PAD_W), row(QK_PAD_W), col(MLSTM_V_W), row(LANES), col(2 * MLSTM_HEADS)] + w_in,
        out_specs=[col(MLSTM_V_W)] + w_out,
        out_shape=[jax.ShapeDtypeStruct((MLSTM_V_W, N), BF16)] + w_shape,
        scratch_shapes=[pltpu.VMEM((MLSTM_HEADS, V_WIN, DQK_PAD), F32),
                        pltpu.VMEM((MLSTM_HEADS, 8, LANES), F32)],
        compiler_params=pltpu.CompilerParams(dimension_semantics=("arbitrary", "arbitrary"),
                                             vmem_limit_bytes=VMEM_LIMIT),
        name="mlstm",
    )(q, k, vt, gt, gtt, g_ffn, w_ffn_in, w_ffn_out)


def _mix_ffn_core(x_ref, main_ref, gate_ref, mo_ref, wo_ref, gpost_ref, gfpost_ref, wi_ref, wd_ref):
    mix_w = D_MODEL - MEM_W
    groups = [slice(r, r + FFN_ROW_GROUP) for r in range(0, x_ref.shape[0], FFN_ROW_GROUP)]
    if gate_ref is None:
        main = [_dot(main_ref[g, :], wo_ref[0:mix_w, :]) for g in groups]
    else:
        main = [_dot_tn(main_ref[:, g] * gate_ref[:, g], wo_ref[0:mix_w, :]) for g in groups]
    mix = [m + _dot(mo_ref[g, :], wo_ref[mix_w:, :]) for g, m in zip(groups, main)]
    x1 = [x_ref[g, :] + _rms(m, gpost_ref[...]) for g, m in zip(groups, mix)]
    hf = [_rms_hat(t).astype(BF16) for t in x1]
    act = [[] for _ in groups]
    for c in range(N_FF_CHUNKS):
        lo = c * FF_CHUNK
        for i, h in enumerate(hf):
            g = _dot(h, wi_ref[:, lo:lo + FF_CHUNK])
            u = _dot(h, wi_ref[:, D_FF + lo:D_FF + lo + FF_CHUNK])
            act[i].append((g * _sigmoid(g) * u).astype(BF16))
    acc = [_dot(jnp.concatenate(a, axis=1), wd_ref[...]) for a in act]
    return jnp.concatenate([t + _rms(a, gfpost_ref[...]) for t, a in zip(x1, acc)], axis=0)


def _mix_ffn_kernel(x_ref, main_ref, mo_ref, wo_ref, gpost_ref, gfpost_ref, wi_ref, wd_ref, xo_ref):
    xo_ref[...] = _mix_ffn_core(x_ref, main_ref, None, mo_ref, wo_ref, gpost_ref, gfpost_ref, wi_ref, wd_ref)


def _mix_ffn_gated_kernel(x_ref, main_ref, gate_ref, mo_ref, wo_ref, gpost_ref, gfpost_ref, wi_ref, wd_ref, xo_ref):
    xo_ref[...] = _mix_ffn_core(x_ref, main_ref, gate_ref, mo_ref, wo_ref, gpost_ref, gfpost_ref, wi_ref, wd_ref)


def _rope(t1, t2, cos, sin):
    return t1 * cos - t2 * sin, t2 * cos + t1 * sin


def _rope_tables(pos_ref, inv_ref):
    ang = pos_ref[...].astype(F32) * inv_ref[...]
    cos_d = jnp.cos(ang)
    sin_d = jnp.sin(ang)
    grp = lax.broadcasted_iota(jnp.int32, ang.shape, 1) >> 5

    def spread(t, g):
        y = jnp.where(grp == g, t, 0.0)
        y = y + pltpu.roll(y, HEAD_DIM // 2, 1)
        return y + pltpu.roll(y, HEAD_DIM, 1)

    n_grp = LANES // (HEAD_DIM // 2)
    cos = jnp.concatenate([spread(cos_d, g) for g in range(n_grp)], axis=0)
    sin = jnp.concatenate([spread(sin_d, g) for g in range(n_grp)], axis=0)
    return cos, sin


def _mix_ffn(layer, x2d, main, mo, wo, g_post, g_fpost, wi, wd, gate=None):
    N = x2d.shape[0]
    tm = FFN_ROW_TILE
    row = lambda w: pl.BlockSpec((tm, w), lambda i: (i, 0))
    if gate is None:
        body, main_specs, mains = _mix_ffn_kernel, [row(main.shape[1])], [main]
    else:
        col = pl.BlockSpec((main.shape[0], tm), lambda i: (0, i))
        body, main_specs, mains = _mix_ffn_gated_kernel, [col, col], [main, gate]
    return pl.pallas_call(
        body,
        grid=(N // tm,),
        in_specs=[row(D_MODEL)] + main_specs + [row(MEM_W),
                  _resident(wo.shape),
                  _resident((1, D_MODEL)), _resident((1, D_MODEL)),
                  _resident(wi.shape), _resident(wd.shape)],
        out_specs=row(D_MODEL),
        out_shape=jax.ShapeDtypeStruct((N, D_MODEL), F32),
        compiler_params=pltpu.CompilerParams(dimension_semantics=("arbitrary",),
                                             vmem_limit_bytes=VMEM_LIMIT),
        name="mix_ffn_%d" % layer,
    )(x2d, *mains, mo, wo, g_post, g_fpost, wi, wd)


def _proj_b_kernel(x_ref, cos_ref, sin_ref, wkv_ref, wb_ref, mk_ref, mv_ref,
                   ks_ref, vs_ref, q1_ref, mo1_ref):
    groups = [slice(r, r + PROJ_B_ROW_GROUP) for r in range(0, x_ref.shape[0], PROJ_B_ROW_GROUP)]
    xn = [_rms_hat(x_ref[g, :]).astype(BF16) for g in groups]
    pb = [_dot(t, wb_ref[...]) for t in xn]
    s = [_mem_scores(t[:, SWA_Q_W:].astype(BF16), mk_ref) for t in pb]
    cos, sin = cos_ref[...], sin_ref[...]
    kv = [_dot(t, wkv_ref[...]) for t in xn]
    p = [_mem_probs(t) for t in s]

    for g, t in zip(groups, pb):
        for gi in range(SWA_GROUPS):
            base = gi * 2 * LANES
            t1, t2 = _rope(t[:, base:base + LANES], t[:, base + LANES:base + 2 * LANES], cos[g], sin[g])
            q1_ref[g, base:base + LANES] = t1.astype(BF16)
            q1_ref[g, base + LANES:base + 2 * LANES] = t2.astype(BF16)
    for g, t in zip(groups, p):
        mo1_ref[g, :] = _mem_out(t, mv_ref).astype(BF16)

    scale = HEAD_DIM ** -0.5 * LOG2E
    for g, t in zip(groups, kv):
        k1, k2 = _rope(t[:, 0:LANES], t[:, LANES:2 * LANES], cos[g], sin[g])
        ks_ref[g, 0:LANES] = (k1 * scale).astype(BF16)
        ks_ref[g, LANES:2 * LANES] = (k2 * scale).astype(BF16)
        vs_ref[g, :] = t[:, SWA_KV_W:].astype(BF16)


def _proj_b(x2d, cos, sin, wkv, wb, mkm, mvm, rows_per_batch):
    N = x2d.shape[0]
    tm = PROJ_B_ROW_TILE
    steps_per_batch = rows_per_batch // tm
    row = lambda w: pl.BlockSpec((tm, w), lambda i: (i, 0))
    mem_spec = pl.BlockSpec((1, 1, MEM_HEADS, MEM_TOKENS, MEM_W),
                            lambda i: (1, i // steps_per_batch, 0, 0, 0))
    return pl.pallas_call(
        _proj_b_kernel,
        grid=(N // tm,),
        in_specs=[row(D_MODEL), row(LANES), row(LANES), _resident(wkv.shape), _resident(wb.shape),
                  mem_spec, mem_spec],
        out_specs=[row(SWA_KV_W), row(SWA_KV_W), row(SWA_Q_W), row(MEM_W)],
        out_shape=[jax.ShapeDtypeStruct((N, SWA_KV_W), BF16), jax.ShapeDtypeStruct((N, SWA_KV_W), BF16),
                   jax.ShapeDtypeStruct((N, SWA_Q_W), BF16), jax.ShapeDtypeStruct((N, MEM_W), BF16)],
        compiler_params=pltpu.CompilerParams(dimension_semantics=("arbitrary",),
                                             vmem_limit_bytes=VMEM_LIMIT),
        name="proj_b",
    )(x2d, cos, sin, wkv, wb, mkm, mvm)


def _swa_kernel(sink_ref, q_ref, kp_ref, kc_ref, vp_ref, vc_ref, g_ref, wi_ref, wd_ref,
                o_ref, wi_out_ref, wd_out_ref):
    _prepare_ffn_weights(g_ref, wi_ref, wd_ref, wi_out_ref, wd_out_ref)
    W = SWA_WINDOW
    G = SWA_GROUPS
    n = pl.program_id(1)
    n_sub = q_ref.shape[0] // W
    H = SWA_KV_HEADS
    kpos = lax.broadcasted_iota(jnp.int32, (2 * W, G * W), 0)
    qcol = lax.broadcasted_iota(jnp.int32, (2 * W, G * W), 1)
    diff = (qcol & (W - 1)) + W - kpos
    in_band = (diff >= 0) & (diff < W)
    k_lane = lax.broadcasted_iota(jnp.int32, (2 * W, SWA_KV_W), 1)
    sink = sink_ref[...]
    subs = range(n_sub)
    q, km, vm, valid = [], [], [], []
    for sub in subs:
        q.append(jnp.concatenate([q_ref[sub * W:(sub + 1) * W, gi * 2 * LANES:(gi + 1) * 2 * LANES]
                                  for gi in range(G)], axis=0))
        if sub == 0:
            k = jnp.concatenate([kp_ref[...], kc_ref[0:W, :]], axis=0)
            v = jnp.concatenate([vp_ref[...], vc_ref[0:W, :]], axis=0)
            valid.append(in_band & ((kpos >= W) | (n > 0)))
        else:
            k = kc_ref[(sub - 1) * W:(sub + 1) * W, :]
            v = vc_ref[(sub - 1) * W:(sub + 1) * W, :]
            valid.append(in_band)
        km.append(jnp.concatenate([jnp.where(((k_lane & (LANES - 1)) >> 5) == j, k, jnp.zeros((), BF16))
                                   for j in range(H)], axis=0))
        vm.append(jnp.concatenate([jnp.where((k_lane >> 6) == j, v, jnp.zeros((), BF16))
                                   for j in range(H)], axis=0))
    s = [_dot_nt(km[i], q[i]).reshape(H, 2 * W, G * W) for i in subs]
    s = [jnp.where(valid[i][None], s[i], -jnp.inf) for i in subs]
    m = [jnp.maximum(jnp.max(s[i], axis=1, keepdims=True), sink) for i in subs]
    e = [jnp.exp2(s[i] - m[i]) for i in subs]
    inv = [1.0 / (jnp.sum(e[i], axis=1, keepdims=True) + jnp.exp2(sink - m[i])) for i in subs]
    p = [e[i].astype(BF16) * inv[i].astype(BF16) for i in subs]
    out = [_dot_tn(p[i].reshape(H * 2 * W, G * W), vm[i]) for i in subs]
    for sub in subs:
        for gi in range(G):
            o_ref[sub * W:(sub + 1) * W, gi * 2 * LANES:(gi + 1) * 2 * LANES] = (
                out[sub][gi * W:(gi + 1) * W].astype(BF16))


def _swa(sink_cols, q1, ks, vs, g_ffn, w_ffn_in, w_ffn_out, B, S):
    W = SWA_WINDOW
    n_sub = SWA_BLOCKS_PER_STEP
    tq = n_sub * W
    steps = S // tq
    cur = lambda w: pl.BlockSpec((tq, w), lambda b, n: (b * steps + n, 0))
    prev = lambda w: pl.BlockSpec((W, w), lambda b, n: (b * (S // W) + jnp.maximum(n_sub * n - 1, 0), 0))
    w_in, w_out, w_shape = _ffn_weight_specs(1, B * steps, lambda b, n: b * steps + n)
    return pl.pallas_call(
        _swa_kernel,
        grid=(B, steps),
        in_specs=[_resident(sink_cols.shape),
                  cur(SWA_Q_W), prev(SWA_KV_W), cur(SWA_KV_W), prev(SWA_KV_W), cur(SWA_KV_W)] + w_in,
        out_specs=[cur(SWA_Q_W)] + w_out,
        out_shape=[jax.ShapeDtypeStruct((B * S, SWA_Q_W), BF16)] + w_shape,
        compiler_params=pltpu.CompilerParams(dimension_semantics=("arbitrary", "arbitrary"),
                                             vmem_limit_bytes=VMEM_LIMIT),
        name="swa",
    )(sink_cols, q1, ks, ks, vs, vs, g_ffn, w_ffn_in, w_ffn_out)


def _pad_heads(w, heads, width, padded):
    w = w.reshape(w.shape[0], heads, width)
    return jnp.pad(w, ((0, 0), (0, 0), (0, padded - width))).reshape(w.shape[0], heads * padded)


def _with_tail(perm, total):
    return np.concatenate([perm, np.arange(len(perm), total, dtype=np.int32)])


def _swa_q_perm():
    idx = np.empty((SWA_GROUPS, 2, SWA_KV_HEADS, HEAD_DIM // 2), np.int32)
    for gi in range(SWA_GROUPS):
        for half in range(2):
            for j in range(SWA_KV_HEADS):
                idx[gi, half, j] = (j * SWA_GROUPS + gi) * HEAD_DIM + half * (HEAD_DIM // 2) + np.arange(HEAD_DIM // 2)
    return idx.reshape(-1)


def _swa_k_perm():
    idx = np.empty((2, SWA_KV_HEADS, HEAD_DIM // 2), np.int32)
    for half in range(2):
        for j in range(SWA_KV_HEADS):
            idx[half, j] = j * HEAD_DIM + half * (HEAD_DIM // 2) + np.arange(HEAD_DIM // 2)
    return idx.reshape(-1)


def _swa_out_perm():
    idx = np.empty((SWA_GROUPS, SWA_KV_HEADS, HEAD_DIM), np.int32)
    for gi in range(SWA_GROUPS):
        for j in range(SWA_KV_HEADS):
            idx[gi, j] = (j * SWA_GROUPS + gi) * HEAD_DIM + np.arange(HEAD_DIM)
    return idx.reshape(-1)


def kernel(x, mem, positions, g_mix_pre, g_mix_post, g_ffn_pre, g_ffn_post, g_mem, w_mem_kv, w_out,
           w_ffn_in, w_ffn_out, w_in_a, b_gates_a, g_mlstm_out, g_kv, w_kv, w_in_b, sinks_b):
    B, S, _ = x.shape
    N = B * S
    assert S % PROJ_A_ROW_TILE == 0 and S % PROJ_B_ROW_TILE == 0 and S % FFN_ROW_TILE == 0
    assert S % (MLSTM_CHUNK * MLSTM_CHUNKS_PER_STEP) == 0
    assert S % (SWA_WINDOW * SWA_BLOCKS_PER_STEP) == 0
    x2d = x.reshape(N, D_MODEL)
    vec = lambda g: g.reshape(1, -1).astype(F32)

    mkm, mvm = _mem_kv(mem, g_mem, w_mem_kv.astype(BF16))

    fold = lambda g, w: g.astype(F32)[:, None] * w

    piece = lambda c0, width: fold(g_mix_pre[0], w_in_a[0][:, c0:c0 + width])
    c0 = 0
    wq = _pad_heads(piece(c0, MLSTM_QK_W), MLSTM_HEADS, MLSTM_DQK, DQK_PAD).astype(BF16)
    c0 += MLSTM_QK_W
    wk = _pad_heads(piece(c0, MLSTM_QK_W) * (MLSTM_DQK ** -0.5), MLSTM_HEADS, MLSTM_DQK, DQK_PAD).astype(BF16)
    c0 += MLSTM_QK_W
    wv = piece(c0, MLSTM_V_W).astype(BF16).T
    c0 += MLSTM_V_W
    wo = piece(c0, MLSTM_V_W).astype(BF16).T
    c0 += MLSTM_V_W
    wgt = jnp.pad(piece(c0, 2 * MLSTM_HEADS), ((0, 0), (0, LANES - 2 * MLSTM_HEADS))).astype(BF16)
    c0 += 2 * MLSTM_HEADS
    wmq = piece(c0, MEM_W).astype(BF16)
    bg = jnp.pad(b_gates_a[0].astype(F32), (0, LANES - 2 * MLSTM_HEADS)).reshape(1, LANES)

    inv = 1.0 / (ROPE_THETA ** (jnp.arange(0, HEAD_DIM, 2, dtype=F32) / HEAD_DIM))
    inv = jnp.tile(inv, SWA_KV_HEADS).reshape(1, LANES)
    n_grp = LANES // (HEAD_DIM // 2)
    pos_dense = positions.reshape(N // PROJ_A_ROW_TILE, n_grp, PROJ_A_ROW_TILE // n_grp).transpose(0, 2, 1)
    pos_dense = jnp.repeat(pos_dense, HEAD_DIM // 2, axis=2).reshape(N // n_grp, LANES)
    q, k, v, o, gt, gtt, mo0, cos, sin = _proj_a(x2d, wq, wk, wv, wo, wmq, wgt, bg, mkm, mvm, pos_dense, inv, S)
    g_ffn = g_ffn_pre.astype(F32)[:, :, None]
    hm, wi0, wd0 = _mlstm(q, k, v, gt, gtt, g_ffn, w_ffn_in, w_ffn_out, B, S)

    wo0 = fold(jnp.concatenate([g_mlstm_out[0].astype(F32), jnp.ones((MEM_W,), F32)]), w_out[0]).astype(BF16)
    wkv = fold(g_kv, w_kv)[:, _with_tail(_swa_k_perm(), 2 * SWA_KV_W)].astype(BF16)
    wb = fold(g_mix_pre[1], w_in_b[0])[:, _with_tail(_swa_q_perm(), SWA_Q_W + MEM_W)].astype(BF16)
    x1 = _mix_ffn(0, x2d, hm, mo0, wo0, vec(g_mix_post[0]), vec(g_ffn_post[0]),
                  wi0, wd0, gate=o)
    ks, vs, q1, mo1 = _proj_b(x1, cos, sin, wkv, wb, mkm, mvm, S)

    sink_rows = jnp.repeat((sinks_b[0].astype(F32) * LOG2E).reshape(SWA_KV_HEADS, SWA_GROUPS), SWA_WINDOW, axis=1)
    attn, wi1, wd1 = _swa(sink_rows.reshape(SWA_KV_HEADS, 1, SWA_GROUPS * SWA_WINDOW), q1, ks, vs,
                          g_ffn, w_ffn_in, w_ffn_out, B, S)
    wo1 = w_out[1][_with_tail(_swa_out_perm(), D_MODEL)].astype(BF16)
    xo = _mix_ffn(1, x1, attn, mo1, wo1, vec(g_mix_post[1]), vec(g_ffn_post[1]),
                  wi1, wd1)
    return xo.reshape(B, S, D_MODEL)
```

```python
import numpy as np
import jax
import jax.numpy as jnp
from jax import lax
from jax.experimental import pallas as pl
from jax.experimental.pallas import tpu as pltpu

F32 = jnp.float32
BF16 = jnp.bfloat16

D_MODEL = 1024
DEPTH = 2
HEAD_DIM = 64
EPS = 1e-6
ROPE_THETA = 10000.0
LOG2E = 1.4426950408889634

MLSTM_HEADS = 4
MLSTM_DV = 192
MLSTM_DQK = 96
MLSTM_QK_W = MLSTM_HEADS * MLSTM_DQK
MLSTM_V_W = MLSTM_HEADS * MLSTM_DV
GATE_SOFTCAP = 15.0
M_INIT = -1e30

SWA_Q_HEADS = 12
SWA_KV_HEADS = 4
SWA_GROUPS = SWA_Q_HEADS // SWA_KV_HEADS
SWA_Q_W = SWA_Q_HEADS * HEAD_DIM
SWA_KV_W = SWA_KV_HEADS * HEAD_DIM
SWA_WINDOW = 128

MEM_TOKENS = 256
MEM_HEADS = 4
MEM_HEAD_DIM = 64
MEM_W = MEM_HEADS * MEM_HEAD_DIM

D_FF = 2816

LANES = 128
MXU_TILE = 256

DQK_PAD = LANES
QK_PAD_W = MLSTM_HEADS * DQK_PAD
V_WIN = MXU_TILE
MLSTM_CHUNK = 256
MLSTM_CHUNKS_PER_STEP = 8
FF_CHUNK = MXU_TILE
N_FF_CHUNKS = D_FF // FF_CHUNK
PROJ_A_ROW_TILE = 1024
PROJ_B_ROW_TILE = 2048
PROJ_B_ROW_GROUP = 512
FFN_ROW_TILE = 1024
FFN_ROW_GROUP = 512
SWA_BLOCKS_PER_STEP = 16
VMEM_LIMIT = 56 * 1024 * 1024

_V_WIN_START = (0, 128, 384, 512)
_V_WIN_OFF = (0, 64, 0, 64)
_V_ONES_COL = (192, 0, 192, 0)


def _rms_hat(x):
    return x * lax.rsqrt(jnp.mean(x * x, axis=-1, keepdims=True) + EPS)


def _rms(x, g):
    return _rms_hat(x) * g


def _dot(a, b):
    return jnp.dot(a, b, preferred_element_type=F32)


def _dot_nt(a, b):
    return lax.dot_general(a, b, (((1,), (1,)), ((), ())), preferred_element_type=F32)


def _dot_tn(a, b):
    return lax.dot_general(a, b, (((0,), (0,)), ((), ())), preferred_element_type=F32)


def _sigmoid(x):
    return 1.0 / (1.0 + jnp.exp(-x))


def _resident(shape):
    nd = len(shape)
    return pl.BlockSpec(shape, lambda *_: (0,) * nd, pipeline_mode=pl.Buffered(1))


def _mem_kv_kernel(mem_ref, g_ref, w_ref, mk_ref, mv_ref):
    B = mem_ref.shape[0]
    hn = _rms(mem_ref[...].reshape(B * MEM_TOKENS, D_MODEL), g_ref[0]).astype(BF16)
    kv = _dot(hn, w_ref[0])
    mk = kv[:, :MEM_W] * (MEM_HEAD_DIM ** -0.5 * LOG2E)
    mv = kv[:, MEM_W:]
    lane_head = lax.broadcasted_iota(jnp.int32, mk.shape, 1) >> 6
    for h in range(MEM_HEADS):
        sel = lane_head == h
        mk_ref[0, :, h] = jnp.where(sel, mk, 0.0).astype(BF16).reshape(B, MEM_TOKENS, MEM_W)
        mv_ref[0, :, h] = jnp.where(sel, mv, 0.0).astype(BF16).reshape(B, MEM_TOKENS, MEM_W)


def _mem_kv(mem, g_mem, w_mem_kv_bf16):
    B = mem.shape[0]
    out_sds = jax.ShapeDtypeStruct((DEPTH, B, MEM_HEADS, MEM_TOKENS, MEM_W), BF16)
    out_spec = pl.BlockSpec((1, B, MEM_HEADS, MEM_TOKENS, MEM_W), lambda l: (l, 0, 0, 0, 0))
    return pl.pallas_call(
        _mem_kv_kernel,
        grid=(DEPTH,),
        in_specs=[
            pl.BlockSpec((B, MEM_TOKENS, D_MODEL), lambda l: (0, 0, 0)),
            pl.BlockSpec((1, 1, D_MODEL), lambda l: (l, 0, 0)),
            pl.BlockSpec((1, D_MODEL, 2 * MEM_W), lambda l: (l, 0, 0)),
        ],
        out_specs=[out_spec, out_spec],
        out_shape=[out_sds, out_sds],
        compiler_params=pltpu.CompilerParams(dimension_semantics=("arbitrary",),
                                             vmem_limit_bytes=VMEM_LIMIT),
        name="mem_kv",
    )(mem, g_mem.reshape(DEPTH, 1, D_MODEL), w_mem_kv_bf16)


def _mem_scores(mq, mk_ref):
    return [_dot_nt(mq, mk_ref[0, 0, h]) for h in range(MEM_HEADS)]


def _mem_probs(s):
    e = [jnp.exp2(sh - jnp.max(sh, axis=-1, keepdims=True)) for sh in s]
    return [(eh * (1.0 / jnp.sum(eh, axis=-1, keepdims=True))).astype(BF16) for eh in e]


def _mem_out(p, mv_ref):
    out = _dot(p[0], mv_ref[0, 0, 0])
    for h in range(1, MEM_HEADS):
        out = out + _dot(p[h], mv_ref[0, 0, h])
    return out


def _proj_a_kernel(x_ref, wq_ref, wk_ref, wv_ref, wo_ref, wmq_ref, wg_ref, bg_ref, mk_ref, mv_ref, pos_ref, inv_ref,
                   q_ref, k_ref, v_ref, o_ref, gt_ref, gtt_ref, mo_ref, cos_ref, sin_ref):
    cos_ref[...], sin_ref[...] = _rope_tables(pos_ref, inv_ref)
    hn = _rms_hat(x_ref[...]).astype(BF16)
    mq = _dot(hn, wmq_ref[...]).astype(BF16)
    s = _mem_scores(mq, mk_ref)
    gates = _dot(hn, wg_ref[...]) + bg_ref[...]
    q_ref[...] = _dot(hn, wq_ref[...]).astype(BF16)
    gates = GATE_SOFTCAP * jnp.tanh(gates * (1.0 / GATE_SOFTCAP))
    log_sig = jnp.minimum(gates, 0.0) - jnp.log1p(jnp.exp(-jnp.abs(gates)))
    lane = lax.broadcasted_iota(jnp.int32, gates.shape, 1)
    gl = jnp.where(lane < MLSTM_HEADS, gates, log_sig) * LOG2E
    p = _mem_probs(s)
    k_ref[...] = _dot(hn, wk_ref[...]).astype(BF16)
    g_hi = gl.astype(BF16)
    rest = gl - g_hi.astype(F32)
    g_mid = rest.astype(BF16)
    g_lo = (rest - g_mid.astype(F32)).astype(BF16)
    L = MLSTM_CHUNK
    tri = (lax.broadcasted_iota(jnp.int32, (L, L), 1) <= lax.broadcasted_iota(jnp.int32, (L, L), 0)).astype(BF16)
    csum = jnp.concatenate(
        [_dot(tri, g_hi[r:r + L]) + _dot(tri, g_mid[r:r + L]) + _dot(tri, g_lo[r:r + L])
         for r in range(0, gl.shape[0], L)], axis=0)
    v_ref[...] = _dot_nt(wv_ref[...], hn).astype(BF16)
    y = jnp.where(lane < MLSTM_HEADS, gl, csum)
    gt_ref[...] = y
    gtt_ref[...] = y.T[0:2 * MLSTM_HEADS, :]
    mo_ref[...] = _mem_out(p, mv_ref).astype(BF16)
    o_ref[...] = _sigmoid(_dot_nt(wo_ref[...], hn)).astype(BF16)


def _proj_a(x2d, wq, wk, wvo, wmq, wg, bg, mkm, mvm, pos, inv, rows_per_batch):
    N = x2d.shape[0]
    tm = PROJ_A_ROW_TILE
    steps_per_batch = rows_per_batch // tm
    row = lambda w: pl.BlockSpec((tm, w), lambda i: (i, 0))
    col = pl.BlockSpec((MLSTM_V_W, tm), lambda i: (0, i))
    piece = lambda j: pl.BlockSpec((None,) + wvo.shape[1:], lambda i: (j, 0, 0), pipeline_mode=pl.Buffered(1))
    mem_spec =pl.BlockSpec((1, 1, MEM_HEADS, MEM_TOKENS, MEM_W),
                            lambda i: (0, i // steps_per_batch, 0, 0, 0))
    return pl.pallas_call(
        _proj_a_kernel,
        grid=(N // tm,),
        in_specs=[row(D_MODEL),
                  _resident(wq.shape), _resident(wk.shape), piece(0), piece(1),
                  _resident(wmq.shape), _resident(wg.shape), _resident((1, LANES)),
                  mem_spec, mem_spec,
                  pl.BlockSpec((tm // (LANES // (HEAD_DIM // 2)), LANES), lambda i: (i, 0)), _resident((1, LANES))],
        out_specs=[row(QK_PAD_W), row(QK_PAD_W), col, col, row(LANES),
                   pl.BlockSpec((2 * MLSTM_HEADS, tm), lambda i: (0, i)), row(MEM_W), row(LANES), row(LANES)],
        out_shape=[jax.ShapeDtypeStruct((N, QK_PAD_W), BF16), jax.ShapeDtypeStruct((N, QK_PAD_W), BF16),
                   jax.ShapeDtypeStruct((MLSTM_V_W, N), BF16), jax.ShapeDtypeStruct((MLSTM_V_W, N), BF16),
                   jax.ShapeDtypeStruct((N, LANES), F32), jax.ShapeDtypeStruct((2 * MLSTM_HEADS, N), F32),
                   jax.ShapeDtypeStruct((N, MEM_W), BF16),
                   jax.ShapeDtypeStruct((N, LANES), F32), jax.ShapeDtypeStruct((N, LANES), F32)],
        compiler_params=pltpu.CompilerParams(dimension_semantics=("arbitrary",),
                                             vmem_limit_bytes=VMEM_LIMIT),
        name="proj_a",
    )(x2d, wq, wk, wvo, wvo, wmq, wg, bg, mkm, mvm, pos, inv)


def _ffn_weight_specs(layer, steps, step_index):
    ri, rd = D_MODEL // steps, D_FF // steps
    assert ri * steps == D_MODEL and rd * steps == D_FF and ri % 16 == 0 and rd % 16 == 0
    src = lambda rows, w: pl.BlockSpec((None, rows, w), lambda *g: (layer, step_index(*g), 0))
    dst = lambda rows, w: pl.BlockSpec((rows, w), lambda *g: (step_index(*g), 0))
    return ([src(ri, 1), src(ri, 2 * D_FF), src(rd, D_MODEL)],
            [dst(ri, 2 * D_FF), dst(rd, D_MODEL)],
            [jax.ShapeDtypeStruct((D_MODEL, 2 * D_FF), BF16), jax.ShapeDtypeStruct((D_FF, D_MODEL), BF16)])


def _prepare_ffn_weights(g_ref, wi_ref, wd_ref, wi_out_ref, wd_out_ref):
    wi_out_ref[...] = (g_ref[...] * wi_ref[...]).astype(BF16)
    wd_out_ref[...] = wd_ref[...].astype(BF16)


def _transposed_weight_kernel(g_ref, w_ref, wt_ref):
    wt_ref[...] = (g_ref[...] * w_ref[...]).T.astype(BF16)


def _transposed_weights(g_col, w, first_col, width, count):
    rows = w.shape[0]
    first_block = first_col // width
    assert first_block * width == first_col and first_col + count * width <= w.shape[1]
    return pl.pallas_call(
        _transposed_weight_kernel,
        grid=(count,),
        in_specs=[pl.BlockSpec((rows, 1), lambda j: (0, 0)),
                  pl.BlockSpec((rows, width), lambda j: (0, first_block + j))],
        out_specs=pl.BlockSpec((None, width, rows), lambda j: (j, 0, 0)),
        out_shape=jax.ShapeDtypeStruct((count, width, rows), BF16),
        compiler_params=pltpu.CompilerParams(dimension_semantics=("arbitrary",),
                                             vmem_limit_bytes=VMEM_LIMIT),
        name="transposed_weights",
    )(g_col, w)


def _mlstm_kernel(q_ref, k_ref, vt_ref, gt_ref, gtt_ref, g_ref, wi_ref, wd_ref,
                  hmt_ref, wi_out_ref, wd_out_ref, c_scr, m_scr):
    _prepare_ffn_weights(g_ref, wi_ref, wd_ref, wi_out_ref, wd_out_ref)
    L = MLSTM_CHUNK

    @pl.when(pl.program_id(1) == 0)
    def _():
        c_scr[...] = jnp.zeros(c_scr.shape, F32)
        m_scr[...] = jnp.full(m_scr.shape, M_INIT, F32)

    causal = lax.broadcasted_iota(jnp.int32, (L, L), 0) <= lax.broadcasted_iota(jnp.int32, (L, L), 1)
    feat = lax.broadcasted_iota(jnp.int32, (V_WIN, L), 0)
    heads = range(MLSTM_HEADS)
    m_st = [m_scr[h][0:1, 0:1] for h in heads]
    c_st = [c_scr[h] for h in heads]

    for r0 in range(0, q_ref.shape[0], L):
        y = gt_ref[r0:r0 + L, :]
        yt = gtt_ref[:, r0:r0 + L]
        b_row = [yt[MLSTM_HEADS + h:MLSTM_HEADS + h + 1, :] for h in heads]
        c_row = [b_row[h] - yt[h:h + 1, :] for h in heads]
        c_col = [y[:, MLSTM_HEADS + h:MLSTM_HEADS + h + 1] - y[:, h:h + 1] for h in heads]
        g_tot = [y[L - 1:L, MLSTM_HEADS + h:MLSTM_HEADS + h + 1] for h in heads]
        qh = [q_ref[r0:r0 + L, h * DQK_PAD:(h + 1) * DQK_PAD] for h in heads]
        kh = [k_ref[r0:r0 + L, h * DQK_PAD:(h + 1) * DQK_PAD] for h in heads]
        vt_aug = [jnp.where(feat == _V_ONES_COL[h], jnp.ones((), BF16),
                            vt_ref[_V_WIN_START[h]:_V_WIN_START[h] + V_WIN, r0:r0 + L]) for h in heads]

        both = [_dot_nt(jnp.concatenate([kh[h], c_st[h].astype(BF16)], axis=0), qh[h]) for h in heads]
        qk_t = [t[:L] for t in both]
        inter_t = [t[L:] for t in both]
        dmat = [jnp.where(causal, b_row[h] - c_col[h], -jnp.inf) for h in heads]
        inter_log = [b_row[h] + m_st[h] for h in heads]
        m_row = [jnp.maximum(inter_log[h], jnp.max(dmat[h], axis=0, keepdims=True)) for h in heads]
        scores_t = [(qk_t[h] * jnp.exp2(dmat[h] - m_row[h])).astype(BF16) for h in heads]
        num_t = [_dot(vt_aug[h], scores_t[h]) + jnp.exp2(inter_log[h] - m_row[h]) * inter_t[h] for h in heads]

        a = [g_tot[h] - c_row[h] for h in heads]
        m_new = [jnp.maximum(g_tot[h] + m_st[h], jnp.max(a[h], axis=1, keepdims=True)) for h in heads]
        vtw = [vt_aug[h] * jnp.exp2(a[h] - m_new[h]).astype(BF16) for h in heads]
        c_upd = [_dot(vtw[h], kh[h]) for h in heads]

        for h in heads:
            off, ones_row, lo = _V_WIN_OFF[h], _V_ONES_COL[h], h * MLSTM_DV
            den = num_t[h][ones_row:ones_row + 1, :]
            inv_dd = 1.0 / jnp.maximum(jnp.abs(den), jnp.exp2(-m_row[h]))
            h_t = num_t[h][off:off + MLSTM_DV, :]
            ms = jnp.sum(h_t * h_t, axis=0, keepdims=True) * (1.0 / MLSTM_DV)
            scale = inv_dd * lax.rsqrt(ms * inv_dd * inv_dd + EPS)
            hmt_ref[lo:lo + MLSTM_DV, r0:r0 + L] = (h_t * scale).astype(BF16)

        c_st = [jnp.exp2(g_tot[h] + m_st[h] - m_new[h]) * c_st[h] + c_upd[h] for h in heads]
        m_st = m_new

    for h in heads:
        c_scr[h] = c_st[h]
        m_scr[h] = jnp.broadcast_to(m_st[h], m_scr.shape[1:])


def _mlstm(q, k, vt, gt, gtt, g_ffn, w_ffn_in, w_ffn_out, B, S):
    rows = MLSTM_CHUNK * MLSTM_CHUNKS_PER_STEP
    steps = S // rows
    N = B * S
    row = lambda w: pl.BlockSpec((rows, w), lambda b, c: (b * steps + c, 0))
    col = lambda h: pl.BlockSpec((h, rows), lambda b, c: (0, b * steps + c))
    w_in, w_out, w_shape = _ffn_weight_specs(0, B * steps, lambda b, c: b * steps + c)
    return pl.pallas_call(
        _mlstm_kernel,
        grid=(B, steps),
        in_specs=[row(QK_PAD_W), row(QK_PAD_W), col(MLSTM_V_W), row(LANES), col(2 * MLSTM_HEADS)] + w_in,
        out_specs=[col(MLSTM_V_W)] + w_out,
        out_shape=[jax.ShapeDtypeStruct((MLSTM_V_W, N), BF16)] + w_shape,
        scratch_shapes=[pltpu.VMEM((MLSTM_HEADS, V_WIN, DQK_PAD), F32),
                        pltpu.VMEM((MLSTM_HEADS, 8, LANES), F32)],
        compiler_params=pltpu.CompilerParams(dimension_semantics=("arbitrary", "arbitrary"),
                                             vmem_limit_bytes=VMEM_LIMIT),
        name="mlstm",
    )(q, k, vt, gt, gtt, g_ffn, w_ffn_in, w_ffn_out)


def _mix_ffn_core(x_ref, main_ref, gate_ref, mo_ref, wo_ref, gpost_ref, gfpost_ref, wi_ref, wd_ref):
    mix_w = D_MODEL - MEM_W
    groups = [slice(r, r + FFN_ROW_GROUP) for r in range(0, x_ref.shape[0], FFN_ROW_GROUP)]
    if gate_ref is None:
        main = [_dot(main_ref[g, :], wo_ref[0:mix_w, :]) for g in groups]
    else:
        main = [_dot_tn(main_ref[:, g] * gate_ref[:, g], wo_ref[0:mix_w, :]) for g in groups]
    mix = [m + _dot(mo_ref[g, :], wo_ref[mix_w:, :]) for g, m in zip(groups, main)]
    x1 = [x_ref[g, :] + _rms(m, gpost_ref[...]) for g, m in zip(groups, mix)]
    hf = [_rms_hat(t).astype(BF16) for t in x1]
    act = [[] for _ in groups]
    for c in range(N_FF_CHUNKS):
        lo = c * FF_CHUNK
        for i, h in enumerate(hf):
            g = _dot(h, wi_ref[:, lo:lo + FF_CHUNK])
            u = _dot(h, wi_ref[:, D_FF + lo:D_FF + lo + FF_CHUNK])
            act[i].append((g * _sigmoid(g) * u).astype(BF16))
    acc = [_dot(jnp.concatenate(a, axis=1), wd_ref[...]) for a in act]
    return jnp.concatenate([t + _rms(a, gfpost_ref[...]) for t, a in zip(x1, acc)], axis=0)


def _mix_ffn_kernel(x_ref, main_ref, mo_ref, wo_ref, gpost_ref, gfpost_ref, wi_ref, wd_ref, xo_ref):
    xo_ref[...] = _mix_ffn_core(x_ref, main_ref, None, mo_ref, wo_ref, gpost_ref, gfpost_ref, wi_ref, wd_ref)


def _mix_ffn_gated_kernel(x_ref, main_ref, gate_ref, mo_ref, wo_ref, gpost_ref, gfpost_ref, wi_ref, wd_ref, xo_ref):
    xo_ref[...] = _mix_ffn_core(x_ref, main_ref, gate_ref, mo_ref, wo_ref, gpost_ref, gfpost_ref, wi_ref, wd_ref)


def _rope(t1, t2, cos, sin):
    return t1 * cos - t2 * sin, t2 * cos + t1 * sin


def _rope_tables(pos_ref, inv_ref):
    ang = pos_ref[...].astype(F32) * inv_ref[...]
    cos_d = jnp.cos(ang)
    sin_d = jnp.sin(ang)
    grp = lax.broadcasted_iota(jnp.int32, ang.shape, 1) >> 5

    def spread(t, g):
        y = jnp.where(grp == g, t, 0.0)
        y = y + pltpu.roll(y, HEAD_DIM // 2, 1)
        return y + pltpu.roll(y, HEAD_DIM, 1)

    n_grp = LANES // (HEAD_DIM // 2)
    cos = jnp.concatenate([spread(cos_d, g) for g in range(n_grp)], axis=0)
    sin = jnp.concatenate([spread(sin_d, g) for g in range(n_grp)], axis=0)
    return cos, sin


def _mix_ffn(layer, x2d, main, mo, wo, g_post, g_fpost, wi, wd, gate=None):
    N = x2d.shape[0]
    tm = FFN_ROW_TILE
    row = lambda w: pl.BlockSpec((tm, w), lambda i: (i, 0))
    if gate is None:
        body, main_specs, mains = _mix_ffn_kernel, [row(main.shape[1])], [main]
    else:
        col = pl.BlockSpec((main.shape[0], tm), lambda i: (0, i))
        body, main_specs, mains = _mix_ffn_gated_kernel, [col, col], [main, gate]
    return pl.pallas_call(
        body,
        grid=(N // tm,),
        in_specs=[row(D_MODEL)] + main_specs + [row(MEM_W),
                  _resident(wo.shape),
                  _resident((1, D_MODEL)), _resident((1, D_MODEL)),
                  _resident(wi.shape), _resident(wd.shape)],
        out_specs=row(D_MODEL),
        out_shape=jax.ShapeDtypeStruct((N, D_MODEL), F32),
        compiler_params=pltpu.CompilerParams(dimension_semantics=("arbitrary",),
                                             vmem_limit_bytes=VMEM_LIMIT),
        name="mix_ffn_%d" % layer,
    )(x2d, *mains, mo, wo, g_post, g_fpost, wi, wd)


def _proj_b_kernel(x_ref, cos_ref, sin_ref, wkv_ref, wb_ref, mk_ref, mv_ref,
                   ks_ref, vs_ref, q1_ref, mo1_ref):
    groups = [slice(r, r + PROJ_B_ROW_GROUP) for r in range(0, x_ref.shape[0], PROJ_B_ROW_GROUP)]
    xn = [_rms_hat(x_ref[g, :]).astype(BF16) for g in groups]
    pb = [_dot(t, wb_ref[...]) for t in xn]
    s = [_mem_scores(t[:, SWA_Q_W:].astype(BF16), mk_ref) for t in pb]
    cos, sin = cos_ref[...], sin_ref[...]
    kv = [_dot(t, wkv_ref[...]) for t in xn]
    p = [_mem_probs(t) for t in s]

    for g, t in zip(groups, pb):
        for gi in range(SWA_GROUPS):
            base = gi * 2 * LANES
            t1, t2 = _rope(t[:, base:base + LANES], t[:, base + LANES:base + 2 * LANES], cos[g], sin[g])
            q1_ref[g, base:base + LANES] = t1.astype(BF16)
            q1_ref[g, base + LANES:base + 2 * LANES] = t2.astype(BF16)
    for g, t in zip(groups, p):
        mo1_ref[g, :] = _mem_out(t, mv_ref).astype(BF16)

    scale = HEAD_DIM ** -0.5 * LOG2E
    for g, t in zip(groups, kv):
        k1, k2 = _rope(t[:, 0:LANES], t[:, LANES:2 * LANES], cos[g], sin[g])
        ks_ref[g, 0:LANES] = (k1 * scale).astype(BF16)
        ks_ref[g, LANES:2 * LANES] = (k2 * scale).astype(BF16)
        vs_ref[g, :] = t[:, SWA_KV_W:].astype(BF16)


def _proj_b(x2d, cos, sin, wkv, wb, mkm, mvm, rows_per_batch):
    N = x2d.shape[0]
    tm = PROJ_B_ROW_TILE
    steps_per_batch = rows_per_batch // tm
    row = lambda w: pl.BlockSpec((tm, w), lambda i: (i, 0))
    mem_spec = pl.BlockSpec((1, 1, MEM_HEADS, MEM_TOKENS, MEM_W),
                            lambda i: (1, i // steps_per_batch, 0, 0, 0))
    return pl.pallas_call(
        _proj_b_kernel,
        grid=(N // tm,),
        in_specs=[row(D_MODEL), row(LANES), row(LANES), _resident(wkv.shape), _resident(wb.shape),
                  mem_spec, mem_spec],
        out_specs=[row(SWA_KV_W), row(SWA_KV_W), row(SWA_Q_W), row(MEM_W)],
        out_shape=[jax.ShapeDtypeStruct((N, SWA_KV_W), BF16), jax.ShapeDtypeStruct((N, SWA_KV_W), BF16),
                   jax.ShapeDtypeStruct((N, SWA_Q_W), BF16), jax.ShapeDtypeStruct((N, MEM_W), BF16)],
        compiler_params=pltpu.CompilerParams(dimension_semantics=("arbitrary",),
                                             vmem_limit_bytes=VMEM_LIMIT),
        name="proj_b",
    )(x2d, cos, sin, wkv, wb, mkm, mvm)


def _swa_kernel(sink_ref, q_ref, kp_ref, kc_ref, vp_ref, vc_ref, g_ref, wi_ref, wd_ref,
                o_ref, wi_out_ref, wd_out_ref):
    _prepare_ffn_weights(g_ref, wi_ref, wd_ref, wi_out_ref, wd_out_ref)
    W = SWA_WINDOW
    G = SWA_GROUPS
    n = pl.program_id(1)
    n_sub = q_ref.shape[0] // W
    H = SWA_KV_HEADS
    kpos = lax.broadcasted_iota(jnp.int32, (2 * W, G * W), 0)
    qcol = lax.broadcasted_iota(jnp.int32, (2 * W, G * W), 1)
    diff = (qcol & (W - 1)) + W - kpos
    in_band = (diff >= 0) & (diff < W)
    k_lane = lax.broadcasted_iota(jnp.int32, (2 * W, SWA_KV_W), 1)
    sink = sink_ref[...]
    subs = range(n_sub)
    q, km, vm, valid = [], [], [], []
    for sub in subs:
        q.append(jnp.concatenate([q_ref[sub * W:(sub + 1) * W, gi * 2 * LANES:(gi + 1) * 2 * LANES]
                                  for gi in range(G)], axis=0))
        if sub == 0:
            k = jnp.concatenate([kp_ref[...], kc_ref[0:W, :]], axis=0)
            v = jnp.concatenate([vp_ref[...], vc_ref[0:W, :]], axis=0)
            valid.append(in_band & ((kpos >= W) | (n > 0)))
        else:
            k = kc_ref[(sub - 1) * W:(sub + 1) * W, :]
            v = vc_ref[(sub - 1) * W:(sub + 1) * W, :]
            valid.append(in_band)
        km.append(jnp.concatenate([jnp.where(((k_lane & (LANES - 1)) >> 5) == j, k, jnp.zeros((), BF16))
                                   for j in range(H)], axis=0))
        vm.append(jnp.concatenate([jnp.where((k_lane >> 6) == j, v, jnp.zeros((), BF16))
                                   for j in range(H)], axis=0))
    s = [_dot_nt(km[i], q[i]).reshape(H, 2 * W, G * W) for i in subs]
    s = [jnp.where(valid[i][None], s[i], -jnp.inf) for i in subs]
    m = [jnp.maximum(jnp.max(s[i], axis=1, keepdims=True), sink) for i in subs]
    e = [jnp.exp2(s[i] - m[i]) for i in subs]
    inv = [1.0 / (jnp.sum(e[i], axis=1, keepdims=True) + jnp.exp2(sink - m[i])) for i in subs]
    p = [e[i].astype(BF16) * inv[i].astype(BF16) for i in subs]
    out = [_dot_tn(p[i].reshape(H * 2 * W, G * W), vm[i]) for i in subs]
    for sub in subs:
        for gi in range(G):
            o_ref[sub * W:(sub + 1) * W, gi * 2 * LANES:(gi + 1) * 2 * LANES] = (
                out[sub][gi * W:(gi + 1) * W].astype(BF16))


def _swa(sink_cols, q1, ks, vs, g_ffn, w_ffn_in, w_ffn_out, B, S):
    W = SWA_WINDOW
    n_sub = SWA_BLOCKS_PER_STEP
    tq = n_sub * W
    steps = S // tq
    cur = lambda w: pl.BlockSpec((tq, w), lambda b, n: (b * steps + n, 0))
    prev = lambda w: pl.BlockSpec((W, w), lambda b, n: (b * (S // W) + jnp.maximum(n_sub * n - 1, 0), 0))
    w_in, w_out, w_shape = _ffn_weight_specs(1, B * steps, lambda b, n: b * steps + n)
    return pl.pallas_call(
        _swa_kernel,
        grid=(B, steps),
        in_specs=[_resident(sink_cols.shape),
                  cur(SWA_Q_W), prev(SWA_KV_W), cur(SWA_KV_W), prev(SWA_KV_W), cur(SWA_KV_W)] + w_in,
        out_specs=[cur(SWA_Q_W)] + w_out,
        out_shape=[jax.ShapeDtypeStruct((B * S, SWA_Q_W), BF16)] + w_shape,
        compiler_params=pltpu.CompilerParams(dimension_semantics=("arbitrary", "arbitrary"),
                                             vmem_limit_bytes=VMEM_LIMIT),
        name="swa",
    )(sink_cols, q1, ks, ks, vs, vs, g_ffn, w_ffn_in, w_ffn_out)


def _pad_heads(w, heads, width, padded):
    w = w.reshape(w.shape[0], heads, width)
    return jnp.pad(w, ((0, 0), (0, 0), (0, padded - width))).reshape(w.shape[0], heads * padded)


def _with_tail(perm, total):
    return np.concatenate([perm, np.arange(len(perm), total, dtype=np.int32)])


def _swa_q_perm():
    idx = np.empty((SWA_GROUPS, 2, SWA_KV_HEADS, HEAD_DIM // 2), np.int32)
    for gi in range(SWA_GROUPS):
        for half in range(2):
            for j in range(SWA_KV_HEADS):
                idx[gi, half, j] = (j * SWA_GROUPS + gi) * HEAD_DIM + half * (HEAD_DIM // 2) + np.arange(HEAD_DIM // 2)
    return idx.reshape(-1)


def _swa_k_perm():
    idx = np.empty((2, SWA_KV_HEADS, HEAD_DIM // 2), np.int32)
    for half in range(2):
        for j in range(SWA_KV_HEADS):
            idx[half, j] = j * HEAD_DIM + half * (HEAD_DIM // 2) + np.arange(HEAD_DIM // 2)
    return idx.reshape(-1)


def _swa_out_perm():
    idx = np.empty((SWA_GROUPS, SWA_KV_HEADS, HEAD_DIM), np.int32)
    for gi in range(SWA_GROUPS):
        for j in range(SWA_KV_HEADS):
            idx[gi, j] = (j * SWA_GROUPS + gi) * HEAD_DIM + np.arange(HEAD_DIM)
    return idx.reshape(-1)


def kernel(x, mem, positions, g_mix_pre, g_mix_post, g_ffn_pre, g_ffn_post, g_mem, w_mem_kv, w_out,
           w_ffn_in, w_ffn_out, w_in_a, b_gates_a, g_mlstm_out, g_kv, w_kv, w_in_b, sinks_b):
    B, S, _ = x.shape
    N = B * S
    assert S % PROJ_A_ROW_TILE == 0 and S % PROJ_B_ROW_TILE == 0 and S % FFN_ROW_TILE == 0
    assert S % (MLSTM_CHUNK * MLSTM_CHUNKS_PER_STEP) == 0
    assert S % (SWA_WINDOW * SWA_BLOCKS_PER_STEP) == 0
    x2d = x.reshape(N, D_MODEL)
    vec = lambda g: g.reshape(1, -1).astype(F32)

    mkm, mvm = _mem_kv(mem, g_mem, w_mem_kv.astype(BF16))

    fold = lambda g, w: g.astype(F32)[:, None] * w

    piece = lambda c0, width: fold(g_mix_pre[0], w_in_a[0][:, c0:c0 + width])
    c0 = 0
    wq = _pad_heads(piece(c0, MLSTM_QK_W), MLSTM_HEADS, MLSTM_DQK, DQK_PAD).astype(BF16)
    c0 += MLSTM_QK_W
    wk = _pad_heads(piece(c0, MLSTM_QK_W) * (MLSTM_DQK ** -0.5), MLSTM_HEADS, MLSTM_DQK, DQK_PAD).astype(BF16)
    c0 += MLSTM_QK_W
    wvo = _transposed_weights(g_mix_pre[0].astype(F32)[:, None], w_in_a[0], c0, MLSTM_V_W, 2)
    c0 += 2 * MLSTM_V_W
    wgt = jnp.pad(piece(c0, 2 * MLSTM_HEADS), ((0, 0), (0, LANES - 2 * MLSTM_HEADS))).astype(BF16)
    c0 += 2 * MLSTM_HEADS
    wmq = piece(c0, MEM_W).astype(BF16)
    bg = jnp.pad(b_gates_a[0].astype(F32), (0, LANES - 2 * MLSTM_HEADS)).reshape(1, LANES)

    inv = 1.0 / (ROPE_THETA ** (jnp.arange(0, HEAD_DIM, 2, dtype=F32) / HEAD_DIM))
    inv = jnp.tile(inv, SWA_KV_HEADS).reshape(1, LANES)
    n_grp = LANES // (HEAD_DIM // 2)
    pos_dense = positions.reshape(N // PROJ_A_ROW_TILE, n_grp, PROJ_A_ROW_TILE // n_grp).transpose(0, 2, 1)
    pos_dense = jnp.repeat(pos_dense, HEAD_DIM // 2, axis=2).reshape(N // n_grp, LANES)
    q, k, v, o, gt, gtt, mo0, cos, sin = _proj_a(x2d, wq, wk, wvo, wmq, wgt, bg, mkm, mvm, pos_dense, inv, S)
    g_ffn = g_ffn_pre.astype(F32)[:, :, None]
    hm, wi0, wd0 = _mlstm(q, k, v, gt, gtt, g_ffn, w_ffn_in, w_ffn_out, B, S)

    wo0 = fold(jnp.concatenate([g_mlstm_out[0].astype(F32), jnp.ones((MEM_W,), F32)]), w_out[0]).astype(BF16)
    wkv = fold(g_kv, w_kv)[:, _with_tail(_swa_k_perm(), 2 * SWA_KV_W)].astype(BF16)
    wb = fold(g_mix_pre[1], w_in_b[0])[:, _with_tail(_swa_q_perm(), SWA_Q_W + MEM_W)].astype(BF16)
    x1 = _mix_ffn(0, x2d, hm, mo0, wo0, vec(g_mix_post[0]), vec(g_ffn_post[0]),
                  wi0, wd0, gate=o)
    ks, vs, q1, mo1 = _proj_b(x1, cos, sin, wkv, wb, mkm, mvm, S)

    sink_rows = jnp.repeat((sinks_b[0].astype(F32) * LOG2E).reshape(SWA_KV_HEADS, SWA_GROUPS), SWA_WINDOW, axis=1)
    attn, wi1, wd1 = _swa(sink_rows.reshape(SWA_KV_HEADS, 1, SWA_GROUPS * SWA_WINDOW), q1, ks, vs,
                          g_ffn, w_ffn_in, w_ffn_out, B, S)
    wo1 = w_out[1][_with_tail(_swa_out_perm(), D_MODEL)].astype(BF16)
    xo = _mix_ffn(1, x1, attn, mo1, wo1, vec(g_mix_post[1]), vec(g_ffn_post[1]),
                  wi1, wd1)
    return xo.reshape(B, S, D_MODEL)
```

```python
import numpy as np
import jax
import jax.numpy as jnp
from jax import lax
from jax.experimental import pallas as pl
from jax.experimental.pallas import tpu as pltpu

F32 = jnp.float32
BF16 = jnp.bfloat16

D_MODEL = 1024
DEPTH = 2
HEAD_DIM = 64
EPS = 1e-6
ROPE_THETA = 10000.0
LOG2E = 1.4426950408889634

MLSTM_HEADS = 4
MLSTM_DV = 192
MLSTM_DQK = 96
MLSTM_QK_W = MLSTM_HEADS * MLSTM_DQK
MLSTM_V_W = MLSTM_HEADS * MLSTM_DV
GATE_SOFTCAP = 15.0
M_INIT = -1e30

SWA_Q_HEADS = 12
SWA_KV_HEADS = 4
SWA_GROUPS = SWA_Q_HEADS // SWA_KV_HEADS
SWA_Q_W = SWA_Q_HEADS * HEAD_DIM
SWA_KV_W = SWA_KV_HEADS * HEAD_DIM
SWA_WINDOW = 128

MEM_TOKENS = 256
MEM_HEADS = 4
MEM_HEAD_DIM = 64
MEM_W = MEM_HEADS * MEM_HEAD_DIM

D_FF = 2816

LANES = 128
MXU_TILE = 256

DQK_PAD = LANES
QK_PAD_W = MLSTM_HEADS * DQK_PAD
V_WIN = MXU_TILE
MLSTM_CHUNK = 256
MLSTM_CHUNKS_PER_STEP = 8
FF_CHUNK = MXU_TILE
N_FF_CHUNKS = D_FF // FF_CHUNK
PROJ_A_ROW_TILE = 1024
PROJ_B_ROW_TILE = 2048
PROJ_B_ROW_GROUP = 512
FFN_ROW_TILE = 1024
FFN_ROW_GROUP = 512
SWA_BLOCKS_PER_STEP = 16
VMEM_LIMIT = 56 * 1024 * 1024

_V_WIN_START = (0, 128, 384, 512)
_V_WIN_OFF = (0, 64, 0, 64)
_V_ONES_COL = (192, 0, 192, 0)


def _rms_hat(x):
    return x * lax.rsqrt(jnp.mean(x * x, axis=-1, keepdims=True) + EPS)


def _rms(x, g):
    return _rms_hat(x) * g


def _dot(a, b):
    return jnp.dot(a, b, preferred_element_type=F32)


def _dot_nt(a, b):
    return lax.dot_general(a, b, (((1,), (1,)), ((), ())), preferred_element_type=F32)


def _dot_tn(a, b):
    return lax.dot_general(a, b, (((0,), (0,)), ((), ())), preferred_element_type=F32)


def _sigmoid(x):
    return 1.0 / (1.0 + jnp.exp(-x))


def _resident(shape):
    nd = len(shape)
    return pl.BlockSpec(shape, lambda *_: (0,) * nd, pipeline_mode=pl.Buffered(1))


def _mem_kv_kernel(mem_ref, g_ref, w_ref, mk_ref, mv_ref):
    B = mem_ref.shape[0]
    hn = _rms(mem_ref[...].reshape(B * MEM_TOKENS, D_MODEL), g_ref[0]).astype(BF16)
    kv = _dot(hn, w_ref[0])
    mk = kv[:, :MEM_W] * (MEM_HEAD_DIM ** -0.5 * LOG2E)
    mv = kv[:, MEM_W:]
    lane_head = lax.broadcasted_iota(jnp.int32, mk.shape, 1) >> 6
    for h in range(MEM_HEADS):
        sel = lane_head == h
        mk_ref[0, :, h] = jnp.where(sel, mk, 0.0).astype(BF16).reshape(B, MEM_TOKENS, MEM_W)
        mv_ref[0, :, h] = jnp.where(sel, mv, 0.0).astype(BF16).reshape(B, MEM_TOKENS, MEM_W)


def _mem_kv(mem, g_mem, w_mem_kv_bf16):
    B = mem.shape[0]
    out_sds = jax.ShapeDtypeStruct((DEPTH, B, MEM_HEADS, MEM_TOKENS, MEM_W), BF16)
    out_spec = pl.BlockSpec((1, B, MEM_HEADS, MEM_TOKENS, MEM_W), lambda l: (l, 0, 0, 0, 0))
    return pl.pallas_call(
        _mem_kv_kernel,
        grid=(DEPTH,),
        in_specs=[
            pl.BlockSpec((B, MEM_TOKENS, D_MODEL), lambda l: (0, 0, 0)),
            pl.BlockSpec((1, 1, D_MODEL), lambda l: (l, 0, 0)),
            pl.BlockSpec((1, D_MODEL, 2 * MEM_W), lambda l: (l, 0, 0)),
        ],
        out_specs=[out_spec, out_spec],
        out_shape=[out_sds, out_sds],
        compiler_params=pltpu.CompilerParams(dimension_semantics=("arbitrary",),
                                             vmem_limit_bytes=VMEM_LIMIT),
        name="mem_kv",
    )(mem, g_mem.reshape(DEPTH, 1, D_MODEL), w_mem_kv_bf16)


def _mem_scores(mq, mk_ref):
    return [_dot_nt(mq, mk_ref[0, 0, h]) for h in range(MEM_HEADS)]


def _mem_probs(s):
    e = [jnp.exp2(sh - jnp.max(sh, axis=-1, keepdims=True)) for sh in s]
    return [(eh * (1.0 / jnp.sum(eh, axis=-1, keepdims=True))).astype(BF16) for eh in e]


def _mem_out(p, mv_ref):
    out = _dot(p[0], mv_ref[0, 0, 0])
    for h in range(1, MEM_HEADS):
        out = out + _dot(p[h], mv_ref[0, 0, h])
    return out


def _proj_a_kernel(x_ref, wq_ref, wk_ref, wv_ref, wo_ref, wmq_ref, wg_ref, bg_ref, mk_ref, mv_ref, pos_ref, inv_ref,
                   q_ref, k_ref, v_ref, o_ref, gt_ref, gtt_ref, mo_ref, cos_ref, sin_ref):
    cos_ref[...], sin_ref[...] = _rope_tables(pos_ref, inv_ref)
    hn = _rms_hat(x_ref[...]).astype(BF16)
    mq = _dot_nt(hn, wmq_ref[...]).astype(BF16)
    s = _mem_scores(mq, mk_ref)
    gates = _dot_nt(hn, wg_ref[...]) + bg_ref[...]
    q_ref[...] = _dot_nt(hn, wq_ref[...]).astype(BF16)
    gates = GATE_SOFTCAP * jnp.tanh(gates * (1.0 / GATE_SOFTCAP))
    log_sig = jnp.minimum(gates, 0.0) - jnp.log1p(jnp.exp(-jnp.abs(gates)))
    lane = lax.broadcasted_iota(jnp.int32, gates.shape, 1)
    gl = jnp.where(lane < MLSTM_HEADS, gates, log_sig) * LOG2E
    p = _mem_probs(s)
    k_ref[...] = _dot_nt(hn, wk_ref[...]).astype(BF16)
    g_hi = gl.astype(BF16)
    rest = gl - g_hi.astype(F32)
    g_mid = rest.astype(BF16)
    g_lo = (rest - g_mid.astype(F32)).astype(BF16)
    L = MLSTM_CHUNK
    tri = (lax.broadcasted_iota(jnp.int32, (L, L), 1) <= lax.broadcasted_iota(jnp.int32, (L, L), 0)).astype(BF16)
    csum = jnp.concatenate(
        [_dot(tri, g_hi[r:r + L]) + _dot(tri, g_mid[r:r + L]) + _dot(tri, g_lo[r:r + L])
         for r in range(0, gl.shape[0], L)], axis=0)
    v_ref[...] = _dot_nt(wv_ref[...], hn).astype(BF16)
    y = jnp.where(lane < MLSTM_HEADS, gl, csum)
    gt_ref[...] = y
    gtt_ref[...] = y.T[0:2 * MLSTM_HEADS, :]
    mo_ref[...] = _mem_out(p, mv_ref).astype(BF16)
    o_ref[...] = _sigmoid(_dot_nt(wo_ref[...], hn)).astype(BF16)


def _proj_a(x2d, wq, wk, wv, wo, wmq, wg, bg, mkm, mvm, pos, inv, rows_per_batch):
    N = x2d.shape[0]
    tm = PROJ_A_ROW_TILE
    steps_per_batch = rows_per_batch // tm
    row = lambda w: pl.BlockSpec((tm, w), lambda i: (i, 0))
    col = pl.BlockSpec((MLSTM_V_W, tm), lambda i: (0, i))
    mem_spec = pl.BlockSpec((1, 1, MEM_HEADS, MEM_TOKENS, MEM_W),
                            lambda i: (0, i // steps_per_batch, 0, 0, 0))
    return pl.pallas_call(
        _proj_a_kernel,
        grid=(N // tm,),
        in_specs=[row(D_MODEL),
                  _resident(wq.shape), _resident(wk.shape), _resident(wv.shape), _resident(wo.shape),
                  _resident(wmq.shape), _resident(wg.shape), _resident((1, LANES)),
                  mem_spec, mem_spec,
                  pl.BlockSpec((tm // (LANES // (HEAD_DIM // 2)), LANES), lambda i: (i, 0)), _resident((1, LANES))],
        out_specs=[row(QK_PAD_W), row(QK_PAD_W), col, col, row(LANES),
                   pl.BlockSpec((2 * MLSTM_HEADS, tm), lambda i: (0, i)), row(MEM_W), row(LANES), row(LANES)],
        out_shape=[jax.ShapeDtypeStruct((N, QK_PAD_W), BF16), jax.ShapeDtypeStruct((N, QK_PAD_W), BF16),
                   jax.ShapeDtypeStruct((MLSTM_V_W, N), BF16), jax.ShapeDtypeStruct((MLSTM_V_W, N), BF16),
                   jax.ShapeDtypeStruct((N, LANES), F32), jax.ShapeDtypeStruct((2 * MLSTM_HEADS, N), F32),
                   jax.ShapeDtypeStruct((N, MEM_W), BF16),
                   jax.ShapeDtypeStruct((N, LANES), F32), jax.ShapeDtypeStruct((N, LANES), F32)],
        compiler_params=pltpu.CompilerParams(dimension_semantics=("arbitrary",),
                                             vmem_limit_bytes=VMEM_LIMIT),
        name="proj_a",
    )(x2d, wq, wk, wv, wo, wmq, wg, bg, mkm, mvm, pos, inv)


def _ffn_weight_specs(layer, steps, step_index):
    ri, rd = D_MODEL // steps, D_FF // steps
    assert ri * steps == D_MODEL and rd * steps == D_FF and ri % 16 == 0 and rd % 16 == 0
    src = lambda rows, w: pl.BlockSpec((None, rows, w), lambda *g: (layer, step_index(*g), 0))
    dst = lambda rows, w: pl.BlockSpec((rows, w), lambda *g: (step_index(*g), 0))
    return ([src(ri, 1), src(ri, 2 * D_FF), src(rd, D_MODEL)],
            [dst(ri, 2 * D_FF), dst(rd, D_MODEL)],
            [jax.ShapeDtypeStruct((D_MODEL, 2 * D_FF), BF16), jax.ShapeDtypeStruct((D_FF, D_MODEL), BF16)])


def _prepare_ffn_weights(g_ref, wi_ref, wd_ref, wi_out_ref, wd_out_ref):
    wi_out_ref[...] = (g_ref[...] * wi_ref[...]).astype(BF16)
    wd_out_ref[...] = wd_ref[...].astype(BF16)


def _mlstm_kernel(q_ref, k_ref, vt_ref, gt_ref, gtt_ref, g_ref, wi_ref, wd_ref,
                  hmt_ref, wi_out_ref, wd_out_ref, c_scr, m_scr):
    _prepare_ffn_weights(g_ref, wi_ref, wd_ref, wi_out_ref, wd_out_ref)
    L = MLSTM_CHUNK

    @pl.when(pl.program_id(1) == 0)
    def _():
        c_scr[...] = jnp.zeros(c_scr.shape, F32)
        m_scr[...] = jnp.full(m_scr.shape, M_INIT, F32)

    causal = lax.broadcasted_iota(jnp.int32, (L, L), 0) <= lax.broadcasted_iota(jnp.int32, (L, L), 1)
    feat = lax.broadcasted_iota(jnp.int32, (V_WIN, L), 0)
    heads = range(MLSTM_HEADS)
    m_st = [m_scr[h][0:1, 0:1] for h in heads]
    c_st = [c_scr[h] for h in heads]

    for r0 in range(0, q_ref.shape[0], L):
        y = gt_ref[r0:r0 + L, :]
        yt = gtt_ref[:, r0:r0 + L]
        b_row = [yt[MLSTM_HEADS + h:MLSTM_HEADS + h + 1, :] for h in heads]
        c_row = [b_row[h] - yt[h:h + 1, :] for h in heads]
        c_col = [y[:, MLSTM_HEADS + h:MLSTM_HEADS + h + 1] - y[:, h:h + 1] for h in heads]
        g_tot = [y[L - 1:L, MLSTM_HEADS + h:MLSTM_HEADS + h + 1] for h in heads]
        qh = [q_ref[r0:r0 + L, h * DQK_PAD:(h + 1) * DQK_PAD] for h in heads]
        kh = [k_ref[r0:r0 + L, h * DQK_PAD:(h + 1) * DQK_PAD] for h in heads]
        vt_aug = [jnp.where(feat == _V_ONES_COL[h], jnp.ones((), BF16),
                            vt_ref[_V_WIN_START[h]:_V_WIN_START[h] + V_WIN, r0:r0 + L]) for h in heads]

        both = [_dot_nt(jnp.concatenate([kh[h], c_st[h].astype(BF16)], axis=0), qh[h]) for h in heads]
        qk_t = [t[:L] for t in both]
        inter_t = [t[L:] for t in both]
        dmat = [jnp.where(causal, b_row[h] - c_col[h], -jnp.inf) for h in heads]
        inter_log = [b_row[h] + m_st[h] for h in heads]
        m_row = [jnp.maximum(inter_log[h], jnp.max(dmat[h], axis=0, keepdims=True)) for h in heads]
        scores_t = [(qk_t[h] * jnp.exp2(dmat[h] - m_row[h])).astype(BF16) for h in heads]
        num_t = [_dot(vt_aug[h], scores_t[h]) + jnp.exp2(inter_log[h] - m_row[h]) * inter_t[h] for h in heads]

        a = [g_tot[h] - c_row[h] for h in heads]
        m_new = [jnp.maximum(g_tot[h] + m_st[h], jnp.max(a[h], axis=1, keepdims=True)) for h in heads]
        vtw = [vt_aug[h] * jnp.exp2(a[h] - m_new[h]).astype(BF16) for h in heads]
        c_upd = [_dot(vtw[h], kh[h]) for h in heads]

        for h in heads:
            off, ones_row, lo = _V_WIN_OFF[h], _V_ONES_COL[h], h * MLSTM_DV
            den = num_t[h][ones_row:ones_row + 1, :]
            inv_dd = 1.0 / jnp.maximum(jnp.abs(den), jnp.exp2(-m_row[h]))
            h_t = num_t[h][off:off + MLSTM_DV, :]
            ms = jnp.sum(h_t * h_t, axis=0, keepdims=True) * (1.0 / MLSTM_DV)
            scale = inv_dd * lax.rsqrt(ms * inv_dd * inv_dd + EPS)
            hmt_ref[lo:lo + MLSTM_DV, r0:r0 + L] = (h_t * scale).astype(BF16)

        c_st = [jnp.exp2(g_tot[h] + m_st[h] - m_new[h]) * c_st[h] + c_upd[h] for h in heads]
        m_st = m_new

    for h in heads:
        c_scr[h] = c_st[h]
        m_scr[h] = jnp.broadcast_to(m_st[h], m_scr.shape[1:])


def _mlstm(q, k, vt, gt, gtt, g_ffn, w_ffn_in, w_ffn_out, B, S):
    rows = MLSTM_CHUNK * MLSTM_CHUNKS_PER_STEP
    steps = S // rows
    N = B * S
    row = lambda w: pl.BlockSpec((rows, w), lambda b, c: (b * steps + c, 0))
    col = lambda h: pl.BlockSpec((h, rows), lambda b, c: (0, b * steps + c))
    w_in, w_out, w_shape = _ffn_weight_specs(0, B * steps, lambda b, c: b * steps + c)
    return pl.pallas_call(
        _mlstm_kernel,
        grid=(B, steps),
        in_specs=[row(QK_PAD_W), row(QK_PAD_W), col(MLSTM_V_W), row(LANES), col(2 * MLSTM_HEADS)] + w_in,
        out_specs=[col(MLSTM_V_W)] + w_out,
        out_shape=[jax.ShapeDtypeStruct((MLSTM_V_W, N), BF16)] + w_shape,
        scratch_shapes=[pltpu.VMEM((MLSTM_HEADS, V_WIN, DQK_PAD), F32),
                        pltpu.VMEM((MLSTM_HEADS, 8, LANES), F32)],
        compiler_params=pltpu.CompilerParams(dimension_semantics=("arbitrary", "arbitrary"),
                                             vmem_limit_bytes=VMEM_LIMIT),
        name="mlstm",
    )(q, k, vt, gt, gtt, g_ffn, w_ffn_in, w_ffn_out)


def _mix_ffn_core(x_ref, main_ref, gate_ref, mo_ref, wo_ref, gpost_ref, gfpost_ref, wi_ref, wd_ref):
    mix_w = D_MODEL - MEM_W
    groups = [slice(r, r + FFN_ROW_GROUP) for r in range(0, x_ref.shape[0], FFN_ROW_GROUP)]
    if gate_ref is None:
        main = [_dot(main_ref[g, :], wo_ref[0:mix_w, :]) for g in groups]
    else:
        main = [_dot_tn(main_ref[:, g] * gate_ref[:, g], wo_ref[0:mix_w, :]) for g in groups]
    mix = [m + _dot(mo_ref[g, :], wo_ref[mix_w:, :]) for g, m in zip(groups, main)]
    x1 = [x_ref[g, :] + _rms(m, gpost_ref[...]) for g, m in zip(groups, mix)]
    hf = [_rms_hat(t).astype(BF16) for t in x1]
    act = [[] for _ in groups]
    for c in range(N_FF_CHUNKS):
        lo = c * FF_CHUNK
        for i, h in enumerate(hf):
            g = _dot(h, wi_ref[:, lo:lo + FF_CHUNK])
            u = _dot(h, wi_ref[:, D_FF + lo:D_FF + lo + FF_CHUNK])
            act[i].append((g * _sigmoid(g) * u).astype(BF16))
    acc = [_dot(jnp.concatenate(a, axis=1), wd_ref[...]) for a in act]
    return jnp.concatenate([t + _rms(a, gfpost_ref[...]) for t, a in zip(x1, acc)], axis=0)


def _mix_ffn_kernel(x_ref, main_ref, mo_ref, wo_ref, gpost_ref, gfpost_ref, wi_ref, wd_ref, xo_ref):
    xo_ref[...] = _mix_ffn_core(x_ref, main_ref, None, mo_ref, wo_ref, gpost_ref, gfpost_ref, wi_ref, wd_ref)


def _mix_ffn_gated_kernel(x_ref, main_ref, gate_ref, mo_ref, wo_ref, gpost_ref, gfpost_ref, wi_ref, wd_ref, xo_ref):
    xo_ref[...] = _mix_ffn_core(x_ref, main_ref, gate_ref, mo_ref, wo_ref, gpost_ref, gfpost_ref, wi_ref, wd_ref)


def _rope(t1, t2, cos, sin):
    return t1 * cos - t2 * sin, t2 * cos + t1 * sin


def _rope_tables(pos_ref, inv_ref):
    ang = pos_ref[...].astype(F32) * inv_ref[...]
    cos_d = jnp.cos(ang)
    sin_d = jnp.sin(ang)
    grp = lax.broadcasted_iota(jnp.int32, ang.shape, 1) >> 5

    def spread(t, g):
        y = jnp.where(grp == g, t, 0.0)
        y = y + pltpu.roll(y, HEAD_DIM // 2, 1)
        return y + pltpu.roll(y, HEAD_DIM, 1)

    n_grp = LANES // (HEAD_DIM // 2)
    cos = jnp.concatenate([spread(cos_d, g) for g in range(n_grp)], axis=0)
    sin = jnp.concatenate([spread(sin_d, g) for g in range(n_grp)], axis=0)
    return cos, sin


def _mix_ffn(layer, x2d, main, mo, wo, g_post, g_fpost, wi, wd, gate=None):
    N = x2d.shape[0]
    tm = FFN_ROW_TILE
    row = lambda w: pl.BlockSpec((tm, w), lambda i: (i, 0))
    if gate is None:
        body, main_specs, mains = _mix_ffn_kernel, [row(main.shape[1])], [main]
    else:
        col = pl.BlockSpec((main.shape[0], tm), lambda i: (0, i))
        body, main_specs, mains = _mix_ffn_gated_kernel, [col, col], [main, gate]
    return pl.pallas_call(
        body,
        grid=(N // tm,),
        in_specs=[row(D_MODEL)] + main_specs + [row(MEM_W),
                  _resident(wo.shape),
                  _resident((1, D_MODEL)), _resident((1, D_MODEL)),
                  _resident(wi.shape), _resident(wd.shape)],
        out_specs=row(D_MODEL),
        out_shape=jax.ShapeDtypeStruct((N, D_MODEL), F32),
        compiler_params=pltpu.CompilerParams(dimension_semantics=("arbitrary",),
                                             vmem_limit_bytes=VMEM_LIMIT),
        name="mix_ffn_%d" % layer,
    )(x2d, *mains, mo, wo, g_post, g_fpost, wi, wd)


def _proj_b_kernel(x_ref, cos_ref, sin_ref, wkv_ref, wb_ref, mk_ref, mv_ref,
                   ks_ref, vs_ref, q1_ref, mo1_ref):
    groups = [slice(r, r + PROJ_B_ROW_GROUP) for r in range(0, x_ref.shape[0], PROJ_B_ROW_GROUP)]
    xn = [_rms_hat(x_ref[g, :]).astype(BF16) for g in groups]
    pb = [_dot(t, wb_ref[...]) for t in xn]
    s = [_mem_scores(t[:, SWA_Q_W:].astype(BF16), mk_ref) for t in pb]
    cos, sin = cos_ref[...], sin_ref[...]
    kv = [_dot(t, wkv_ref[...]) for t in xn]
    p = [_mem_probs(t) for t in s]

    for g, t in zip(groups, pb):
        for gi in range(SWA_GROUPS):
            base = gi * 2 * LANES
            t1, t2 = _rope(t[:, base:base + LANES], t[:, base + LANES:base + 2 * LANES], cos[g], sin[g])
            q1_ref[g, base:base + LANES] = t1.astype(BF16)
            q1_ref[g, base + LANES:base + 2 * LANES] = t2.astype(BF16)
    for g, t in zip(groups, p):
        mo1_ref[g, :] = _mem_out(t, mv_ref).astype(BF16)

    scale = HEAD_DIM ** -0.5 * LOG2E
    for g, t in zip(groups, kv):
        k1, k2 = _rope(t[:, 0:LANES], t[:, LANES:2 * LANES], cos[g], sin[g])
        ks_ref[g, 0:LANES] = (k1 * scale).astype(BF16)
        ks_ref[g, LANES:2 * LANES] = (k2 * scale).astype(BF16)
        vs_ref[g, :] = t[:, SWA_KV_W:].astype(BF16)


def _proj_b(x2d, cos, sin, wkv, wb, mkm, mvm, rows_per_batch):
    N = x2d.shape[0]
    tm = PROJ_B_ROW_TILE
    steps_per_batch = rows_per_batch // tm
    row = lambda w: pl.BlockSpec((tm, w), lambda i: (i, 0))
    mem_spec = pl.BlockSpec((1, 1, MEM_HEADS, MEM_TOKENS, MEM_W),
                            lambda i: (1, i // steps_per_batch, 0, 0, 0))
    return pl.pallas_call(
        _proj_b_kernel,
        grid=(N // tm,),
        in_specs=[row(D_MODEL), row(LANES), row(LANES), _resident(wkv.shape), _resident(wb.shape),
                  mem_spec, mem_spec],
        out_specs=[row(SWA_KV_W), row(SWA_KV_W), row(SWA_Q_W), row(MEM_W)],
        out_shape=[jax.ShapeDtypeStruct((N, SWA_KV_W), BF16), jax.ShapeDtypeStruct((N, SWA_KV_W), BF16),
                   jax.ShapeDtypeStruct((N, SWA_Q_W), BF16), jax.ShapeDtypeStruct((N, MEM_W), BF16)],
        compiler_params=pltpu.CompilerParams(dimension_semantics=("arbitrary",),
                                             vmem_limit_bytes=VMEM_LIMIT),
        name="proj_b",
    )(x2d, cos, sin, wkv, wb, mkm, mvm)


def _swa_kernel(sink_ref, q_ref, kp_ref, kc_ref, vp_ref, vc_ref, g_ref, wi_ref, wd_ref,
                o_ref, wi_out_ref, wd_out_ref):
    _prepare_ffn_weights(g_ref, wi_ref, wd_ref, wi_out_ref, wd_out_ref)
    W = SWA_WINDOW
    G = SWA_GROUPS
    n = pl.program_id(1)
    n_sub = q_ref.shape[0] // W
    H = SWA_KV_HEADS
    kpos = lax.broadcasted_iota(jnp.int32, (2 * W, G * W), 0)
    qcol = lax.broadcasted_iota(jnp.int32, (2 * W, G * W), 1)
    diff = (qcol & (W - 1)) + W - kpos
    in_band = (diff >= 0) & (diff < W)
    k_lane = lax.broadcasted_iota(jnp.int32, (2 * W, SWA_KV_W), 1)
    sink = sink_ref[...]
    subs = range(n_sub)
    q, km, vm, valid = [], [], [], []
    for sub in subs:
        q.append(jnp.concatenate([q_ref[sub * W:(sub + 1) * W, gi * 2 * LANES:(gi + 1) * 2 * LANES]
                                  for gi in range(G)], axis=0))
        if sub == 0:
            k = jnp.concatenate([kp_ref[...], kc_ref[0:W, :]], axis=0)
            v = jnp.concatenate([vp_ref[...], vc_ref[0:W, :]], axis=0)
            valid.append(in_band & ((kpos >= W) | (n > 0)))
        else:
            k = kc_ref[(sub - 1) * W:(sub + 1) * W, :]
            v = vc_ref[(sub - 1) * W:(sub + 1) * W, :]
            valid.append(in_band)
        km.append(jnp.concatenate([jnp.where(((k_lane & (LANES - 1)) >> 5) == j, k, jnp.zeros((), BF16))
                                   for j in range(H)], axis=0))
        vm.append(jnp.concatenate([jnp.where((k_lane >> 6) == j, v, jnp.zeros((), BF16))
                                   for j in range(H)], axis=0))
    s = [_dot_nt(km[i], q[i]).reshape(H, 2 * W, G * W) for i in subs]
    s = [jnp.where(valid[i][None], s[i], -jnp.inf) for i in subs]
    m = [jnp.maximum(jnp.max(s[i], axis=1, keepdims=True), sink) for i in subs]
    e = [jnp.exp2(s[i] - m[i]) for i in subs]
    inv = [1.0 / (jnp.sum(e[i], axis=1, keepdims=True) + jnp.exp2(sink - m[i])) for i in subs]
    p = [e[i].astype(BF16) * inv[i].astype(BF16) for i in subs]
    out = [_dot_tn(p[i].reshape(H * 2 * W, G * W), vm[i]) for i in subs]
    for sub in subs:
        for gi in range(G):
            o_ref[sub * W:(sub + 1) * W, gi * 2 * LANES:(gi + 1) * 2 * LANES] = (
                out[sub][gi * W:(gi + 1) * W].astype(BF16))


def _swa(sink_cols, q1, ks, vs, g_ffn, w_ffn_in, w_ffn_out, B, S):
    W = SWA_WINDOW
    n_sub = SWA_BLOCKS_PER_STEP
    tq = n_sub * W
    steps = S // tq
    cur = lambda w: pl.BlockSpec((tq, w), lambda b, n: (b * steps + n, 0))
    prev = lambda w: pl.BlockSpec((W, w), lambda b, n: (b * (S // W) + jnp.maximum(n_sub * n - 1, 0), 0))
    w_in, w_out, w_shape = _ffn_weight_specs(1, B * steps, lambda b, n: b * steps + n)
    return pl.pallas_call(
        _swa_kernel,
        grid=(B, steps),
        in_specs=[_resident(sink_cols.shape),
                  cur(SWA_Q_W), prev(SWA_KV_W), cur(SWA_KV_W), prev(SWA_KV_W), cur(SWA_KV_W)] + w_in,
        out_specs=[cur(SWA_Q_W)] + w_out,
        out_shape=[jax.ShapeDtypeStruct((B * S, SWA_Q_W), BF16)] + w_shape,
        compiler_params=pltpu.CompilerParams(dimension_semantics=("arbitrary", "arbitrary"),
                                             vmem_limit_bytes=VMEM_LIMIT),
        name="swa",
    )(sink_cols, q1, ks, ks, vs, vs, g_ffn, w_ffn_in, w_ffn_out)


def _pad_heads(wt, heads, width, padded):
    wt = wt.reshape(heads, width, wt.shape[1])
    return jnp.pad(wt, ((0, 0), (0, padded - width), (0, 0))).reshape(heads * padded, wt.shape[2])


def _with_tail(perm, total):
    return np.concatenate([perm, np.arange(len(perm), total, dtype=np.int32)])


def _swa_q_perm():
    idx = np.empty((SWA_GROUPS, 2, SWA_KV_HEADS, HEAD_DIM // 2), np.int32)
    for gi in range(SWA_GROUPS):
        for half in range(2):
            for j in range(SWA_KV_HEADS):
                idx[gi, half, j] = (j * SWA_GROUPS + gi) * HEAD_DIM + half * (HEAD_DIM // 2) + np.arange(HEAD_DIM // 2)
    return idx.reshape(-1)


def _swa_k_perm():
    idx = np.empty((2, SWA_KV_HEADS, HEAD_DIM // 2), np.int32)
    for half in range(2):
        for j in range(SWA_KV_HEADS):
            idx[half, j] = j * HEAD_DIM + half * (HEAD_DIM // 2) + np.arange(HEAD_DIM // 2)
    return idx.reshape(-1)


def _swa_out_perm():
    idx = np.empty((SWA_GROUPS, SWA_KV_HEADS, HEAD_DIM), np.int32)
    for gi in range(SWA_GROUPS):
        for j in range(SWA_KV_HEADS):
            idx[gi, j] = (j * SWA_GROUPS + gi) * HEAD_DIM + np.arange(HEAD_DIM)
    return idx.reshape(-1)


def kernel(x, mem, positions, g_mix_pre, g_mix_post, g_ffn_pre, g_ffn_post, g_mem, w_mem_kv, w_out,
           w_ffn_in, w_ffn_out, w_in_a, b_gates_a, g_mlstm_out, g_kv, w_kv, w_in_b, sinks_b):
    B, S, _ = x.shape
    N = B * S
    assert S % PROJ_A_ROW_TILE == 0 and S % PROJ_B_ROW_TILE == 0 and S % FFN_ROW_TILE == 0
    assert S % (MLSTM_CHUNK * MLSTM_CHUNKS_PER_STEP) == 0
    assert S % (SWA_WINDOW * SWA_BLOCKS_PER_STEP) == 0
    x2d = x.reshape(N, D_MODEL)
    vec = lambda g: g.reshape(1, -1).astype(F32)

    mkm, mvm = _mem_kv(mem, g_mem, w_mem_kv.astype(BF16))

    fold = lambda g, w: g.astype(F32)[:, None] * w

    w_a_t = w_in_a[0].T
    g_a = g_mix_pre[0].astype(F32)[None, :]
    piece = lambda c0, width: w_a_t[c0:c0 + width] * g_a
    c0 = 0
    wq = _pad_heads(piece(c0, MLSTM_QK_W), MLSTM_HEADS, MLSTM_DQK, DQK_PAD).astype(BF16)
    c0 += MLSTM_QK_W
    wk = _pad_heads(piece(c0, MLSTM_QK_W) * (MLSTM_DQK ** -0.5), MLSTM_HEADS, MLSTM_DQK, DQK_PAD).astype(BF16)
    c0 += MLSTM_QK_W
    wv = piece(c0, MLSTM_V_W).astype(BF16)
    c0 += MLSTM_V_W
    wo = piece(c0, MLSTM_V_W).astype(BF16)
    c0 += MLSTM_V_W
    wgt = jnp.pad(piece(c0, 2 * MLSTM_HEADS), ((0, LANES - 2 * MLSTM_HEADS), (0, 0))).astype(BF16)
    c0 += 2 * MLSTM_HEADS
    wmq = piece(c0, MEM_W).astype(BF16)
    bg = jnp.pad(b_gates_a[0].astype(F32), (0, LANES - 2 * MLSTM_HEADS)).reshape(1, LANES)

    inv = 1.0 / (ROPE_THETA ** (jnp.arange(0, HEAD_DIM, 2, dtype=F32) / HEAD_DIM))
    inv = jnp.tile(inv, SWA_KV_HEADS).reshape(1, LANES)
    n_grp = LANES // (HEAD_DIM // 2)
    pos_dense = positions.reshape(N // PROJ_A_ROW_TILE, n_grp, PROJ_A_ROW_TILE // n_grp).transpose(0, 2, 1)
    pos_dense = jnp.repeat(pos_dense, HEAD_DIM // 2, axis=2).reshape(N // n_grp, LANES)
    q, k, v, o, gt, gtt, mo0, cos, sin = _proj_a(x2d, wq, wk, wv, wo, wmq, wgt, bg, mkm, mvm, pos_dense, inv, S)
    g_ffn = g_ffn_pre.astype(F32)[:, :, None]
    hm, wi0, wd0 = _mlstm(q, k, v, gt, gtt, g_ffn, w_ffn_in, w_ffn_out, B, S)

    wo0 = fold(jnp.concatenate([g_mlstm_out[0].astype(F32), jnp.ones((MEM_W,), F32)]), w_out[0]).astype(BF16)
    wkv = fold(g_kv, w_kv)[:, _with_tail(_swa_k_perm(), 2 * SWA_KV_W)].astype(BF16)
    wb = fold(g_mix_pre[1], w_in_b[0])[:, _with_tail(_swa_q_perm(), SWA_Q_W + MEM_W)].astype(BF16)
    x1 = _mix_ffn(0, x2d, hm, mo0, wo0, vec(g_mix_post[0]), vec(g_ffn_post[0]),
                  wi0, wd0, gate=o)
    ks, vs, q1, mo1 = _proj_b(x1, cos, sin, wkv, wb, mkm, mvm, S)

    sink_rows = jnp.repeat((sinks_b[0].astype(F32) * LOG2E).reshape(SWA_KV_HEADS, SWA_GROUPS), SWA_WINDOW, axis=1)
    attn, wi1, wd1 = _swa(sink_rows.reshape(SWA_KV_HEADS, 1, SWA_GROUPS * SWA_WINDOW), q1, ks, vs,
                          g_ffn, w_ffn_in, w_ffn_out, B, S)
    wo1 = w_out[1][_with_tail(_swa_out_perm(), D_MODEL)].astype(BF16)
    xo = _mix_ffn(1, x1, attn, mo1, wo1, vec(g_mix_post[1]), vec(g_ffn_post[1]),
                  wi1, wd1)
    return xo.reshape(B, S, D_MODEL)
```

```python
import numpy as np
import jax
import jax.numpy as jnp
from jax import lax
from jax.experimental import pallas as pl
from jax.experimental.pallas import tpu as pltpu

F32 = jnp.float32
BF16 = jnp.bfloat16

D_MODEL = 1024
DEPTH = 2
HEAD_DIM = 64
EPS = 1e-6
ROPE_THETA = 10000.0
LOG2E = 1.4426950408889634

MLSTM_HEADS = 4
MLSTM_DV = 192
MLSTM_DQK = 96
MLSTM_QK_W = MLSTM_HEADS * MLSTM_DQK
MLSTM_V_W = MLSTM_HEADS * MLSTM_DV
GATE_SOFTCAP = 15.0
M_INIT = -1e30

SWA_Q_HEADS = 12
SWA_KV_HEADS = 4
SWA_GROUPS = SWA_Q_HEADS // SWA_KV_HEADS
SWA_Q_W = SWA_Q_HEADS * HEAD_DIM
SWA_KV_W = SWA_KV_HEADS * HEAD_DIM
SWA_WINDOW = 128

MEM_TOKENS = 256
MEM_HEADS = 4
MEM_HEAD_DIM = 64
MEM_W = MEM_HEADS * MEM_HEAD_DIM

D_FF = 2816

LANES = 128
MXU_TILE = 256

DQK_PAD = LANES
QK_PAD_W = MLSTM_HEADS * DQK_PAD
V_WIN = MXU_TILE
MLSTM_CHUNK = 256
MLSTM_CHUNKS_PER_STEP = 8
FF_CHUNK = MXU_TILE
N_FF_CHUNKS = D_FF // FF_CHUNK
PROJ_A_ROW_TILE = 1024
PROJ_B_ROW_TILE = 2048
PROJ_B_ROW_GROUP = 512
FFN_ROW_TILE = 1024
FFN_ROW_GROUP = 512
SWA_BLOCKS_PER_STEP = 16
VMEM_LIMIT = 56 * 1024 * 1024

_V_WIN_START = (0, 128, 384, 512)
_V_WIN_OFF = (0, 64, 0, 64)
_V_ONES_COL = (192, 0, 192, 0)


def _rms_hat(x):
    return x * lax.rsqrt(jnp.mean(x * x, axis=-1, keepdims=True) + EPS)


def _rms(x, g):
    return _rms_hat(x) * g


def _dot(a, b):
    return jnp.dot(a, b, preferred_element_type=F32)


def _dot_nt(a, b):
    return lax.dot_general(a, b, (((1,), (1,)), ((), ())), preferred_element_type=F32)


def _dot_tn(a, b):
    return lax.dot_general(a, b, (((0,), (0,)), ((), ())), preferred_element_type=F32)


def _sigmoid(x):
    return 1.0 / (1.0 + jnp.exp(-x))


def _resident(shape):
    nd = len(shape)
    return pl.BlockSpec(shape, lambda *_: (0,) * nd, pipeline_mode=pl.Buffered(1))


def _mem_kv_kernel(mem_ref, g_ref, w_ref, mk_ref, mv_ref):
    B = mem_ref.shape[0]
    hn = _rms(mem_ref[...].reshape(B * MEM_TOKENS, D_MODEL), g_ref[0]).astype(BF16)
    kv = _dot(hn, w_ref[0])
    mk = kv[:, :MEM_W] * (MEM_HEAD_DIM ** -0.5 * LOG2E)
    mv = kv[:, MEM_W:]
    lane_head = lax.broadcasted_iota(jnp.int32, mk.shape, 1) >> 6
    for h in range(MEM_HEADS):
        sel = lane_head == h
        mk_ref[0, :, h] = jnp.where(sel, mk, 0.0).astype(BF16).reshape(B, MEM_TOKENS, MEM_W)
        mv_ref[0, :, h] = jnp.where(sel, mv, 0.0).astype(BF16).reshape(B, MEM_TOKENS, MEM_W)


def _mem_kv(mem, g_mem, w_mem_kv_bf16):
    B = mem.shape[0]
    out_sds = jax.ShapeDtypeStruct((DEPTH, B, MEM_HEADS, MEM_TOKENS, MEM_W), BF16)
    out_spec = pl.BlockSpec((1, B, MEM_HEADS, MEM_TOKENS, MEM_W), lambda l: (l, 0, 0, 0, 0))
    return pl.pallas_call(
        _mem_kv_kernel,
        grid=(DEPTH,),
        in_specs=[
            pl.BlockSpec((B, MEM_TOKENS, D_MODEL), lambda l: (0, 0, 0)),
            pl.BlockSpec((1, 1, D_MODEL), lambda l: (l, 0, 0)),
            pl.BlockSpec((1, D_MODEL, 2 * MEM_W), lambda l: (l, 0, 0)),
        ],
        out_specs=[out_spec, out_spec],
        out_shape=[out_sds, out_sds],
        compiler_params=pltpu.CompilerParams(dimension_semantics=("arbitrary",),
                                             vmem_limit_bytes=VMEM_LIMIT),
        name="mem_kv",
    )(mem, g_mem.reshape(DEPTH, 1, D_MODEL), w_mem_kv_bf16)


def _mem_scores(mq, mk_ref):
    return [_dot_nt(mq, mk_ref[0, 0, h]) for h in range(MEM_HEADS)]


def _mem_probs(s):
    e = [jnp.exp2(sh - jnp.max(sh, axis=-1, keepdims=True)) for sh in s]
    return [(eh * (1.0 / jnp.sum(eh, axis=-1, keepdims=True))).astype(BF16) for eh in e]


def _mem_out(p, mv_ref):
    out = _dot(p[0], mv_ref[0, 0, 0])
    for h in range(1, MEM_HEADS):
        out = out + _dot(p[h], mv_ref[0, 0, h])
    return out


def _proj_a_kernel(x_ref, wq_ref, wk_ref, wv_ref, wo_ref, wmq_ref, wg_ref, bg_ref, mk_ref, mv_ref, pos_ref, inv_ref,
                   q_ref, k_ref, v_ref, o_ref, gt_ref, gtt_ref, mo_ref, cos_ref, sin_ref):
    cos_ref[...], sin_ref[...] = _rope_tables(pos_ref, inv_ref)
    hn = _rms_hat(x_ref[...]).astype(BF16)
    mq = _dot_nt(hn, wmq_ref[...]).astype(BF16)
    s = _mem_scores(mq, mk_ref)
    gates = _dot_nt(hn, wg_ref[...]) + bg_ref[...]
    q_ref[...] = _dot_nt(hn, wq_ref[...]).astype(BF16)
    gates = GATE_SOFTCAP * jnp.tanh(gates * (1.0 / GATE_SOFTCAP))
    log_sig = jnp.minimum(gates, 0.0) - jnp.log1p(jnp.exp(-jnp.abs(gates)))
    lane = lax.broadcasted_iota(jnp.int32, gates.shape, 1)
    gl = jnp.where(lane < MLSTM_HEADS, gates, log_sig) * LOG2E
    p = _mem_probs(s)
    k_ref[...] = _dot_nt(hn, wk_ref[...]).astype(BF16)
    g_hi = gl.astype(BF16)
    rest = gl - g_hi.astype(F32)
    g_mid = rest.astype(BF16)
    g_lo = (rest - g_mid.astype(F32)).astype(BF16)
    L = MLSTM_CHUNK
    tri = (lax.broadcasted_iota(jnp.int32, (L, L), 1) <= lax.broadcasted_iota(jnp.int32, (L, L), 0)).astype(BF16)
    csum = jnp.concatenate(
        [_dot(tri, g_hi[r:r + L]) + _dot(tri, g_mid[r:r + L]) + _dot(tri, g_lo[r:r + L])
         for r in range(0, gl.shape[0], L)], axis=0)
    v_ref[...] = _dot_nt(wv_ref[...], hn).astype(BF16)
    y = jnp.where(lane < MLSTM_HEADS, gl, csum)
    gt_ref[...] = y
    gtt_ref[...] = y.T[0:2 * MLSTM_HEADS, :]
    mo_ref[...] = _mem_out(p, mv_ref).astype(BF16)
    o_ref[...] = _sigmoid(_dot_nt(wo_ref[...], hn)).astype(BF16)


def _proj_a(x2d, wq, wk, wv, wo, wmq, wg, bg, mkm, mvm, pos, inv, rows_per_batch):
    N = x2d.shape[0]
    tm = PROJ_A_ROW_TILE
    steps_per_batch = rows_per_batch // tm
    row = lambda w: pl.BlockSpec((tm, w), lambda i: (i, 0))
    col = pl.BlockSpec((MLSTM_V_W, tm), lambda i: (0, i))
    mem_spec = pl.BlockSpec((1, 1, MEM_HEADS, MEM_TOKENS, MEM_W),
                            lambda i: (0, i // steps_per_batch, 0, 0, 0))
    return pl.pallas_call(
        _proj_a_kernel,
        grid=(N // tm,),
        in_specs=[row(D_MODEL),
                  _resident(wq.shape), _resident(wk.shape), _resident(wv.shape), _resident(wo.shape),
                  _resident(wmq.shape), _resident(wg.shape), _resident((1, LANES)),
                  mem_spec, mem_spec,
                  pl.BlockSpec((tm // (LANES // (HEAD_DIM // 2)), LANES), lambda i: (i, 0)), _resident((1, LANES))],
        out_specs=[row(QK_PAD_W), row(QK_PAD_W), col, col, row(LANES),
                   pl.BlockSpec((2 * MLSTM_HEADS, tm), lambda i: (0, i)), row(MEM_W), row(LANES), row(LANES)],
        out_shape=[jax.ShapeDtypeStruct((N, QK_PAD_W), BF16), jax.ShapeDtypeStruct((N, QK_PAD_W), BF16),
                   jax.ShapeDtypeStruct((MLSTM_V_W, N), BF16), jax.ShapeDtypeStruct((MLSTM_V_W, N), BF16),
                   jax.ShapeDtypeStruct((N, LANES), F32), jax.ShapeDtypeStruct((2 * MLSTM_HEADS, N), F32),
                   jax.ShapeDtypeStruct((N, MEM_W), BF16),
                   jax.ShapeDtypeStruct((N, LANES), F32), jax.ShapeDtypeStruct((N, LANES), F32)],
        compiler_params=pltpu.CompilerParams(dimension_semantics=("arbitrary",),
                                             vmem_limit_bytes=VMEM_LIMIT),
        name="proj_a",
    )(x2d, wq, wk, wv, wo, wmq, wg, bg, mkm, mvm, pos, inv)


def _ffn_weight_specs(layer, steps, step_index):
    ri, rd = D_MODEL // steps, D_FF // steps
    assert ri * steps == D_MODEL and rd * steps == D_FF and ri % 16 == 0 and rd % 16 == 0
    src = lambda rows, w: pl.BlockSpec((None, rows, w), lambda *g: (layer, step_index(*g), 0))
    dst = lambda rows, w: pl.BlockSpec((rows, w), lambda *g: (step_index(*g), 0))
    return ([src(ri, 1), src(ri, 2 * D_FF), src(rd, D_MODEL)],
            [dst(ri, 2 * D_FF), dst(rd, D_MODEL)],
            [jax.ShapeDtypeStruct((D_MODEL, 2 * D_FF), BF16), jax.ShapeDtypeStruct((D_FF, D_MODEL), BF16)])


def _prepare_ffn_weights(g_ref, wi_ref, wd_ref, wi_out_ref, wd_out_ref):
    wi_out_ref[...] = (g_ref[...] * wi_ref[...]).astype(BF16)
    wd_out_ref[...] = wd_ref[...].astype(BF16)


def _mlstm_kernel(q_ref, k_ref, vt_ref, gt_ref, gtt_ref, g_ref, wi_ref, wd_ref,
                  hmt_ref, wi_out_ref, wd_out_ref, c_scr, m_scr):
    _prepare_ffn_weights(g_ref, wi_ref, wd_ref, wi_out_ref, wd_out_ref)
    L = MLSTM_CHUNK

    @pl.when(pl.program_id(1) == 0)
    def _():
        c_scr[...] = jnp.zeros(c_scr.shape, F32)
        m_scr[...] = jnp.full(m_scr.shape, M_INIT, F32)

    causal = lax.broadcasted_iota(jnp.int32, (L, L), 0) <= lax.broadcasted_iota(jnp.int32, (L, L), 1)
    feat = lax.broadcasted_iota(jnp.int32, (V_WIN, L), 0)
    heads = range(MLSTM_HEADS)
    m_st = [m_scr[h][0:1, 0:1] for h in heads]
    c_st = [c_scr[h] for h in heads]

    for r0 in range(0, q_ref.shape[0], L):
        y = gt_ref[r0:r0 + L, :]
        yt = gtt_ref[:, r0:r0 + L]
        b_row = [yt[MLSTM_HEADS + h:MLSTM_HEADS + h + 1, :] for h in heads]
        c_row = [b_row[h] - yt[h:h + 1, :] for h in heads]
        c_col = [y[:, MLSTM_HEADS + h:MLSTM_HEADS + h + 1] - y[:, h:h + 1] for h in heads]
        g_tot = [y[L - 1:L, MLSTM_HEADS + h:MLSTM_HEADS + h + 1] for h in heads]
        qh = [q_ref[r0:r0 + L, h * DQK_PAD:(h + 1) * DQK_PAD] for h in heads]
        kh = [k_ref[r0:r0 + L, h * DQK_PAD:(h + 1) * DQK_PAD] for h in heads]
        vt_aug = [jnp.where(feat == _V_ONES_COL[h], jnp.ones((), BF16),
                            vt_ref[_V_WIN_START[h]:_V_WIN_START[h] + V_WIN, r0:r0 + L]) for h in heads]

        both = [_dot_nt(jnp.concatenate([kh[h], c_st[h].astype(BF16)], axis=0), qh[h]) for h in heads]
        qk_t = [t[:L] for t in both]
        inter_t = [t[L:] for t in both]
        dmat = [jnp.where(causal, b_row[h] - c_col[h], -jnp.inf) for h in heads]
        inter_log = [b_row[h] + m_st[h] for h in heads]
        m_row = [jnp.maximum(inter_log[h], jnp.max(dmat[h], axis=0, keepdims=True)) for h in heads]
        scores_t = [(qk_t[h] * jnp.exp2(dmat[h] - m_row[h])).astype(BF16) for h in heads]
        num_t = [_dot(vt_aug[h], scores_t[h]) + jnp.exp2(inter_log[h] - m_row[h]) * inter_t[h] for h in heads]

        a = [g_tot[h] - c_row[h] for h in heads]
        m_new = [jnp.maximum(g_tot[h] + m_st[h], jnp.max(a[h], axis=1, keepdims=True)) for h in heads]
        vtw = [vt_aug[h] * jnp.exp2(a[h] - m_new[h]).astype(BF16) for h in heads]
        c_upd = [_dot(vtw[h], kh[h]) for h in heads]

        for h in heads:
            off, ones_row, lo = _V_WIN_OFF[h], _V_ONES_COL[h], h * MLSTM_DV
            den = num_t[h][ones_row:ones_row + 1, :]
            inv_dd = 1.0 / jnp.maximum(jnp.abs(den), jnp.exp2(-m_row[h]))
            h_t = num_t[h][off:off + MLSTM_DV, :]
            ms = jnp.sum(h_t * h_t, axis=0, keepdims=True) * (1.0 / MLSTM_DV)
            scale = inv_dd * lax.rsqrt(ms * inv_dd * inv_dd + EPS)
            hmt_ref[lo:lo + MLSTM_DV, r0:r0 + L] = (h_t * scale).astype(BF16)

        c_st = [jnp.exp2(g_tot[h] + m_st[h] - m_new[h]) * c_st[h] + c_upd[h] for h in heads]
        m_st = m_new

    for h in heads:
        c_scr[h] = c_st[h]
        m_scr[h] = jnp.broadcast_to(m_st[h], m_scr.shape[1:])


def _mlstm(q, k, vt, gt, gtt, g_ffn, w_ffn_in, w_ffn_out, B, S):
    rows = MLSTM_CHUNK * MLSTM_CHUNKS_PER_STEP
    steps = S // rows
    N = B * S
    row = lambda w: pl.BlockSpec((rows, w), lambda b, c: (b * steps + c, 0))
    col = lambda h: pl.BlockSpec((h, rows), lambda b, c: (0, b * steps + c))
    w_in, w_out, w_shape = _ffn_weight_specs(0, B * steps, lambda b, c: b * steps + c)
    return pl.pallas_call(
        _mlstm_kernel,
        grid=(B, steps),
        in_specs=[row(QK_PAD_W), row(QK_PAD_W), col(MLSTM_V_W), row(LANES), col(2 * MLSTM_HEADS)] + w_in,
        out_specs=[col(MLSTM_V_W)] + w_out,
        out_shape=[jax.ShapeDtypeStruct((MLSTM_V_W, N), BF16)] + w_shape,
        scratch_shapes=[pltpu.VMEM((MLSTM_HEADS, V_WIN, DQK_PAD), F32),
                        pltpu.VMEM((MLSTM_HEADS, 8, LANES), F32)],
        compiler_params=pltpu.CompilerParams(dimension_semantics=("arbitrary", "arbitrary"),
                                             vmem_limit_bytes=VMEM_LIMIT),
        name="mlstm",
    )(q, k, vt, gt, gtt, g_ffn, w_ffn_in, w_ffn_out)


def _mix_ffn_core(x_ref, main_ref, gate_ref, mo_ref, wo_ref, gpost_ref, gfpost_ref, wi_ref, wd_ref):
    mix_w = D_MODEL - MEM_W
    groups = [slice(r, r + FFN_ROW_GROUP) for r in range(0, x_ref.shape[0], FFN_ROW_GROUP)]
    if gate_ref is None:
        main = [_dot(main_ref[g, :], wo_ref[0:mix_w, :]) for g in groups]
    else:
        main = [_dot_tn(main_ref[:, g] * gate_ref[:, g], wo_ref[0:mix_w, :]) for g in groups]
    mix = [m + _dot(mo_ref[g, :], wo_ref[mix_w:, :]) for g, m in zip(groups, main)]
    x1 = [x_ref[g, :] + _rms(m, gpost_ref[...]) for g, m in zip(groups, mix)]
    hf = [_rms_hat(t).astype(BF16) for t in x1]
    act = [[] for _ in groups]
    for c in range(N_FF_CHUNKS):
        lo = c * FF_CHUNK
        for i, h in enumerate(hf):
            g = _dot(h, wi_ref[:, lo:lo + FF_CHUNK])
            u = _dot(h, wi_ref[:, D_FF + lo:D_FF + lo + FF_CHUNK])
            act[i].append((g * _sigmoid(g) * u).astype(BF16))
    acc = [_dot(jnp.concatenate(a, axis=1), wd_ref[...]) for a in act]
    return jnp.concatenate([t + _rms(a, gfpost_ref[...]) for t, a in zip(x1, acc)], axis=0)


def _mix_ffn_kernel(x_ref, main_ref, mo_ref, wo_ref, gpost_ref, gfpost_ref, wi_ref, wd_ref, xo_ref):
    xo_ref[...] = _mix_ffn_core(x_ref, main_ref, None, mo_ref, wo_ref, gpost_ref, gfpost_ref, wi_ref, wd_ref)


def _mix_ffn_gated_kernel(x_ref, main_ref, gate_ref, mo_ref, wo_ref, gpost_ref, gfpost_ref, wi_ref, wd_ref, xo_ref):
    xo_ref[...] = _mix_ffn_core(x_ref, main_ref, gate_ref, mo_ref, wo_ref, gpost_ref, gfpost_ref, wi_ref, wd_ref)


def _rope(t1, t2, cos, sin):
    return t1 * cos - t2 * sin, t2 * cos + t1 * sin


def _rope_tables(pos_ref, inv_ref):
    ang = pos_ref[...].astype(F32) * inv_ref[...]
    cos_d = jnp.cos(ang)
    sin_d = jnp.sin(ang)
    grp = lax.broadcasted_iota(jnp.int32, ang.shape, 1) >> 5

    def spread(t, g):
        y = jnp.where(grp == g, t, 0.0)
        y = y + pltpu.roll(y, HEAD_DIM // 2, 1)
        return y + pltpu.roll(y, HEAD_DIM, 1)

    n_grp = LANES // (HEAD_DIM // 2)
    cos = jnp.concatenate([spread(cos_d, g) for g in range(n_grp)], axis=0)
    sin = jnp.concatenate([spread(sin_d, g) for g in range(n_grp)], axis=0)
    return cos, sin


def _mix_ffn(layer, x2d, main, mo, wo, g_post, g_fpost, wi, wd, gate=None):
    N = x2d.shape[0]
    tm = FFN_ROW_TILE
    row = lambda w: pl.BlockSpec((tm, w), lambda i: (i, 0))
    if gate is None:
        body, main_specs, mains = _mix_ffn_kernel, [row(main.shape[1])], [main]
    else:
        col = pl.BlockSpec((main.shape[0], tm), lambda i: (0, i))
        body, main_specs, mains = _mix_ffn_gated_kernel, [col, col], [main, gate]
    return pl.pallas_call(
        body,
        grid=(N // tm,),
        in_specs=[row(D_MODEL)] + main_specs + [row(MEM_W),
                  _resident(wo.shape),
                  _resident((1, D_MODEL)), _resident((1, D_MODEL)),
                  _resident(wi.shape), _resident(wd.shape)],
        out_specs=row(D_MODEL),
        out_shape=jax.ShapeDtypeStruct((N, D_MODEL), F32),
        compiler_params=pltpu.CompilerParams(dimension_semantics=("arbitrary",),
                                             vmem_limit_bytes=VMEM_LIMIT),
        name="mix_ffn_%d" % layer,
    )(x2d, *mains, mo, wo, g_post, g_fpost, wi, wd)


def _proj_b_kernel(x_ref, cos_ref, sin_ref, wkv_ref, wb_ref, mk_ref, mv_ref,
                   ks_ref, vs_ref, q1_ref, mo1_ref):
    groups = [slice(r, r + PROJ_B_ROW_GROUP) for r in range(0, x_ref.shape[0], PROJ_B_ROW_GROUP)]
    xn = [_rms_hat(x_ref[g, :]).astype(BF16) for g in groups]
    pb = [_dot_nt(t, wb_ref[...]) for t in xn]
    s = [_mem_scores(t[:, SWA_Q_W:].astype(BF16), mk_ref) for t in pb]
    cos, sin = cos_ref[...], sin_ref[...]
    kv = [_dot_nt(t, wkv_ref[...]) for t in xn]
    p = [_mem_probs(t) for t in s]

    for g, t in zip(groups, pb):
        for gi in range(SWA_GROUPS):
            base = gi * 2 * LANES
            t1, t2 = _rope(t[:, base:base + LANES], t[:, base + LANES:base + 2 * LANES], cos[g], sin[g])
            q1_ref[g, base:base + LANES] = t1.astype(BF16)
            q1_ref[g, base + LANES:base + 2 * LANES] = t2.astype(BF16)
    for g, t in zip(groups, p):
        mo1_ref[g, :] = _mem_out(t, mv_ref).astype(BF16)

    scale = HEAD_DIM ** -0.5 * LOG2E
    for g, t in zip(groups, kv):
        k1, k2 = _rope(t[:, 0:LANES], t[:, LANES:2 * LANES], cos[g], sin[g])
        ks_ref[g, 0:LANES] = (k1 * scale).astype(BF16)
        ks_ref[g, LANES:2 * LANES] = (k2 * scale).astype(BF16)
        vs_ref[g, :] = t[:, SWA_KV_W:].astype(BF16)


def _proj_b(x2d, cos, sin, wkv, wb, mkm, mvm, rows_per_batch):
    N = x2d.shape[0]
    tm = PROJ_B_ROW_TILE
    steps_per_batch = rows_per_batch // tm
    row = lambda w: pl.BlockSpec((tm, w), lambda i: (i, 0))
    mem_spec = pl.BlockSpec((1, 1, MEM_HEADS, MEM_TOKENS, MEM_W),
                            lambda i: (1, i // steps_per_batch, 0, 0, 0))
    return pl.pallas_call(
        _proj_b_kernel,
        grid=(N // tm,),
        in_specs=[row(D_MODEL), row(LANES), row(LANES), _resident(wkv.shape), _resident(wb.shape),
                  mem_spec, mem_spec],
        out_specs=[row(SWA_KV_W), row(SWA_KV_W), row(SWA_Q_W), row(MEM_W)],
        out_shape=[jax.ShapeDtypeStruct((N, SWA_KV_W), BF16), jax.ShapeDtypeStruct((N, SWA_KV_W), BF16),
                   jax.ShapeDtypeStruct((N, SWA_Q_W), BF16), jax.ShapeDtypeStruct((N, MEM_W), BF16)],
        compiler_params=pltpu.CompilerParams(dimension_semantics=("arbitrary",),
                                             vmem_limit_bytes=VMEM_LIMIT),
        name="proj_b",
    )(x2d, cos, sin, wkv, wb, mkm, mvm)


def _swa_kernel(sink_ref, q_ref, kp_ref, kc_ref, vp_ref, vc_ref, g_ref, wi_ref, wd_ref,
                o_ref, wi_out_ref, wd_out_ref):
    _prepare_ffn_weights(g_ref, wi_ref, wd_ref, wi_out_ref, wd_out_ref)
    W = SWA_WINDOW
    G = SWA_GROUPS
    n = pl.program_id(1)
    n_sub = q_ref.shape[0] // W
    H = SWA_KV_HEADS
    kpos = lax.broadcasted_iota(jnp.int32, (2 * W, G * W), 0)
    qcol = lax.broadcasted_iota(jnp.int32, (2 * W, G * W), 1)
    diff = (qcol & (W - 1)) + W - kpos
    in_band = (diff >= 0) & (diff < W)
    k_lane = lax.broadcasted_iota(jnp.int32, (2 * W, SWA_KV_W), 1)
    sink = sink_ref[...]
    subs = range(n_sub)
    q, km, vm, valid = [], [], [], []
    for sub in subs:
        q.append(jnp.concatenate([q_ref[sub * W:(sub + 1) * W, gi * 2 * LANES:(gi + 1) * 2 * LANES]
                                  for gi in range(G)], axis=0))
        if sub == 0:
            k = jnp.concatenate([kp_ref[...], kc_ref[0:W, :]], axis=0)
            v = jnp.concatenate([vp_ref[...], vc_ref[0:W, :]], axis=0)
            valid.append(in_band & ((kpos >= W) | (n > 0)))
        else:
            k = kc_ref[(sub - 1) * W:(sub + 1) * W, :]
            v = vc_ref[(sub - 1) * W:(sub + 1) * W, :]
            valid.append(in_band)
        km.append(jnp.concatenate([jnp.where(((k_lane & (LANES - 1)) >> 5) == j, k, jnp.zeros((), BF16))
                                   for j in range(H)], axis=0))
        vm.append(jnp.concatenate([jnp.where((k_lane >> 6) == j, v, jnp.zeros((), BF16))
                                   for j in range(H)], axis=0))
    s = [_dot_nt(km[i], q[i]).reshape(H, 2 * W, G * W) for i in subs]
    s = [jnp.where(valid[i][None], s[i], -jnp.inf) for i in subs]
    m = [jnp.maximum(jnp.max(s[i], axis=1, keepdims=True), sink) for i in subs]
    e = [jnp.exp2(s[i] - m[i]) for i in subs]
    inv = [1.0 / (jnp.sum(e[i], axis=1, keepdims=True) + jnp.exp2(sink - m[i])) for i in subs]
    p = [e[i].astype(BF16) * inv[i].astype(BF16) for i in subs]
    out = [_dot_tn(p[i].reshape(H * 2 * W, G * W), vm[i]) for i in subs]
    for sub in subs:
        for gi in range(G):
            o_ref[sub * W:(sub + 1) * W, gi * 2 * LANES:(gi + 1) * 2 * LANES] = (
                out[sub][gi * W:(gi + 1) * W].astype(BF16))


def _swa(sink_cols, q1, ks, vs, g_ffn, w_ffn_in, w_ffn_out, B, S):
    W = SWA_WINDOW
    n_sub = SWA_BLOCKS_PER_STEP
    tq = n_sub * W
    steps = S // tq
    cur = lambda w: pl.BlockSpec((tq, w), lambda b, n: (b * steps + n, 0))
    prev = lambda w: pl.BlockSpec((W, w), lambda b, n: (b * (S // W) + jnp.maximum(n_sub * n - 1, 0), 0))
    w_in, w_out, w_shape = _ffn_weight_specs(1, B * steps, lambda b, n: b * steps + n)
    return pl.pallas_call(
        _swa_kernel,
        grid=(B, steps),
        in_specs=[_resident(sink_cols.shape),
                  cur(SWA_Q_W), prev(SWA_KV_W), cur(SWA_KV_W), prev(SWA_KV_W), cur(SWA_KV_W)] + w_in,
        out_specs=[cur(SWA_Q_W)] + w_out,
        out_shape=[jax.ShapeDtypeStruct((B * S, SWA_Q_W), BF16)] + w_shape,
        compiler_params=pltpu.CompilerParams(dimension_semantics=("arbitrary", "arbitrary"),
                                             vmem_limit_bytes=VMEM_LIMIT),
        name="swa",
    )(sink_cols, q1, ks, ks, vs, vs, g_ffn, w_ffn_in, w_ffn_out)


def _pad_heads(wt, heads, width, padded):
    wt = wt.reshape(heads, width, wt.shape[1])
    return jnp.pad(wt, ((0, 0), (0, padded - width), (0, 0))).reshape(heads * padded, wt.shape[2])


def _with_tail(perm, total):
    return np.concatenate([perm, np.arange(len(perm), total, dtype=np.int32)])


def _swa_q_perm():
    idx = np.empty((SWA_GROUPS, 2, SWA_KV_HEADS, HEAD_DIM // 2), np.int32)
    for gi in range(SWA_GROUPS):
        for half in range(2):
            for j in range(SWA_KV_HEADS):
                idx[gi, half, j] = (j * SWA_GROUPS + gi) * HEAD_DIM + half * (HEAD_DIM // 2) + np.arange(HEAD_DIM // 2)
    return idx.reshape(-1)


def _swa_k_perm():
    idx = np.empty((2, SWA_KV_HEADS, HEAD_DIM // 2), np.int32)
    for half in range(2):
        for j in range(SWA_KV_HEADS):
            idx[half, j] = j * HEAD_DIM + half * (HEAD_DIM // 2) + np.arange(HEAD_DIM // 2)
    return idx.reshape(-1)


def _swa_out_perm():
    idx = np.empty((SWA_GROUPS, SWA_KV_HEADS, HEAD_DIM), np.int32)
    for gi in range(SWA_GROUPS):
        for j in range(SWA_KV_HEADS):
            idx[gi, j] = (j * SWA_GROUPS + gi) * HEAD_DIM + np.arange(HEAD_DIM)
    return idx.reshape(-1)


def kernel(x, mem, positions, g_mix_pre, g_mix_post, g_ffn_pre, g_ffn_post, g_mem, w_mem_kv, w_out,
           w_ffn_in, w_ffn_out, w_in_a, b_gates_a, g_mlstm_out, g_kv, w_kv, w_in_b, sinks_b):
    B, S, _ = x.shape
    N = B * S
    assert S % PROJ_A_ROW_TILE == 0 and S % PROJ_B_ROW_TILE == 0 and S % FFN_ROW_TILE == 0
    assert S % (MLSTM_CHUNK * MLSTM_CHUNKS_PER_STEP) == 0
    assert S % (SWA_WINDOW * SWA_BLOCKS_PER_STEP) == 0
    x2d = x.reshape(N, D_MODEL)
    vec = lambda g: g.reshape(1, -1).astype(F32)

    mkm, mvm = _mem_kv(mem, g_mem, w_mem_kv.astype(BF16))

    fold = lambda g, w: g.astype(F32)[:, None] * w

    w_a_t = w_in_a[0].T
    g_a = g_mix_pre[0].astype(F32)[None, :]
    piece = lambda c0, width: w_a_t[c0:c0 + width] * g_a
    c0 = 0
    wq = _pad_heads(piece(c0, MLSTM_QK_W), MLSTM_HEADS, MLSTM_DQK, DQK_PAD).astype(BF16)
    c0 += MLSTM_QK_W
    wk = _pad_heads(piece(c0, MLSTM_QK_W) * (MLSTM_DQK ** -0.5), MLSTM_HEADS, MLSTM_DQK, DQK_PAD).astype(BF16)
    c0 += MLSTM_QK_W
    wv = piece(c0, MLSTM_V_W).astype(BF16)
    c0 += MLSTM_V_W
    wo = piece(c0, MLSTM_V_W).astype(BF16)
    c0 += MLSTM_V_W
    wgt = jnp.pad(piece(c0, 2 * MLSTM_HEADS), ((0, LANES - 2 * MLSTM_HEADS), (0, 0))).astype(BF16)
    c0 += 2 * MLSTM_HEADS
    wmq = piece(c0, MEM_W).astype(BF16)
    bg = jnp.pad(b_gates_a[0].astype(F32), (0, LANES - 2 * MLSTM_HEADS)).reshape(1, LANES)

    inv = 1.0 / (ROPE_THETA ** (jnp.arange(0, HEAD_DIM, 2, dtype=F32) / HEAD_DIM))
    inv = jnp.tile(inv, SWA_KV_HEADS).reshape(1, LANES)
    n_grp = LANES // (HEAD_DIM // 2)
    pos_dense = positions.reshape(N // PROJ_A_ROW_TILE, n_grp, PROJ_A_ROW_TILE // n_grp).transpose(0, 2, 1)
    pos_dense = jnp.repeat(pos_dense, HEAD_DIM // 2, axis=2).reshape(N // n_grp, LANES)
    q, k, v, o, gt, gtt, mo0, cos, sin = _proj_a(x2d, wq, wk, wv, wo, wmq, wgt, bg, mkm, mvm, pos_dense, inv, S)
    g_ffn = g_ffn_pre.astype(F32)[:, :, None]
    hm, wi0, wd0 = _mlstm(q, k, v, gt, gtt, g_ffn, w_ffn_in, w_ffn_out, B, S)

    wo0 = fold(jnp.concatenate([g_mlstm_out[0].astype(F32), jnp.ones((MEM_W,), F32)]), w_out[0]).astype(BF16)
    fold_t = lambda g, wt: wt * g.astype(F32)[None, :]
    wkv = fold_t(g_kv, w_kv.T[_with_tail(_swa_k_perm(), 2 * SWA_KV_W)]).astype(BF16)
    wb = fold_t(g_mix_pre[1], w_in_b[0].T[_with_tail(_swa_q_perm(), SWA_Q_W + MEM_W)]).astype(BF16)
    x1 = _mix_ffn(0, x2d, hm, mo0, wo0, vec(g_mix_post[0]), vec(g_ffn_post[0]),
                  wi0, wd0, gate=o)
    ks, vs, q1, mo1 = _proj_b(x1, cos, sin, wkv, wb, mkm, mvm, S)

    sink_rows = jnp.repeat((sinks_b[0].astype(F32) * LOG2E).reshape(SWA_KV_HEADS, SWA_GROUPS), SWA_WINDOW, axis=1)
    attn, wi1, wd1 = _swa(sink_rows.reshape(SWA_KV_HEADS, 1, SWA_GROUPS * SWA_WINDOW), q1, ks, vs,
                          g_ffn, w_ffn_in, w_ffn_out, B, S)
    wo1 = w_out[1][_with_tail(_swa_out_perm(), D_MODEL)].astype(BF16)
    xo = _mix_ffn(1, x1, attn, mo1, wo1, vec(g_mix_post[1]), vec(g_ffn_post[1]),
                  wi1, wd1)
    return xo.reshape(B, S, D_MODEL)
```
